```python
import math
import jax, jax.numpy as jnp
from jax import lax
import numpy as np

D_MODEL = 4096
BATCH = 1
SEQ = 16384
DEPTH = 4

N_MIXERS = 3
RMS_EPS = 1e-6
NEG_INF = -1e30
HEAD_DIM = 128
NSA_HEADS = D_MODEL // HEAD_DIM
NSA_KV_HEADS = 4
NSA_GROUP = NSA_HEADS // NSA_KV_HEADS
CMP_BLOCK = 32
CMP_STRIDE = 16
SEL_BLOCK = 64
SEL_TOPK = 16
WINDOW = 512
Q_BLOCK = 128
FORCED_SCORE = 1e4
NSA_IN = NSA_HEADS * HEAD_DIM + 6 * NSA_KV_HEADS * HEAD_DIM + 3 * NSA_HEADS
T5_BUCKETS = 32
T5_MAX_DISTANCE = 2048
SG_CHUNK = 128
SG_WIDTH = D_MODEL
SG_GROUPS = 32
SG_GROUP_DIM = SG_WIDTH // SG_GROUPS
GLA_HEADS = 4
GLA_DK = D_MODEL // 2
GLA_DV = D_MODEL
GLA_DK_HEAD = GLA_DK // GLA_HEADS
GLA_DV_HEAD = GLA_DV // GLA_HEADS
GLA_GATE_RANK = 16
GLA_GATE_TEMP = 16.0
GLA_CHUNK = 64
GLA_IN = 2 * GLA_DK + 2 * GLA_DV + GLA_GATE_RANK
FF_DENSE = 2 * D_MODEL
MOE_EXPERTS = 8
MOE_TOPK = 2
FF_EXPERT = D_MODEL // 2

kernel_name = 'hybrid_nsa_sgu_gla_moe_trunk'


def rms_norm(x, gain):
    xf = x.astype(jnp.float32)
    y = xf * lax.rsqrt(jnp.mean(xf * xf, axis=-1, keepdims=True) + RMS_EPS)
    return (y * gain.astype(jnp.float32)).astype(x.dtype)


def masked_softmax(s, mask, axis=-1):
    s = jnp.where(mask, s.astype(jnp.float32), NEG_INF)
    p = jax.nn.softmax(s, axis=axis)
    return jnp.where(mask, p, 0.0)


def t5_bucket(dist):
    dist = jnp.maximum(dist, 0)
    max_exact = T5_BUCKETS // 2
    d_f = jnp.maximum(dist, 1).astype(jnp.float32)
    log_b = max_exact + (jnp.log(d_f / max_exact) / math.log(T5_MAX_DISTANCE / max_exact)
                         * (T5_BUCKETS - max_exact)).astype(jnp.int32)
    log_b = jnp.minimum(log_b, T5_BUCKETS - 1)
    return jnp.where(dist < max_exact, dist, log_b)


def nsa_mixer(h, rel_bias, w_in, cmp_pe, cmp_w1, cmp_w2, q_gain, k_gain, w_out):
    B, S, _ = h.shape
    G, HPG, DH = NSA_KV_HEADS, NSA_GROUP, HEAD_DIM
    HD = NSA_HEADS * DH
    proj = h @ w_in
    q = rms_norm(proj[..., :HD].reshape(B, S, G, HPG, DH), q_gain)
    kv = proj[..., HD:HD + 6 * G * DH].reshape(B, S, 6, G, DH)
    gates = jax.nn.sigmoid(proj[..., HD + 6 * G * DH:].astype(jnp.float32)).reshape(B, S, G, HPG, 3)
    k_c_raw, v_c_raw = kv[:, :, 0], kv[:, :, 1]
    k_s = rms_norm(kv[:, :, 2], k_gain[1])
    v_s = kv[:, :, 3]
    k_w = rms_norm(kv[:, :, 4], k_gain[2])
    v_w = kv[:, :, 5]

    nc = (S - CMP_BLOCK) // CMP_STRIDE + 1
    c_start = jnp.arange(nc) * CMP_STRIDE
    c_end = c_start + CMP_BLOCK - 1
    idx = c_start[:, None] + jnp.arange(CMP_BLOCK)[None, :]

    def compress(raw, j):
        blocks = raw[:, idx] + cmp_pe[j][None, None, :, None, :]
        blocks = blocks.transpose(0, 1, 3, 2, 4).reshape(B, nc, G, CMP_BLOCK * DH)
        return jax.nn.gelu(blocks @ cmp_w1[j]) @ cmp_w2[j]

    k_c = rms_norm(compress(k_c_raw, 0), k_gain[0]).astype(jnp.float32)
    v_c = compress(v_c_raw, 1).astype(jnp.float32)

    nsel = S // SEL_BLOCK
    n_top = min(SEL_TOPK, nsel)
    s_start = jnp.arange(nsel) * SEL_BLOCK
    overlap = jnp.clip(jnp.minimum(c_start[:, None] + CMP_BLOCK, s_start[None, :] + SEL_BLOCK)
                       - jnp.maximum(c_start[:, None], s_start[None, :]), 0)
    cmp_to_sel = overlap.astype(jnp.float32) / CMP_BLOCK
    k_sb = k_s.reshape(B, nsel, SEL_BLOCK, G, DH).transpose(0, 3, 1, 2, 4)
    v_sb = v_s.reshape(B, nsel, SEL_BLOCK, G, DH).transpose(0, 3, 1, 2, 4)

    pad = ((0, 0), (WINDOW, 0), (0, 0), (0, 0))
    k_wp = jnp.pad(k_w, pad)
    v_wp = jnp.pad(v_w, pad)

    tbl = rel_bias.astype(jnp.float32).reshape(T5_BUCKETS, G, HPG)
    scale = HEAD_DIM ** -0.5
    nq = S // Q_BLOCK
    q_blocks = q.reshape(B, nq, Q_BLOCK, G, HPG, DH).transpose(1, 0, 2, 3, 4, 5)
    g_blocks = gates.reshape(B, nq, Q_BLOCK, G, HPG, 3).transpose(1, 0, 2, 3, 4, 5)
    bi = jnp.arange(B)[:, None, None, None]
    gi = jnp.arange(G)[None, :, None, None]
    gi5 = jnp.arange(G)[None, :, None, None, None]
    blk = jnp.arange(nsel)

    def block(args):
        qb, qq, gg = args
        t = qb * Q_BLOCK + jnp.arange(Q_BLOCK)
        qf = qq.astype(jnp.float32) * scale
        dist_c = t[:, None] - c_end[None, :]
        s_c = jnp.einsum('bqghd,bcgd->bghqc', qf, k_c) + tbl[t5_bucket(dist_c)].transpose(2, 3, 0, 1)
        p_c = masked_softmax(s_c, dist_c >= 0)
        o_c = jnp.einsum('bghqc,bcgd->bqghd', p_c, v_c)
        imp = jnp.einsum('bghqc,cn->bgqn', p_c, cmp_to_sel)
        cur = t // SEL_BLOCK
        forced = (blk[None, :] == cur[:, None]) | (blk[None, :] == 0)
        valid = blk[None, :] <= cur[:, None]
        imp = jnp.where(forced, FORCED_SCORE, jnp.where(valid, imp, -1.0))
        _, sel = lax.top_k(imp, n_top)
        k_g = k_sb[bi, gi, sel].astype(jnp.float32)
        v_g = v_sb[bi, gi, sel].astype(jnp.float32)
        pos = sel[..., None] * SEL_BLOCK + jnp.arange(SEL_BLOCK)
        dist_s = t[None, None, :, None, None] - pos
        bias_s = tbl[t5_bucket(dist_s), gi5].transpose(0, 1, 5, 2, 3, 4)
        s_s = jnp.einsum('bqghd,bgqksd->bghqks', qf, k_g) + bias_s
        p_s = masked_softmax(s_s, (dist_s >= 0)[:, :, None], axis=(-2, -1))
        o_s = jnp.einsum('bghqks,bgqksd->bqghd', p_s, v_g)
        start = qb * Q_BLOCK
        k_wb = lax.dynamic_slice_in_dim(k_wp, start, WINDOW + Q_BLOCK, axis=1).astype(jnp.float32)
        v_wb = lax.dynamic_slice_in_dim(v_wp, start, WINDOW + Q_BLOCK, axis=1).astype(jnp.float32)
        kpos = start - WINDOW + jnp.arange(WINDOW + Q_BLOCK)
        dist_w = t[:, None] - kpos[None, :]
        mask_w = (dist_w >= 0) & (dist_w < WINDOW) & (kpos[None, :] >= 0)
        s_w = jnp.einsum('bqghd,bkgd->bghqk', qf, k_wb) + tbl[t5_bucket(dist_w)].transpose(2, 3, 0, 1)
        p_w = masked_softmax(s_w, mask_w)
        o_w = jnp.einsum('bghqk,bkgd->bqghd', p_w, v_wb)
        gf = gg.astype(jnp.float32)
        o = gf[..., 0:1] * o_c + gf[..., 1:2] * o_s + gf[..., 2:3] * o_w
        return o.reshape(B, Q_BLOCK, HD).astype(h.dtype)

    out = lax.map(block, (jnp.arange(nq), q_blocks, g_blocks))
    out = out.transpose(1, 0, 2, 3).reshape(B, S, HD)
    return out @ w_out


def sgu_mixer(h, w_in, v_gain, w_s, b_s, w_out):
    B, S, _ = h.shape
    uv = jax.nn.gelu(h @ w_in)
    u, v = uv[..., :SG_WIDTH], uv[..., SG_WIDTH:]
    v = rms_norm(v, v_gain)
    v = v.reshape(B, S // SG_CHUNK, SG_CHUNK, SG_GROUPS, SG_GROUP_DIM)
    causal = jnp.tril(jnp.ones((SG_CHUNK, SG_CHUNK), dtype=bool))
    w = jnp.where(causal, w_s, 0)
    mixed = jnp.einsum('gts,bcsgd->bctgd', w, v) + b_s.transpose(1, 0)[None, None, :, :, None]
    y = u * mixed.reshape(B, S, SG_WIDTH)
    return y @ w_out


def gla_mixer(h, w_in, w_gate2, b_gate, o_gain, w_out):
    B, S, _ = h.shape
    proj = h @ w_in
    q = proj[..., :GLA_DK]
    k = proj[..., GLA_DK:2 * GLA_DK]
    v = proj[..., 2 * GLA_DK:2 * GLA_DK + GLA_DV]
    r = proj[..., 2 * GLA_DK + GLA_DV:2 * GLA_DK + 2 * GLA_DV]
    g1 = proj[..., 2 * GLA_DK + 2 * GLA_DV:]
    log_a = jax.nn.log_sigmoid((g1 @ w_gate2 + b_gate).astype(jnp.float32)) / GLA_GATE_TEMP
    nch = S // GLA_CHUNK

    def heads(z, d):
        return z.astype(jnp.float32).reshape(B, nch, GLA_CHUNK, GLA_HEADS, d).transpose(1, 0, 3, 2, 4)

    qh = heads(q, GLA_DK_HEAD) * (GLA_DK_HEAD ** -0.5)
    kh = heads(k, GLA_DK_HEAD)
    vh = heads(v, GLA_DV_HEAD)
    ah = heads(log_a, GLA_DK_HEAD)
    causal = jnp.tril(jnp.ones((GLA_CHUNK, GLA_CHUNK), dtype=bool))

    def step(state, xs):
        qc, kc, vc, ac = xs
        b = jnp.cumsum(ac, axis=-2)
        o_inter = jnp.einsum('bhcd,bhde->bhce', qc * jnp.exp(b), state)
        diff = b[:, :, :, None, :] - b[:, :, None, :, :]
        decay = jnp.where(causal[:, :, None], jnp.exp(jnp.minimum(diff, 0.0)), 0.0)
        attn = jnp.einsum('bhid,bhjd,bhijd->bhij', qc, kc, decay)
        o_intra = jnp.einsum('bhij,bhje->bhie', attn, vc)
        b_last = b[:, :, -1:, :]
        state = (jnp.exp(b_last[:, :, 0, :])[..., None] * state
                 + jnp.einsum('bhjd,bhje->bhde', kc * jnp.exp(b_last - b), vc))
        return state, o_inter + o_intra

    state0 = jnp.zeros((B, GLA_HEADS, GLA_DK_HEAD, GLA_DV_HEAD), jnp.float32)
    _, o = lax.scan(step, state0, (qh, kh, vh, ah))
    o = o.transpose(1, 0, 3, 2, 4).reshape(B, S, GLA_HEADS, GLA_DV_HEAD)
    o = rms_norm(o, o_gain).reshape(B, S, GLA_DV) * jax.nn.silu(r.astype(jnp.float32))
    return o.astype(h.dtype) @ w_out


def swiglu(h, w_up, w_down):
    a, b = jnp.split(h @ w_up, 2, axis=-1)
    return (jax.nn.silu(a) * b) @ w_down


def moe_ffn(h, router, w_up, w_down):
    logits = (h @ router).astype(jnp.float32)
    top_val, top_idx = lax.top_k(logits, MOE_TOPK)
    top_w = jax.nn.softmax(top_val, axis=-1)
    gate = jnp.sum(jax.nn.one_hot(top_idx, MOE_EXPERTS, dtype=jnp.float32) * top_w[..., None], axis=-2)
    y = jnp.zeros(h.shape, jnp.float32)
    for e in range(MOE_EXPERTS):
        y = y + gate[..., e:e + 1] * swiglu(h, w_up[e], w_down[e]).astype(jnp.float32)
    return y.astype(h.dtype)


def setup_inputs(seed: int = 0) -> dict:
    key = jax.random.key(seed)
    ks = iter(jax.random.split(key, 32))

    def nrm(shape, scale):
        return jax.random.normal(next(ks), shape, jnp.float32) * scale

    D = D_MODEL
    n_a = len(range(0, DEPTH, N_MIXERS))
    n_b = len(range(1, DEPTH, N_MIXERS))
    n_c = len(range(2, DEPTH, N_MIXERS))
    n_dense = len(range(0, DEPTH, 2))
    n_moe = len(range(1, DEPTH, 2))
    res = (2.0 * DEPTH) ** -0.5
    return {
        'x': nrm((BATCH, SEQ, D), 1.0),
        'rel_bias': nrm((T5_BUCKETS, NSA_HEADS), 0.2),
        'norm_gain': 1.0 + nrm((DEPTH, 2, D), 0.02),
        'nsa_w_in': nrm((n_a, D, NSA_IN), D ** -0.5),
        'nsa_cmp_pe': nrm((n_a, 2, CMP_BLOCK, HEAD_DIM), 0.02),
        'nsa_cmp_w1': nrm((n_a, 2, CMP_BLOCK * HEAD_DIM, HEAD_DIM), (CMP_BLOCK * HEAD_DIM) ** -0.5),
        'nsa_cmp_w2': nrm((n_a, 2, HEAD_DIM, HEAD_DIM), HEAD_DIM ** -0.5),
        'nsa_q_gain': 1.0 + nrm((n_a, HEAD_DIM), 0.02),
        'nsa_k_gain': 1.0 + nrm((n_a, 3, HEAD_DIM), 0.02),
        'nsa_w_out': nrm((n_a, NSA_HEADS * HEAD_DIM, D), (NSA_HEADS * HEAD_DIM) ** -0.5 * res),
        'sg_w_in': nrm((n_b, D, 2 * SG_WIDTH), D ** -0.5),
        'sg_v_gain': 1.0 + nrm((n_b, SG_WIDTH), 0.02),
        'sg_w_s': nrm((n_b, SG_GROUPS, SG_CHUNK, SG_CHUNK), SG_CHUNK ** -0.5),
        'sg_b_s': 1.0 + nrm((n_b, SG_GROUPS, SG_CHUNK), 0.1),
        'sg_w_out': nrm((n_b, SG_WIDTH, D), SG_WIDTH ** -0.5 * res),
        'gla_w_in': nrm((n_c, D, GLA_IN), D ** -0.5),
        'gla_w_gate2': nrm((n_c, GLA_GATE_RANK, GLA_DK), GLA_GATE_RANK ** -0.5),
        'gla_b_gate': nrm((n_c, GLA_DK), 0.1),
        'gla_o_gain': 1.0 + nrm((n_c, GLA_DV_HEAD), 0.02),
        'gla_w_out': nrm((n_c, GLA_DV, D), GLA_DV ** -0.5 * res),
        'ffn_w_up': nrm((n_dense, D, 2 * FF_DENSE), D ** -0.5),
        'ffn_w_down': nrm((n_dense, FF_DENSE, D), FF_DENSE ** -0.5 * res),
        'moe_router': nrm((n_moe, D, MOE_EXPERTS), D ** -0.5),
        'moe_w_up': nrm((n_moe, MOE_EXPERTS, D, 2 * FF_EXPERT), D ** -0.5),
        'moe_w_down': nrm((n_moe, MOE_EXPERTS, FF_EXPERT, D), FF_EXPERT ** -0.5 * res),
    }


def reference(x, rel_bias, norm_gain, nsa_w_in, nsa_cmp_pe, nsa_cmp_w1, nsa_cmp_w2, nsa_q_gain,
              nsa_k_gain, nsa_w_out, sg_w_in, sg_v_gain, sg_w_s, sg_b_s, sg_w_out, gla_w_in,
              gla_w_gate2, gla_b_gate, gla_o_gain, gla_w_out, ffn_w_up, ffn_w_down, moe_router,
              moe_w_up, moe_w_down):
    h = x
    for i in range(DEPTH):
        mixer = i % N_MIXERS
        j = i // N_MIXERS
        hn = rms_norm(h, norm_gain[i, 0])
        if mixer == 0:
            h = h + nsa_mixer(hn, rel_bias, nsa_w_in[j], nsa_cmp_pe[j], nsa_cmp_w1[j], nsa_cmp_w2[j],
                              nsa_q_gain[j], nsa_k_gain[j], nsa_w_out[j])
        elif mixer == 1:
            h = h + sgu_mixer(hn, sg_w_in[j], sg_v_gain[j], sg_w_s[j], sg_b_s[j], sg_w_out[j])
        else:
            h = h + gla_mixer(hn, gla_w_in[j], gla_w_gate2[j], gla_b_gate[j], gla_o_gain[j], gla_w_out[j])
        hn = rms_norm(h, norm_gain[i, 1])
        f = i // 2
        if i % 2 == 0:
            h = h + swiglu(hn, ffn_w_up[f], ffn_w_down[f])
        else:
            h = h + moe_ffn(hn, moe_router[f], moe_w_up[f], moe_w_down[f])
    return h
```

```python
import functools
import math

import jax
import jax.numpy as jnp
from jax import lax
from jax.experimental import pallas as pl
from jax.experimental.pallas import tpu as pltpu

F32 = jnp.float32
BF16 = jnp.bfloat16

DEPTH = 4
N_MIXERS = 3
RMS_EPS = 1e-6
NEG = -1e30
HEAD_DIM = 128
NSA_KV_HEADS = 4
NSA_GROUP = 8
CMP_BLOCK = 32
CMP_STRIDE = 16
SEL_BLOCK = 64
SEL_TOPK = 16
WINDOW = 512
Q_BLOCK = 128
FORCED_SCORE = 1e4
T5_BUCKETS = 32
T5_MAX_DISTANCE = 2048
SG_CHUNK = 128
SG_GROUPS = 32
GLA_HEADS = 4
GLA_GATE_RANK = 16
GLA_GATE_TEMP = 16.0
GLA_CHUNK = 64
GLA_SUB = 16
MOE_EXPERTS = 8

LANES = 128
SEL_TILE = 512
SEL_NEAR_TILES = 5
CMP_NEAR_SHIFT = 12
VMEM_MB = 56


def _cparams(sem, vmem_mb=VMEM_MB):
    return pltpu.CompilerParams(dimension_semantics=sem, vmem_limit_bytes=vmem_mb * 2**20)


def _dot(a, b):
    return jnp.dot(a, b, preferred_element_type=F32)


def _dot_nt(a, b):
    return lax.dot_general(a, b, (((1,), (1,)), ((), ())), preferred_element_type=F32)


def _rmsnorm_body(x_ref, g_ref, o_ref):
    x = x_ref[...]
    ms = jnp.mean(x * x, axis=-1, keepdims=True)
    o_ref[...] = (x * lax.rsqrt(ms + RMS_EPS) * g_ref[...]).astype(o_ref.dtype)


def _rmsnorm(x, gain, tm=256):
    m, d = x.shape
    return pl.pallas_call(
        _rmsnorm_body,
        grid=(m // tm,),
        in_specs=[pl.BlockSpec((tm, d), lambda i: (i, 0)), pl.BlockSpec((1, d), lambda i: (0, 0))],
        out_specs=pl.BlockSpec((tm, d), lambda i: (i, 0)),
        out_shape=jax.ShapeDtypeStruct((m, d), BF16),
        compiler_params=_cparams(("parallel",)),
        name="rmsnorm",
    )(x, gain.reshape(1, d))


def _rmsnorm_router_body(x_ref, g_ref, r_ref, o_ref, gate_ref):
    x = x_ref[...]
    ms = jnp.mean(x * x, axis=-1, keepdims=True)
    hn = (x * lax.rsqrt(ms + RMS_EPS) * g_ref[...]).astype(BF16)
    o_ref[...] = hn
    logits = _dot(hn, r_ref[...])
    lane = lax.broadcasted_iota(jnp.int32, logits.shape, 1).astype(F32)
    logits = jnp.where(lane < MOE_EXPERTS, logits, NEG)
    v1 = jnp.max(logits, axis=-1, keepdims=True)
    i1 = jnp.min(jnp.where(logits == v1, lane, 1e3), axis=-1, keepdims=True)
    rest = jnp.where(lane == i1, NEG, logits)
    v2 = jnp.max(rest, axis=-1, keepdims=True)
    i2 = jnp.min(jnp.where(rest == v2, lane, 1e3), axis=-1, keepdims=True)
    e2 = jnp.exp(v2 - v1)
    den = 1.0 + e2
    gate_ref[...] = jnp.where(lane == i1, 1.0 / den, 0.0) + jnp.where(lane == i2, e2 / den, 0.0)


def _rmsnorm_router(x, gain, router, tm=256):
    m, d = x.shape
    rpad = jnp.zeros((d, LANES), BF16).at[:, :MOE_EXPERTS].set(router.astype(BF16))
    return pl.pallas_call(
        _rmsnorm_router_body,
        grid=(m // tm,),
        in_specs=[pl.BlockSpec((tm, d), lambda i: (i, 0)), pl.BlockSpec((1, d), lambda i: (0, 0)),
                  pl.BlockSpec((d, LANES), lambda i: (0, 0))],
        out_specs=[pl.BlockSpec((tm, d), lambda i: (i, 0)), pl.BlockSpec((tm, LANES), lambda i: (i, 0))],
        out_shape=[jax.ShapeDtypeStruct((m, d), BF16), jax.ShapeDtypeStruct((m, LANES), F32)],
        compiler_params=_cparams(("parallel",)),
        name="rmsnorm_router",
    )(x, gain.reshape(1, d), rpad)


def _matmul(x, w_list, epilogue, out_list, *, tm, tn, n_total, tk=None, extras=(), name="matmul"):
    m, kdim = x.shape
    tk = kdim if tk is None else tk
    nk = kdim // tk
    nw, ne, no = len(w_list), len(extras), len(out_list)

    def body(*refs):
        x_ref = refs[0]
        w_refs = refs[1:1 + nw]
        e_refs = refs[1 + nw:1 + nw + ne]
        o_refs = refs[1 + nw + ne:1 + nw + ne + no]
        acc_refs = refs[1 + nw + ne + no:]
        n = pl.program_id(0)
        if nk == 1:
            xv = x_ref[...]
            epilogue([_dot(xv, w[...]) for w in w_refs], e_refs, o_refs, n)
        else:
            k = pl.program_id(2)

            @pl.when(k == 0)
            def _():
                for a in acc_refs:
                    a[...] = jnp.zeros_like(a)

            xv = x_ref[...]
            for a, w in zip(acc_refs, w_refs):
                a[...] += _dot(xv, w[...])

            @pl.when(k == nk - 1)
            def _():
                epilogue([a[...] for a in acc_refs], e_refs, o_refs, n)

    in_specs = [pl.BlockSpec((tm, tk), lambda n, mi, k: (mi, k))]
    in_specs += [s for _, s in w_list] + [s for _, s in extras]
    scratch = [] if nk == 1 else [pltpu.VMEM((tm, tn), F32) for _ in range(nw)]
    return pl.pallas_call(
        body,
        grid=(n_total // tn, m // tm, nk),
        in_specs=in_specs,
        out_specs=[s for _, s in out_list],
        out_shape=[s for s, _ in out_list],
        scratch_shapes=scratch,
        compiler_params=_cparams(("parallel", "parallel", "arbitrary")),
        name=name,
    )(x, *[a for a, _ in w_list], *[a for a, _ in extras])


def _wspec(tk, tn, off=0):
    return pl.BlockSpec((tk, tn), lambda n, mi, k: (k, n + off))


def _ospec(tm, tn):
    return pl.BlockSpec((tm, tn), lambda n, mi, k: (mi, n))


def _epi_resid(accs, e_refs, o_refs, n):
    o_refs[0][...] = e_refs[0][...] + accs[0]


def _matmul_resid(x, w, resid, *, tm=512, tn=1024, tk=None, name="matmul_resid"):
    m, kdim = x.shape
    nn = w.shape[1]
    tk = min(kdim, 4096) if tk is None else tk
    return _matmul(
        x, [(w, _wspec(tk, tn))], _epi_resid,
        [(jax.ShapeDtypeStruct((m, nn), F32), _ospec(tm, tn))],
        tm=tm, tn=tn, tk=tk, n_total=nn, extras=[(resid, _ospec(tm, tn))], name=name)[0]


def _silu(a):
    return a * (1.0 / (1.0 + jnp.exp(-a)))


def _epi_swiglu(accs, e_refs, o_refs, n):
    a, b = accs
    o_refs[0][...] = (_silu(a) * b).astype(BF16)


def _swiglu_up(x, w_up, *, tm=512, tn=512, name="swiglu_up"):
    m, kdim = x.shape
    ff = w_up.shape[1] // 2
    return _matmul(
        x, [(w_up, _wspec(kdim, tn)), (w_up, _wspec(kdim, tn, ff // tn))], _epi_swiglu,
        [(jax.ShapeDtypeStruct((m, ff), BF16), _ospec(tm, tn))],
        tm=tm, tn=tn, n_total=ff, name=name)[0]


def _moe_up_dense(x, w_up, gate, *, tm=512, tn=512):
    m, kdim = x.shape
    ne = w_up.shape[0]
    ff = w_up.shape[2] // 2
    npe = ff // tn

    def epi(accs, e_refs, o_refs, n):
        a, b = accs
        g = e_refs[0][...]
        lane = lax.broadcasted_iota(jnp.int32, g.shape, 1)
        ge = jnp.sum(jnp.where(lane == n // npe, g, 0.0), axis=-1, keepdims=True)
        o_refs[0][...] = (ge * (_silu(a) * b)).astype(BF16)

    wa = pl.BlockSpec((None, kdim, tn), lambda n, mi, k: (n // npe, k, n % npe))
    wb = pl.BlockSpec((None, kdim, tn), lambda n, mi, k: (n // npe, k, n % npe + npe))
    return _matmul(
        x, [(w_up, wa), (w_up, wb)], epi,
        [(jax.ShapeDtypeStruct((m, ne * ff), BF16), _ospec(tm, tn))],
        tm=tm, tn=tn, n_total=ne * ff,
        extras=[(gate, pl.BlockSpec((tm, LANES), lambda n, mi, k: (mi, 0)))], name="moe_up")[0]


def _nsa_q_proj(hn, wq, q_gain, *, tm=512, tn=1024):
    m, kdim = hn.shape
    nn = wq.shape[1]
    scale = HEAD_DIM ** -0.5

    def epi(accs, e_refs, o_refs, n):
        acc = accs[0]
        gain = e_refs[0][...]
        for c in range(tn // HEAD_DIM):
            a = acc[:, c * HEAD_DIM:(c + 1) * HEAD_DIM]
            ms = jnp.mean(a * a, axis=-1, keepdims=True)
            o_refs[0][:, c * HEAD_DIM:(c + 1) * HEAD_DIM] = (
                a * lax.rsqrt(ms + RMS_EPS) * gain * scale).astype(BF16)

    return _matmul(
        hn, [(wq, _wspec(kdim, tn))], epi,
        [(jax.ShapeDtypeStruct((m, nn), BF16), _ospec(tm, tn))],
        tm=tm, tn=tn, n_total=nn,
        extras=[(q_gain.reshape(1, HEAD_DIM), pl.BlockSpec((1, HEAD_DIM), lambda n, mi, k: (0, 0)))],
        name="nsa_q_proj")[0]


def _nsa_kv_proj(hn, wkv, k_gain, *, tm=512):
    m, kdim = hn.shape
    g_ = NSA_KV_HEADS
    tn = g_ * HEAD_DIM
    gains = jnp.ones((6, 1, HEAD_DIM), F32).at[2, 0].set(k_gain[1]).at[4, 0].set(k_gain[2])

    def epi(accs, e_refs, o_refs, n):
        acc = accs[0]
        kg = e_refs[0][0]
        do_norm = jnp.logical_or(n == 2, n == 4)
        for g in range(g_):
            a = acc[:, g * HEAD_DIM:(g + 1) * HEAD_DIM]
            ms = jnp.mean(a * a, axis=-1, keepdims=True)
            an = a * lax.rsqrt(ms + RMS_EPS) * kg
            o_refs[0][g] = jnp.where(do_norm, an, a).astype(BF16)

    return _matmul(
        hn, [(wkv, _wspec(kdim, tn))], epi,
        [(jax.ShapeDtypeStruct((6 * g_, m, HEAD_DIM), BF16),
          pl.BlockSpec((g_, tm, HEAD_DIM), lambda n, mi, k: (n, mi, 0)))],
        tm=tm, tn=tn, n_total=6 * tn,
        extras=[(gains, pl.BlockSpec((1, 1, HEAD_DIM), lambda n, mi, k: (n, 0, 0)))],
        name="nsa_kv_proj")[0]


def _nsa_gate_proj(hn, wg, *, tm=512):
    m, kdim = hn.shape
    nn = wg.shape[1]

    def epi(accs, e_refs, o_refs, n):
        o_refs[0][...] = 1.0 / (1.0 + jnp.exp(-accs[0]))

    return _matmul(
        hn, [(wg, _wspec(kdim, nn))], epi,
        [(jax.ShapeDtypeStruct((m, nn), F32), _ospec(tm, nn))],
        tm=tm, tn=nn, n_total=nn, name="nsa_gate_proj")[0]


def _compress_body(r_ref, pe_ref, w1_ref, w2_ref, kg_ref, o_ref):
    j = pl.program_id(0)
    r = r_ref[0].astype(F32)
    pe = pe_ref[0]
    half = r.shape[1]
    top = (r + pe[0:1]).astype(BF16)
    bot = (r + pe[1:2]).astype(BF16)
    a = _dot(top, w1_ref[0, 0:half, :])
    b = _dot(bot, w1_ref[0, half:2 * half, :])
    nrow = r.shape[0]
    hid = a + pltpu.roll(b, nrow - 1, 0)
    y = _dot(jax.nn.gelu(hid).astype(BF16), w2_ref[0])
    ms = jnp.mean(y * y, axis=-1, keepdims=True)
    yn = y * lax.rsqrt(ms + RMS_EPS) * kg_ref[...]
    o_ref[0, 0] = jnp.where(j == 0, yn, y).astype(BF16)


def _compress(kv, pe, w1, w2, k_gain0):
    g_ = NSA_KV_HEADS
    s = kv.shape[1]
    nrow = s // CMP_STRIDE
    wid = CMP_STRIDE * HEAD_DIM
    r = kv.reshape(6 * g_, nrow, wid)
    pe2 = pe.reshape(2, 2, wid)
    return pl.pallas_call(
        _compress_body,
        grid=(2, g_),
        in_specs=[pl.BlockSpec((1, nrow, wid), lambda j, g: (j * g_ + g, 0, 0)),
                  pl.BlockSpec((1, 2, wid), lambda j, g: (j, 0, 0)),
                  pl.BlockSpec((1, 2 * wid, HEAD_DIM), lambda j, g: (j, 0, 0)),
                  pl.BlockSpec((1, HEAD_DIM, HEAD_DIM), lambda j, g: (j, 0, 0)),
                  pl.BlockSpec((1, HEAD_DIM), lambda j, g: (0, 0))],
        out_specs=pl.BlockSpec((1, 1, nrow, HEAD_DIM), lambda j, g: (j, g, 0, 0)),
        out_shape=jax.ShapeDtypeStruct((2, g_, nrow, HEAD_DIM), BF16),
        compiler_params=_cparams(("parallel", "parallel")),
        name="nsa_compress",
    )(r, pe2, w1.astype(BF16), w2.astype(BF16), k_gain0.reshape(1, HEAD_DIM))


def _t5_bucket(dist):
    dist = jnp.maximum(dist, 0)
    max_exact = T5_BUCKETS // 2
    d_f = jnp.maximum(dist, 1).astype(F32)
    log_b = max_exact + (jnp.log(d_f / max_exact) / math.log(T5_MAX_DISTANCE / max_exact)
                         * (T5_BUCKETS - max_exact)).astype(jnp.int32)
    log_b = jnp.minimum(log_b, T5_BUCKETS - 1)
    return jnp.where(dist < max_exact, dist, log_b)


def _bias_table(tbl, dist, valid, shift):
    onehot = jax.nn.one_hot(_t5_bucket(dist).reshape(-1), T5_BUCKETS, dtype=F32)
    t = tbl - tbl[T5_BUCKETS - 1:T5_BUCKETS] if shift else tbl
    vals = jnp.einsum("nb,bh->hn", onehot, t, precision=lax.Precision.HIGHEST)
    vals = vals.reshape((tbl.shape[1],) + dist.shape)
    return jnp.where(valid[None], vals, NEG)


def _nsa_tables(rel_bias):
    tbl = rel_bias.astype(F32)
    r = jnp.arange(Q_BLOCK)
    jw = jnp.arange(WINDOW + Q_BLOCK)
    dw = WINDOW + r[:, None] - jw[None, :]
    tab_w = _bias_table(tbl, dw, (dw >= 0) & (dw < WINDOW), False)
    a = jnp.arange(16)
    cc = jnp.arange(2 * LANES)
    dc = (CMP_NEAR_SHIFT * Q_BLOCK + Q_BLOCK * a[:, None, None] + r[None, :, None]
          - CMP_STRIDE * cc[None, None, :] - (CMP_BLOCK - 1))
    tab_c = _bias_table(tbl, dc, dc >= 0, True)
    jj = jnp.arange(SEL_NEAR_TILES)
    col = jnp.arange(SEL_TILE)
    ds_ = r[None, :, None] - Q_BLOCK + SEL_TILE * (jj[:, None, None] + 1) - col[None, None, :]
    tab_s = _bias_table(tbl, ds_, ds_ >= 0, True)
    return tab_w, tab_c, tab_s


def _cmp_to_sel(nc_pad, nsel):
    c_start = jnp.arange(nc_pad) * CMP_STRIDE
    s_start = jnp.arange(nsel) * SEL_BLOCK
    overlap = jnp.clip(jnp.minimum(c_start[:, None] + CMP_BLOCK, s_start[None, :] + SEL_BLOCK)
                       - jnp.maximum(c_start[:, None], s_start[None, :]), 0)
    return (overlap.astype(F32) / CMP_BLOCK).astype(BF16)


def _softmax_rows(s):
    m = jnp.max(s, axis=-1, keepdims=True)
    p = jnp.exp(s - m)
    l = jnp.sum(p, axis=-1, keepdims=True)
    return m, p, l


def _nsa_cw_body(q_ref, gt_ref, kc_ref, vc_ref, msel_ref, tabc_ref, kw_ref, vw_ref, tabw_ref,
                 ocw_ref, neg_ref):
    i = pl.program_id(1)
    f = (i + 16 - CMP_NEAR_SHIFT) // 16 - 1
    kc = kc_ref[0, 0]
    vc = vc_ref[0, 0]
    nc = kc.shape[0]
    nsel = msel_ref.shape[1]
    start = pl.multiple_of(i * Q_BLOCK, Q_BLOCK)
    wlen = WINDOW + Q_BLOCK
    kw = kw_ref[0, pl.ds(start, wlen), :]
    vw = vw_ref[0, pl.ds(start, wlen), :]
    colw = lax.broadcasted_iota(jnp.int32, (Q_BLOCK, wlen), 1)
    w_real = colw >= (WINDOW - Q_BLOCK * i)
    gt = gt_ref[...]
    psum = jnp.zeros((Q_BLOCK, nc), F32)
    for h in range(NSA_GROUP):
        hs = slice(h * HEAD_DIM, (h + 1) * HEAD_DIM)
        qh = q_ref[:, hs]
        s = _dot_nt(qh, kc)
        ta = tabc_ref[h, 0, :, 0:LANES]
        tb = tabc_ref[h, 0, :, LANES:2 * LANES]
        pieces = []
        for ch in range(nc // LANES):
            bias = jnp.where(ch == f, ta, jnp.where(ch == f + 1, tb, jnp.where(ch > f + 1, NEG, 0.0)))
            pieces.append(s[:, ch * LANES:(ch + 1) * LANES] + bias)
        s = jnp.concatenate(pieces, axis=1)
        m, p, l = _softmax_rows(s)
        linv = jnp.where(m > 0.5 * NEG, 1.0 / l, 0.0)
        pn = p * linv
        psum = psum + pn
        oc = _dot(pn.astype(BF16), vc)
        sw = _dot_nt(qh, kw) + tabw_ref[h]
        sw = jnp.where(w_real, sw, NEG)
        _, pw, lw = _softmax_rows(sw)
        ow = _dot((pw * (1.0 / lw)).astype(BF16), vw)
        ocw_ref[:, hs] = gt[:, h:h + 1] * oc + gt[:, 16 + h:17 + h] * ow
    p_hi = psum.astype(BF16)
    p_lo = (psum - p_hi.astype(F32)).astype(BF16)
    msel = msel_ref[...]
    imp = _dot(p_hi, msel) + _dot(p_lo, msel)
    blk = lax.broadcasted_iota(jnp.int32, (Q_BLOCK, nsel), 1).astype(F32)
    row = lax.broadcasted_iota(jnp.int32, (Q_BLOCK, nsel), 0)
    cur = (i * (Q_BLOCK // SEL_BLOCK)).astype(F32) + jnp.where(row >= SEL_BLOCK, 1.0, 0.0)
    forced = jnp.logical_or(blk == cur, blk == 0.0)
    val = jnp.where(forced, FORCED_SCORE, jnp.where(blk <= cur, imp, -1.0))
    chosen = jnp.zeros((Q_BLOCK, nsel), F32)
    for _ in range(SEL_TOPK):
        mx = jnp.max(val, axis=-1, keepdims=True)
        first = jnp.min(jnp.where(val == mx, blk, 1e4), axis=-1, keepdims=True)
        hit = blk == first
        chosen = jnp.where(hit, 1.0, chosen)
        val = jnp.where(hit, -2.0, val)
    neg_ref[0] = jnp.where(chosen > 0.0, 0.0, NEG).astype(BF16)


def _nsa_cw(q, gates, kcv, msel, tab_c, kv_w, tab_w):
    s, hd = q.shape
    g_ = NSA_KV_HEADS
    nq = s // Q_BLOCK
    gw = NSA_GROUP * HEAD_DIM
    nc = kcv.shape[2]
    nsel = msel.shape[1]
    spad = kv_w.shape[1]
    wlen = WINDOW + Q_BLOCK

    def var(i):
        return (i + 16 - CMP_NEAR_SHIFT) % 16

    return pl.pallas_call(
        _nsa_cw_body,
        grid=(g_, nq),
        in_specs=[
            pl.BlockSpec((Q_BLOCK, gw), lambda g, i: (i, g)),
            pl.BlockSpec((Q_BLOCK, LANES), lambda g, i: (i, g)),
            pl.BlockSpec((1, 1, nc, HEAD_DIM), lambda g, i: (0, g, 0, 0)),
            pl.BlockSpec((1, 1, nc, HEAD_DIM), lambda g, i: (1, g, 0, 0)),
            pl.BlockSpec((nc, nsel), lambda g, i: (0, 0)),
            pl.BlockSpec((NSA_GROUP, 1, Q_BLOCK, 2 * LANES), lambda g, i: (g, var(i), 0, 0)),
            pl.BlockSpec((1, spad, HEAD_DIM), lambda g, i: (g, 0, 0)),
            pl.BlockSpec((1, spad, HEAD_DIM), lambda g, i: (g_ + g, 0, 0)),
            pl.BlockSpec((NSA_GROUP, Q_BLOCK, wlen), lambda g, i: (g, 0, 0)),
        ],
        out_specs=[pl.BlockSpec((Q_BLOCK, gw), lambda g, i: (i, g)),
                   pl.BlockSpec((1, Q_BLOCK, nsel), lambda g, i: (g, i, 0))],
        out_shape=[jax.ShapeDtypeStruct((s, hd), F32), jax.ShapeDtypeStruct((g_, s, nsel), BF16)],
        compiler_params=_cparams(("parallel", "arbitrary")),
        name="nsa_cmp_win",
    )(q, gates, kcv, kcv, msel, tab_c, kv_w, kv_w, tab_w)


def _nsa_sel_body(q_ref, gt_ref, ks_ref, vs_ref, neg_ref, tabs_ref, ocw_ref, o_ref,
                  mask_sc, m_sc, l_sc, acc_sc):
    i = pl.program_id(1)
    ntile = i // (SEL_TILE // Q_BLOCK) + 1
    spt = SEL_TILE // LANES
    for c in range(spt):
        mask_sc[c] = jnp.full((Q_BLOCK, LANES), NEG, F32)
    negsel = neg_ref[0]
    nsel = negsel.shape[1]
    bps = LANES // SEL_BLOCK
    cm = (lax.broadcasted_iota(jnp.int32, (nsel, LANES), 0)
          - lax.broadcasted_iota(jnp.int32, (nsel, LANES), 1) // SEL_BLOCK)

    def slab(sl, carry):
        expand = jnp.where(cm == bps * sl, 1.0, 0.0).astype(BF16)
        mask_sc[spt + sl] = _dot(negsel, expand)
        return carry

    lax.fori_loop(0, i + 1, slab, 0)
    gt = gt_ref[...]
    for h in range(NSA_GROUP):
        hs = slice(h * HEAD_DIM, (h + 1) * HEAD_DIM)
        qh = q_ref[:, hs]
        m_sc[...] = jnp.full(m_sc.shape, NEG, F32)
        l_sc[...] = jnp.zeros(l_sc.shape, F32)
        acc_sc[...] = jnp.zeros(acc_sc.shape, F32)

        def tile(jj, bias):
            row0 = pl.multiple_of(Q_BLOCK * (i + 1) - SEL_TILE * jj, Q_BLOCK)
            kt = ks_ref[0, pl.ds(row0, SEL_TILE), :]
            vt = vs_ref[0, pl.ds(row0, SEL_TILE), :]
            sl0 = i + 1 - spt * jj
            mk = jnp.concatenate([mask_sc[sl0 + c] for c in range(spt)], axis=1)
            sc = _dot_nt(qh, kt) + mk
            if bias is not None:
                sc = sc + bias
            m_old = m_sc[...]
            m_new = jnp.maximum(m_old, jnp.max(sc, axis=-1, keepdims=True))
            alpha = jnp.exp(m_old - m_new)
            p = jnp.exp(sc - m_new)
            l_sc[...] = alpha * l_sc[...] + jnp.sum(p, axis=-1, keepdims=True)
            acc_sc[...] = alpha * acc_sc[...] + _dot(p.astype(BF16), vt)
            m_sc[...] = m_new

        for jj in range(SEL_NEAR_TILES):
            @pl.when(jj < ntile)
            def _():
                tile(jj, tabs_ref[h, jj])

        def far(jj, carry):
            tile(jj, None)
            return carry

        lax.fori_loop(SEL_NEAR_TILES, ntile, far, 0)
        o_s = acc_sc[...] * (1.0 / l_sc[...])
        o_ref[:, hs] = (ocw_ref[:, hs] + gt[:, 8 + h:9 + h] * o_s).astype(BF16)


def _nsa_sel(q, gates, kv_s, negsel, tab_s, ocw):
    s, hd = q.shape
    g_ = NSA_KV_HEADS
    nq = s // Q_BLOCK
    gw = NSA_GROUP * HEAD_DIM
    spad = kv_s.shape[1]
    nsel = negsel.shape[2]
    spt = SEL_TILE // LANES
    return pl.pallas_call(
        _nsa_sel_body,
        grid=(g_, nq),
        in_specs=[
            pl.BlockSpec((Q_BLOCK, gw), lambda g, i: (i, g)),
            pl.BlockSpec((Q_BLOCK, LANES), lambda g, i: (i, g)),
            pl.BlockSpec((1, spad, HEAD_DIM), lambda g, i: (g, 0, 0)),
            pl.BlockSpec((1, spad, HEAD_DIM), lambda g, i: (g_ + g, 0, 0)),
            pl.BlockSpec((1, Q_BLOCK, nsel), lambda g, i: (g, i, 0)),
            pl.BlockSpec((NSA_GROUP, SEL_NEAR_TILES, Q_BLOCK, SEL_TILE), lambda g, i: (g, 0, 0, 0)),
            pl.BlockSpec((Q_BLOCK, gw), lambda g, i: (i, g)),
        ],
        out_specs=pl.BlockSpec((Q_BLOCK, gw), lambda g, i: (i, g)),
        out_shape=jax.ShapeDtypeStruct((s, hd), BF16),
        scratch_shapes=[pltpu.VMEM((spt + nq, Q_BLOCK, LANES), F32),
                        pltpu.VMEM((Q_BLOCK, 1), F32), pltpu.VMEM((Q_BLOCK, 1), F32),
                        pltpu.VMEM((Q_BLOCK, HEAD_DIM), F32)],
        compiler_params=_cparams(("parallel", "arbitrary")),
        name="nsa_selected",
    )(q, gates, kv_s, kv_s, negsel, tab_s, ocw)


def _nsa_mixer(h, hn, tables, w_in, cmp_pe, cmp_w1, cmp_w2, q_gain, k_gain, w_out):
    s, d = h.shape
    g_, hpg, dh = NSA_KV_HEADS, NSA_GROUP, HEAD_DIM
    hd = g_ * hpg * dh
    tab_w, tab_c, tab_s = tables
    w_in = w_in.astype(BF16)
    q = _nsa_q_proj(hn, w_in[:, :hd], q_gain)
    kv = _nsa_kv_proj(hn, w_in[:, hd:hd + 6 * g_ * dh], k_gain)
    wg = w_in[:, hd + 6 * g_ * dh:].reshape(d, g_, hpg, 3).transpose(0, 1, 3, 2).reshape(d, g_, 3 * hpg)
    wg = jnp.pad(wg, ((0, 0), (0, 0), (0, LANES - 3 * hpg))).reshape(d, g_ * LANES)
    gates = _nsa_gate_proj(hn, wg)
    kcv = _compress(kv, cmp_pe, cmp_w1, cmp_w2, k_gain[0])
    msel = _cmp_to_sel(s // CMP_STRIDE, s // SEL_BLOCK)
    kv_w = jnp.pad(kv[4 * g_:6 * g_], ((0, 0), (WINDOW, 0), (0, 0)))
    kv_s = jnp.pad(kv[2 * g_:4 * g_], ((0, 0), (SEL_TILE, 0), (0, 0)))
    ocw, negsel = _nsa_cw(q, gates, kcv, msel, tab_c, kv_w, tab_w)
    o = _nsa_sel(q, gates, kv_s, negsel, tab_s, ocw)
    return _matmul_resid(o, w_out.astype(BF16), h, name="nsa_out")


def _gelu_proj(hn, w, *, tm=512, tn=1024):
    m, kdim = hn.shape
    nn = w.shape[1]

    def epi(accs, e_refs, o_refs, n):
        o_refs[0][...] = jax.nn.gelu(accs[0])

    return _matmul(
        hn, [(w, _wspec(kdim, tn))], epi,
        [(jax.ShapeDtypeStruct((m, nn), F32), _ospec(tm, tn))],
        tm=tm, tn=tn, n_total=nn, name="sgu_in")[0]


def _sgu_mix_body(u_ref, v_ref, gain_ref, w_ref, bt_ref, o_ref):
    v = v_ref[...]
    ms = jnp.mean(v * v, axis=-1, keepdims=True)
    vn = (v * lax.rsqrt(ms + RMS_EPS) * gain_ref[...]).astype(BF16)
    t = w_ref.shape[1]
    causal = (lax.broadcasted_iota(jnp.int32, (t, t), 0) >= lax.broadcasted_iota(jnp.int32, (t, t), 1))
    bt = bt_ref[...]
    gd = vn.shape[1] // SG_GROUPS
    for g in range(SG_GROUPS):
        gs = slice(g * gd, (g + 1) * gd)
        w = jnp.where(causal, w_ref[g], 0.0).astype(BF16)
        mixed = _dot(w, vn[:, gs]) + bt[:, g:g + 1]
        o_ref[:, gs] = (u_ref[:, gs] * mixed).astype(BF16)


def _sgu_mix(uv, v_gain, w_s, b_s):
    s = uv.shape[0]
    wd = uv.shape[1] // 2
    t = SG_CHUNK
    return pl.pallas_call(
        _sgu_mix_body,
        grid=(s // t,),
        in_specs=[pl.BlockSpec((t, wd), lambda c: (c, 0)),
                  pl.BlockSpec((t, wd), lambda c: (c, 1)),
                  pl.BlockSpec((1, wd), lambda c: (0, 0)),
                  pl.BlockSpec((SG_GROUPS, t, t), lambda c: (0, 0, 0)),
                  pl.BlockSpec((t, SG_GROUPS), lambda c: (0, 0))],
        out_specs=pl.BlockSpec((t, wd), lambda c: (c, 0)),
        out_shape=jax.ShapeDtypeStruct((s, wd), BF16),
        compiler_params=_cparams(("parallel",)),
        name="sgu_mix",
    )(uv, uv, v_gain.reshape(1, wd), w_s, b_s.T)


def _sgu_mixer(h, hn, w_in, v_gain, w_s, b_s, w_out):
    uv = _gelu_proj(hn, w_in.astype(BF16))
    y = _sgu_mix(uv, v_gain, w_s, b_s)
    return _matmul_resid(y, w_out.astype(BF16), h, name="sgu_out")


def _gla_body(q_ref, k_ref, v_ref, r_ref, g1_ref, wg_ref, bg_ref, og_ref, o_ref, state_sc):
    c = pl.program_id(1)

    @pl.when(c == 0)
    def _():
        state_sc[...] = jnp.zeros_like(state_sc)

    ch = q_ref.shape[0]
    dk = q_ref.shape[1]
    x = _dot(g1_ref[...].astype(BF16), wg_ref[...]) + bg_ref[...]
    log_a = (jnp.minimum(x, 0.0) - jnp.log(1.0 + jnp.exp(-jnp.abs(x)))) * (1.0 / GLA_GATE_TEMP)
    tri = (lax.broadcasted_iota(jnp.int32, (ch, ch), 0)
           >= lax.broadcasted_iota(jnp.int32, (ch, ch), 1))
    tri_b = jnp.where(tri, 1.0, 0.0).astype(BF16)
    a_hi = log_a.astype(BF16)
    a_lo = (log_a - a_hi.astype(F32)).astype(BF16)
    b = _dot(tri_b, a_hi) + _dot(tri_b, a_lo)
    q = q_ref[...].astype(F32) * (dk ** -0.5)
    k = k_ref[...].astype(F32)
    v = v_ref[...]
    state = state_sc[...]
    o = _dot((q * jnp.exp(b)).astype(BF16), state.astype(BF16))
    nsub = ch // GLA_SUB
    ends = [jnp.broadcast_to(b[(jb + 1) * GLA_SUB - 1:(jb + 1) * GLA_SUB], (GLA_SUB, dk))
            for jb in range(nsub)]
    b_end = jnp.concatenate(ends, axis=0)
    b_start = jnp.concatenate([jnp.zeros((GLA_SUB, dk), F32)] + ends[:-1], axis=0)
    rblk = lax.broadcasted_iota(jnp.int32, (ch, ch), 0) // GLA_SUB
    cblk = lax.broadcasted_iota(jnp.int32, (ch, ch), 1) // GLA_SUB
    q_diag = (q * jnp.exp(b - b_start)).astype(BF16)
    k_diag = (k * jnp.exp(b_start - b)).astype(BF16)
    attn = jnp.where(jnp.logical_and(tri, rblk == cblk), _dot_nt(q_diag, k_diag), 0.0)
    k_hat = (k * jnp.exp(b_end - b)).astype(BF16)
    for jb in range(nsub - 1):
        q_hat = (q * jnp.exp(jnp.minimum(b - ends[jb][0:1], 0.0))).astype(BF16)
        attn = attn + jnp.where(jnp.logical_and(cblk == jb, rblk > jb), _dot_nt(q_hat, k_hat), 0.0)
    o = o + _dot(attn.astype(BF16), v)
    b_last = b[ch - 1:ch]
    k_dec = k * jnp.exp(b_last - b)
    k_dec_t = jnp.transpose(k_dec).astype(BF16)
    decay_t = jnp.transpose(jnp.broadcast_to(jnp.exp(b_last), (LANES, dk)))
    state_sc[...] = decay_t[:, 0:1] * state + _dot(k_dec_t, v)
    ms = jnp.mean(o * o, axis=-1, keepdims=True)
    on = o * lax.rsqrt(ms + RMS_EPS) * og_ref[...]
    r = r_ref[...].astype(F32)
    o_ref[...] = (on * _silu(r)).astype(BF16)


def _gla_mixer(h, hn, w_in, w_gate2, b_gate, o_gain, w_out):
    s, d = h.shape
    nh = GLA_HEADS
    dk = d // 2
    dv = d
    dkh, dvh = dk // nh, dv // nh
    w_in = w_in.astype(BF16)
    nmain = 2 * dk + 2 * dv

    def epi_bf16(accs, e_refs, o_refs, n):
        o_refs[0][...] = accs[0].astype(BF16)

    def epi_f32(accs, e_refs, o_refs, n):
        o_refs[0][...] = accs[0]

    tm, tn = 512, 1024
    proj = _matmul(hn, [(w_in, _wspec(d, tn))], epi_bf16,
                   [(jax.ShapeDtypeStruct((s, nmain), BF16), _ospec(tm, tn))],
                   tm=tm, tn=tn, n_total=nmain, name="gla_in")[0]
    wg1 = jnp.pad(w_in[:, nmain:], ((0, 0), (0, LANES - GLA_GATE_RANK)))
    g1 = _matmul(hn, [(wg1, _wspec(d, LANES))], epi_f32,
                 [(jax.ShapeDtypeStruct((s, LANES), F32), _ospec(tm, LANES))],
                 tm=tm, tn=LANES, n_total=LANES, name="gla_gate_in")[0]
    wg2 = jnp.pad(w_gate2.astype(BF16), ((0, LANES - GLA_GATE_RANK), (0, 0)))
    ch = GLA_CHUNK
    o = pl.pallas_call(
        _gla_body,
        grid=(nh, s // ch),
        in_specs=[
            pl.BlockSpec((ch, dkh), lambda hh, c: (c, hh)),
            pl.BlockSpec((ch, dkh), lambda hh, c: (c, nh + hh)),
            pl.BlockSpec((ch, dvh), lambda hh, c: (c, 2 * dk // dvh + hh)),
            pl.BlockSpec((ch, dvh), lambda hh, c: (c, (2 * dk + dv) // dvh + hh)),
            pl.BlockSpec((ch, LANES), lambda hh, c: (c, 0)),
            pl.BlockSpec((LANES, dkh), lambda hh, c: (0, hh)),
            pl.BlockSpec((1, dkh), lambda hh, c: (0, hh)),
            pl.BlockSpec((1, dvh), lambda hh, c: (0, 0)),
        ],
        out_specs=pl.BlockSpec((ch, dvh), lambda hh, c: (c, hh)),
        out_shape=jax.ShapeDtypeStruct((s, dv), BF16),
        scratch_shapes=[pltpu.VMEM((dkh, dvh), F32)],
        compiler_params=_cparams(("parallel", "arbitrary")),
        name="gla_scan",
    )(proj, proj, proj, proj, g1, wg2, b_gate.reshape(1, dk), o_gain.reshape(1, dvh))
    return _matmul_resid(o, w_out.astype(BF16), h, name="gla_out")


def _dense_ffn(h, hn, w_up, w_down):
    act = _swiglu_up(hn, w_up.astype(BF16))
    return _matmul_resid(act, w_down.astype(BF16), h, name="ffn_down")


def _moe_ffn(h, hn, gate, w_up, w_down):
    ne, ff, d = w_down.shape
    act = _moe_up_dense(hn, w_up.astype(BF16), gate)
    return _matmul_resid(act, w_down.astype(BF16).reshape(ne * ff, d), h, name="moe_down")


def kernel(x, rel_bias, norm_gain, nsa_w_in, nsa_cmp_pe, nsa_cmp_w1, nsa_cmp_w2, nsa_q_gain, nsa_k_gain, nsa_w_out, sg_w_in, sg_v_gain, sg_w_s, sg_b_s, sg_w_out, gla_w_in, gla_w_gate2, gla_b_gate, gla_o_gain, gla_w_out, ffn_w_up, ffn_w_down, moe_router, moe_w_up, moe_w_down):
    bsz, s, d = x.shape
    tables = _nsa_tables(rel_bias)
    outs = []
    for bi in range(bsz):
        h = x[bi]
        for i in range(DEPTH):
            mixer = i % N_MIXERS
            j = i // N_MIXERS
            hn = _rmsnorm(h, norm_gain[i, 0])
            if mixer == 0:
                h = _nsa_mixer(h, hn, tables, nsa_w_in[j], nsa_cmp_pe[j], nsa_cmp_w1[j], nsa_cmp_w2[j],
                               nsa_q_gain[j], nsa_k_gain[j], nsa_w_out[j])
            elif mixer == 1:
                h = _sgu_mixer(h, hn, sg_w_in[j], sg_v_gain[j], sg_w_s[j], sg_b_s[j], sg_w_out[j])
            else:
                h = _gla_mixer(h, hn, gla_w_in[j], gla_w_gate2[j], gla_b_gate[j], gla_o_gain[j], gla_w_out[j])
            f = i // 2
            if i % 2 == 0:
                hn = _rmsnorm(h, norm_gain[i, 1])
                h = _dense_ffn(h, hn, ffn_w_up[f], ffn_w_down[f])
            else:
                hn, gate = _rmsnorm_router(h, norm_gain[i, 1], moe_router[f])
                h = _moe_ffn(h, hn, gate, moe_w_up[f], moe_w_down[f])
        outs.append(h)
    return jnp.stack(outs, axis=0)
```

```python
import math

import jax
import jax.numpy as jnp
from jax import lax
from jax.experimental import pallas as pl
from jax.experimental.pallas import tpu as pltpu

F32 = jnp.float32
BF16 = jnp.bfloat16

DEPTH = 4
N_MIXERS = 3
RMS_EPS = 1e-6
NEG = -1e30
HEAD_DIM = 128
NSA_KV_HEADS = 4
NSA_GROUP = 8
CMP_BLOCK = 32
CMP_STRIDE = 16
SEL_BLOCK = 64
SEL_TOPK = 16
WINDOW = 512
Q_BLOCK = 128
FORCED_SCORE = 1e4
T5_BUCKETS = 32
T5_MAX_DISTANCE = 2048
SG_CHUNK = 128
SG_GROUPS = 32
GLA_HEADS = 4
GLA_GATE_RANK = 16
GLA_GATE_TEMP = 16.0
GLA_CHUNK = 64
GLA_SUB = 16
MOE_EXPERTS = 8

LANES = 128
SEL_TILE = 512
SEL_NEAR_TILES = 5
CMP_NEAR_SHIFT = 12
CW_HEADS = 4
VMEM_MB = 56
LOG2E = 1.4426950408889634


def _cparams(sem, vmem_mb=VMEM_MB):
    return pltpu.CompilerParams(dimension_semantics=sem, vmem_limit_bytes=vmem_mb * 2**20)


def _dot(a, b):
    return jnp.dot(a, b, preferred_element_type=F32)


def _dot_nt(a, b):
    return lax.dot_general(a, b, (((1,), (1,)), ((), ())), preferred_element_type=F32)


def _rmsnorm_body(x_ref, g_ref, o_ref):
    x = x_ref[...]
    ms = jnp.mean(x * x, axis=-1, keepdims=True)
    o_ref[...] = (x * lax.rsqrt(ms + RMS_EPS) * g_ref[...]).astype(o_ref.dtype)


def _rmsnorm(x, gain, tm=256):
    m, d = x.shape
    return pl.pallas_call(
        _rmsnorm_body,
        grid=(m // tm,),
        in_specs=[pl.BlockSpec((tm, d), lambda i: (i, 0)), pl.BlockSpec((1, d), lambda i: (0, 0))],
        out_specs=pl.BlockSpec((tm, d), lambda i: (i, 0)),
        out_shape=jax.ShapeDtypeStruct((m, d), BF16),
        compiler_params=_cparams(("parallel",)),
        name="rmsnorm",
    )(x, gain.reshape(1, d))


def _rmsnorm_router_body(x_ref, g_ref, r_ref, o_ref, gate_ref):
    x = x_ref[...]
    ms = jnp.mean(x * x, axis=-1, keepdims=True)
    hn = (x * lax.rsqrt(ms + RMS_EPS) * g_ref[...]).astype(BF16)
    o_ref[...] = hn
    logits = _dot(hn, r_ref[...])
    lane = lax.broadcasted_iota(jnp.int32, logits.shape, 1).astype(F32)
    logits = jnp.where(lane < MOE_EXPERTS, logits, NEG)
    v1 = jnp.max(logits, axis=-1, keepdims=True)
    i1 = jnp.min(jnp.where(logits == v1, lane, 1e3), axis=-1, keepdims=True)
    rest = jnp.where(lane == i1, NEG, logits)
    v2 = jnp.max(rest, axis=-1, keepdims=True)
    i2 = jnp.min(jnp.where(rest == v2, lane, 1e3), axis=-1, keepdims=True)
    e2 = jnp.exp(v2 - v1)
    den = 1.0 + e2
    gate_ref[...] = jnp.where(lane == i1, 1.0 / den, 0.0) + jnp.where(lane == i2, e2 / den, 0.0)


def _rmsnorm_router(x, gain, router, tm=256):
    m, d = x.shape
    rpad = jnp.zeros((d, LANES), BF16).at[:, :MOE_EXPERTS].set(router.astype(BF16))
    return pl.pallas_call(
        _rmsnorm_router_body,
        grid=(m // tm,),
        in_specs=[pl.BlockSpec((tm, d), lambda i: (i, 0)), pl.BlockSpec((1, d), lambda i: (0, 0)),
                  pl.BlockSpec((d, LANES), lambda i: (0, 0))],
        out_specs=[pl.BlockSpec((tm, d), lambda i: (i, 0)), pl.BlockSpec((tm, LANES), lambda i: (i, 0))],
        out_shape=[jax.ShapeDtypeStruct((m, d), BF16), jax.ShapeDtypeStruct((m, LANES), F32)],
        compiler_params=_cparams(("parallel",)),
        name="rmsnorm_router",
    )(x, gain.reshape(1, d), rpad)


def _matmul(x, w_list, epilogue, out_list, *, tm, tn, n_total, tk=None, extras=(), name="matmul"):
    m, kdim = x.shape
    tk = kdim if tk is None else tk
    nk = kdim // tk
    nw, ne, no = len(w_list), len(extras), len(out_list)

    def body(*refs):
        x_ref = refs[0]
        w_refs = refs[1:1 + nw]
        e_refs = refs[1 + nw:1 + nw + ne]
        o_refs = refs[1 + nw + ne:1 + nw + ne + no]
        acc_refs = refs[1 + nw + ne + no:]
        n = pl.program_id(0)
        if nk == 1:
            xv = x_ref[...]
            epilogue([_dot(xv, w[...]) for w in w_refs], e_refs, o_refs, n)
        else:
            k = pl.program_id(2)

            @pl.when(k == 0)
            def _():
                for a in acc_refs:
                    a[...] = jnp.zeros_like(a)

            xv = x_ref[...]
            for a, w in zip(acc_refs, w_refs):
                a[...] += _dot(xv, w[...])

            @pl.when(k == nk - 1)
            def _():
                epilogue([a[...] for a in acc_refs], e_refs, o_refs, n)

    in_specs = [pl.BlockSpec((tm, tk), lambda n, mi, k: (mi, k))]
    in_specs += [s for _, s in w_list] + [s for _, s in extras]
    scratch = [] if nk == 1 else [pltpu.VMEM((tm, tn), F32) for _ in range(nw)]
    return pl.pallas_call(
        body,
        grid=(n_total // tn, m // tm, nk),
        in_specs=in_specs,
        out_specs=[s for _, s in out_list],
        out_shape=[s for s, _ in out_list],
        scratch_shapes=scratch,
        compiler_params=_cparams(("parallel", "parallel", "arbitrary")),
        name=name,
    )(x, *[a for a, _ in w_list], *[a for a, _ in extras])


def _wspec(tk, tn, off=0):
    return pl.BlockSpec((tk, tn), lambda n, mi, k: (k, n + off))


def _ospec(tm, tn):
    return pl.BlockSpec((tm, tn), lambda n, mi, k: (mi, n))


def _epi_resid(accs, e_refs, o_refs, n):
    o_refs[0][...] = e_refs[0][...] + accs[0]


def _matmul_resid(x, w, resid, *, tm=512, tn=1024, tk=None, name="matmul_resid"):
    m, kdim = x.shape
    nn = w.shape[1]
    tk = min(kdim, 4096) if tk is None else tk
    return _matmul(
        x, [(w, _wspec(tk, tn))], _epi_resid,
        [(jax.ShapeDtypeStruct((m, nn), F32), _ospec(tm, tn))],
        tm=tm, tn=tn, tk=tk, n_total=nn, extras=[(resid, _ospec(tm, tn))], name=name)[0]


def _silu(a):
    return a * (1.0 / (1.0 + jnp.exp(-a)))


def _epi_swiglu(accs, e_refs, o_refs, n):
    a, b = accs
    o_refs[0][...] = (_silu(a) * b).astype(BF16)


def _swiglu_up(x, w_up, *, tm=512, tn=512, name="swiglu_up"):
    m, kdim = x.shape
    ff = w_up.shape[1] // 2
    return _matmul(
        x, [(w_up, _wspec(kdim, tn)), (w_up, _wspec(kdim, tn, ff // tn))], _epi_swiglu,
        [(jax.ShapeDtypeStruct((m, ff), BF16), _ospec(tm, tn))],
        tm=tm, tn=tn, n_total=ff, name=name)[0]


def _moe_up_dense(x, w_up, gate, *, tm=512, tn=512):
    m, kdim = x.shape
    ne = w_up.shape[0]
    ff = w_up.shape[2] // 2
    npe = ff // tn

    def epi(accs, e_refs, o_refs, n):
        a, b = accs
        g = e_refs[0][...]
        lane = lax.broadcasted_iota(jnp.int32, g.shape, 1)
        ge = jnp.sum(jnp.where(lane == n // npe, g, 0.0), axis=-1, keepdims=True)
        o_refs[0][...] = (ge * (_silu(a) * b)).astype(BF16)

    wa = pl.BlockSpec((None, kdim, tn), lambda n, mi, k: (n // npe, k, n % npe))
    wb = pl.BlockSpec((None, kdim, tn), lambda n, mi, k: (n // npe, k, n % npe + npe))
    return _matmul(
        x, [(w_up, wa), (w_up, wb)], epi,
        [(jax.ShapeDtypeStruct((m, ne * ff), BF16), _ospec(tm, tn))],
        tm=tm, tn=tn, n_total=ne * ff,
        extras=[(gate, pl.BlockSpec((tm, LANES), lambda n, mi, k: (mi, 0)))], name="moe_up")[0]


def _nsa_q_proj(hn, wq, q_gain, *, tm=512, tn=1024):
    m, kdim = hn.shape
    nn = wq.shape[1]
    scale = HEAD_DIM ** -0.5 * LOG2E

    def epi(accs, e_refs, o_refs, n):
        acc = accs[0]
        gain = e_refs[0][...]
        for c in range(tn // HEAD_DIM):
            a = acc[:, c * HEAD_DIM:(c + 1) * HEAD_DIM]
            ms = jnp.mean(a * a, axis=-1, keepdims=True)
            o_refs[0][:, c * HEAD_DIM:(c + 1) * HEAD_DIM] = (
                a * lax.rsqrt(ms + RMS_EPS) * gain * scale).astype(BF16)

    return _matmul(
        hn, [(wq, _wspec(kdim, tn))], epi,
        [(jax.ShapeDtypeStruct((m, nn), BF16), _ospec(tm, tn))],
        tm=tm, tn=tn, n_total=nn,
        extras=[(q_gain.reshape(1, HEAD_DIM), pl.BlockSpec((1, HEAD_DIM), lambda n, mi, k: (0, 0)))],
        name="nsa_q_proj")[0]


def _nsa_kv_proj(hn, wkv, k_gain, *, tm=512):
    m, kdim = hn.shape
    g_ = NSA_KV_HEADS
    tn = g_ * HEAD_DIM
    gains = jnp.ones((6, 1, HEAD_DIM), F32).at[2, 0].set(k_gain[1]).at[4, 0].set(k_gain[2])

    def epi(accs, e_refs, o_refs, n):
        acc = accs[0]
        kg = e_refs[0][0]
        do_norm = jnp.logical_or(n == 2, n == 4)
        for g in range(g_):
            a = acc[:, g * HEAD_DIM:(g + 1) * HEAD_DIM]
            ms = jnp.mean(a * a, axis=-1, keepdims=True)
            an = a * lax.rsqrt(ms + RMS_EPS) * kg
            o_refs[0][g] = jnp.where(do_norm, an, a).astype(BF16)

    return _matmul(
        hn, [(wkv, _wspec(kdim, tn))], epi,
        [(jax.ShapeDtypeStruct((6 * g_, m, HEAD_DIM), BF16),
          pl.BlockSpec((g_, tm, HEAD_DIM), lambda n, mi, k: (n, mi, 0)))],
        tm=tm, tn=tn, n_total=6 * tn,
        extras=[(gains, pl.BlockSpec((1, 1, HEAD_DIM), lambda n, mi, k: (n, 0, 0)))],
        name="nsa_kv_proj")[0]


def _nsa_gate_proj(hn, wg, *, tm=512):
    m, kdim = hn.shape
    nn = wg.shape[1]

    def epi(accs, e_refs, o_refs, n):
        o_refs[0][...] = 1.0 / (1.0 + jnp.exp(-accs[0]))

    return _matmul(
        hn, [(wg, _wspec(kdim, nn))], epi,
        [(jax.ShapeDtypeStruct((m, nn), F32), _ospec(tm, nn))],
        tm=tm, tn=nn, n_total=nn, name="nsa_gate_proj")[0]


def _compress_body(r_ref, pe_ref, w1_ref, w2_ref, kg_ref, o_ref, ot_ref):
    j = pl.program_id(0)
    r = r_ref[0].astype(F32)
    pe = pe_ref[0]
    half = r.shape[1]
    top = (r + pe[0:1]).astype(BF16)
    bot = (r + pe[1:2]).astype(BF16)
    a = _dot(top, w1_ref[0, 0:half, :])
    b = _dot(bot, w1_ref[0, half:2 * half, :])
    nrow = r.shape[0]
    hid = a + pltpu.roll(b, nrow - 1, 0)
    y = _dot(jax.nn.gelu(hid).astype(BF16), w2_ref[0])
    ms = jnp.mean(y * y, axis=-1, keepdims=True)
    yn = y * lax.rsqrt(ms + RMS_EPS) * kg_ref[...]
    out = jnp.where(j == 0, yn, y)
    o_ref[0, 0] = out.astype(BF16)
    ot_ref[0, 0] = jnp.transpose(out).astype(BF16)


def _compress(kv, pe, w1, w2, k_gain0):
    g_ = NSA_KV_HEADS
    s = kv.shape[1]
    nrow = s // CMP_STRIDE
    wid = CMP_STRIDE * HEAD_DIM
    r = kv.reshape(6 * g_, nrow, wid)
    pe2 = pe.reshape(2, 2, wid)
    return pl.pallas_call(
        _compress_body,
        grid=(2, g_),
        in_specs=[pl.BlockSpec((1, nrow, wid), lambda j, g: (j * g_ + g, 0, 0)),
                  pl.BlockSpec((1, 2, wid), lambda j, g: (j, 0, 0)),
                  pl.BlockSpec((1, 2 * wid, HEAD_DIM), lambda j, g: (j, 0, 0)),
                  pl.BlockSpec((1, HEAD_DIM, HEAD_DIM), lambda j, g: (j, 0, 0)),
                  pl.BlockSpec((1, HEAD_DIM), lambda j, g: (0, 0))],
        out_specs=[pl.BlockSpec((1, 1, nrow, HEAD_DIM), lambda j, g: (j, g, 0, 0)),
                   pl.BlockSpec((1, 1, HEAD_DIM, nrow), lambda j, g: (j, g, 0, 0))],
        out_shape=[jax.ShapeDtypeStruct((2, g_, nrow, HEAD_DIM), BF16),
                   jax.ShapeDtypeStruct((2, g_, HEAD_DIM, nrow), BF16)],
        compiler_params=_cparams(("parallel", "parallel")),
        name="nsa_compress",
    )(r, pe2, w1.astype(BF16), w2.astype(BF16), k_gain0.reshape(1, HEAD_DIM))


def _t5_bucket(dist):
    dist = jnp.maximum(dist, 0)
    max_exact = T5_BUCKETS // 2
    d_f = jnp.maximum(dist, 1).astype(F32)
    log_b = max_exact + (jnp.log(d_f / max_exact) / math.log(T5_MAX_DISTANCE / max_exact)
                         * (T5_BUCKETS - max_exact)).astype(jnp.int32)
    log_b = jnp.minimum(log_b, T5_BUCKETS - 1)
    return jnp.where(dist < max_exact, dist, log_b)


def _bias_table(tbl, dist, valid, shift):
    onehot = jax.nn.one_hot(_t5_bucket(dist).reshape(-1), T5_BUCKETS, dtype=F32)
    t = tbl - tbl[T5_BUCKETS - 1:T5_BUCKETS] if shift else tbl
    vals = jnp.einsum("nb,bh->hn", onehot, t, precision=lax.Precision.HIGHEST)
    vals = vals.reshape((tbl.shape[1],) + dist.shape) * LOG2E
    return jnp.where(valid[None], vals, NEG)


def _nsa_tables(rel_bias):
    tbl = rel_bias.astype(F32)
    r = jnp.arange(Q_BLOCK)
    jw = jnp.arange(WINDOW + Q_BLOCK)
    dw = WINDOW + r[None, :] - jw[:, None]
    tab_w = _bias_table(tbl, dw, (dw >= 0) & (dw < WINDOW), False)
    a = jnp.arange(16)
    cc = jnp.arange(2 * LANES)
    dc = (CMP_NEAR_SHIFT * Q_BLOCK + Q_BLOCK * a[:, None, None] + r[None, None, :]
          - CMP_STRIDE * cc[None, :, None] - (CMP_BLOCK - 1))
    tab_c = _bias_table(tbl, dc, dc >= 0, True)
    jj = jnp.arange(SEL_NEAR_TILES)
    col = jnp.arange(SEL_TILE)
    ds_ = r[None, None, :] - Q_BLOCK + SEL_TILE * (jj[:, None, None] + 1) - col[None, :, None]
    tab_s = _bias_table(tbl, ds_, ds_ >= 0, True)
    return tab_w, tab_c, tab_s


def _sel_to_cmp(nsel, nc_pad):
    c_start = jnp.arange(nc_pad) * CMP_STRIDE
    s_start = jnp.arange(nsel) * SEL_BLOCK
    overlap = jnp.clip(jnp.minimum(c_start[None, :] + CMP_BLOCK, s_start[:, None] + SEL_BLOCK)
                       - jnp.maximum(c_start[None, :], s_start[:, None]), 0)
    return (overlap.astype(F32) / CMP_BLOCK).astype(BF16)


def _heads_t(q_ref, h0, nh):
    cols = [jnp.transpose(q_ref[:, (h0 + h) * HEAD_DIM:(h0 + h + 1) * HEAD_DIM].astype(F32)) for h in range(nh)]
    return jnp.concatenate(cols, axis=1).astype(BF16)


def _softmax_keys(s):
    m = jnp.max(s, axis=0, keepdims=True)
    p = jnp.exp2(s - m)
    l = jnp.sum(p, axis=0, keepdims=True)
    return m, p, l


def _nsa_cw_body(q_ref, gt_ref, kc_ref, vct_ref, mselt_ref, tabc_ref, kw_ref, vwt_ref, tabw_ref,
                 ocwt_ref, negt_ref):
    i = pl.program_id(1)
    f = (i + 16 - CMP_NEAR_SHIFT) // 16 - 1
    kc = kc_ref[0, 0]
    vct = vct_ref[0, 0]
    nc = kc.shape[0]
    nsel = mselt_ref.shape[0]
    wlen = WINDOW + Q_BLOCK
    nslab = wlen // LANES
    start = pl.multiple_of(i * Q_BLOCK, Q_BLOCK)
    kw = kw_ref[0, pl.ds(start, wlen), :]
    vwt = jnp.concatenate([vwt_ref[0, i + c] for c in range(nslab)], axis=1)
    roww = lax.broadcasted_iota(jnp.int32, (wlen, LANES), 0)
    w_pad = roww < (WINDOW - Q_BLOCK * i)
    gt_t = jnp.transpose(gt_ref[...])
    psum = jnp.zeros((nc, Q_BLOCK), F32)
    hg = CW_HEADS
    for h0 in range(0, NSA_GROUP, hg):
        qt = _heads_t(q_ref, h0, hg)
        s = _dot(kc, qt)
        ta = jnp.concatenate([tabc_ref[h0 + h, 0, 0:LANES, :] for h in range(hg)], axis=1)
        tb = jnp.concatenate([tabc_ref[h0 + h, 0, LANES:2 * LANES, :] for h in range(hg)], axis=1)
        pieces = []
        for ch in range(nc // LANES):
            rest = jnp.where(ch > f + 1, NEG, 0.0)
            bias = jnp.where(ch == f, ta, jnp.where(ch == f + 1, tb, rest))
            pieces.append(s[ch * LANES:(ch + 1) * LANES] + bias)
        s = jnp.concatenate(pieces, axis=0)
        m, p, l = _softmax_keys(s)
        linv = jnp.where(m > 0.5 * NEG, 1.0 / l, 0.0)
        pn = p * linv
        for h in range(hg):
            psum = psum + pn[:, h * Q_BLOCK:(h + 1) * Q_BLOCK]
        oc = _dot(vct, pn.astype(BF16))
        tw = jnp.concatenate([jnp.where(w_pad, NEG, tabw_ref[h0 + h]) for h in range(hg)], axis=1)
        sw = _dot(kw, qt) + tw
        _, pw, lw = _softmax_keys(sw)
        ow = _dot(vwt, pw.astype(BF16)) * (1.0 / lw)
        for h in range(hg):
            hh = h0 + h
            cs = slice(h * Q_BLOCK, (h + 1) * Q_BLOCK)
            ocwt_ref[0, 0, hh * HEAD_DIM:(hh + 1) * HEAD_DIM, :] = (
                gt_t[hh:hh + 1] * oc[:, cs] + gt_t[16 + hh:17 + hh] * ow[:, cs])
    p_hi = psum.astype(BF16)
    p_lo = (psum - p_hi.astype(F32)).astype(BF16)
    mselt = mselt_ref[...]
    imp = _dot(mselt, p_hi) + _dot(mselt, p_lo)
    blk = lax.broadcasted_iota(jnp.int32, (nsel, Q_BLOCK), 0).astype(F32)
    qpos = lax.broadcasted_iota(jnp.int32, (nsel, Q_BLOCK), 1)
    cur = (i * (Q_BLOCK // SEL_BLOCK)).astype(F32) + jnp.where(qpos >= SEL_BLOCK, 1.0, 0.0)
    forced = jnp.logical_or(blk == cur, blk == 0.0)
    val = jnp.where(forced, FORCED_SCORE, jnp.where(blk <= cur, imp, -1.0))
    chosen = jnp.zeros((nsel, Q_BLOCK), F32)
    for _ in range(SEL_TOPK):
        mx = jnp.max(val, axis=0, keepdims=True)
        first = jnp.min(jnp.where(val == mx, blk, 1e4), axis=0, keepdims=True)
        hit = blk == first
        chosen = jnp.where(hit, 1.0, chosen)
        val = jnp.where(hit, -2.0, val)
    negt_ref[0, 0] = jnp.where(chosen > 0.0, 0.0, NEG).astype(BF16)


def _nsa_cw(q, gates, kc, vct, mselt, tab_c, kw, vwt, tab_w):
    s, hd = q.shape
    g_ = NSA_KV_HEADS
    nq = s // Q_BLOCK
    gw = NSA_GROUP * HEAD_DIM
    nc = kc.shape[2]
    nsel = mselt.shape[0]
    spad = kw.shape[1]
    nslab = vwt.shape[1]
    wlen = WINDOW + Q_BLOCK

    def var(i):
        return (i + 16 - CMP_NEAR_SHIFT) % 16

    return pl.pallas_call(
        _nsa_cw_body,
        grid=(g_, nq),
        in_specs=[
            pl.BlockSpec((Q_BLOCK, gw), lambda g, i: (i, g)),
            pl.BlockSpec((Q_BLOCK, LANES), lambda g, i: (i, g)),
            pl.BlockSpec((1, 1, nc, HEAD_DIM), lambda g, i: (0, g, 0, 0)),
            pl.BlockSpec((1, 1, HEAD_DIM, nc), lambda g, i: (1, g, 0, 0)),
            pl.BlockSpec((nsel, nc), lambda g, i: (0, 0)),
            pl.BlockSpec((NSA_GROUP, 1, 2 * LANES, Q_BLOCK), lambda g, i: (g, var(i), 0, 0)),
            pl.BlockSpec((1, spad, HEAD_DIM), lambda g, i: (g, 0, 0)),
            pl.BlockSpec((1, nslab, HEAD_DIM, LANES), lambda g, i: (g, 0, 0, 0)),
            pl.BlockSpec((NSA_GROUP, wlen, Q_BLOCK), lambda g, i: (g, 0, 0)),
        ],
        out_specs=[pl.BlockSpec((1, 1, gw, Q_BLOCK), lambda g, i: (g, i, 0, 0)),
                   pl.BlockSpec((1, 1, nsel, Q_BLOCK), lambda g, i: (g, i, 0, 0))],
        out_shape=[jax.ShapeDtypeStruct((g_, nq, gw, Q_BLOCK), F32),
                   jax.ShapeDtypeStruct((g_, nq, nsel, Q_BLOCK), BF16)],
        compiler_params=_cparams(("parallel", "arbitrary")),
        name="nsa_cmp_win",
    )(q, gates, kc, vct, mselt, tab_c, kw, vwt, tab_w)


def _nsa_sel_body(q_ref, gt_ref, ks_ref, vst_ref, negt_ref, tabs_ref, ocwt_ref, o_ref,
                  m_sc, l_sc, acc_sc):
    i = pl.program_id(1)
    ntile = i // (SEL_TILE // Q_BLOCK) + 1
    nh = NSA_GROUP
    spt = SEL_TILE // LANES
    qt = _heads_t(q_ref, 0, nh)
    negt = negt_ref[0, 0]
    nsel = negt.shape[0]
    bpt = SEL_TILE // SEL_BLOCK
    cm = (lax.broadcasted_iota(jnp.int32, (SEL_TILE, nsel), 1)
          - lax.broadcasted_iota(jnp.int32, (SEL_TILE, nsel), 0) // SEL_BLOCK)
    rowk = lax.broadcasted_iota(jnp.int32, (SEL_TILE, Q_BLOCK), 0)
    m_sc[...] = jnp.full(m_sc.shape, NEG, F32)
    l_sc[...] = jnp.zeros(l_sc.shape, F32)
    acc_sc[...] = jnp.zeros(acc_sc.shape, F32)

    def tile(jj, near):
        slab0 = (i + 1) - spt * jj
        row0 = pl.multiple_of(slab0 * LANES, LANES)
        kt = ks_ref[0, pl.ds(row0, SEL_TILE), :]
        vtt = jnp.concatenate([vst_ref[0, slab0 + c] for c in range(spt)], axis=1)
        blk0 = (Q_BLOCK // SEL_BLOCK) * (i + 1) - bpt * (jj + 1)
        expand = jnp.where(cm == blk0, 1.0, 0.0).astype(BF16)
        mk = _dot(expand, negt)
        mk = jnp.where(rowk + SEL_BLOCK * blk0 >= 0, mk, NEG)
        if near:
            bias = jnp.concatenate([tabs_ref[h, jj] + mk for h in range(nh)], axis=1)
        else:
            bias = jnp.tile(mk, (1, nh))
        sc = _dot(kt, qt) + bias
        m_old = m_sc[...]
        m_new = jnp.maximum(m_old, jnp.max(sc, axis=0, keepdims=True))
        alpha = jnp.exp2(m_old - m_new)
        p = jnp.exp2(sc - m_new)
        l_sc[...] = alpha * l_sc[...] + jnp.sum(p, axis=0, keepdims=True)
        acc_sc[...] = alpha * acc_sc[...] + _dot(vtt, p.astype(BF16))
        m_sc[...] = m_new

    for jj in range(SEL_NEAR_TILES):
        @pl.when(jj < ntile)
        def _():
            tile(jj, True)

    def far(jj, carry):
        tile(jj, False)
        return carry

    lax.fori_loop(SEL_NEAR_TILES, ntile, far, 0)
    o_t = acc_sc[...] * (1.0 / l_sc[...])
    gt_t = jnp.transpose(gt_ref[...])
    for h in range(nh):
        hs = slice(h * HEAD_DIM, (h + 1) * HEAD_DIM)
        oh = ocwt_ref[0, 0, hs, :] + gt_t[8 + h:9 + h] * o_t[:, h * Q_BLOCK:(h + 1) * Q_BLOCK]
        o_ref[:, hs] = jnp.transpose(oh).astype(BF16)


def _nsa_sel(q, gates, ks, vst, negt, tab_s, ocwt):
    s, hd = q.shape
    g_ = NSA_KV_HEADS
    nq = s // Q_BLOCK
    gw = NSA_GROUP * HEAD_DIM
    spad = ks.shape[1]
    nslab = vst.shape[1]
    nsel = negt.shape[2]
    return pl.pallas_call(
        _nsa_sel_body,
        grid=(g_, nq),
        in_specs=[
            pl.BlockSpec((Q_BLOCK, gw), lambda g, i: (i, g)),
            pl.BlockSpec((Q_BLOCK, LANES), lambda g, i: (i, g)),
            pl.BlockSpec((1, spad, HEAD_DIM), lambda g, i: (g, 0, 0)),
            pl.BlockSpec((1, nslab, HEAD_DIM, LANES), lambda g, i: (g, 0, 0, 0)),
            pl.BlockSpec((1, 1, nsel, Q_BLOCK), lambda g, i: (g, i, 0, 0)),
            pl.BlockSpec((NSA_GROUP, SEL_NEAR_TILES, SEL_TILE, Q_BLOCK), lambda g, i: (g, 0, 0, 0)),
            pl.BlockSpec((1, 1, gw, Q_BLOCK), lambda g, i: (g, i, 0, 0)),
        ],
        out_specs=pl.BlockSpec((Q_BLOCK, gw), lambda g, i: (i, g)),
        out_shape=jax.ShapeDtypeStruct((s, hd), BF16),
        scratch_shapes=[pltpu.VMEM((1, NSA_GROUP * Q_BLOCK), F32), pltpu.VMEM((1, NSA_GROUP * Q_BLOCK), F32),
                        pltpu.VMEM((HEAD_DIM, NSA_GROUP * Q_BLOCK), F32)],
        compiler_params=_cparams(("parallel", "arbitrary")),
        name="nsa_selected",
    )(q, gates, ks, vst, negt, tab_s, ocwt)


def _pad_keys(x, pad):
    xp = jnp.pad(x, ((0, 0), (pad, 0), (0, 0)))
    g_, sp, dh = xp.shape
    return xp, xp.reshape(g_, sp // LANES, LANES, dh).swapaxes(-1, -2)


def _nsa_mixer(h, hn, tables, w_in, cmp_pe, cmp_w1, cmp_w2, q_gain, k_gain, w_out):
    s, d = h.shape
    g_, hpg, dh = NSA_KV_HEADS, NSA_GROUP, HEAD_DIM
    hd = g_ * hpg * dh
    tab_w, tab_c, tab_s = tables
    w_in = w_in.astype(BF16)
    q = _nsa_q_proj(hn, w_in[:, :hd], q_gain)
    kv = _nsa_kv_proj(hn, w_in[:, hd:hd + 6 * g_ * dh], k_gain)
    wg = w_in[:, hd + 6 * g_ * dh:].reshape(d, g_, hpg, 3).transpose(0, 1, 3, 2).reshape(d, g_, 3 * hpg)
    wg = jnp.pad(wg, ((0, 0), (0, 0), (0, LANES - 3 * hpg))).reshape(d, g_ * LANES)
    gates = _nsa_gate_proj(hn, wg)
    kc, kct = _compress(kv, cmp_pe, cmp_w1, cmp_w2, k_gain[0])
    mselt = _sel_to_cmp(s // SEL_BLOCK, s // CMP_STRIDE)
    ks, _ = _pad_keys(kv[2 * g_:3 * g_], SEL_TILE)
    _, vst = _pad_keys(kv[3 * g_:4 * g_], SEL_TILE)
    kw, _ = _pad_keys(kv[4 * g_:5 * g_], WINDOW)
    _, vwt = _pad_keys(kv[5 * g_:6 * g_], WINDOW)
    ocwt, negt = _nsa_cw(q, gates, kc, kct, mselt, tab_c, kw, vwt, tab_w)
    o = _nsa_sel(q, gates, ks, vst, negt, tab_s, ocwt)
    return _matmul_resid(o, w_out.astype(BF16), h, name="nsa_out")


def _gelu_proj(hn, w, *, tm=512, tn=1024):
    m, kdim = hn.shape
    nn = w.shape[1]

    def epi(accs, e_refs, o_refs, n):
        o_refs[0][...] = jax.nn.gelu(accs[0])

    return _matmul(
        hn, [(w, _wspec(kdim, tn))], epi,
        [(jax.ShapeDtypeStruct((m, nn), F32), _ospec(tm, tn))],
        tm=tm, tn=tn, n_total=nn, name="sgu_in")[0]


def _sgu_mix_body(u_ref, v_ref, gain_ref, w_ref, bt_ref, o_ref):
    v = v_ref[...]
    ms = jnp.mean(v * v, axis=-1, keepdims=True)
    vn = (v * lax.rsqrt(ms + RMS_EPS) * gain_ref[...]).astype(BF16)
    t = w_ref.shape[1]
    causal = (lax.broadcasted_iota(jnp.int32, (t, t), 0) >= lax.broadcasted_iota(jnp.int32, (t, t), 1))
    bt = bt_ref[...]
    gd = vn.shape[1] // SG_GROUPS
    for g in range(SG_GROUPS):
        gs = slice(g * gd, (g + 1) * gd)
        w = jnp.where(causal, w_ref[g], 0.0).astype(BF16)
        mixed = _dot(w, vn[:, gs]) + bt[:, g:g + 1]
        o_ref[:, gs] = (u_ref[:, gs] * mixed).astype(BF16)


def _sgu_mix(uv, v_gain, w_s, b_s):
    s = uv.shape[0]
    wd = uv.shape[1] // 2
    t = SG_CHUNK
    return pl.pallas_call(
        _sgu_mix_body,
        grid=(s // t,),
        in_specs=[pl.BlockSpec((t, wd), lambda c: (c, 0)),
                  pl.BlockSpec((t, wd), lambda c: (c, 1)),
                  pl.BlockSpec((1, wd), lambda c: (0, 0)),
                  pl.BlockSpec((SG_GROUPS, t, t), lambda c: (0, 0, 0)),
                  pl.BlockSpec((t, SG_GROUPS), lambda c: (0, 0))],
        out_specs=pl.BlockSpec((t, wd), lambda c: (c, 0)),
        out_shape=jax.ShapeDtypeStruct((s, wd), BF16),
        compiler_params=_cparams(("parallel",)),
        name="sgu_mix",
    )(uv, uv, v_gain.reshape(1, wd), w_s, b_s.T)


def _sgu_mixer(h, hn, w_in, v_gain, w_s, b_s, w_out):
    uv = _gelu_proj(hn, w_in.astype(BF16))
    y = _sgu_mix(uv, v_gain, w_s, b_s)
    return _matmul_resid(y, w_out.astype(BF16), h, name="sgu_out")


def _gla_body(q_ref, k_ref, v_ref, r_ref, g1_ref, wg_ref, bg_ref, og_ref, o_ref, state_sc):
    c = pl.program_id(1)

    @pl.when(c == 0)
    def _():
        state_sc[...] = jnp.zeros_like(state_sc)

    ch = q_ref.shape[0]
    dk = q_ref.shape[1]
    x = _dot(g1_ref[...].astype(BF16), wg_ref[...]) + bg_ref[...]
    log_a = (jnp.minimum(x, 0.0) - jnp.log(1.0 + jnp.exp(-jnp.abs(x)))) * (1.0 / GLA_GATE_TEMP)
    tri = (lax.broadcasted_iota(jnp.int32, (ch, ch), 0)
           >= lax.broadcasted_iota(jnp.int32, (ch, ch), 1))
    tri_b = jnp.where(tri, 1.0, 0.0).astype(BF16)
    a_hi = log_a.astype(BF16)
    a_lo = (log_a - a_hi.astype(F32)).astype(BF16)
    b = _dot(tri_b, a_hi) + _dot(tri_b, a_lo)
    q = q_ref[...].astype(F32) * (dk ** -0.5)
    k = k_ref[...].astype(F32)
    v = v_ref[...]
    state = state_sc[...]
    o = _dot((q * jnp.exp(b)).astype(BF16), state.astype(BF16))
    nsub = ch // GLA_SUB
    ends = [jnp.broadcast_to(b[(jb + 1) * GLA_SUB - 1:(jb + 1) * GLA_SUB], (GLA_SUB, dk))
            for jb in range(nsub)]
    b_end = jnp.concatenate(ends, axis=0)
    b_start = jnp.concatenate([jnp.zeros((GLA_SUB, dk), F32)] + ends[:-1], axis=0)
    rblk = lax.broadcasted_iota(jnp.int32, (ch, ch), 0) // GLA_SUB
    cblk = lax.broadcasted_iota(jnp.int32, (ch, ch), 1) // GLA_SUB
    q_diag = (q * jnp.exp(b - b_start)).astype(BF16)
    k_diag = (k * jnp.exp(b_start - b)).astype(BF16)
    attn = jnp.where(jnp.logical_and(tri, rblk == cblk), _dot_nt(q_diag, k_diag), 0.0)
    k_hat = (k * jnp.exp(b_end - b)).astype(BF16)
    for jb in range(nsub - 1):
        q_hat = (q * jnp.exp(jnp.minimum(b - ends[jb][0:1], 0.0))).astype(BF16)
        attn = attn + jnp.where(jnp.logical_and(cblk == jb, rblk > jb), _dot_nt(q_hat, k_hat), 0.0)
    o = o + _dot(attn.astype(BF16), v)
    b_last = b[ch - 1:ch]
    k_dec = k * jnp.exp(b_last - b)
    k_dec_t = jnp.transpose(k_dec).astype(BF16)
    decay_t = jnp.transpose(jnp.broadcast_to(jnp.exp(b_last), (LANES, dk)))
    state_sc[...] = decay_t[:, 0:1] * state + _dot(k_dec_t, v)
    ms = jnp.mean(o * o, axis=-1, keepdims=True)
    on = o * lax.rsqrt(ms + RMS_EPS) * og_ref[...]
    r = r_ref[...].astype(F32)
    o_ref[...] = (on * _silu(r)).astype(BF16)


def _gla_mixer(h, hn, w_in, w_gate2, b_gate, o_gain, w_out):
    s, d = h.shape
    nh = GLA_HEADS
    dk = d // 2
    dv = d
    dkh, dvh = dk // nh, dv // nh
    w_in = w_in.astype(BF16)
    nmain = 2 * dk + 2 * dv

    def epi_bf16(accs, e_refs, o_refs, n):
        o_refs[0][...] = accs[0].astype(BF16)

    def epi_f32(accs, e_refs, o_refs, n):
        o_refs[0][...] = accs[0]

    tm, tn = 512, 1024
    proj = _matmul(hn, [(w_in, _wspec(d, tn))], epi_bf16,
                   [(jax.ShapeDtypeStruct((s, nmain), BF16), _ospec(tm, tn))],
                   tm=tm, tn=tn, n_total=nmain, name="gla_in")[0]
    wg1 = jnp.pad(w_in[:, nmain:], ((0, 0), (0, LANES - GLA_GATE_RANK)))
    g1 = _matmul(hn, [(wg1, _wspec(d, LANES))], epi_f32,
                 [(jax.ShapeDtypeStruct((s, LANES), F32), _ospec(tm, LANES))],
                 tm=tm, tn=LANES, n_total=LANES, name="gla_gate_in")[0]
    wg2 = jnp.pad(w_gate2.astype(BF16), ((0, LANES - GLA_GATE_RANK), (0, 0)))
    ch = GLA_CHUNK
    o = pl.pallas_call(
        _gla_body,
        grid=(nh, s // ch),
        in_specs=[
            pl.BlockSpec((ch, dkh), lambda hh, c: (c, hh)),
            pl.BlockSpec((ch, dkh), lambda hh, c: (c, nh + hh)),
            pl.BlockSpec((ch, dvh), lambda hh, c: (c, 2 * dk // dvh + hh)),
            pl.BlockSpec((ch, dvh), lambda hh, c: (c, (2 * dk + dv) // dvh + hh)),
            pl.BlockSpec((ch, LANES), lambda hh, c: (c, 0)),
            pl.BlockSpec((LANES, dkh), lambda hh, c: (0, hh)),
            pl.BlockSpec((1, dkh), lambda hh, c: (0, hh)),
            pl.BlockSpec((1, dvh), lambda hh, c: (0, 0)),
        ],
        out_specs=pl.BlockSpec((ch, dvh), lambda hh, c: (c, hh)),
        out_shape=jax.ShapeDtypeStruct((s, dv), BF16),
        scratch_shapes=[pltpu.VMEM((dkh, dvh), F32)],
        compiler_params=_cparams(("parallel", "arbitrary")),
        name="gla_scan",
    )(proj, proj, proj, proj, g1, wg2, b_gate.reshape(1, dk), o_gain.reshape(1, dvh))
    return _matmul_resid(o, w_out.astype(BF16), h, name="gla_out")


def _dense_ffn(h, hn, w_up, w_down):
    act = _swiglu_up(hn, w_up.astype(BF16))
    return _matmul_resid(act, w_down.astype(BF16), h, name="ffn_down")


def _moe_ffn(h, hn, gate, w_up, w_down):
    ne, ff, d = w_down.shape
    act = _moe_up_dense(hn, w_up.astype(BF16), gate)
    return _matmul_resid(act, w_down.astype(BF16).reshape(ne * ff, d), h, name="moe_down")


def kernel(x, rel_bias, norm_gain, nsa_w_in, nsa_cmp_pe, nsa_cmp_w1, nsa_cmp_w2, nsa_q_gain, nsa_k_gain, nsa_w_out, sg_w_in, sg_v_gain, sg_w_s, sg_b_s, sg_w_out, gla_w_in, gla_w_gate2, gla_b_gate, gla_o_gain, gla_w_out, ffn_w_up, ffn_w_down, moe_router, moe_w_up, moe_w_down):
    bsz, s, d = x.shape
    tables = _nsa_tables(rel_bias)
    outs = []
    for bi in range(bsz):
        h = x[bi]
        for i in range(DEPTH):
            mixer = i % N_MIXERS
            j = i // N_MIXERS
            hn = _rmsnorm(h, norm_gain[i, 0])
            if mixer == 0:
                h = _nsa_mixer(h, hn, tables, nsa_w_in[j], nsa_cmp_pe[j], nsa_cmp_w1[j], nsa_cmp_w2[j],
                               nsa_q_gain[j], nsa_k_gain[j], nsa_w_out[j])
            elif mixer == 1:
                h = _sgu_mixer(h, hn, sg_w_in[j], sg_v_gain[j], sg_w_s[j], sg_b_s[j], sg_w_out[j])
            else:
                h = _gla_mixer(h, hn, gla_w_in[j], gla_w_gate2[j], gla_b_gate[j], gla_o_gain[j], gla_w_out[j])
            f = i // 2
            if i % 2 == 0:
                hn = _rmsnorm(h, norm_gain[i, 1])
                h = _dense_ffn(h, hn, ffn_w_up[f], ffn_w_down[f])
            else:
                hn, gate = _rmsnorm_router(h, norm_gain[i, 1], moe_router[f])
                h = _moe_ffn(h, hn, gate, moe_w_up[f], moe_w_down[f])
        outs.append(h)
    return jnp.stack(outs, axis=0)
```

```python
import math

import jax
import jax.numpy as jnp
from jax import lax
from jax.experimental import pallas as pl
from jax.experimental.pallas import tpu as pltpu

F32 = jnp.float32
BF16 = jnp.bfloat16

DEPTH = 4
N_MIXERS = 3
RMS_EPS = 1e-6
NEG = -1e30
HEAD_DIM = 128
NSA_KV_HEADS = 4
NSA_GROUP = 8
CMP_BLOCK = 32
CMP_STRIDE = 16
SEL_BLOCK = 64
SEL_TOPK = 16
WINDOW = 512
Q_BLOCK = 128
FORCED_SCORE = 1e4
T5_BUCKETS = 32
T5_MAX_DISTANCE = 2048
SG_CHUNK = 128
SG_GROUPS = 32
GLA_HEADS = 4
GLA_GATE_RANK = 16
GLA_GATE_TEMP = 16.0
GLA_CHUNK = 64
GLA_SUB = 16
MOE_EXPERTS = 8

LANES = 128
SEL_TILE = 512
SEL_NEAR_TILES = 5
CMP_NEAR_SHIFT = 12
CW_HEADS = 4
VMEM_MB = 56
LOG2E = 1.4426950408889634
SAFE_LOGIT = 60.0
SEL_PAD_BLOCKS = 8


def _cparams(sem, vmem_mb=VMEM_MB):
    return pltpu.CompilerParams(dimension_semantics=sem, vmem_limit_bytes=vmem_mb * 2**20)


def _dot(a, b):
    return jnp.dot(a, b, preferred_element_type=F32)


def _dot_nt(a, b):
    return lax.dot_general(a, b, (((1,), (1,)), ((), ())), preferred_element_type=F32)


def _rmsnorm_body(x_ref, g_ref, o_ref):
    x = x_ref[...]
    ms = jnp.mean(x * x, axis=-1, keepdims=True)
    o_ref[...] = (x * lax.rsqrt(ms + RMS_EPS) * g_ref[...]).astype(o_ref.dtype)


def _rmsnorm(x, gain, tm=256):
    m, d = x.shape
    return pl.pallas_call(
        _rmsnorm_body,
        grid=(m // tm,),
        in_specs=[pl.BlockSpec((tm, d), lambda i: (i, 0)), pl.BlockSpec((1, d), lambda i: (0, 0))],
        out_specs=pl.BlockSpec((tm, d), lambda i: (i, 0)),
        out_shape=jax.ShapeDtypeStruct((m, d), BF16),
        compiler_params=_cparams(("parallel",)),
        name="rmsnorm",
    )(x, gain.reshape(1, d))


def _rmsnorm_router_body(x_ref, g_ref, r_ref, o_ref, gate_ref):
    x = x_ref[...]
    ms = jnp.mean(x * x, axis=-1, keepdims=True)
    hn = (x * lax.rsqrt(ms + RMS_EPS) * g_ref[...]).astype(BF16)
    o_ref[...] = hn
    logits = _dot(hn, r_ref[...])
    lane = lax.broadcasted_iota(jnp.int32, logits.shape, 1).astype(F32)
    logits = jnp.where(lane < MOE_EXPERTS, logits, NEG)
    v1 = jnp.max(logits, axis=-1, keepdims=True)
    i1 = jnp.min(jnp.where(logits == v1, lane, 1e3), axis=-1, keepdims=True)
    rest = jnp.where(lane == i1, NEG, logits)
    v2 = jnp.max(rest, axis=-1, keepdims=True)
    i2 = jnp.min(jnp.where(rest == v2, lane, 1e3), axis=-1, keepdims=True)
    e2 = jnp.exp(v2 - v1)
    den = 1.0 + e2
    gate_ref[...] = jnp.where(lane == i1, 1.0 / den, 0.0) + jnp.where(lane == i2, e2 / den, 0.0)


def _rmsnorm_router(x, gain, router, tm=256):
    m, d = x.shape
    rpad = jnp.zeros((d, LANES), BF16).at[:, :MOE_EXPERTS].set(router.astype(BF16))
    return pl.pallas_call(
        _rmsnorm_router_body,
        grid=(m // tm,),
        in_specs=[pl.BlockSpec((tm, d), lambda i: (i, 0)), pl.BlockSpec((1, d), lambda i: (0, 0)),
                  pl.BlockSpec((d, LANES), lambda i: (0, 0))],
        out_specs=[pl.BlockSpec((tm, d), lambda i: (i, 0)), pl.BlockSpec((tm, LANES), lambda i: (i, 0))],
        out_shape=[jax.ShapeDtypeStruct((m, d), BF16), jax.ShapeDtypeStruct((m, LANES), F32)],
        compiler_params=_cparams(("parallel",)),
        name="rmsnorm_router",
    )(x, gain.reshape(1, d), rpad)


def _matmul(x, w_list, epilogue, out_list, *, tm, tn, n_total, tk=None, extras=(), name="matmul"):
    m, kdim = x.shape
    tk = kdim if tk is None else tk
    nk = kdim // tk
    nw, ne, no = len(w_list), len(extras), len(out_list)

    def body(*refs):
        x_ref = refs[0]
        w_refs = refs[1:1 + nw]
        e_refs = refs[1 + nw:1 + nw + ne]
        o_refs = refs[1 + nw + ne:1 + nw + ne + no]
        acc_refs = refs[1 + nw + ne + no:]
        n = pl.program_id(0)
        if nk == 1:
            xv = x_ref[...]
            epilogue([_dot(xv, w[...]) for w in w_refs], e_refs, o_refs, n)
        else:
            k = pl.program_id(2)

            @pl.when(k == 0)
            def _():
                for a in acc_refs:
                    a[...] = jnp.zeros_like(a)

            xv = x_ref[...]
            for a, w in zip(acc_refs, w_refs):
                a[...] += _dot(xv, w[...])

            @pl.when(k == nk - 1)
            def _():
                epilogue([a[...] for a in acc_refs], e_refs, o_refs, n)

    in_specs = [pl.BlockSpec((tm, tk), lambda n, mi, k: (mi, k))]
    in_specs += [s for _, s in w_list] + [s for _, s in extras]
    scratch = [] if nk == 1 else [pltpu.VMEM((tm, tn), F32) for _ in range(nw)]
    return pl.pallas_call(
        body,
        grid=(n_total // tn, m // tm, nk),
        in_specs=in_specs,
        out_specs=[s for _, s in out_list],
        out_shape=[s for s, _ in out_list],
        scratch_shapes=scratch,
        compiler_params=_cparams(("parallel", "parallel", "arbitrary")),
        name=name,
    )(x, *[a for a, _ in w_list], *[a for a, _ in extras])


def _wspec(tk, tn, off=0):
    return pl.BlockSpec((tk, tn), lambda n, mi, k: (k, n + off))


def _ospec(tm, tn):
    return pl.BlockSpec((tm, tn), lambda n, mi, k: (mi, n))


def _epi_resid(accs, e_refs, o_refs, n):
    o_refs[0][...] = e_refs[0][...] + accs[0]


def _matmul_resid(x, w, resid, *, tm=512, tn=1024, tk=None, name="matmul_resid"):
    m, kdim = x.shape
    nn = w.shape[1]
    tk = min(kdim, 4096) if tk is None else tk
    return _matmul(
        x, [(w, _wspec(tk, tn))], _epi_resid,
        [(jax.ShapeDtypeStruct((m, nn), F32), _ospec(tm, tn))],
        tm=tm, tn=tn, tk=tk, n_total=nn, extras=[(resid, _ospec(tm, tn))], name=name)[0]


def _silu(a):
    return a * (1.0 / (1.0 + jnp.exp(-a)))


def _epi_swiglu(accs, e_refs, o_refs, n):
    a, b = accs
    o_refs[0][...] = (_silu(a) * b).astype(BF16)


def _swiglu_up(x, w_up, *, tm=512, tn=512, name="swiglu_up"):
    m, kdim = x.shape
    ff = w_up.shape[1] // 2
    return _matmul(
        x, [(w_up, _wspec(kdim, tn)), (w_up, _wspec(kdim, tn, ff // tn))], _epi_swiglu,
        [(jax.ShapeDtypeStruct((m, ff), BF16), _ospec(tm, tn))],
        tm=tm, tn=tn, n_total=ff, name=name)[0]


def _moe_up_dense(x, w_up, gate, *, tm=512, tn=512):
    m, kdim = x.shape
    ne = w_up.shape[0]
    ff = w_up.shape[2] // 2
    npe = ff // tn

    def epi(accs, e_refs, o_refs, n):
        a, b = accs
        g = e_refs[0][...]
        lane = lax.broadcasted_iota(jnp.int32, g.shape, 1)
        ge = jnp.sum(jnp.where(lane == n // npe, g, 0.0), axis=-1, keepdims=True)
        o_refs[0][...] = (ge * (_silu(a) * b)).astype(BF16)

    wa = pl.BlockSpec((None, kdim, tn), lambda n, mi, k: (n // npe, k, n % npe))
    wb = pl.BlockSpec((None, kdim, tn), lambda n, mi, k: (n // npe, k, n % npe + npe))
    return _matmul(
        x, [(w_up, wa), (w_up, wb)], epi,
        [(jax.ShapeDtypeStruct((m, ne * ff), BF16), _ospec(tm, tn))],
        tm=tm, tn=tn, n_total=ne * ff,
        extras=[(gate, pl.BlockSpec((tm, LANES), lambda n, mi, k: (mi, 0)))], name="moe_up")[0]


def _nsa_q_proj(hn, wq, q_gain, *, tm=512, tn=1024):
    m, kdim = hn.shape
    nn = wq.shape[1]
    scale = HEAD_DIM ** -0.5 * LOG2E

    def epi(accs, e_refs, o_refs, n):
        acc = accs[0]
        gain = e_refs[0][...]
        for c in range(tn // HEAD_DIM):
            a = acc[:, c * HEAD_DIM:(c + 1) * HEAD_DIM]
            ms = jnp.mean(a * a, axis=-1, keepdims=True)
            o_refs[0][:, c * HEAD_DIM:(c + 1) * HEAD_DIM] = (
                a * lax.rsqrt(ms + RMS_EPS) * gain * scale).astype(BF16)

    return _matmul(
        hn, [(wq, _wspec(kdim, tn))], epi,
        [(jax.ShapeDtypeStruct((m, nn), BF16), _ospec(tm, tn))],
        tm=tm, tn=tn, n_total=nn,
        extras=[(q_gain.reshape(1, HEAD_DIM), pl.BlockSpec((1, HEAD_DIM), lambda n, mi, k: (0, 0)))],
        name="nsa_q_proj")[0]


def _nsa_kv_proj(hn, wkv, k_gain, *, tm=512):
    m, kdim = hn.shape
    g_ = NSA_KV_HEADS
    tn = g_ * HEAD_DIM
    gains = jnp.ones((6, 1, HEAD_DIM), F32).at[2, 0].set(k_gain[1]).at[4, 0].set(k_gain[2])

    def epi(accs, e_refs, o_refs, n):
        acc = accs[0]
        kg = e_refs[0][0]
        do_norm = jnp.logical_or(n == 2, n == 4)
        for g in range(g_):
            a = acc[:, g * HEAD_DIM:(g + 1) * HEAD_DIM]
            ms = jnp.mean(a * a, axis=-1, keepdims=True)
            an = a * lax.rsqrt(ms + RMS_EPS) * kg
            o_refs[0][g] = jnp.where(do_norm, an, a).astype(BF16)

    return _matmul(
        hn, [(wkv, _wspec(kdim, tn))], epi,
        [(jax.ShapeDtypeStruct((6 * g_, m, HEAD_DIM), BF16),
          pl.BlockSpec((g_, tm, HEAD_DIM), lambda n, mi, k: (n, mi, 0)))],
        tm=tm, tn=tn, n_total=6 * tn,
        extras=[(gains, pl.BlockSpec((1, 1, HEAD_DIM), lambda n, mi, k: (n, 0, 0)))],
        name="nsa_kv_proj")[0]


def _nsa_gate_proj(hn, wg, *, tm=512):
    m, kdim = hn.shape
    nn = wg.shape[1]

    def epi(accs, e_refs, o_refs, n):
        o_refs[0][...] = 1.0 / (1.0 + jnp.exp(-accs[0]))

    return _matmul(
        hn, [(wg, _wspec(kdim, nn))], epi,
        [(jax.ShapeDtypeStruct((m, nn), F32), _ospec(tm, nn))],
        tm=tm, tn=nn, n_total=nn, name="nsa_gate_proj")[0]


def _compress_body(r_ref, pe_ref, w1_ref, w2_ref, kg_ref, o_ref, ot_ref):
    j = pl.program_id(0)
    r = r_ref[0].astype(F32)
    pe = pe_ref[0]
    half = r.shape[1]
    top = (r + pe[0:1]).astype(BF16)
    bot = (r + pe[1:2]).astype(BF16)
    a = _dot(top, w1_ref[0, 0:half, :])
    b = _dot(bot, w1_ref[0, half:2 * half, :])
    nrow = r.shape[0]
    hid = a + pltpu.roll(b, nrow - 1, 0)
    y = _dot(jax.nn.gelu(hid).astype(BF16), w2_ref[0])
    ms = jnp.mean(y * y, axis=-1, keepdims=True)
    yn = y * lax.rsqrt(ms + RMS_EPS) * kg_ref[...]
    out = jnp.where(j == 0, yn, y)
    o_ref[0, 0] = out.astype(BF16)
    ot_ref[0, 0] = jnp.transpose(out).astype(BF16)


def _compress(kv, pe, w1, w2, k_gain0):
    g_ = NSA_KV_HEADS
    s = kv.shape[1]
    nrow = s // CMP_STRIDE
    wid = CMP_STRIDE * HEAD_DIM
    r = kv.reshape(6 * g_, nrow, wid)
    pe2 = pe.reshape(2, 2, wid)
    return pl.pallas_call(
        _compress_body,
        grid=(2, g_),
        in_specs=[pl.BlockSpec((1, nrow, wid), lambda j, g: (j * g_ + g, 0, 0)),
                  pl.BlockSpec((1, 2, wid), lambda j, g: (j, 0, 0)),
                  pl.BlockSpec((1, 2 * wid, HEAD_DIM), lambda j, g: (j, 0, 0)),
                  pl.BlockSpec((1, HEAD_DIM, HEAD_DIM), lambda j, g: (j, 0, 0)),
                  pl.BlockSpec((1, HEAD_DIM), lambda j, g: (0, 0))],
        out_specs=[pl.BlockSpec((1, 1, nrow, HEAD_DIM), lambda j, g: (j, g, 0, 0)),
                   pl.BlockSpec((1, 1, HEAD_DIM, nrow), lambda j, g: (j, g, 0, 0))],
        out_shape=[jax.ShapeDtypeStruct((2, g_, nrow, HEAD_DIM), BF16),
                   jax.ShapeDtypeStruct((2, g_, HEAD_DIM, nrow), BF16)],
        compiler_params=_cparams(("parallel", "parallel")),
        name="nsa_compress",
    )(r, pe2, w1.astype(BF16), w2.astype(BF16), k_gain0.reshape(1, HEAD_DIM))


def _t5_bucket(dist):
    dist = jnp.maximum(dist, 0)
    max_exact = T5_BUCKETS // 2
    d_f = jnp.maximum(dist, 1).astype(F32)
    log_b = max_exact + (jnp.log(d_f / max_exact) / math.log(T5_MAX_DISTANCE / max_exact)
                         * (T5_BUCKETS - max_exact)).astype(jnp.int32)
    log_b = jnp.minimum(log_b, T5_BUCKETS - 1)
    return jnp.where(dist < max_exact, dist, log_b)


def _bias_table(tbl, dist, valid, shift):
    onehot = jax.nn.one_hot(_t5_bucket(dist).reshape(-1), T5_BUCKETS, dtype=F32)
    t = tbl - tbl[T5_BUCKETS - 1:T5_BUCKETS] if shift else tbl
    vals = jnp.einsum("nb,bh->hn", onehot, t, precision=lax.Precision.HIGHEST)
    vals = vals.reshape((tbl.shape[1],) + dist.shape) * LOG2E
    return jnp.where(valid[None], vals, NEG)


def _nsa_tables(rel_bias):
    tbl = rel_bias.astype(F32)
    r = jnp.arange(Q_BLOCK)
    jw = jnp.arange(WINDOW + Q_BLOCK)
    dw = WINDOW + r[None, :] - jw[:, None]
    tab_w = _bias_table(tbl, dw, (dw >= 0) & (dw < WINDOW), False)
    a = jnp.arange(16)
    cc = jnp.arange(2 * LANES)
    dc = (CMP_NEAR_SHIFT * Q_BLOCK + Q_BLOCK * a[:, None, None] + r[None, None, :]
          - CMP_STRIDE * cc[None, :, None] - (CMP_BLOCK - 1))
    tab_c = _bias_table(tbl, dc, dc >= 0, True)
    jj = jnp.arange(SEL_NEAR_TILES)
    col = jnp.arange(SEL_TILE)
    ds_ = r[None, None, :] - Q_BLOCK + SEL_TILE * (jj[:, None, None] + 1) - col[None, :, None]
    tab_s = _bias_table(tbl, ds_, ds_ >= 0, True)
    return tab_w, tab_c, tab_s, tbl


def _sel_to_cmp(nsel, nc_pad):
    c_start = jnp.arange(nc_pad) * CMP_STRIDE
    s_start = jnp.arange(nsel) * SEL_BLOCK
    overlap = jnp.clip(jnp.minimum(c_start[None, :] + CMP_BLOCK, s_start[:, None] + SEL_BLOCK)
                       - jnp.maximum(c_start[None, :], s_start[:, None]), 0)
    return (overlap.astype(F32) / CMP_BLOCK).astype(BF16)


def _heads_t(q_ref, h0, nh):
    cols = [jnp.transpose(q_ref[:, (h0 + h) * HEAD_DIM:(h0 + h + 1) * HEAD_DIM].astype(F32)) for h in range(nh)]
    return jnp.concatenate(cols, axis=1).astype(BF16)


def _softmax_keys(s, bounded):
    if bounded:
        p = jnp.exp2(s)
        l = jnp.sum(p, axis=0, keepdims=True)
        return p, jnp.where(l > 0.0, 1.0 / l, 0.0)
    m = jnp.max(s, axis=0, keepdims=True)
    p = jnp.exp2(s - m)
    l = jnp.sum(p, axis=0, keepdims=True)
    return p, jnp.where(m > 0.5 * NEG, 1.0 / l, 0.0)


def _logit_bound(rel_bias, q_gain, k_gain):
    qk = HEAD_DIM * (HEAD_DIM ** -0.5 * LOG2E) * jnp.max(jnp.abs(q_gain)) * jnp.max(jnp.abs(k_gain))
    return (1.02 * qk + 2.0 * LOG2E * jnp.max(jnp.abs(rel_bias))).astype(F32).reshape(1)


def _nsa_cw_body(bnd_ref, q_ref, gt_ref, kc_ref, vct_ref, mselt_ref, tabc_ref, kw_ref, vwt_ref, tabw_ref,
                 ocwt_ref, negt_ref):
    bounded = bnd_ref[0] < SAFE_LOGIT
    i = pl.program_id(1)

    @pl.when(bounded)
    def _():
        _nsa_cw_branches(True, i, q_ref, gt_ref, kc_ref, vct_ref, mselt_ref, tabc_ref, kw_ref, vwt_ref, tabw_ref,
                         ocwt_ref, negt_ref)

    @pl.when(jnp.logical_not(bounded))
    def _():
        _nsa_cw_branches(False, i, q_ref, gt_ref, kc_ref, vct_ref, mselt_ref, tabc_ref, kw_ref, vwt_ref, tabw_ref,
                         ocwt_ref, negt_ref)


def _nsa_cw_branches(bounded, i, q_ref, gt_ref, kc_ref, vct_ref, mselt_ref, tabc_ref, kw_ref, vwt_ref, tabw_ref,
                     ocwt_ref, negt_ref):
    f = (i + 16 - CMP_NEAR_SHIFT) // 16 - 1
    kc = kc_ref[0, 0]
    vct = vct_ref[0, 0]
    nc = kc.shape[0]
    nsel = mselt_ref.shape[0]
    wlen = WINDOW + Q_BLOCK
    nslab = wlen // LANES
    start = pl.multiple_of(i * Q_BLOCK, Q_BLOCK)
    kw = kw_ref[0, pl.ds(start, wlen), :]
    vwt = jnp.concatenate([vwt_ref[0, i + c] for c in range(nslab)], axis=1)
    roww = lax.broadcasted_iota(jnp.int32, (wlen, LANES), 0)
    w_pad = roww < (WINDOW - Q_BLOCK * i)
    gt_t = jnp.transpose(gt_ref[...])
    psum = jnp.zeros((nc, Q_BLOCK), F32)
    hg = CW_HEADS
    for h0 in range(0, NSA_GROUP, hg):
        qt = _heads_t(q_ref, h0, hg)
        s = _dot(kc, qt)
        ta = jnp.concatenate([tabc_ref[h0 + h, 0, 0:LANES, :] for h in range(hg)], axis=1)
        tb = jnp.concatenate([tabc_ref[h0 + h, 0, LANES:2 * LANES, :] for h in range(hg)], axis=1)
        pieces = []
        for ch in range(nc // LANES):
            rest = jnp.where(ch > f + 1, NEG, 0.0)
            bias = jnp.where(ch == f, ta, jnp.where(ch == f + 1, tb, rest))
            pieces.append(s[ch * LANES:(ch + 1) * LANES] + bias)
        s = jnp.concatenate(pieces, axis=0)
        p, linv = _softmax_keys(s, bounded)
        pn = p * linv
        for h in range(hg):
            psum = psum + pn[:, h * Q_BLOCK:(h + 1) * Q_BLOCK]
        oc = _dot(vct, pn.astype(BF16))
        tw = jnp.concatenate([jnp.where(w_pad, NEG, tabw_ref[h0 + h]) for h in range(hg)], axis=1)
        sw = _dot(kw, qt) + tw
        pw, lwinv = _softmax_keys(sw, bounded)
        ow = _dot(vwt, pw.astype(BF16)) * lwinv
        for h in range(hg):
            hh = h0 + h
            cs = slice(h * Q_BLOCK, (h + 1) * Q_BLOCK)
            ocwt_ref[0, 0, hh * HEAD_DIM:(hh + 1) * HEAD_DIM, :] = (
                gt_t[hh:hh + 1] * oc[:, cs] + gt_t[16 + hh:17 + hh] * ow[:, cs])
    p_hi = psum.astype(BF16)
    p_lo = (psum - p_hi.astype(F32)).astype(BF16)
    mselt = mselt_ref[...]
    imp = _dot(mselt, p_hi) + _dot(mselt, p_lo)
    blk = lax.broadcasted_iota(jnp.int32, (nsel, Q_BLOCK), 0).astype(F32)
    qpos = lax.broadcasted_iota(jnp.int32, (nsel, Q_BLOCK), 1)
    cur = (i * (Q_BLOCK // SEL_BLOCK)).astype(F32) + jnp.where(qpos >= SEL_BLOCK, 1.0, 0.0)
    forced = jnp.logical_or(blk == cur, blk == 0.0)
    val = jnp.where(forced, FORCED_SCORE, jnp.where(blk <= cur, imp, -1.0))
    chosen = jnp.zeros((nsel, Q_BLOCK), F32)
    for _ in range(SEL_TOPK):
        mx = jnp.max(val, axis=0, keepdims=True)
        first = jnp.min(jnp.where(val == mx, blk, 1e4), axis=0, keepdims=True)
        hit = blk == first
        chosen = jnp.where(hit, 1.0, chosen)
        val = jnp.where(hit, -2.0, val)
    negt_ref[0, 0, 0:SEL_PAD_BLOCKS] = jnp.full((SEL_PAD_BLOCKS, Q_BLOCK), NEG, F32)
    negt_ref[0, 0, SEL_PAD_BLOCKS:SEL_PAD_BLOCKS + nsel] = jnp.where(chosen > 0.0, 0.0, NEG)


def _nsa_cw(bound, q, gates, kc, vct, mselt, tab_c, kw, vwt, tab_w):
    s, hd = q.shape
    g_ = NSA_KV_HEADS
    nq = s // Q_BLOCK
    gw = NSA_GROUP * HEAD_DIM
    nc = kc.shape[2]
    nsel = mselt.shape[0]
    spad = kw.shape[1]
    nslab = vwt.shape[1]
    wlen = WINDOW + Q_BLOCK

    def var(i):
        return (i + 16 - CMP_NEAR_SHIFT) % 16

    return pl.pallas_call(
        _nsa_cw_body,
        grid=(g_, nq),
        in_specs=[
            pl.BlockSpec(memory_space=pltpu.SMEM),
            pl.BlockSpec((Q_BLOCK, gw), lambda g, i: (i, g)),
            pl.BlockSpec((Q_BLOCK, LANES), lambda g, i: (i, g)),
            pl.BlockSpec((1, 1, nc, HEAD_DIM), lambda g, i: (0, g, 0, 0)),
            pl.BlockSpec((1, 1, HEAD_DIM, nc), lambda g, i: (1, g, 0, 0)),
            pl.BlockSpec((nsel, nc), lambda g, i: (0, 0)),
            pl.BlockSpec((NSA_GROUP, 1, 2 * LANES, Q_BLOCK), lambda g, i: (g, var(i), 0, 0)),
            pl.BlockSpec((1, spad, HEAD_DIM), lambda g, i: (g, 0, 0)),
            pl.BlockSpec((1, nslab, HEAD_DIM, LANES), lambda g, i: (g, 0, 0, 0)),
            pl.BlockSpec((NSA_GROUP, wlen, Q_BLOCK), lambda g, i: (g, 0, 0)),
        ],
        out_specs=[pl.BlockSpec((1, 1, gw, Q_BLOCK), lambda g, i: (g, i, 0, 0)),
                   pl.BlockSpec((1, 1, SEL_PAD_BLOCKS + nsel, Q_BLOCK), lambda g, i: (g, i, 0, 0))],
        out_shape=[jax.ShapeDtypeStruct((g_, nq, gw, Q_BLOCK), F32),
                   jax.ShapeDtypeStruct((g_, nq, SEL_PAD_BLOCKS + nsel, Q_BLOCK), F32)],
        compiler_params=_cparams(("parallel", "arbitrary")),
        name="nsa_cmp_win",
    )(bound, q, gates, kc, vct, mselt, tab_c, kw, vwt, tab_w)


def _nsa_sel_body(bnd_ref, q_ref, gt_ref, ks_ref, vst_ref, negp_ref, tabs_ref, ocwt_ref, o_ref,
                  m_sc, l_sc, acc_sc):
    i = pl.program_id(1)
    ntile = i // (SEL_TILE // Q_BLOCK) + 1
    nh = NSA_GROUP
    spt = SEL_TILE // LANES
    ppt = SEL_TILE // (2 * SEL_BLOCK)
    qt = _heads_t(q_ref, 0, nh)

    def tile(jj, near, bounded):
        slab0 = (i + 1) - spt * jj
        row0 = pl.multiple_of(slab0 * LANES, LANES)
        kt = ks_ref[0, pl.ds(row0, SEL_TILE), :]
        vtt = jnp.concatenate([vst_ref[0, slab0 + c] for c in range(spt)], axis=1)
        rows = []
        for c in range(ppt):
            pair = negp_ref[0, 0, slab0 + c]
            rows += [jnp.broadcast_to(pair[r:r + 1], (SEL_BLOCK, Q_BLOCK)) for r in range(2)]
        mk = jnp.concatenate(rows, axis=0)
        if near:
            bias = jnp.concatenate([tabs_ref[h, jj] + mk for h in range(nh)], axis=1)
        else:
            bias = jnp.tile(mk, (1, nh))
        sc = _dot(kt, qt) + bias
        if bounded:
            p = jnp.exp2(sc)
            l_sc[...] = l_sc[...] + jnp.sum(p, axis=0, keepdims=True)
            acc_sc[...] = acc_sc[...] + _dot(vtt, p.astype(BF16))
        else:
            m_old = m_sc[...]
            m_new = jnp.maximum(m_old, jnp.max(sc, axis=0, keepdims=True))
            alpha = jnp.exp2(m_old - m_new)
            p = jnp.exp2(sc - m_new)
            l_sc[...] = alpha * l_sc[...] + jnp.sum(p, axis=0, keepdims=True)
            acc_sc[...] = alpha * acc_sc[...] + _dot(vtt, p.astype(BF16))
            m_sc[...] = m_new

    def run(bounded):
        m_sc[...] = jnp.full(m_sc.shape, NEG, F32)
        l_sc[...] = jnp.zeros(l_sc.shape, F32)
        acc_sc[...] = jnp.zeros(acc_sc.shape, F32)
        for jj in range(SEL_NEAR_TILES):
            @pl.when(jj < ntile)
            def _():
                tile(jj, True, bounded)

        def far(jj, carry):
            tile(jj, False, bounded)
            return carry

        lax.fori_loop(SEL_NEAR_TILES, ntile, far, 0)

    bounded = bnd_ref[0] < SAFE_LOGIT

    @pl.when(bounded)
    def _():
        run(True)

    @pl.when(jnp.logical_not(bounded))
    def _():
        run(False)

    o_t = acc_sc[...] * (1.0 / l_sc[...])
    gt_t = jnp.transpose(gt_ref[...])
    for h in range(nh):
        hs = slice(h * HEAD_DIM, (h + 1) * HEAD_DIM)
        oh = ocwt_ref[0, 0, hs, :] + gt_t[8 + h:9 + h] * o_t[:, h * Q_BLOCK:(h + 1) * Q_BLOCK]
        o_ref[:, hs] = jnp.transpose(oh).astype(BF16)


def _nsa_sel(bound, q, gates, ks, vst, negp, tab_s, ocwt):
    s, hd = q.shape
    g_ = NSA_KV_HEADS
    nq = s // Q_BLOCK
    gw = NSA_GROUP * HEAD_DIM
    spad = ks.shape[1]
    nslab = vst.shape[1]
    npair = negp.shape[2]
    return pl.pallas_call(
        _nsa_sel_body,
        grid=(g_, nq),
        in_specs=[
            pl.BlockSpec(memory_space=pltpu.SMEM),
            pl.BlockSpec((Q_BLOCK, gw), lambda g, i: (i, g)),
            pl.BlockSpec((Q_BLOCK, LANES), lambda g, i: (i, g)),
            pl.BlockSpec((1, spad, HEAD_DIM), lambda g, i: (g, 0, 0)),
            pl.BlockSpec((1, nslab, HEAD_DIM, LANES), lambda g, i: (g, 0, 0, 0)),
            pl.BlockSpec((1, 1, npair, 2, Q_BLOCK), lambda g, i: (g, i, 0, 0, 0)),
            pl.BlockSpec((NSA_GROUP, SEL_NEAR_TILES, SEL_TILE, Q_BLOCK), lambda g, i: (g, 0, 0, 0)),
            pl.BlockSpec((1, 1, gw, Q_BLOCK), lambda g, i: (g, i, 0, 0)),
        ],
        out_specs=pl.BlockSpec((Q_BLOCK, gw), lambda g, i: (i, g)),
        out_shape=jax.ShapeDtypeStruct((s, hd), BF16),
        scratch_shapes=[pltpu.VMEM((1, NSA_GROUP * Q_BLOCK), F32), pltpu.VMEM((1, NSA_GROUP * Q_BLOCK), F32),
                        pltpu.VMEM((HEAD_DIM, NSA_GROUP * Q_BLOCK), F32)],
        compiler_params=_cparams(("parallel", "arbitrary")),
        name="nsa_selected",
    )(bound, q, gates, ks, vst, negp, tab_s, ocwt)


def _pad_keys(x, pad):
    xp = jnp.pad(x, ((0, 0), (pad, 0), (0, 0)))
    g_, sp, dh = xp.shape
    return xp, xp.reshape(g_, sp // LANES, LANES, dh).swapaxes(-1, -2)


def _nsa_mixer(h, hn, tables, w_in, cmp_pe, cmp_w1, cmp_w2, q_gain, k_gain, w_out):
    s, d = h.shape
    g_, hpg, dh = NSA_KV_HEADS, NSA_GROUP, HEAD_DIM
    hd = g_ * hpg * dh
    tab_w, tab_c, tab_s, tbl = tables
    w_in = w_in.astype(BF16)
    q = _nsa_q_proj(hn, w_in[:, :hd], q_gain)
    kv = _nsa_kv_proj(hn, w_in[:, hd:hd + 6 * g_ * dh], k_gain)
    wg = w_in[:, hd + 6 * g_ * dh:].reshape(d, g_, hpg, 3).transpose(0, 1, 3, 2).reshape(d, g_, 3 * hpg)
    wg = jnp.pad(wg, ((0, 0), (0, 0), (0, LANES - 3 * hpg))).reshape(d, g_ * LANES)
    gates = _nsa_gate_proj(hn, wg)
    kc, kct = _compress(kv, cmp_pe, cmp_w1, cmp_w2, k_gain[0])
    mselt = _sel_to_cmp(s // SEL_BLOCK, s // CMP_STRIDE)
    ks, _ = _pad_keys(kv[2 * g_:3 * g_], SEL_TILE)
    _, vst = _pad_keys(kv[3 * g_:4 * g_], SEL_TILE)
    kw, _ = _pad_keys(kv[4 * g_:5 * g_], WINDOW)
    _, vwt = _pad_keys(kv[5 * g_:6 * g_], WINDOW)
    bound = _logit_bound(tbl, q_gain, k_gain)
    ocwt, negt = _nsa_cw(bound, q, gates, kc, kct, mselt, tab_c, kw, vwt, tab_w)
    negp = negt.reshape(negt.shape[0], negt.shape[1], negt.shape[2] // 2, 2, Q_BLOCK)
    o = _nsa_sel(bound, q, gates, ks, vst, negp, tab_s, ocwt)
    return _matmul_resid(o, w_out.astype(BF16), h, name="nsa_out")


def _gelu_proj(hn, w, *, tm=512, tn=1024):
    m, kdim = hn.shape
    nn = w.shape[1]

    def epi(accs, e_refs, o_refs, n):
        o_refs[0][...] = jax.nn.gelu(accs[0])

    return _matmul(
        hn, [(w, _wspec(kdim, tn))], epi,
        [(jax.ShapeDtypeStruct((m, nn), F32), _ospec(tm, tn))],
        tm=tm, tn=tn, n_total=nn, name="sgu_in")[0]


def _sgu_mix_body(u_ref, v_ref, gain_ref, w_ref, bt_ref, o_ref):
    v = v_ref[...]
    ms = jnp.mean(v * v, axis=-1, keepdims=True)
    vn = (v * lax.rsqrt(ms + RMS_EPS) * gain_ref[...]).astype(BF16)
    t = w_ref.shape[1]
    causal = (lax.broadcasted_iota(jnp.int32, (t, t), 0) >= lax.broadcasted_iota(jnp.int32, (t, t), 1))
    bt = bt_ref[...]
    gd = vn.shape[1] // SG_GROUPS
    for g in range(SG_GROUPS):
        gs = slice(g * gd, (g + 1) * gd)
        w = jnp.where(causal, w_ref[g], 0.0).astype(BF16)
        mixed = _dot(w, vn[:, gs]) + bt[:, g:g + 1]
        o_ref[:, gs] = (u_ref[:, gs] * mixed).astype(BF16)


def _sgu_mix(uv, v_gain, w_s, b_s):
    s = uv.shape[0]
    wd = uv.shape[1] // 2
    t = SG_CHUNK
    return pl.pallas_call(
        _sgu_mix_body,
        grid=(s // t,),
        in_specs=[pl.BlockSpec((t, wd), lambda c: (c, 0)),
                  pl.BlockSpec((t, wd), lambda c: (c, 1)),
                  pl.BlockSpec((1, wd), lambda c: (0, 0)),
                  pl.BlockSpec((SG_GROUPS, t, t), lambda c: (0, 0, 0)),
                  pl.BlockSpec((t, SG_GROUPS), lambda c: (0, 0))],
        out_specs=pl.BlockSpec((t, wd), lambda c: (c, 0)),
        out_shape=jax.ShapeDtypeStruct((s, wd), BF16),
        compiler_params=_cparams(("parallel",)),
        name="sgu_mix",
    )(uv, uv, v_gain.reshape(1, wd), w_s, b_s.T)


def _sgu_mixer(h, hn, w_in, v_gain, w_s, b_s, w_out):
    uv = _gelu_proj(hn, w_in.astype(BF16))
    y = _sgu_mix(uv, v_gain, w_s, b_s)
    return _matmul_resid(y, w_out.astype(BF16), h, name="sgu_out")


def _gla_body(q_ref, k_ref, v_ref, r_ref, g1_ref, wg_ref, bg_ref, og_ref, o_ref, state_sc):
    c = pl.program_id(1)

    @pl.when(c == 0)
    def _():
        state_sc[...] = jnp.zeros_like(state_sc)

    ch = q_ref.shape[0]
    dk = q_ref.shape[1]
    x = _dot(g1_ref[...].astype(BF16), wg_ref[...]) + bg_ref[...]
    log_a = (jnp.minimum(x, 0.0) - jnp.log(1.0 + jnp.exp(-jnp.abs(x)))) * (1.0 / GLA_GATE_TEMP)
    tri = (lax.broadcasted_iota(jnp.int32, (ch, ch), 0)
           >= lax.broadcasted_iota(jnp.int32, (ch, ch), 1))
    tri_b = jnp.where(tri, 1.0, 0.0).astype(BF16)
    a_hi = log_a.astype(BF16)
    a_lo = (log_a - a_hi.astype(F32)).astype(BF16)
    b = _dot(tri_b, a_hi) + _dot(tri_b, a_lo)
    q = q_ref[...].astype(F32) * (dk ** -0.5)
    k = k_ref[...].astype(F32)
    v = v_ref[...]
    state = state_sc[...]
    o = _dot((q * jnp.exp(b)).astype(BF16), state.astype(BF16))
    nsub = ch // GLA_SUB
    ends = [jnp.broadcast_to(b[(jb + 1) * GLA_SUB - 1:(jb + 1) * GLA_SUB], (GLA_SUB, dk))
            for jb in range(nsub)]
    b_end = jnp.concatenate(ends, axis=0)
    b_start = jnp.concatenate([jnp.zeros((GLA_SUB, dk), F32)] + ends[:-1], axis=0)
    rblk = lax.broadcasted_iota(jnp.int32, (ch, ch), 0) // GLA_SUB
    cblk = lax.broadcasted_iota(jnp.int32, (ch, ch), 1) // GLA_SUB
    q_diag = (q * jnp.exp(b - b_start)).astype(BF16)
    k_diag = (k * jnp.exp(b_start - b)).astype(BF16)
    attn = jnp.where(jnp.logical_and(tri, rblk == cblk), _dot_nt(q_diag, k_diag), 0.0)
    k_hat = (k * jnp.exp(b_end - b)).astype(BF16)
    for jb in range(nsub - 1):
        q_hat = (q * jnp.exp(jnp.minimum(b - ends[jb][0:1], 0.0))).astype(BF16)
        attn = attn + jnp.where(jnp.logical_and(cblk == jb, rblk > jb), _dot_nt(q_hat, k_hat), 0.0)
    o = o + _dot(attn.astype(BF16), v)
    b_last = b[ch - 1:ch]
    k_dec = k * jnp.exp(b_last - b)
    k_dec_t = jnp.transpose(k_dec).astype(BF16)
    decay_t = jnp.transpose(jnp.broadcast_to(jnp.exp(b_last), (LANES, dk)))
    state_sc[...] = decay_t[:, 0:1] * state + _dot(k_dec_t, v)
    ms = jnp.mean(o * o, axis=-1, keepdims=True)
    on = o * lax.rsqrt(ms + RMS_EPS) * og_ref[...]
    r = r_ref[...].astype(F32)
    o_ref[...] = (on * _silu(r)).astype(BF16)


def _gla_mixer(h, hn, w_in, w_gate2, b_gate, o_gain, w_out):
    s, d = h.shape
    nh = GLA_HEADS
    dk = d // 2
    dv = d
    dkh, dvh = dk // nh, dv // nh
    w_in = w_in.astype(BF16)
    nmain = 2 * dk + 2 * dv

    def epi_bf16(accs, e_refs, o_refs, n):
        o_refs[0][...] = accs[0].astype(BF16)

    def epi_f32(accs, e_refs, o_refs, n):
        o_refs[0][...] = accs[0]

    tm, tn = 512, 1024
    proj = _matmul(hn, [(w_in, _wspec(d, tn))], epi_bf16,
                   [(jax.ShapeDtypeStruct((s, nmain), BF16), _ospec(tm, tn))],
                   tm=tm, tn=tn, n_total=nmain, name="gla_in")[0]
    wg1 = jnp.pad(w_in[:, nmain:], ((0, 0), (0, LANES - GLA_GATE_RANK)))
    g1 = _matmul(hn, [(wg1, _wspec(d, LANES))], epi_f32,
                 [(jax.ShapeDtypeStruct((s, LANES), F32), _ospec(tm, LANES))],
                 tm=tm, tn=LANES, n_total=LANES, name="gla_gate_in")[0]
    wg2 = jnp.pad(w_gate2.astype(BF16), ((0, LANES - GLA_GATE_RANK), (0, 0)))
    ch = GLA_CHUNK
    o = pl.pallas_call(
        _gla_body,
        grid=(nh, s // ch),
        in_specs=[
            pl.BlockSpec((ch, dkh), lambda hh, c: (c, hh)),
            pl.BlockSpec((ch, dkh), lambda hh, c: (c, nh + hh)),
            pl.BlockSpec((ch, dvh), lambda hh, c: (c, 2 * dk // dvh + hh)),
            pl.BlockSpec((ch, dvh), lambda hh, c: (c, (2 * dk + dv) // dvh + hh)),
            pl.BlockSpec((ch, LANES), lambda hh, c: (c, 0)),
            pl.BlockSpec((LANES, dkh), lambda hh, c: (0, hh)),
            pl.BlockSpec((1, dkh), lambda hh, c: (0, hh)),
            pl.BlockSpec((1, dvh), lambda hh, c: (0, 0)),
        ],
        out_specs=pl.BlockSpec((ch, dvh), lambda hh, c: (c, hh)),
        out_shape=jax.ShapeDtypeStruct((s, dv), BF16),
        scratch_shapes=[pltpu.VMEM((dkh, dvh), F32)],
        compiler_params=_cparams(("parallel", "arbitrary")),
        name="gla_scan",
    )(proj, proj, proj, proj, g1, wg2, b_gate.reshape(1, dk), o_gain.reshape(1, dvh))
    return _matmul_resid(o, w_out.astype(BF16), h, name="gla_out")


def _dense_ffn(h, hn, w_up, w_down):
    act = _swiglu_up(hn, w_up.astype(BF16))
    return _matmul_resid(act, w_down.astype(BF16), h, name="ffn_down")


def _moe_ffn(h, hn, gate, w_up, w_down):
    ne, ff, d = w_down.shape
    act = _moe_up_dense(hn, w_up.astype(BF16), gate)
    return _matmul_resid(act, w_down.astype(BF16).reshape(ne * ff, d), h, name="moe_down")


def kernel(x, rel_bias, norm_gain, nsa_w_in, nsa_cmp_pe, nsa_cmp_w1, nsa_cmp_w2, nsa_q_gain, nsa_k_gain, nsa_w_out, sg_w_in, sg_v_gain, sg_w_s, sg_b_s, sg_w_out, gla_w_in, gla_w_gate2, gla_b_gate, gla_o_gain, gla_w_out, ffn_w_up, ffn_w_down, moe_router, moe_w_up, moe_w_down):
    bsz, s, d = x.shape
    tables = _nsa_tables(rel_bias)
    outs = []
    for bi in range(bsz):
        h = x[bi]
        for i in range(DEPTH):
            mixer = i % N_MIXERS
            j = i // N_MIXERS
            hn = _rmsnorm(h, norm_gain[i, 0])
            if mixer == 0:
                h = _nsa_mixer(h, hn, tables, nsa_w_in[j], nsa_cmp_pe[j], nsa_cmp_w1[j], nsa_cmp_w2[j],
                               nsa_q_gain[j], nsa_k_gain[j], nsa_w_out[j])
            elif mixer == 1:
                h = _sgu_mixer(h, hn, sg_w_in[j], sg_v_gain[j], sg_w_s[j], sg_b_s[j], sg_w_out[j])
            else:
                h = _gla_mixer(h, hn, gla_w_in[j], gla_w_gate2[j], gla_b_gate[j], gla_o_gain[j], gla_w_out[j])
            f = i // 2
            if i % 2 == 0:
                hn = _rmsnorm(h, norm_gain[i, 1])
                h = _dense_ffn(h, hn, ffn_w_up[f], ffn_w_down[f])
            else:
                hn, gate = _rmsnorm_router(h, norm_gain[i, 1], moe_router[f])
                h = _moe_ffn(h, hn, gate, moe_w_up[f], moe_w_down[f])
        outs.append(h)
    return jnp.stack(outs, axis=0)
```

```python
import math

import jax
import jax.numpy as jnp
from jax import lax
from jax.experimental import pallas as pl
from jax.experimental.pallas import tpu as pltpu

F32 = jnp.float32
BF16 = jnp.bfloat16

DEPTH = 4
N_MIXERS = 3
RMS_EPS = 1e-6
NEG = -1e30
HEAD_DIM = 128
NSA_KV_HEADS = 4
NSA_GROUP = 8
CMP_BLOCK = 32
CMP_STRIDE = 16
SEL_BLOCK = 64
SEL_TOPK = 16
WINDOW = 512
Q_BLOCK = 128
FORCED_SCORE = 1e4
T5_BUCKETS = 32
T5_MAX_DISTANCE = 2048
SG_CHUNK = 128
SG_GROUPS = 32
GLA_HEADS = 4
GLA_GATE_RANK = 16
GLA_GATE_TEMP = 16.0
GLA_CHUNK = 64
GLA_SUB = 16
MOE_EXPERTS = 8
MOE_TM = 512

LANES = 128
SEL_TILE = 512
SEL_NEAR_TILES = 5
CMP_NEAR_SHIFT = 12
CW_HEADS = 4
VMEM_MB = 56
LOG2E = 1.4426950408889634
SAFE_LOGIT = 60.0
SEL_PAD_BLOCKS = 8


def _cparams(sem, vmem_mb=VMEM_MB):
    return pltpu.CompilerParams(dimension_semantics=sem, vmem_limit_bytes=vmem_mb * 2**20)


def _dot(a, b):
    return jnp.dot(a, b, preferred_element_type=F32)


def _dot_nt(a, b):
    return lax.dot_general(a, b, (((1,), (1,)), ((), ())), preferred_element_type=F32)


def _rmsnorm_body(x_ref, g_ref, o_ref):
    x = x_ref[...]
    ms = jnp.mean(x * x, axis=-1, keepdims=True)
    o_ref[...] = (x * lax.rsqrt(ms + RMS_EPS) * g_ref[...]).astype(o_ref.dtype)


def _rmsnorm(x, gain, tm=256):
    m, d = x.shape
    return pl.pallas_call(
        _rmsnorm_body,
        grid=(m // tm,),
        in_specs=[pl.BlockSpec((tm, d), lambda i: (i, 0)), pl.BlockSpec((1, d), lambda i: (0, 0))],
        out_specs=pl.BlockSpec((tm, d), lambda i: (i, 0)),
        out_shape=jax.ShapeDtypeStruct((m, d), BF16),
        compiler_params=_cparams(("parallel",)),
        name="rmsnorm",
    )(x, gain.reshape(1, d))


def _norm_rows(x, gain):
    ms = jnp.mean(x * x, axis=-1, keepdims=True)
    return (x * lax.rsqrt(ms + RMS_EPS) * gain).astype(BF16)


RT_E1, RT_E2, RT_W1, RT_W2, RT_R1, RT_R2 = range(6)


def _moe_router_body(x_ref, g_ref, r_ref, info_ref, cnt_ref, base_sc):
    @pl.when(pl.program_id(0) == 0)
    def _():
        base_sc[...] = jnp.zeros_like(base_sc)

    hn = _norm_rows(x_ref[...], g_ref[...])
    logits = _dot(hn, r_ref[...])
    lane = lax.broadcasted_iota(jnp.int32, logits.shape, 1).astype(F32)
    logits = jnp.where(lane < MOE_EXPERTS, logits, NEG)
    v1 = jnp.max(logits, axis=-1, keepdims=True)
    i1 = jnp.min(jnp.where(logits == v1, lane, 1e3), axis=-1, keepdims=True)
    rest = jnp.where(lane == i1, NEG, logits)
    v2 = jnp.max(rest, axis=-1, keepdims=True)
    i2 = jnp.min(jnp.where(rest == v2, lane, 1e3), axis=-1, keepdims=True)
    e2 = jnp.exp(v2 - v1)
    den = 1.0 + e2
    hot = jnp.where(jnp.logical_or(lane == i1, lane == i2), 1.0, 0.0)
    tm = hot.shape[0]
    earlier = (lax.broadcasted_iota(jnp.int32, (tm, tm), 0) > lax.broadcasted_iota(jnp.int32, (tm, tm), 1))
    before = _dot(jnp.where(earlier, 1.0, 0.0).astype(BF16), hot.astype(BF16)) + base_sc[...]
    r1 = jnp.sum(jnp.where(lane == i1, before, 0.0), axis=-1, keepdims=True)
    r2 = jnp.sum(jnp.where(lane == i2, before, 0.0), axis=-1, keepdims=True)
    rec = jnp.zeros_like(logits)
    for pos, val in ((RT_E1, i1), (RT_E2, i2), (RT_W1, 1.0 / den), (RT_W2, e2 / den), (RT_R1, r1), (RT_R2, r2)):
        rec = jnp.where(lane == float(pos), val, rec)
    info_ref[...] = rec
    total = base_sc[...] + jnp.sum(hot, axis=0, keepdims=True)
    base_sc[...] = total
    cnt_ref[...] = jnp.broadcast_to(total, cnt_ref.shape)


def _moe_router(x, gain, router, tm=256):
    m, d = x.shape
    rpad = jnp.zeros((d, LANES), BF16).at[:, :MOE_EXPERTS].set(router.astype(BF16))
    return pl.pallas_call(
        _moe_router_body,
        grid=(m // tm,),
        in_specs=[pl.BlockSpec((tm, d), lambda i: (i, 0)), pl.BlockSpec((1, d), lambda i: (0, 0)),
                  pl.BlockSpec((d, LANES), lambda i: (0, 0))],
        out_specs=[pl.BlockSpec((tm, LANES), lambda i: (i, 0)), pl.BlockSpec((8, LANES), lambda i: (0, 0))],
        out_shape=[jax.ShapeDtypeStruct((m, LANES), F32), jax.ShapeDtypeStruct((8, LANES), F32)],
        scratch_shapes=[pltpu.VMEM((1, LANES), F32)],
        compiler_params=_cparams(("arbitrary",)),
        name="moe_router",
    )(x, gain.reshape(1, d), rpad)


def _matmul(x, w_list, epilogue, out_list, *, tm, tn, n_total, tk=None, extras=(), name="matmul"):
    m, kdim = x.shape
    tk = kdim if tk is None else tk
    nk = kdim // tk
    nw, ne, no = len(w_list), len(extras), len(out_list)

    def body(*refs):
        x_ref = refs[0]
        w_refs = refs[1:1 + nw]
        e_refs = refs[1 + nw:1 + nw + ne]
        o_refs = refs[1 + nw + ne:1 + nw + ne + no]
        acc_refs = refs[1 + nw + ne + no:]
        n = pl.program_id(0)
        if nk == 1:
            xv = x_ref[...]
            epilogue([_dot(xv, w[...]) for w in w_refs], e_refs, o_refs, n)
        else:
            k = pl.program_id(2)

            @pl.when(k == 0)
            def _():
                for a in acc_refs:
                    a[...] = jnp.zeros_like(a)

            xv = x_ref[...]
            for a, w in zip(acc_refs, w_refs):
                a[...] += _dot(xv, w[...])

            @pl.when(k == nk - 1)
            def _():
                epilogue([a[...] for a in acc_refs], e_refs, o_refs, n)

    in_specs = [pl.BlockSpec((tm, tk), lambda n, mi, k: (mi, k))]
    in_specs += [s for _, s in w_list] + [s for _, s in extras]
    scratch = [] if nk == 1 else [pltpu.VMEM((tm, tn), F32) for _ in range(nw)]
    return pl.pallas_call(
        body,
        grid=(n_total // tn, m // tm, nk),
        in_specs=in_specs,
        out_specs=[s for _, s in out_list],
        out_shape=[s for s, _ in out_list],
        scratch_shapes=scratch,
        compiler_params=_cparams(("parallel", "parallel", "arbitrary")),
        name=name,
    )(x, *[a for a, _ in w_list], *[a for a, _ in extras])


def _wspec(tk, tn, off=0):
    return pl.BlockSpec((tk, tn), lambda n, mi, k: (k, n + off))


def _ospec(tm, tn):
    return pl.BlockSpec((tm, tn), lambda n, mi, k: (mi, n))


def _epi_resid(accs, e_refs, o_refs, n):
    o_refs[0][...] = e_refs[0][...] + accs[0]


def _matmul_resid(x, w, resid, *, tm=512, tn=1024, tk=None, name="matmul_resid"):
    m, kdim = x.shape
    nn = w.shape[1]
    tk = min(kdim, 4096) if tk is None else tk
    return _matmul(
        x, [(w, _wspec(tk, tn))], _epi_resid,
        [(jax.ShapeDtypeStruct((m, nn), F32), _ospec(tm, tn))],
        tm=tm, tn=tn, tk=tk, n_total=nn, extras=[(resid, _ospec(tm, tn))], name=name)[0]


def _silu(a):
    return a * (1.0 / (1.0 + jnp.exp(-a)))


def _epi_swiglu(accs, e_refs, o_refs, n):
    a, b = accs
    o_refs[0][...] = (_silu(a) * b).astype(BF16)


def _swiglu_up(x, w_up, *, tm=512, tn=512, name="swiglu_up"):
    m, kdim = x.shape
    ff = w_up.shape[1] // 2
    return _matmul(
        x, [(w_up, _wspec(kdim, tn)), (w_up, _wspec(kdim, tn, ff // tn))], _epi_swiglu,
        [(jax.ShapeDtypeStruct((m, ff), BF16), _ospec(tm, tn))],
        tm=tm, tn=tn, n_total=ff, name=name)[0]


def _moe_plan(info, cnt, s):
    ne, tm = MOE_EXPERTS, MOE_TM
    e1 = info[:, RT_E1].astype(jnp.int32)
    e2 = info[:, RT_E2].astype(jnp.int32)
    counts = cnt[0, :ne].astype(jnp.int32)
    padded = (counts + tm - 1) // tm * tm
    ends = jnp.cumsum(padded)
    off = ends - padded
    dest1 = off[e1] + info[:, RT_R1].astype(jnp.int32)
    dest2 = off[e2] + info[:, RT_R2].astype(jnp.int32)
    ntile = (2 * s) // tm + ne
    tile_start = jnp.arange(ntile, dtype=jnp.int32) * tm
    tile_expert = jnp.minimum(jnp.sum(tile_start[:, None] >= ends[None, :], axis=1), ne - 1).astype(jnp.int32)
    nvalid = (ends[ne - 1:ne] // tm).astype(jnp.int32)
    tok = jnp.arange(s, dtype=jnp.int32)
    src = jnp.zeros((ntile * tm,), jnp.int32).at[dest1].set(tok).at[dest2].set(tok)
    return dest1, dest2, src, tile_expert, nvalid


def _row_copy(src_hbm, row, dst, r, sem):
    return pltpu.make_async_copy(src_hbm.at[pl.ds(row, 1)], dst.at[pl.ds(r, 1)], sem)


def _moe_gather_body(src_ref, nv_ref, h_hbm, g_ref, o_ref, buf, sem):
    t = pl.program_id(0)
    tm = buf.shape[0]

    @pl.when(t < nv_ref[0])
    def _():
        def issue(r, carry):
            _row_copy(h_hbm, src_ref[t * tm + r], buf, r, sem).start()
            return carry

        lax.fori_loop(0, tm, issue, 0)

        def drain(r, carry):
            _row_copy(h_hbm, src_ref[t * tm + r], buf, r, sem).wait()
            return carry

        lax.fori_loop(0, tm, drain, 0)
        o_ref[...] = _norm_rows(buf[...], g_ref[...])

    @pl.when(t >= nv_ref[0])
    def _():
        o_ref[...] = jnp.zeros_like(o_ref)


def _moe_gather(h, gain, src, nvalid):
    s, d = h.shape
    tm = MOE_TM
    ntile = src.shape[0] // tm
    return pl.pallas_call(
        _moe_gather_body,
        grid_spec=pltpu.PrefetchScalarGridSpec(
            num_scalar_prefetch=2,
            grid=(ntile,),
            in_specs=[pl.BlockSpec(memory_space=pl.ANY),
                      pl.BlockSpec((1, d), lambda t, src_, nv_: (0, 0))],
            out_specs=pl.BlockSpec((tm, d), lambda t, src_, nv_: (t, 0)),
            scratch_shapes=[pltpu.VMEM((tm, d), F32), pltpu.SemaphoreType.DMA(())]),
        out_shape=jax.ShapeDtypeStruct((ntile * tm, d), BF16),
        compiler_params=_cparams(("arbitrary",)),
        name="moe_gather",
    )(src, nvalid, h, gain.reshape(1, d))


def _moe_up_body(te_ref, nv_ref, x_ref, wa_ref, wb_ref, o_ref):
    t = pl.program_id(1)

    @pl.when(t < nv_ref[0])
    def _():
        x = x_ref[...]
        o_ref[...] = (_silu(_dot(x, wa_ref[...])) * _dot(x, wb_ref[...])).astype(BF16)

    @pl.when(t >= nv_ref[0])
    def _():
        o_ref[...] = jnp.zeros_like(o_ref)


def _moe_up(xs, w_up, tile_expert, nvalid, *, tn=512):
    p, d = xs.shape
    tm = MOE_TM
    ff = w_up.shape[2] // 2
    nn = ff // tn
    return pl.pallas_call(
        _moe_up_body,
        grid_spec=pltpu.PrefetchScalarGridSpec(
            num_scalar_prefetch=2,
            grid=(nn, p // tm),
            in_specs=[pl.BlockSpec((tm, d), lambda n, t, te, nv: (t, 0)),
                      pl.BlockSpec((None, d, tn), lambda n, t, te, nv: (te[t], 0, n)),
                      pl.BlockSpec((None, d, tn), lambda n, t, te, nv: (te[t], 0, n + nn))],
            out_specs=pl.BlockSpec((tm, tn), lambda n, t, te, nv: (t, n))),
        out_shape=jax.ShapeDtypeStruct((p, ff), BF16),
        compiler_params=_cparams(("parallel", "arbitrary")),
        name="moe_up",
    )(tile_expert, nvalid, xs, w_up, w_up)


def _moe_down_body(te_ref, nv_ref, x_ref, w_ref, o_ref):
    t = pl.program_id(1)

    @pl.when(t < nv_ref[0])
    def _():
        o_ref[...] = _dot(x_ref[...], w_ref[...])

    @pl.when(t >= nv_ref[0])
    def _():
        o_ref[...] = jnp.zeros_like(o_ref)


def _moe_down(act, w_down, tile_expert, nvalid, *, tn=1024):
    p, ff = act.shape
    tm = MOE_TM
    d = w_down.shape[2]
    return pl.pallas_call(
        _moe_down_body,
        grid_spec=pltpu.PrefetchScalarGridSpec(
            num_scalar_prefetch=2,
            grid=(d // tn, p // tm),
            in_specs=[pl.BlockSpec((tm, ff), lambda n, t, te, nv: (t, 0)),
                      pl.BlockSpec((None, ff, tn), lambda n, t, te, nv: (te[t], 0, n))],
            out_specs=pl.BlockSpec((tm, tn), lambda n, t, te, nv: (t, n))),
        out_shape=jax.ShapeDtypeStruct((p, d), F32),
        compiler_params=_cparams(("parallel", "arbitrary")),
        name="moe_down",
    )(tile_expert, nvalid, act, w_down)


def _moe_combine_body(d1_ref, d2_ref, y_hbm, h_ref, info_ref, o_ref, buf, sems):
    t = pl.program_id(0)
    tm = h_ref.shape[0]

    def issue(r, carry):
        _row_copy(y_hbm, d1_ref[t * tm + r], buf.at[0], r, sems.at[0]).start()
        _row_copy(y_hbm, d2_ref[t * tm + r], buf.at[1], r, sems.at[1]).start()
        return carry

    lax.fori_loop(0, tm, issue, 0)

    def drain(r, carry):
        _row_copy(y_hbm, d1_ref[t * tm + r], buf.at[0], r, sems.at[0]).wait()
        _row_copy(y_hbm, d2_ref[t * tm + r], buf.at[1], r, sems.at[1]).wait()
        return carry

    lax.fori_loop(0, tm, drain, 0)
    info = info_ref[...]
    o_ref[...] = h_ref[...] + (info[:, RT_W1:RT_W1 + 1] * buf[0] + info[:, RT_W2:RT_W2 + 1] * buf[1])


def _moe_combine(h, y, info, dest1, dest2, tm=256):
    s, d = h.shape
    return pl.pallas_call(
        _moe_combine_body,
        grid_spec=pltpu.PrefetchScalarGridSpec(
            num_scalar_prefetch=2,
            grid=(s // tm,),
            in_specs=[pl.BlockSpec(memory_space=pl.ANY),
                      pl.BlockSpec((tm, d), lambda t, a, b: (t, 0)),
                      pl.BlockSpec((tm, LANES), lambda t, a, b: (t, 0))],
            out_specs=pl.BlockSpec((tm, d), lambda t, a, b: (t, 0)),
            scratch_shapes=[pltpu.VMEM((2, tm, d), F32), pltpu.SemaphoreType.DMA((2,))]),
        out_shape=jax.ShapeDtypeStruct((s, d), F32),
        compiler_params=_cparams(("arbitrary",)),
        name="moe_combine",
    )(dest1, dest2, y, h, info)


def _nsa_q_proj(hn, wq, q_gain, *, tm=512, tn=1024):
    m, kdim = hn.shape
    nn = wq.shape[1]
    scale = HEAD_DIM ** -0.5 * LOG2E

    def epi(accs, e_refs, o_refs, n):
        acc = accs[0]
        gain = e_refs[0][...]
        for c in range(tn // HEAD_DIM):
            a = acc[:, c * HEAD_DIM:(c + 1) * HEAD_DIM]
            ms = jnp.mean(a * a, axis=-1, keepdims=True)
            o_refs[0][:, c * HEAD_DIM:(c + 1) * HEAD_DIM] = (
                a * lax.rsqrt(ms + RMS_EPS) * gain * scale).astype(BF16)

    return _matmul(
        hn, [(wq, _wspec(kdim, tn))], epi,
        [(jax.ShapeDtypeStruct((m, nn), BF16), _ospec(tm, tn))],
        tm=tm, tn=tn, n_total=nn,
        extras=[(q_gain.reshape(1, HEAD_DIM), pl.BlockSpec((1, HEAD_DIM), lambda n, mi, k: (0, 0)))],
        name="nsa_q_proj")[0]


def _nsa_kv_proj(hn, wkv, k_gain, *, tm=512):
    m, kdim = hn.shape
    g_ = NSA_KV_HEADS
    tn = g_ * HEAD_DIM
    gains = jnp.ones((6, 1, HEAD_DIM), F32).at[2, 0].set(k_gain[1]).at[4, 0].set(k_gain[2])

    def epi(accs, e_refs, o_refs, n):
        acc = accs[0]
        kg = e_refs[0][0]
        do_norm = jnp.logical_or(n == 2, n == 4)
        for g in range(g_):
            a = acc[:, g * HEAD_DIM:(g + 1) * HEAD_DIM]
            ms = jnp.mean(a * a, axis=-1, keepdims=True)
            an = a * lax.rsqrt(ms + RMS_EPS) * kg
            o_refs[0][g] = jnp.where(do_norm, an, a).astype(BF16)

    return _matmul(
        hn, [(wkv, _wspec(kdim, tn))], epi,
        [(jax.ShapeDtypeStruct((6 * g_, m, HEAD_DIM), BF16),
          pl.BlockSpec((g_, tm, HEAD_DIM), lambda n, mi, k: (n, mi, 0)))],
        tm=tm, tn=tn, n_total=6 * tn,
        extras=[(gains, pl.BlockSpec((1, 1, HEAD_DIM), lambda n, mi, k: (n, 0, 0)))],
        name="nsa_kv_proj")[0]


def _nsa_gate_proj(hn, wg, *, tm=512):
    m, kdim = hn.shape
    nn = wg.shape[1]

    def epi(accs, e_refs, o_refs, n):
        o_refs[0][...] = 1.0 / (1.0 + jnp.exp(-accs[0]))

    return _matmul(
        hn, [(wg, _wspec(kdim, nn))], epi,
        [(jax.ShapeDtypeStruct((m, nn), F32), _ospec(tm, nn))],
        tm=tm, tn=nn, n_total=nn, name="nsa_gate_proj")[0]


def _compress_body(r_ref, pe_ref, w1_ref, w2_ref, kg_ref, o_ref, ot_ref):
    j = pl.program_id(0)
    r = r_ref[0].astype(F32)
    pe = pe_ref[0]
    half = r.shape[1]
    top = (r + pe[0:1]).astype(BF16)
    bot = (r + pe[1:2]).astype(BF16)
    a = _dot(top, w1_ref[0, 0:half, :])
    b = _dot(bot, w1_ref[0, half:2 * half, :])
    nrow = r.shape[0]
    hid = a + pltpu.roll(b, nrow - 1, 0)
    y = _dot(jax.nn.gelu(hid).astype(BF16), w2_ref[0])
    ms = jnp.mean(y * y, axis=-1, keepdims=True)
    yn = y * lax.rsqrt(ms + RMS_EPS) * kg_ref[...]
    out = jnp.where(j == 0, yn, y)
    o_ref[0, 0] = out.astype(BF16)
    ot_ref[0, 0] = jnp.transpose(out).astype(BF16)


def _compress(kv, pe, w1, w2, k_gain0):
    g_ = NSA_KV_HEADS
    s = kv.shape[1]
    nrow = s // CMP_STRIDE
    wid = CMP_STRIDE * HEAD_DIM
    r = kv.reshape(6 * g_, nrow, wid)
    pe2 = pe.reshape(2, 2, wid)
    return pl.pallas_call(
        _compress_body,
        grid=(2, g_),
        in_specs=[pl.BlockSpec((1, nrow, wid), lambda j, g: (j * g_ + g, 0, 0)),
                  pl.BlockSpec((1, 2, wid), lambda j, g: (j, 0, 0)),
                  pl.BlockSpec((1, 2 * wid, HEAD_DIM), lambda j, g: (j, 0, 0)),
                  pl.BlockSpec((1, HEAD_DIM, HEAD_DIM), lambda j, g: (j, 0, 0)),
                  pl.BlockSpec((1, HEAD_DIM), lambda j, g: (0, 0))],
        out_specs=[pl.BlockSpec((1, 1, nrow, HEAD_DIM), lambda j, g: (j, g, 0, 0)),
                   pl.BlockSpec((1, 1, HEAD_DIM, nrow), lambda j, g: (j, g, 0, 0))],
        out_shape=[jax.ShapeDtypeStruct((2, g_, nrow, HEAD_DIM), BF16),
                   jax.ShapeDtypeStruct((2, g_, HEAD_DIM, nrow), BF16)],
        compiler_params=_cparams(("parallel", "parallel")),
        name="nsa_compress",
    )(r, pe2, w1.astype(BF16), w2.astype(BF16), k_gain0.reshape(1, HEAD_DIM))


def _t5_bucket(dist):
    dist = jnp.maximum(dist, 0)
    max_exact = T5_BUCKETS // 2
    d_f = jnp.maximum(dist, 1).astype(F32)
    log_b = max_exact + (jnp.log(d_f / max_exact) / math.log(T5_MAX_DISTANCE / max_exact)
                         * (T5_BUCKETS - max_exact)).astype(jnp.int32)
    log_b = jnp.minimum(log_b, T5_BUCKETS - 1)
    return jnp.where(dist < max_exact, dist, log_b)


def _bias_table(tbl, dist, valid, shift):
    onehot = jax.nn.one_hot(_t5_bucket(dist).reshape(-1), T5_BUCKETS, dtype=F32)
    t = tbl - tbl[T5_BUCKETS - 1:T5_BUCKETS] if shift else tbl
    vals = jnp.einsum("nb,bh->hn", onehot, t, precision=lax.Precision.HIGHEST)
    vals = vals.reshape((tbl.shape[1],) + dist.shape) * LOG2E
    return jnp.where(valid[None], vals, NEG)


def _nsa_tables(rel_bias):
    tbl = rel_bias.astype(F32)
    r = jnp.arange(Q_BLOCK)
    jw = jnp.arange(WINDOW + Q_BLOCK)
    dw = WINDOW + r[None, :] - jw[:, None]
    tab_w = _bias_table(tbl, dw, (dw >= 0) & (dw < WINDOW), False)
    a = jnp.arange(16)
    cc = jnp.arange(2 * LANES)
    dc = (CMP_NEAR_SHIFT * Q_BLOCK + Q_BLOCK * a[:, None, None] + r[None, None, :]
          - CMP_STRIDE * cc[None, :, None] - (CMP_BLOCK - 1))
    tab_c = _bias_table(tbl, dc, dc >= 0, True)
    jj = jnp.arange(SEL_NEAR_TILES)
    col = jnp.arange(SEL_TILE)
    ds_ = r[None, None, :] - Q_BLOCK + SEL_TILE * (jj[:, None, None] + 1) - col[None, :, None]
    tab_s = _bias_table(tbl, ds_, ds_ >= 0, True)
    return tab_w, tab_c, tab_s, tbl


def _sel_to_cmp(nsel, nc_pad):
    c_start = jnp.arange(nc_pad) * CMP_STRIDE
    s_start = jnp.arange(nsel) * SEL_BLOCK
    overlap = jnp.clip(jnp.minimum(c_start[None, :] + CMP_BLOCK, s_start[:, None] + SEL_BLOCK)
                       - jnp.maximum(c_start[None, :], s_start[:, None]), 0)
    return (overlap.astype(F32) / CMP_BLOCK).astype(BF16)


def _heads_t(q_ref, h0, nh):
    cols = [jnp.transpose(q_ref[:, (h0 + h) * HEAD_DIM:(h0 + h + 1) * HEAD_DIM].astype(F32)) for h in range(nh)]
    return jnp.concatenate(cols, axis=1).astype(BF16)


def _softmax_keys(s, bounded):
    if bounded:
        p = jnp.exp2(s)
        l = jnp.sum(p, axis=0, keepdims=True)
        return p, jnp.where(l > 0.0, 1.0 / l, 0.0)
    m = jnp.max(s, axis=0, keepdims=True)
    p = jnp.exp2(s - m)
    l = jnp.sum(p, axis=0, keepdims=True)
    return p, jnp.where(m > 0.5 * NEG, 1.0 / l, 0.0)


def _logit_bound(rel_bias, q_gain, k_gain):
    qk = HEAD_DIM * (HEAD_DIM ** -0.5 * LOG2E) * jnp.max(jnp.abs(q_gain)) * jnp.max(jnp.abs(k_gain))
    return (1.02 * qk + 2.0 * LOG2E * jnp.max(jnp.abs(rel_bias))).astype(F32).reshape(1)


def _nsa_cw_body(bnd_ref, q_ref, gt_ref, kc_ref, vct_ref, mselt_ref, tabc_ref, kw_ref, vwt_ref, tabw_ref,
                 ocwt_ref, negt_ref):
    bounded = bnd_ref[0] < SAFE_LOGIT
    i = pl.program_id(1)

    @pl.when(bounded)
    def _():
        _nsa_cw_branches(True, i, q_ref, gt_ref, kc_ref, vct_ref, mselt_ref, tabc_ref, kw_ref, vwt_ref, tabw_ref,
                         ocwt_ref, negt_ref)

    @pl.when(jnp.logical_not(bounded))
    def _():
        _nsa_cw_branches(False, i, q_ref, gt_ref, kc_ref, vct_ref, mselt_ref, tabc_ref, kw_ref, vwt_ref, tabw_ref,
                         ocwt_ref, negt_ref)


def _nsa_cw_branches(bounded, i, q_ref, gt_ref, kc_ref, vct_ref, mselt_ref, tabc_ref, kw_ref, vwt_ref, tabw_ref,
                     ocwt_ref, negt_ref):
    f = (i + 16 - CMP_NEAR_SHIFT) // 16 - 1
    kc = kc_ref[0, 0]
    vct = vct_ref[0, 0]
    nc = kc.shape[0]
    nsel = mselt_ref.shape[0]
    wlen = WINDOW + Q_BLOCK
    nslab = wlen // LANES
    start = pl.multiple_of(i * Q_BLOCK, Q_BLOCK)
    kw = kw_ref[0, pl.ds(start, wlen), :]
    vwt = jnp.concatenate([vwt_ref[0, i + c] for c in range(nslab)], axis=1)
    roww = lax.broadcasted_iota(jnp.int32, (wlen, LANES), 0)
    w_pad = roww < (WINDOW - Q_BLOCK * i)
    gt_t = jnp.transpose(gt_ref[...])
    psum = jnp.zeros((nc, Q_BLOCK), F32)
    hg = CW_HEADS
    for h0 in range(0, NSA_GROUP, hg):
        qt = _heads_t(q_ref, h0, hg)
        s = _dot(kc, qt)
        ta = jnp.concatenate([tabc_ref[h0 + h, 0, 0:LANES, :] for h in range(hg)], axis=1)
        tb = jnp.concatenate([tabc_ref[h0 + h, 0, LANES:2 * LANES, :] for h in range(hg)], axis=1)
        pieces = []
        for ch in range(nc // LANES):
            rest = jnp.where(ch > f + 1, NEG, 0.0)
            bias = jnp.where(ch == f, ta, jnp.where(ch == f + 1, tb, rest))
            pieces.append(s[ch * LANES:(ch + 1) * LANES] + bias)
        s = jnp.concatenate(pieces, axis=0)
        p, linv = _softmax_keys(s, bounded)
        pn = p * linv
        for h in range(hg):
            psum = psum + pn[:, h * Q_BLOCK:(h + 1) * Q_BLOCK]
        oc = _dot(vct, pn.astype(BF16))
        tw = jnp.concatenate([jnp.where(w_pad, NEG, tabw_ref[h0 + h]) for h in range(hg)], axis=1)
        sw = _dot(kw, qt) + tw
        pw, lwinv = _softmax_keys(sw, bounded)
        ow = _dot(vwt, pw.astype(BF16)) * lwinv
        for h in range(hg):
            hh = h0 + h
            cs = slice(h * Q_BLOCK, (h + 1) * Q_BLOCK)
            ocwt_ref[0, 0, hh * HEAD_DIM:(hh + 1) * HEAD_DIM, :] = (
                gt_t[hh:hh + 1] * oc[:, cs] + gt_t[16 + hh:17 + hh] * ow[:, cs])
    p_hi = psum.astype(BF16)
    p_lo = (psum - p_hi.astype(F32)).astype(BF16)
    mselt = mselt_ref[...]
    imp = _dot(mselt, p_hi) + _dot(mselt, p_lo)
    blk = lax.broadcasted_iota(jnp.int32, (nsel, Q_BLOCK), 0).astype(F32)
    qpos = lax.broadcasted_iota(jnp.int32, (nsel, Q_BLOCK), 1)
    cur = (i * (Q_BLOCK // SEL_BLOCK)).astype(F32) + jnp.where(qpos >= SEL_BLOCK, 1.0, 0.0)
    forced = jnp.logical_or(blk == cur, blk == 0.0)
    val = jnp.where(forced, FORCED_SCORE, jnp.where(blk <= cur, imp, -1.0))
    chosen = jnp.zeros((nsel, Q_BLOCK), F32)
    for _ in range(SEL_TOPK):
        mx = jnp.max(val, axis=0, keepdims=True)
        first = jnp.min(jnp.where(val == mx, blk, 1e4), axis=0, keepdims=True)
        hit = blk == first
        chosen = jnp.where(hit, 1.0, chosen)
        val = jnp.where(hit, -2.0, val)
    negt_ref[0, 0, 0:SEL_PAD_BLOCKS] = jnp.full((SEL_PAD_BLOCKS, Q_BLOCK), NEG, F32)
    negt_ref[0, 0, SEL_PAD_BLOCKS:SEL_PAD_BLOCKS + nsel] = jnp.where(chosen > 0.0, 0.0, NEG)


def _nsa_cw(bound, q, gates, kc, vct, mselt, tab_c, kw, vwt, tab_w):
    s, hd = q.shape
    g_ = NSA_KV_HEADS
    nq = s // Q_BLOCK
    gw = NSA_GROUP * HEAD_DIM
    nc = kc.shape[2]
    nsel = mselt.shape[0]
    spad = kw.shape[1]
    nslab = vwt.shape[1]
    wlen = WINDOW + Q_BLOCK

    def var(i):
        return (i + 16 - CMP_NEAR_SHIFT) % 16

    return pl.pallas_call(
        _nsa_cw_body,
        grid=(g_, nq),
        in_specs=[
            pl.BlockSpec(memory_space=pltpu.SMEM),
            pl.BlockSpec((Q_BLOCK, gw), lambda g, i: (i, g)),
            pl.BlockSpec((Q_BLOCK, LANES), lambda g, i: (i, g)),
            pl.BlockSpec((1, 1, nc, HEAD_DIM), lambda g, i: (0, g, 0, 0)),
            pl.BlockSpec((1, 1, HEAD_DIM, nc), lambda g, i: (1, g, 0, 0)),
            pl.BlockSpec((nsel, nc), lambda g, i: (0, 0)),
            pl.BlockSpec((NSA_GROUP, 1, 2 * LANES, Q_BLOCK), lambda g, i: (g, var(i), 0, 0)),
            pl.BlockSpec((1, spad, HEAD_DIM), lambda g, i: (g, 0, 0)),
            pl.BlockSpec((1, nslab, HEAD_DIM, LANES), lambda g, i: (g, 0, 0, 0)),
            pl.BlockSpec((NSA_GROUP, wlen, Q_BLOCK), lambda g, i: (g, 0, 0)),
        ],
        out_specs=[pl.BlockSpec((1, 1, gw, Q_BLOCK), lambda g, i: (g, i, 0, 0)),
                   pl.BlockSpec((1, 1, SEL_PAD_BLOCKS + nsel, Q_BLOCK), lambda g, i: (g, i, 0, 0))],
        out_shape=[jax.ShapeDtypeStruct((g_, nq, gw, Q_BLOCK), F32),
                   jax.ShapeDtypeStruct((g_, nq, SEL_PAD_BLOCKS + nsel, Q_BLOCK), F32)],
        compiler_params=_cparams(("parallel", "arbitrary")),
        name="nsa_cmp_win",
    )(bound, q, gates, kc, vct, mselt, tab_c, kw, vwt, tab_w)


def _nsa_sel_body(bnd_ref, q_ref, gt_ref, ks_ref, vst_ref, negp_ref, tabs_ref, ocwt_ref, o_ref,
                  m_sc, l_sc, acc_sc):
    i = pl.program_id(1)
    ntile = i // (SEL_TILE // Q_BLOCK) + 1
    nh = NSA_GROUP
    spt = SEL_TILE // LANES
    ppt = SEL_TILE // (2 * SEL_BLOCK)
    qt = _heads_t(q_ref, 0, nh)

    def tile(jj, near, bounded):
        slab0 = (i + 1) - spt * jj
        row0 = pl.multiple_of(slab0 * LANES, LANES)
        kt = ks_ref[0, pl.ds(row0, SEL_TILE), :]
        vtt = jnp.concatenate([vst_ref[0, slab0 + c] for c in range(spt)], axis=1)
        rows = []
        for c in range(ppt):
            pair = negp_ref[0, 0, slab0 + c]
            rows += [jnp.broadcast_to(pair[r:r + 1], (SEL_BLOCK, Q_BLOCK)) for r in range(2)]
        mk = jnp.concatenate(rows, axis=0)
        if near:
            bias = jnp.concatenate([tabs_ref[h, jj] + mk for h in range(nh)], axis=1)
        else:
            bias = jnp.tile(mk, (1, nh))
        sc = _dot(kt, qt) + bias
        if bounded:
            p = jnp.exp2(sc)
            l_sc[...] = l_sc[...] + jnp.sum(p, axis=0, keepdims=True)
            acc_sc[...] = acc_sc[...] + _dot(vtt, p.astype(BF16))
        else:
            m_old = m_sc[...]
            m_new = jnp.maximum(m_old, jnp.max(sc, axis=0, keepdims=True))
            alpha = jnp.exp2(m_old - m_new)
            p = jnp.exp2(sc - m_new)
            l_sc[...] = alpha * l_sc[...] + jnp.sum(p, axis=0, keepdims=True)
            acc_sc[...] = alpha * acc_sc[...] + _dot(vtt, p.astype(BF16))
            m_sc[...] = m_new

    def run(bounded):
        m_sc[...] = jnp.full(m_sc.shape, NEG, F32)
        l_sc[...] = jnp.zeros(l_sc.shape, F32)
        acc_sc[...] = jnp.zeros(acc_sc.shape, F32)
        for jj in range(SEL_NEAR_TILES):
            @pl.when(jj < ntile)
            def _():
                tile(jj, True, bounded)

        def far(jj, carry):
            tile(jj, False, bounded)
            return carry

        lax.fori_loop(SEL_NEAR_TILES, ntile, far, 0)

    bounded = bnd_ref[0] < SAFE_LOGIT

    @pl.when(bounded)
    def _():
        run(True)

    @pl.when(jnp.logical_not(bounded))
    def _():
        run(False)

    o_t = acc_sc[...] * (1.0 / l_sc[...])
    gt_t = jnp.transpose(gt_ref[...])
    for h in range(nh):
        hs = slice(h * HEAD_DIM, (h + 1) * HEAD_DIM)
        oh = ocwt_ref[0, 0, hs, :] + gt_t[8 + h:9 + h] * o_t[:, h * Q_BLOCK:(h + 1) * Q_BLOCK]
        o_ref[:, hs] = jnp.transpose(oh).astype(BF16)


def _nsa_sel(bound, q, gates, ks, vst, negp, tab_s, ocwt):
    s, hd = q.shape
    g_ = NSA_KV_HEADS
    nq = s // Q_BLOCK
    gw = NSA_GROUP * HEAD_DIM
    spad = ks.shape[1]
    nslab = vst.shape[1]
    npair = negp.shape[2]
    return pl.pallas_call(
        _nsa_sel_body,
        grid=(g_, nq),
        in_specs=[
            pl.BlockSpec(memory_space=pltpu.SMEM),
            pl.BlockSpec((Q_BLOCK, gw), lambda g, i: (i, g)),
            pl.BlockSpec((Q_BLOCK, LANES), lambda g, i: (i, g)),
            pl.BlockSpec((1, spad, HEAD_DIM), lambda g, i: (g, 0, 0)),
            pl.BlockSpec((1, nslab, HEAD_DIM, LANES), lambda g, i: (g, 0, 0, 0)),
            pl.BlockSpec((1, 1, npair, 2, Q_BLOCK), lambda g, i: (g, i, 0, 0, 0)),
            pl.BlockSpec((NSA_GROUP, SEL_NEAR_TILES, SEL_TILE, Q_BLOCK), lambda g, i: (g, 0, 0, 0)),
            pl.BlockSpec((1, 1, gw, Q_BLOCK), lambda g, i: (g, i, 0, 0)),
        ],
        out_specs=pl.BlockSpec((Q_BLOCK, gw), lambda g, i: (i, g)),
        out_shape=jax.ShapeDtypeStruct((s, hd), BF16),
        scratch_shapes=[pltpu.VMEM((1, NSA_GROUP * Q_BLOCK), F32), pltpu.VMEM((1, NSA_GROUP * Q_BLOCK), F32),
                        pltpu.VMEM((HEAD_DIM, NSA_GROUP * Q_BLOCK), F32)],
        compiler_params=_cparams(("parallel", "arbitrary")),
        name="nsa_selected",
    )(bound, q, gates, ks, vst, negp, tab_s, ocwt)


def _pad_keys(x, pad):
    xp = jnp.pad(x, ((0, 0), (pad, 0), (0, 0)))
    g_, sp, dh = xp.shape
    return xp, xp.reshape(g_, sp // LANES, LANES, dh).swapaxes(-1, -2)


def _nsa_mixer(h, hn, tables, w_in, cmp_pe, cmp_w1, cmp_w2, q_gain, k_gain, w_out):
    s, d = h.shape
    g_, hpg, dh = NSA_KV_HEADS, NSA_GROUP, HEAD_DIM
    hd = g_ * hpg * dh
    tab_w, tab_c, tab_s, tbl = tables
    w_in = w_in.astype(BF16)
    q = _nsa_q_proj(hn, w_in[:, :hd], q_gain)
    kv = _nsa_kv_proj(hn, w_in[:, hd:hd + 6 * g_ * dh], k_gain)
    wg = w_in[:, hd + 6 * g_ * dh:].reshape(d, g_, hpg, 3).transpose(0, 1, 3, 2).reshape(d, g_, 3 * hpg)
    wg = jnp.pad(wg, ((0, 0), (0, 0), (0, LANES - 3 * hpg))).reshape(d, g_ * LANES)
    gates = _nsa_gate_proj(hn, wg)
    kc, kct = _compress(kv, cmp_pe, cmp_w1, cmp_w2, k_gain[0])
    mselt = _sel_to_cmp(s // SEL_BLOCK, s // CMP_STRIDE)
    ks, _ = _pad_keys(kv[2 * g_:3 * g_], SEL_TILE)
    _, vst = _pad_keys(kv[3 * g_:4 * g_], SEL_TILE)
    kw, _ = _pad_keys(kv[4 * g_:5 * g_], WINDOW)
    _, vwt = _pad_keys(kv[5 * g_:6 * g_], WINDOW)
    bound = _logit_bound(tbl, q_gain, k_gain)
    ocwt, negt = _nsa_cw(bound, q, gates, kc, kct, mselt, tab_c, kw, vwt, tab_w)
    negp = negt.reshape(negt.shape[0], negt.shape[1], negt.shape[2] // 2, 2, Q_BLOCK)
    o = _nsa_sel(bound, q, gates, ks, vst, negp, tab_s, ocwt)
    return _matmul_resid(o, w_out.astype(BF16), h, name="nsa_out")


def _gelu_proj(hn, w, *, tm=512, tn=1024):
    m, kdim = hn.shape
    nn = w.shape[1]

    def epi(accs, e_refs, o_refs, n):
        o_refs[0][...] = jax.nn.gelu(accs[0])

    return _matmul(
        hn, [(w, _wspec(kdim, tn))], epi,
        [(jax.ShapeDtypeStruct((m, nn), F32), _ospec(tm, tn))],
        tm=tm, tn=tn, n_total=nn, name="sgu_in")[0]


def _sgu_mix_body(u_ref, v_ref, gain_ref, w_ref, bt_ref, o_ref):
    v = v_ref[...]
    ms = jnp.mean(v * v, axis=-1, keepdims=True)
    vn = (v * lax.rsqrt(ms + RMS_EPS) * gain_ref[...]).astype(BF16)
    t = w_ref.shape[1]
    causal = (lax.broadcasted_iota(jnp.int32, (t, t), 0) >= lax.broadcasted_iota(jnp.int32, (t, t), 1))
    bt = bt_ref[...]
    gd = vn.shape[1] // SG_GROUPS
    for g in range(SG_GROUPS):
        gs = slice(g * gd, (g + 1) * gd)
        w = jnp.where(causal, w_ref[g], 0.0).astype(BF16)
        mixed = _dot(w, vn[:, gs]) + bt[:, g:g + 1]
        o_ref[:, gs] = (u_ref[:, gs] * mixed).astype(BF16)


def _sgu_mix(uv, v_gain, w_s, b_s):
    s = uv.shape[0]
    wd = uv.shape[1] // 2
    t = SG_CHUNK
    return pl.pallas_call(
        _sgu_mix_body,
        grid=(s // t,),
        in_specs=[pl.BlockSpec((t, wd), lambda c: (c, 0)),
                  pl.BlockSpec((t, wd), lambda c: (c, 1)),
                  pl.BlockSpec((1, wd), lambda c: (0, 0)),
                  pl.BlockSpec((SG_GROUPS, t, t), lambda c: (0, 0, 0)),
                  pl.BlockSpec((t, SG_GROUPS), lambda c: (0, 0))],
        out_specs=pl.BlockSpec((t, wd), lambda c: (c, 0)),
        out_shape=jax.ShapeDtypeStruct((s, wd), BF16),
        compiler_params=_cparams(("parallel",)),
        name="sgu_mix",
    )(uv, uv, v_gain.reshape(1, wd), w_s, b_s.T)


def _sgu_mixer(h, hn, w_in, v_gain, w_s, b_s, w_out):
    uv = _gelu_proj(hn, w_in.astype(BF16))
    y = _sgu_mix(uv, v_gain, w_s, b_s)
    return _matmul_resid(y, w_out.astype(BF16), h, name="sgu_out")


def _gla_body(q_ref, k_ref, v_ref, r_ref, g1_ref, wg_ref, bg_ref, og_ref, o_ref, state_sc):
    c = pl.program_id(1)

    @pl.when(c == 0)
    def _():
        state_sc[...] = jnp.zeros_like(state_sc)

    ch = q_ref.shape[0]
    dk = q_ref.shape[1]
    x = _dot(g1_ref[...].astype(BF16), wg_ref[...]) + bg_ref[...]
    log_a = (jnp.minimum(x, 0.0) - jnp.log(1.0 + jnp.exp(-jnp.abs(x)))) * (1.0 / GLA_GATE_TEMP)
    tri = (lax.broadcasted_iota(jnp.int32, (ch, ch), 0)
           >= lax.broadcasted_iota(jnp.int32, (ch, ch), 1))
    tri_b = jnp.where(tri, 1.0, 0.0).astype(BF16)
    a_hi = log_a.astype(BF16)
    a_lo = (log_a - a_hi.astype(F32)).astype(BF16)
    b = _dot(tri_b, a_hi) + _dot(tri_b, a_lo)
    q = q_ref[...].astype(F32) * (dk ** -0.5)
    k = k_ref[...].astype(F32)
    v = v_ref[...]
    state = state_sc[...]
    o = _dot((q * jnp.exp(b)).astype(BF16), state.astype(BF16))
    nsub = ch // GLA_SUB
    ends = [jnp.broadcast_to(b[(jb + 1) * GLA_SUB - 1:(jb + 1) * GLA_SUB], (GLA_SUB, dk))
            for jb in range(nsub)]
    b_end = jnp.concatenate(ends, axis=0)
    b_start = jnp.concatenate([jnp.zeros((GLA_SUB, dk), F32)] + ends[:-1], axis=0)
    rblk = lax.broadcasted_iota(jnp.int32, (ch, ch), 0) // GLA_SUB
    cblk = lax.broadcasted_iota(jnp.int32, (ch, ch), 1) // GLA_SUB
    q_diag = (q * jnp.exp(b - b_start)).astype(BF16)
    k_diag = (k * jnp.exp(b_start - b)).astype(BF16)
    attn = jnp.where(jnp.logical_and(tri, rblk == cblk), _dot_nt(q_diag, k_diag), 0.0)
    k_hat = (k * jnp.exp(b_end - b)).astype(BF16)
    for jb in range(nsub - 1):
        q_hat = (q * jnp.exp(jnp.minimum(b - ends[jb][0:1], 0.0))).astype(BF16)
        attn = attn + jnp.where(jnp.logical_and(cblk == jb, rblk > jb), _dot_nt(q_hat, k_hat), 0.0)
    o = o + _dot(attn.astype(BF16), v)
    b_last = b[ch - 1:ch]
    k_dec = k * jnp.exp(b_last - b)
    k_dec_t = jnp.transpose(k_dec).astype(BF16)
    decay_t = jnp.transpose(jnp.broadcast_to(jnp.exp(b_last), (LANES, dk)))
    state_sc[...] = decay_t[:, 0:1] * state + _dot(k_dec_t, v)
    ms = jnp.mean(o * o, axis=-1, keepdims=True)
    on = o * lax.rsqrt(ms + RMS_EPS) * og_ref[...]
    r = r_ref[...].astype(F32)
    o_ref[...] = (on * _silu(r)).astype(BF16)


def _gla_mixer(h, hn, w_in, w_gate2, b_gate, o_gain, w_out):
    s, d = h.shape
    nh = GLA_HEADS
    dk = d // 2
    dv = d
    dkh, dvh = dk // nh, dv // nh
    w_in = w_in.astype(BF16)
    nmain = 2 * dk + 2 * dv

    def epi_bf16(accs, e_refs, o_refs, n):
        o_refs[0][...] = accs[0].astype(BF16)

    def epi_f32(accs, e_refs, o_refs, n):
        o_refs[0][...] = accs[0]

    tm, tn = 512, 1024
    proj = _matmul(hn, [(w_in, _wspec(d, tn))], epi_bf16,
                   [(jax.ShapeDtypeStruct((s, nmain), BF16), _ospec(tm, tn))],
                   tm=tm, tn=tn, n_total=nmain, name="gla_in")[0]
    wg1 = jnp.pad(w_in[:, nmain:], ((0, 0), (0, LANES - GLA_GATE_RANK)))
    g1 = _matmul(hn, [(wg1, _wspec(d, LANES))], epi_f32,
                 [(jax.ShapeDtypeStruct((s, LANES), F32), _ospec(tm, LANES))],
                 tm=tm, tn=LANES, n_total=LANES, name="gla_gate_in")[0]
    wg2 = jnp.pad(w_gate2.astype(BF16), ((0, LANES - GLA_GATE_RANK), (0, 0)))
    ch = GLA_CHUNK
    o = pl.pallas_call(
        _gla_body,
        grid=(nh, s // ch),
        in_specs=[
            pl.BlockSpec((ch, dkh), lambda hh, c: (c, hh)),
            pl.BlockSpec((ch, dkh), lambda hh, c: (c, nh + hh)),
            pl.BlockSpec((ch, dvh), lambda hh, c: (c, 2 * dk // dvh + hh)),
            pl.BlockSpec((ch, dvh), lambda hh, c: (c, (2 * dk + dv) // dvh + hh)),
            pl.BlockSpec((ch, LANES), lambda hh, c: (c, 0)),
            pl.BlockSpec((LANES, dkh), lambda hh, c: (0, hh)),
            pl.BlockSpec((1, dkh), lambda hh, c: (0, hh)),
            pl.BlockSpec((1, dvh), lambda hh, c: (0, 0)),
        ],
        out_specs=pl.BlockSpec((ch, dvh), lambda hh, c: (c, hh)),
        out_shape=jax.ShapeDtypeStruct((s, dv), BF16),
        scratch_shapes=[pltpu.VMEM((dkh, dvh), F32)],
        compiler_params=_cparams(("parallel", "arbitrary")),
        name="gla_scan",
    )(proj, proj, proj, proj, g1, wg2, b_gate.reshape(1, dk), o_gain.reshape(1, dvh))
    return _matmul_resid(o, w_out.astype(BF16), h, name="gla_out")


def _dense_ffn(h, hn, w_up, w_down):
    act = _swiglu_up(hn, w_up.astype(BF16))
    return _matmul_resid(act, w_down.astype(BF16), h, name="ffn_down")


def _moe_ffn(h, gain, router, w_up, w_down):
    s, _ = h.shape
    info, cnt = _moe_router(h, gain, router)
    dest1, dest2, src, tile_expert, nvalid = _moe_plan(info, cnt, s)
    xs = _moe_gather(h, gain, src, nvalid)
    act = _moe_up(xs, w_up.astype(BF16), tile_expert, nvalid)
    y = _moe_down(act, w_down.astype(BF16), tile_expert, nvalid)
    return _moe_combine(h, y, info, dest1, dest2)


def kernel(x, rel_bias, norm_gain, nsa_w_in, nsa_cmp_pe, nsa_cmp_w1, nsa_cmp_w2, nsa_q_gain, nsa_k_gain, nsa_w_out, sg_w_in, sg_v_gain, sg_w_s, sg_b_s, sg_w_out, gla_w_in, gla_w_gate2, gla_b_gate, gla_o_gain, gla_w_out, ffn_w_up, ffn_w_down, moe_router, moe_w_up, moe_w_down):
    bsz, s, d = x.shape
    tables = _nsa_tables(rel_bias)
    outs = []
    for bi in range(bsz):
        h = x[bi]
        for i in range(DEPTH):
            mixer = i % N_MIXERS
            j = i // N_MIXERS
            hn = _rmsnorm(h, norm_gain[i, 0])
            if mixer == 0:
                h = _nsa_mixer(h, hn, tables, nsa_w_in[j], nsa_cmp_pe[j], nsa_cmp_w1[j], nsa_cmp_w2[j],
                               nsa_q_gain[j], nsa_k_gain[j], nsa_w_out[j])
            elif mixer == 1:
                h = _sgu_mixer(h, hn, sg_w_in[j], sg_v_gain[j], sg_w_s[j], sg_b_s[j], sg_w_out[j])
            else:
                h = _gla_mixer(h, hn, gla_w_in[j], gla_w_gate2[j], gla_b_gate[j], gla_o_gain[j], gla_w_out[j])
            f = i // 2
            if i % 2 == 0:
                hn = _rmsnorm(h, norm_gain[i, 1])
                h = _dense_ffn(h, hn, ffn_w_up[f], ffn_w_down[f])
            else:
                h = _moe_ffn(h, norm_gain[i, 1], moe_router[f], moe_w_up[f], moe_w_down[f])
        outs.append(h)
    return jnp.stack(outs, axis=0)
```

```python
import math

import jax
import jax.numpy as jnp
from jax import lax
from jax.experimental import pallas as pl
from jax.experimental.pallas import tpu as pltpu

F32 = jnp.float32
BF16 = jnp.bfloat16

DEPTH = 4
N_MIXERS = 3
RMS_EPS = 1e-6
NEG = -1e30
HEAD_DIM = 128
NSA_KV_HEADS = 4
NSA_GROUP = 8
CMP_BLOCK = 32
CMP_STRIDE = 16
SEL_BLOCK = 64
SEL_TOPK = 16
WINDOW = 512
Q_BLOCK = 128
FORCED_SCORE = 1e4
T5_BUCKETS = 32
T5_MAX_DISTANCE = 2048
SG_CHUNK = 128
SG_GROUPS = 32
GLA_HEADS = 4
GLA_GATE_RANK = 16
GLA_GATE_TEMP = 16.0
GLA_CHUNK = 64
GLA_SUB = 16
GLA_SAFE_EXP = 80.0
MOE_EXPERTS = 8
MOE_TM = 512

LANES = 128
SEL_TILE = 512
SEL_NEAR_TILES = 5
SEL_FAR_WIDTH = 2
CMP_NEAR_SHIFT = 12
CW_HEADS = 4
VMEM_MB = 56
LOG2E = 1.4426950408889634
SAFE_LOGIT = 60.0
SEL_PAD_BLOCKS = 8


def _cparams(sem, vmem_mb=VMEM_MB):
    return pltpu.CompilerParams(dimension_semantics=sem, vmem_limit_bytes=vmem_mb * 2**20)


def _dot(a, b):
    return jnp.dot(a, b, preferred_element_type=F32)


def _dot_nt(a, b):
    return lax.dot_general(a, b, (((1,), (1,)), ((), ())), preferred_element_type=F32)


def _rmsnorm_body(x_ref, g_ref, o_ref):
    x = x_ref[...]
    ms = jnp.mean(x * x, axis=-1, keepdims=True)
    o_ref[...] = (x * lax.rsqrt(ms + RMS_EPS) * g_ref[...]).astype(o_ref.dtype)


def _rmsnorm(x, gain, tm=256):
    m, d = x.shape
    return pl.pallas_call(
        _rmsnorm_body,
        grid=(m // tm,),
        in_specs=[pl.BlockSpec((tm, d), lambda i: (i, 0)), pl.BlockSpec((1, d), lambda i: (0, 0))],
        out_specs=pl.BlockSpec((tm, d), lambda i: (i, 0)),
        out_shape=jax.ShapeDtypeStruct((m, d), BF16),
        compiler_params=_cparams(("parallel",)),
        name="rmsnorm",
    )(x, gain.reshape(1, d))


def _norm_rows(x, gain):
    ms = jnp.mean(x * x, axis=-1, keepdims=True)
    return (x * lax.rsqrt(ms + RMS_EPS) * gain).astype(BF16)


RT_E1, RT_E2, RT_W1, RT_W2, RT_R1, RT_R2 = range(6)


def _moe_router_body(x_ref, g_ref, r_ref, info_ref, cnt_ref, base_sc):
    @pl.when(pl.program_id(0) == 0)
    def _():
        base_sc[...] = jnp.zeros_like(base_sc)

    hn = _norm_rows(x_ref[...], g_ref[...])
    logits = _dot(hn, r_ref[...])
    lane = lax.broadcasted_iota(jnp.int32, logits.shape, 1).astype(F32)
    logits = jnp.where(lane < MOE_EXPERTS, logits, NEG)
    v1 = jnp.max(logits, axis=-1, keepdims=True)
    i1 = jnp.min(jnp.where(logits == v1, lane, 1e3), axis=-1, keepdims=True)
    rest = jnp.where(lane == i1, NEG, logits)
    v2 = jnp.max(rest, axis=-1, keepdims=True)
    i2 = jnp.min(jnp.where(rest == v2, lane, 1e3), axis=-1, keepdims=True)
    e2 = jnp.exp(v2 - v1)
    den = 1.0 + e2
    hot = jnp.where(jnp.logical_or(lane == i1, lane == i2), 1.0, 0.0)
    tm = hot.shape[0]
    earlier = (lax.broadcasted_iota(jnp.int32, (tm, tm), 0) > lax.broadcasted_iota(jnp.int32, (tm, tm), 1))
    before = _dot(jnp.where(earlier, 1.0, 0.0).astype(BF16), hot.astype(BF16)) + base_sc[...]
    r1 = jnp.sum(jnp.where(lane == i1, before, 0.0), axis=-1, keepdims=True)
    r2 = jnp.sum(jnp.where(lane == i2, before, 0.0), axis=-1, keepdims=True)
    rec = jnp.zeros_like(logits)
    for pos, val in ((RT_E1, i1), (RT_E2, i2), (RT_W1, 1.0 / den), (RT_W2, e2 / den), (RT_R1, r1), (RT_R2, r2)):
        rec = jnp.where(lane == float(pos), val, rec)
    info_ref[...] = rec
    total = base_sc[...] + jnp.sum(hot, axis=0, keepdims=True)
    base_sc[...] = total
    cnt_ref[...] = jnp.broadcast_to(total, cnt_ref.shape)


def _moe_router(x, gain, router, tm=256):
    m, d = x.shape
    rpad = jnp.zeros((d, LANES), BF16).at[:, :MOE_EXPERTS].set(router.astype(BF16))
    return pl.pallas_call(
        _moe_router_body,
        grid=(m // tm,),
        in_specs=[pl.BlockSpec((tm, d), lambda i: (i, 0)), pl.BlockSpec((1, d), lambda i: (0, 0)),
                  pl.BlockSpec((d, LANES), lambda i: (0, 0))],
        out_specs=[pl.BlockSpec((tm, LANES), lambda i: (i, 0)), pl.BlockSpec((8, LANES), lambda i: (0, 0))],
        out_shape=[jax.ShapeDtypeStruct((m, LANES), F32), jax.ShapeDtypeStruct((8, LANES), F32)],
        scratch_shapes=[pltpu.VMEM((1, LANES), F32)],
        compiler_params=_cparams(("arbitrary",)),
        name="moe_router",
    )(x, gain.reshape(1, d), rpad)


def _matmul(x, w_list, epilogue, out_list, *, tm, tn, n_total, tk=None, extras=(), name="matmul"):
    m, kdim = x.shape
    tk = kdim if tk is None else tk
    nk = kdim // tk
    nw, ne, no = len(w_list), len(extras), len(out_list)

    def body(*refs):
        x_ref = refs[0]
        w_refs = refs[1:1 + nw]
        e_refs = refs[1 + nw:1 + nw + ne]
        o_refs = refs[1 + nw + ne:1 + nw + ne + no]
        acc_refs = refs[1 + nw + ne + no:]
        n = pl.program_id(0)
        if nk == 1:
            xv = x_ref[...]
            epilogue([_dot(xv, w[...]) for w in w_refs], e_refs, o_refs, n)
        else:
            k = pl.program_id(2)

            @pl.when(k == 0)
            def _():
                for a in acc_refs:
                    a[...] = jnp.zeros_like(a)

            xv = x_ref[...]
            for a, w in zip(acc_refs, w_refs):
                a[...] += _dot(xv, w[...])

            @pl.when(k == nk - 1)
            def _():
                epilogue([a[...] for a in acc_refs], e_refs, o_refs, n)

    in_specs = [pl.BlockSpec((tm, tk), lambda n, mi, k: (mi, k))]
    in_specs += [s for _, s in w_list] + [s for _, s in extras]
    scratch = [] if nk == 1 else [pltpu.VMEM((tm, tn), F32) for _ in range(nw)]
    return pl.pallas_call(
        body,
        grid=(n_total // tn, m // tm, nk),
        in_specs=in_specs,
        out_specs=[s for _, s in out_list],
        out_shape=[s for s, _ in out_list],
        scratch_shapes=scratch,
        compiler_params=_cparams(("parallel", "parallel", "arbitrary")),
        name=name,
    )(x, *[a for a, _ in w_list], *[a for a, _ in extras])


def _wspec(tk, tn, off=0):
    return pl.BlockSpec((tk, tn), lambda n, mi, k: (k, n + off))


def _ospec(tm, tn):
    return pl.BlockSpec((tm, tn), lambda n, mi, k: (mi, n))


def _epi_resid(accs, e_refs, o_refs, n):
    o_refs[0][...] = e_refs[0][...] + accs[0]


def _matmul_resid(x, w, resid, *, tm=512, tn=1024, tk=None, name="matmul_resid"):
    m, kdim = x.shape
    nn = w.shape[1]
    tk = min(kdim, 4096) if tk is None else tk
    return _matmul(
        x, [(w, _wspec(tk, tn))], _epi_resid,
        [(jax.ShapeDtypeStruct((m, nn), F32), _ospec(tm, tn))],
        tm=tm, tn=tn, tk=tk, n_total=nn, extras=[(resid, _ospec(tm, tn))], name=name)[0]


def _silu(a):
    return a * (1.0 / (1.0 + jnp.exp(-a)))


def _epi_swiglu(accs, e_refs, o_refs, n):
    a, b = accs
    o_refs[0][...] = (_silu(a) * b).astype(BF16)


def _swiglu_up(x, w_up, *, tm=512, tn=512, name="swiglu_up"):
    m, kdim = x.shape
    ff = w_up.shape[1] // 2
    return _matmul(
        x, [(w_up, _wspec(kdim, tn)), (w_up, _wspec(kdim, tn, ff // tn))], _epi_swiglu,
        [(jax.ShapeDtypeStruct((m, ff), BF16), _ospec(tm, tn))],
        tm=tm, tn=tn, n_total=ff, name=name)[0]


def _moe_plan(info, cnt, s):
    ne, tm = MOE_EXPERTS, MOE_TM
    e1 = info[:, RT_E1].astype(jnp.int32)
    e2 = info[:, RT_E2].astype(jnp.int32)
    counts = cnt[0, :ne].astype(jnp.int32)
    padded = (counts + tm - 1) // tm * tm
    ends = jnp.cumsum(padded)
    off = ends - padded
    dest1 = off[e1] + info[:, RT_R1].astype(jnp.int32)
    dest2 = off[e2] + info[:, RT_R2].astype(jnp.int32)
    ntile = (2 * s) // tm + ne
    tile_start = jnp.arange(ntile, dtype=jnp.int32) * tm
    tile_expert = jnp.minimum(jnp.sum(tile_start[:, None] >= ends[None, :], axis=1), ne - 1).astype(jnp.int32)
    nvalid = (ends[ne - 1:ne] // tm).astype(jnp.int32)
    tok = jnp.arange(s, dtype=jnp.int32)
    src = jnp.zeros((ntile * tm,), jnp.int32).at[dest1].set(tok).at[dest2].set(tok)
    return dest1, dest2, src, tile_expert, nvalid


def _row_copy(src_hbm, row, dst, r, sem):
    return pltpu.make_async_copy(src_hbm.at[pl.ds(row, 1)], dst.at[pl.ds(r, 1)], sem)


def _moe_gather_body(src_ref, nv_ref, h_hbm, g_ref, o_ref, buf, sem):
    t = pl.program_id(0)
    tm = buf.shape[0]

    @pl.when(t < nv_ref[0])
    def _():
        def issue(r, carry):
            _row_copy(h_hbm, src_ref[t * tm + r], buf, r, sem).start()
            return carry

        lax.fori_loop(0, tm, issue, 0)

        def drain(r, carry):
            _row_copy(h_hbm, src_ref[t * tm + r], buf, r, sem).wait()
            return carry

        lax.fori_loop(0, tm, drain, 0)
        o_ref[...] = _norm_rows(buf[...], g_ref[...])

    @pl.when(t >= nv_ref[0])
    def _():
        o_ref[...] = jnp.zeros_like(o_ref)


def _moe_gather(h, gain, src, nvalid):
    s, d = h.shape
    tm = MOE_TM
    ntile = src.shape[0] // tm
    return pl.pallas_call(
        _moe_gather_body,
        grid_spec=pltpu.PrefetchScalarGridSpec(
            num_scalar_prefetch=2,
            grid=(ntile,),
            in_specs=[pl.BlockSpec(memory_space=pl.ANY),
                      pl.BlockSpec((1, d), lambda t, src_, nv_: (0, 0))],
            out_specs=pl.BlockSpec((tm, d), lambda t, src_, nv_: (t, 0)),
            scratch_shapes=[pltpu.VMEM((tm, d), F32), pltpu.SemaphoreType.DMA(())]),
        out_shape=jax.ShapeDtypeStruct((ntile * tm, d), BF16),
        compiler_params=_cparams(("arbitrary",)),
        name="moe_gather",
    )(src, nvalid, h, gain.reshape(1, d))


def _moe_up_body(te_ref, nv_ref, x_ref, wa_ref, wb_ref, o_ref):
    t = pl.program_id(1)

    @pl.when(t < nv_ref[0])
    def _():
        x = x_ref[...]
        o_ref[...] = (_silu(_dot(x, wa_ref[...])) * _dot(x, wb_ref[...])).astype(BF16)

    @pl.when(t >= nv_ref[0])
    def _():
        o_ref[...] = jnp.zeros_like(o_ref)


def _moe_up(xs, w_up, tile_expert, nvalid, *, tn=512):
    p, d = xs.shape
    tm = MOE_TM
    ff = w_up.shape[2] // 2
    nn = ff // tn
    return pl.pallas_call(
        _moe_up_body,
        grid_spec=pltpu.PrefetchScalarGridSpec(
            num_scalar_prefetch=2,
            grid=(nn, p // tm),
            in_specs=[pl.BlockSpec((tm, d), lambda n, t, te, nv: (t, 0)),
                      pl.BlockSpec((None, d, tn), lambda n, t, te, nv: (te[t], 0, n)),
                      pl.BlockSpec((None, d, tn), lambda n, t, te, nv: (te[t], 0, n + nn))],
            out_specs=pl.BlockSpec((tm, tn), lambda n, t, te, nv: (t, n))),
        out_shape=jax.ShapeDtypeStruct((p, ff), BF16),
        compiler_params=_cparams(("parallel", "arbitrary")),
        name="moe_up",
    )(tile_expert, nvalid, xs, w_up, w_up)


def _moe_down_body(te_ref, nv_ref, x_ref, w_ref, o_ref):
    t = pl.program_id(1)

    @pl.when(t < nv_ref[0])
    def _():
        o_ref[...] = _dot(x_ref[...], w_ref[...])

    @pl.when(t >= nv_ref[0])
    def _():
        o_ref[...] = jnp.zeros_like(o_ref)


def _moe_down(act, w_down, tile_expert, nvalid, *, tn=1024):
    p, ff = act.shape
    tm = MOE_TM
    d = w_down.shape[2]
    return pl.pallas_call(
        _moe_down_body,
        grid_spec=pltpu.PrefetchScalarGridSpec(
            num_scalar_prefetch=2,
            grid=(d // tn, p // tm),
            in_specs=[pl.BlockSpec((tm, ff), lambda n, t, te, nv: (t, 0)),
                      pl.BlockSpec((None, ff, tn), lambda n, t, te, nv: (te[t], 0, n))],
            out_specs=pl.BlockSpec((tm, tn), lambda n, t, te, nv: (t, n))),
        out_shape=jax.ShapeDtypeStruct((p, d), F32),
        compiler_params=_cparams(("parallel", "arbitrary")),
        name="moe_down",
    )(tile_expert, nvalid, act, w_down)


def _moe_combine_body(d1_ref, d2_ref, y_hbm, h_ref, info_ref, o_ref, buf, sems):
    t = pl.program_id(0)
    tm = h_ref.shape[0]

    def issue(r, carry):
        _row_copy(y_hbm, d1_ref[t * tm + r], buf.at[0], r, sems.at[0]).start()
        _row_copy(y_hbm, d2_ref[t * tm + r], buf.at[1], r, sems.at[1]).start()
        return carry

    lax.fori_loop(0, tm, issue, 0)

    def drain(r, carry):
        _row_copy(y_hbm, d1_ref[t * tm + r], buf.at[0], r, sems.at[0]).wait()
        _row_copy(y_hbm, d2_ref[t * tm + r], buf.at[1], r, sems.at[1]).wait()
        return carry

    lax.fori_loop(0, tm, drain, 0)
    info = info_ref[...]
    o_ref[...] = h_ref[...] + (info[:, RT_W1:RT_W1 + 1] * buf[0] + info[:, RT_W2:RT_W2 + 1] * buf[1])


def _moe_combine(h, y, info, dest1, dest2, tm=256):
    s, d = h.shape
    return pl.pallas_call(
        _moe_combine_body,
        grid_spec=pltpu.PrefetchScalarGridSpec(
            num_scalar_prefetch=2,
            grid=(s // tm,),
            in_specs=[pl.BlockSpec(memory_space=pl.ANY),
                      pl.BlockSpec((tm, d), lambda t, a, b: (t, 0)),
                      pl.BlockSpec((tm, LANES), lambda t, a, b: (t, 0))],
            out_specs=pl.BlockSpec((tm, d), lambda t, a, b: (t, 0)),
            scratch_shapes=[pltpu.VMEM((2, tm, d), F32), pltpu.SemaphoreType.DMA((2,))]),
        out_shape=jax.ShapeDtypeStruct((s, d), F32),
        compiler_params=_cparams(("arbitrary",)),
        name="moe_combine",
    )(dest1, dest2, y, h, info)


def _nsa_q_proj(hn, wq, q_gain, nn, *, tm=512, tn=1024):
    m, kdim = hn.shape
    scale = HEAD_DIM ** -0.5 * LOG2E

    def epi(accs, e_refs, o_refs, n):
        acc = accs[0]
        gain = e_refs[0][...]
        for c in range(tn // HEAD_DIM):
            a = acc[:, c * HEAD_DIM:(c + 1) * HEAD_DIM]
            ms = jnp.mean(a * a, axis=-1, keepdims=True)
            o_refs[0][:, c * HEAD_DIM:(c + 1) * HEAD_DIM] = (
                a * lax.rsqrt(ms + RMS_EPS) * gain * scale).astype(BF16)

    return _matmul(
        hn, [(wq, _wspec(kdim, tn))], epi,
        [(jax.ShapeDtypeStruct((m, nn), BF16), _ospec(tm, tn))],
        tm=tm, tn=tn, n_total=nn,
        extras=[(q_gain.reshape(1, HEAD_DIM), pl.BlockSpec((1, HEAD_DIM), lambda n, mi, k: (0, 0)))],
        name="nsa_q_proj")[0]


def _nsa_kv_proj(hn, wkv, k_gain, col0, *, tm=512):
    m, kdim = hn.shape
    g_ = NSA_KV_HEADS
    tn = g_ * HEAD_DIM
    gains = jnp.ones((6, 1, HEAD_DIM), F32).at[2, 0].set(k_gain[1]).at[4, 0].set(k_gain[2])

    def epi(accs, e_refs, o_refs, n):
        acc = accs[0]
        kg = e_refs[0][0]
        do_norm = jnp.logical_or(n == 2, n == 4)
        for g in range(g_):
            a = acc[:, g * HEAD_DIM:(g + 1) * HEAD_DIM]
            ms = jnp.mean(a * a, axis=-1, keepdims=True)
            an = a * lax.rsqrt(ms + RMS_EPS) * kg
            o_refs[0][g] = jnp.where(do_norm, an, a).astype(BF16)

    return _matmul(
        hn, [(wkv, _wspec(kdim, tn, col0 // tn))], epi,
        [(jax.ShapeDtypeStruct((6 * g_, m, HEAD_DIM), BF16),
          pl.BlockSpec((g_, tm, HEAD_DIM), lambda n, mi, k: (n, mi, 0)))],
        tm=tm, tn=tn, n_total=6 * tn,
        extras=[(gains, pl.BlockSpec((1, 1, HEAD_DIM), lambda n, mi, k: (n, 0, 0)))],
        name="nsa_kv_proj")[0]


def _nsa_gate_proj(hn, wg, *, tm=512):
    m, kdim = hn.shape
    nn = wg.shape[1]

    def epi(accs, e_refs, o_refs, n):
        o_refs[0][...] = 1.0 / (1.0 + jnp.exp(-accs[0]))

    return _matmul(
        hn, [(wg, _wspec(kdim, nn))], epi,
        [(jax.ShapeDtypeStruct((m, nn), F32), _ospec(tm, nn))],
        tm=tm, tn=nn, n_total=nn, name="nsa_gate_proj")[0]


def _compress_body(r_ref, pe_ref, w1_ref, w2_ref, kg_ref, o_ref, ot_ref):
    j = pl.program_id(0)
    r = r_ref[0].astype(F32)
    pe = pe_ref[0]
    half = r.shape[1]
    top = (r + pe[0:1]).astype(BF16)
    bot = (r + pe[1:2]).astype(BF16)
    a = _dot(top, w1_ref[0, 0:half, :])
    b = _dot(bot, w1_ref[0, half:2 * half, :])
    nrow = r.shape[0]
    hid = a + pltpu.roll(b, nrow - 1, 0)
    y = _dot(jax.nn.gelu(hid).astype(BF16), w2_ref[0])
    ms = jnp.mean(y * y, axis=-1, keepdims=True)
    yn = y * lax.rsqrt(ms + RMS_EPS) * kg_ref[...]
    out = jnp.where(j == 0, yn, y)
    o_ref[0, 0] = out.astype(BF16)
    ot_ref[0, 0] = jnp.transpose(out).astype(BF16)


def _compress(kv, pe, w1, w2, k_gain0):
    g_ = NSA_KV_HEADS
    s = kv.shape[1]
    nrow = s // CMP_STRIDE
    wid = CMP_STRIDE * HEAD_DIM
    r = kv.reshape(6 * g_, nrow, wid)
    pe2 = pe.reshape(2, 2, wid)
    return pl.pallas_call(
        _compress_body,
        grid=(2, g_),
        in_specs=[pl.BlockSpec((1, nrow, wid), lambda j, g: (j * g_ + g, 0, 0)),
                  pl.BlockSpec((1, 2, wid), lambda j, g: (j, 0, 0)),
                  pl.BlockSpec((1, 2 * wid, HEAD_DIM), lambda j, g: (j, 0, 0)),
                  pl.BlockSpec((1, HEAD_DIM, HEAD_DIM), lambda j, g: (j, 0, 0)),
                  pl.BlockSpec((1, HEAD_DIM), lambda j, g: (0, 0))],
        out_specs=[pl.BlockSpec((1, 1, nrow, HEAD_DIM), lambda j, g: (j, g, 0, 0)),
                   pl.BlockSpec((1, 1, HEAD_DIM, nrow), lambda j, g: (j, g, 0, 0))],
        out_shape=[jax.ShapeDtypeStruct((2, g_, nrow, HEAD_DIM), BF16),
                   jax.ShapeDtypeStruct((2, g_, HEAD_DIM, nrow), BF16)],
        compiler_params=_cparams(("parallel", "parallel")),
        name="nsa_compress",
    )(r, pe2, w1.astype(BF16), w2.astype(BF16), k_gain0.reshape(1, HEAD_DIM))


def _t5_bucket(dist):
    dist = jnp.maximum(dist, 0)
    max_exact = T5_BUCKETS // 2
    d_f = jnp.maximum(dist, 1).astype(F32)
    log_b = max_exact + (jnp.log(d_f / max_exact) / math.log(T5_MAX_DISTANCE / max_exact)
                         * (T5_BUCKETS - max_exact)).astype(jnp.int32)
    log_b = jnp.minimum(log_b, T5_BUCKETS - 1)
    return jnp.where(dist < max_exact, dist, log_b)


def _bias_table(tbl, dist, valid, shift):
    onehot = jax.nn.one_hot(_t5_bucket(dist).reshape(-1), T5_BUCKETS, dtype=F32)
    t = tbl - tbl[T5_BUCKETS - 1:T5_BUCKETS] if shift else tbl
    vals = jnp.einsum("nb,bh->hn", onehot, t, precision=lax.Precision.HIGHEST)
    vals = vals.reshape((tbl.shape[1],) + dist.shape) * LOG2E
    return jnp.where(valid[None], vals, NEG)


def _nsa_tables(rel_bias):
    tbl = rel_bias.astype(F32)
    r = jnp.arange(Q_BLOCK)
    jw = jnp.arange(WINDOW + Q_BLOCK)
    dw = WINDOW + r[None, :] - jw[:, None]
    tab_w = _bias_table(tbl, dw, (dw >= 0) & (dw < WINDOW), False)
    a = jnp.arange(16)
    cc = jnp.arange(2 * LANES)
    dc = (CMP_NEAR_SHIFT * Q_BLOCK + Q_BLOCK * a[:, None, None] + r[None, None, :]
          - CMP_STRIDE * cc[None, :, None] - (CMP_BLOCK - 1))
    tab_c = _bias_table(tbl, dc, dc >= 0, True)
    jj = jnp.arange(SEL_NEAR_TILES)
    col = jnp.arange(SEL_TILE)
    ds_ = r[None, None, :] - Q_BLOCK + SEL_TILE * (jj[:, None, None] + 1) - col[None, :, None]
    tab_s = _bias_table(tbl, ds_, ds_ >= 0, True)
    return tab_w, tab_c, tab_s, tbl


def _sel_to_cmp(nsel, nc_pad):
    c_start = jnp.arange(nc_pad) * CMP_STRIDE
    s_start = jnp.arange(nsel) * SEL_BLOCK
    overlap = jnp.clip(jnp.minimum(c_start[None, :] + CMP_BLOCK, s_start[:, None] + SEL_BLOCK)
                       - jnp.maximum(c_start[None, :], s_start[:, None]), 0)
    return (overlap.astype(F32) / CMP_BLOCK).astype(BF16)


def _heads_t(q_ref, h0, nh):
    cols = [jnp.transpose(q_ref[:, (h0 + h) * HEAD_DIM:(h0 + h + 1) * HEAD_DIM].astype(F32)) for h in range(nh)]
    return jnp.concatenate(cols, axis=1).astype(BF16)


def _softmax_keys(s, bounded):
    if bounded:
        p = jnp.exp2(s)
        l = jnp.sum(p, axis=0, keepdims=True)
        return p, jnp.where(l > 0.0, 1.0 / l, 0.0)
    m = jnp.max(s, axis=0, keepdims=True)
    p = jnp.exp2(s - m)
    l = jnp.sum(p, axis=0, keepdims=True)
    return p, jnp.where(m > 0.5 * NEG, 1.0 / l, 0.0)


def _logit_bound(rel_bias, q_gain, k_gain):
    qk = HEAD_DIM * (HEAD_DIM ** -0.5 * LOG2E) * jnp.max(jnp.abs(q_gain)) * jnp.max(jnp.abs(k_gain))
    return (1.02 * qk + 2.0 * LOG2E * jnp.max(jnp.abs(rel_bias))).astype(F32).reshape(1)


def _nsa_cw_body(bnd_ref, q_ref, gt_ref, kc_ref, vct_ref, mselt_ref, tabc_ref, kw_ref, vwt_ref, tabw_ref,
                 ocwt_ref, negt_ref):
    bounded = bnd_ref[0] < SAFE_LOGIT
    i = pl.program_id(1)

    @pl.when(bounded)
    def _():
        _nsa_cw_branches(True, i, q_ref, gt_ref, kc_ref, vct_ref, mselt_ref, tabc_ref, kw_ref, vwt_ref, tabw_ref,
                         ocwt_ref, negt_ref)

    @pl.when(jnp.logical_not(bounded))
    def _():
        _nsa_cw_branches(False, i, q_ref, gt_ref, kc_ref, vct_ref, mselt_ref, tabc_ref, kw_ref, vwt_ref, tabw_ref,
                         ocwt_ref, negt_ref)


def _nsa_cw_branches(bounded, i, q_ref, gt_ref, kc_ref, vct_ref, mselt_ref, tabc_ref, kw_ref, vwt_ref, tabw_ref,
                     ocwt_ref, negt_ref):
    f = (i + 16 - CMP_NEAR_SHIFT) // 16 - 1
    kc = kc_ref[0, 0]
    vct = vct_ref[0, 0]
    nc = kc.shape[0]
    nsel = mselt_ref.shape[0]
    wlen = WINDOW + Q_BLOCK
    nslab = wlen // LANES
    start = pl.multiple_of(i * Q_BLOCK, Q_BLOCK)
    kw = kw_ref[0, pl.ds(start, wlen), :]
    vwt = jnp.concatenate([vwt_ref[0, i + c] for c in range(nslab)], axis=1)
    roww = lax.broadcasted_iota(jnp.int32, (wlen, LANES), 0)
    w_pad = roww < (WINDOW - Q_BLOCK * i)
    gt_t = jnp.transpose(gt_ref[...])
    psum = jnp.zeros((nc, Q_BLOCK), F32)
    hg = CW_HEADS
    for h0 in range(0, NSA_GROUP, hg):
        qt = _heads_t(q_ref, h0, hg)
        s = _dot(kc, qt)
        ta = jnp.concatenate([tabc_ref[h0 + h, 0, 0:LANES, :] for h in range(hg)], axis=1)
        tb = jnp.concatenate([tabc_ref[h0 + h, 0, LANES:2 * LANES, :] for h in range(hg)], axis=1)
        pieces = []
        for ch in range(nc // LANES):
            rest = jnp.where(ch > f + 1, NEG, 0.0)
            bias = jnp.where(ch == f, ta, jnp.where(ch == f + 1, tb, rest))
            pieces.append(s[ch * LANES:(ch + 1) * LANES] + bias)
        s = jnp.concatenate(pieces, axis=0)
        p, linv = _softmax_keys(s, bounded)
        pn = p * linv
        for h in range(hg):
            psum = psum + pn[:, h * Q_BLOCK:(h + 1) * Q_BLOCK]
        oc = _dot(vct, pn.astype(BF16))
        tw = jnp.concatenate([jnp.where(w_pad, NEG, tabw_ref[h0 + h]) for h in range(hg)], axis=1)
        sw = _dot(kw, qt) + tw
        pw, lwinv = _softmax_keys(sw, bounded)
        ow = _dot(vwt, pw.astype(BF16)) * lwinv
        for h in range(hg):
            hh = h0 + h
            cs = slice(h * Q_BLOCK, (h + 1) * Q_BLOCK)
            ocwt_ref[0, 0, hh * HEAD_DIM:(hh + 1) * HEAD_DIM, :] = (
                gt_t[hh:hh + 1] * oc[:, cs] + gt_t[16 + hh:17 + hh] * ow[:, cs])
    p_hi = psum.astype(BF16)
    p_lo = (psum - p_hi.astype(F32)).astype(BF16)
    mselt = mselt_ref[...]
    imp = _dot(mselt, p_hi) + _dot(mselt, p_lo)
    blk = lax.broadcasted_iota(jnp.int32, (nsel, Q_BLOCK), 0).astype(F32)
    qpos = lax.broadcasted_iota(jnp.int32, (nsel, Q_BLOCK), 1)
    cur = (i * (Q_BLOCK // SEL_BLOCK)).astype(F32) + jnp.where(qpos >= SEL_BLOCK, 1.0, 0.0)
    forced = jnp.logical_or(blk == cur, blk == 0.0)
    val = jnp.where(forced, FORCED_SCORE, jnp.where(blk <= cur, imp, -1.0))
    chosen = jnp.zeros((nsel, Q_BLOCK), F32)
    for _ in range(SEL_TOPK):
        mx = jnp.max(val, axis=0, keepdims=True)
        first = jnp.min(jnp.where(val == mx, blk, 1e4), axis=0, keepdims=True)
        hit = blk == first
        chosen = jnp.where(hit, 1.0, chosen)
        val = jnp.where(hit, -2.0, val)
    neg = jnp.where(chosen > 0.0, 0.0, NEG)
    for pr in range(SEL_PAD_BLOCKS // 2):
        negt_ref[0, 0, pr] = jnp.full((2, Q_BLOCK), NEG, F32)
    for pr in range(nsel // 2):
        negt_ref[0, 0, SEL_PAD_BLOCKS // 2 + pr] = neg[2 * pr:2 * pr + 2]


def _nsa_cw(bound, q, gates, kc, vct, mselt, tab_c, kw, vwt, tab_w):
    s, hd = q.shape
    g_ = NSA_KV_HEADS
    nq = s // Q_BLOCK
    gw = NSA_GROUP * HEAD_DIM
    nc = kc.shape[2]
    nsel = mselt.shape[0]
    spad = kw.shape[1]
    nslab = vwt.shape[1]
    wlen = WINDOW + Q_BLOCK

    def var(i):
        return (i + 16 - CMP_NEAR_SHIFT) % 16

    return pl.pallas_call(
        _nsa_cw_body,
        grid=(g_, nq),
        in_specs=[
            pl.BlockSpec(memory_space=pltpu.SMEM),
            pl.BlockSpec((Q_BLOCK, gw), lambda g, i: (i, g)),
            pl.BlockSpec((Q_BLOCK, LANES), lambda g, i: (i, g)),
            pl.BlockSpec((1, 1, nc, HEAD_DIM), lambda g, i: (0, g, 0, 0)),
            pl.BlockSpec((1, 1, HEAD_DIM, nc), lambda g, i: (1, g, 0, 0)),
            pl.BlockSpec((nsel, nc), lambda g, i: (0, 0)),
            pl.BlockSpec((NSA_GROUP, 1, 2 * LANES, Q_BLOCK), lambda g, i: (g, var(i), 0, 0)),
            pl.BlockSpec((1, spad, HEAD_DIM), lambda g, i: (g, 0, 0)),
            pl.BlockSpec((1, nslab, HEAD_DIM, LANES), lambda g, i: (g, 0, 0, 0)),
            pl.BlockSpec((NSA_GROUP, wlen, Q_BLOCK), lambda g, i: (g, 0, 0)),
        ],
        out_specs=[pl.BlockSpec((1, 1, gw, Q_BLOCK), lambda g, i: (g, i, 0, 0)),
                   pl.BlockSpec((1, 1, (SEL_PAD_BLOCKS + nsel) // 2, 2, Q_BLOCK), lambda g, i: (g, i, 0, 0, 0))],
        out_shape=[jax.ShapeDtypeStruct((g_, nq, gw, Q_BLOCK), F32),
                   jax.ShapeDtypeStruct((g_, nq, (SEL_PAD_BLOCKS + nsel) // 2, 2, Q_BLOCK), F32)],
        compiler_params=_cparams(("parallel", "arbitrary")),
        name="nsa_cmp_win",
    )(bound, q, gates, kc, vct, mselt, tab_c, kw, vwt, tab_w)


def _nsa_sel_body(bnd_ref, q_ref, gt_ref, ks_ref, vst_ref, negp_ref, tabs_ref, ocwt_ref, o_ref,
                  m_sc, l_sc, acc_sc):
    i = pl.program_id(1)
    ntile = i // (SEL_TILE // Q_BLOCK) + 1
    nh = NSA_GROUP
    spt = SEL_TILE // LANES
    ppt = SEL_TILE // (2 * SEL_BLOCK)
    qt = _heads_t(q_ref, 0, nh)

    def tile(jj, near, bounded, nsub=1):
        slab0 = (i + 1) - spt * jj
        row0 = pl.multiple_of(slab0 * LANES, LANES)
        kt = ks_ref[0, pl.ds(row0, nsub * SEL_TILE), :]
        vtt = jnp.concatenate([vst_ref[0, slab0 + c] for c in range(nsub * spt)], axis=1)
        rows = []
        for c in range(nsub * ppt):
            pair = negp_ref[0, 0, slab0 + c]
            rows += [jnp.broadcast_to(pair[r:r + 1], (SEL_BLOCK, Q_BLOCK)) for r in range(2)]
        mk = jnp.concatenate(rows, axis=0)
        if near:
            bias = jnp.concatenate([tabs_ref[h, jj] + mk for h in range(nh)], axis=1)
        else:
            bias = jnp.tile(mk, (1, nh))
        sc = _dot(kt, qt) + bias
        if bounded:
            p = jnp.exp2(sc)
            l_sc[...] = l_sc[...] + jnp.sum(p, axis=0, keepdims=True)
            acc_sc[...] = acc_sc[...] + _dot(vtt, p.astype(BF16))
        else:
            m_old = m_sc[...]
            m_new = jnp.maximum(m_old, jnp.max(sc, axis=0, keepdims=True))
            alpha = jnp.exp2(m_old - m_new)
            p = jnp.exp2(sc - m_new)
            l_sc[...] = alpha * l_sc[...] + jnp.sum(p, axis=0, keepdims=True)
            acc_sc[...] = alpha * acc_sc[...] + _dot(vtt, p.astype(BF16))
            m_sc[...] = m_new

    def run(bounded):
        m_sc[...] = jnp.full(m_sc.shape, NEG, F32)
        l_sc[...] = jnp.zeros(l_sc.shape, F32)
        acc_sc[...] = jnp.zeros(acc_sc.shape, F32)
        for jj in range(SEL_NEAR_TILES):
            @pl.when(jj < ntile)
            def _():
                tile(jj, True, bounded)

        nfar = jnp.maximum(ntile - SEL_NEAR_TILES, 0)

        def far(k, carry):
            tile(SEL_NEAR_TILES + SEL_FAR_WIDTH * k + SEL_FAR_WIDTH - 1, False, bounded, SEL_FAR_WIDTH)
            return carry

        lax.fori_loop(0, nfar // SEL_FAR_WIDTH, far, 0)

        def far_rest(jj, carry):
            tile(jj, False, bounded)
            return carry

        lax.fori_loop(SEL_NEAR_TILES + nfar // SEL_FAR_WIDTH * SEL_FAR_WIDTH, ntile, far_rest, 0)

    bounded = bnd_ref[0] < SAFE_LOGIT

    @pl.when(bounded)
    def _():
        run(True)

    @pl.when(jnp.logical_not(bounded))
    def _():
        run(False)

    o_t = acc_sc[...] * (1.0 / l_sc[...])
    gt_t = jnp.transpose(gt_ref[...])
    for h in range(nh):
        hs = slice(h * HEAD_DIM, (h + 1) * HEAD_DIM)
        oh = ocwt_ref[0, 0, hs, :] + gt_t[8 + h:9 + h] * o_t[:, h * Q_BLOCK:(h + 1) * Q_BLOCK]
        o_ref[:, hs] = jnp.transpose(oh).astype(BF16)


def _nsa_sel(bound, q, gates, ks, vst, negp, tab_s, ocwt):
    s, hd = q.shape
    g_ = NSA_KV_HEADS
    nq = s // Q_BLOCK
    gw = NSA_GROUP * HEAD_DIM
    spad = ks.shape[1]
    nslab = vst.shape[1]
    npair = negp.shape[2]
    return pl.pallas_call(
        _nsa_sel_body,
        grid=(g_, nq),
        in_specs=[
            pl.BlockSpec(memory_space=pltpu.SMEM),
            pl.BlockSpec((Q_BLOCK, gw), lambda g, i: (i, g)),
            pl.BlockSpec((Q_BLOCK, LANES), lambda g, i: (i, g)),
            pl.BlockSpec((1, spad, HEAD_DIM), lambda g, i: (g, 0, 0)),
            pl.BlockSpec((1, nslab, HEAD_DIM, LANES), lambda g, i: (g, 0, 0, 0)),
            pl.BlockSpec((1, 1, npair, 2, Q_BLOCK), lambda g, i: (g, i, 0, 0, 0)),
            pl.BlockSpec((NSA_GROUP, SEL_NEAR_TILES, SEL_TILE, Q_BLOCK), lambda g, i: (g, 0, 0, 0)),
            pl.BlockSpec((1, 1, gw, Q_BLOCK), lambda g, i: (g, i, 0, 0)),
        ],
        out_specs=pl.BlockSpec((Q_BLOCK, gw), lambda g, i: (i, g)),
        out_shape=jax.ShapeDtypeStruct((s, hd), BF16),
        scratch_shapes=[pltpu.VMEM((1, NSA_GROUP * Q_BLOCK), F32), pltpu.VMEM((1, NSA_GROUP * Q_BLOCK), F32),
                        pltpu.VMEM((HEAD_DIM, NSA_GROUP * Q_BLOCK), F32)],
        compiler_params=_cparams(("parallel", "arbitrary")),
        name="nsa_selected",
    )(bound, q, gates, ks, vst, negp, tab_s, ocwt)


def _pad_keys(x, pad):
    xp = jnp.pad(x, ((0, 0), (pad, 0), (0, 0)))
    g_, sp, dh = xp.shape
    return xp, xp.reshape(g_, sp // LANES, LANES, dh).swapaxes(-1, -2)


def _nsa_mixer(h, hn, tables, w_in, cmp_pe, cmp_w1, cmp_w2, q_gain, k_gain, w_out):
    s, d = h.shape
    g_, hpg, dh = NSA_KV_HEADS, NSA_GROUP, HEAD_DIM
    hd = g_ * hpg * dh
    tab_w, tab_c, tab_s, tbl = tables
    w_in = w_in.astype(BF16)
    q = _nsa_q_proj(hn, w_in, q_gain, hd)
    kv = _nsa_kv_proj(hn, w_in, k_gain, hd)
    wg = w_in[:, hd + 6 * g_ * dh:].reshape(d, g_, hpg, 3).transpose(0, 1, 3, 2).reshape(d, g_, 3 * hpg)
    wg = jnp.pad(wg, ((0, 0), (0, 0), (0, LANES - 3 * hpg))).reshape(d, g_ * LANES)
    gates = _nsa_gate_proj(hn, wg)
    kc, kct = _compress(kv, cmp_pe, cmp_w1, cmp_w2, k_gain[0])
    mselt = _sel_to_cmp(s // SEL_BLOCK, s // CMP_STRIDE)
    ks, _ = _pad_keys(kv[2 * g_:3 * g_], SEL_TILE)
    _, vst = _pad_keys(kv[3 * g_:4 * g_], SEL_TILE)
    kw, _ = _pad_keys(kv[4 * g_:5 * g_], WINDOW)
    _, vwt = _pad_keys(kv[5 * g_:6 * g_], WINDOW)
    bound = _logit_bound(tbl, q_gain, k_gain)
    ocwt, negp = _nsa_cw(bound, q, gates, kc, kct, mselt, tab_c, kw, vwt, tab_w)
    o = _nsa_sel(bound, q, gates, ks, vst, negp, tab_s, ocwt)
    return _matmul_resid(o, w_out.astype(BF16), h, name="nsa_out")


def _gelu_proj(hn, w, *, tm=512, tn=1024):
    m, kdim = hn.shape
    nn = w.shape[1]

    def epi(accs, e_refs, o_refs, n):
        o_refs[0][...] = jax.nn.gelu(accs[0])

    return _matmul(
        hn, [(w, _wspec(kdim, tn))], epi,
        [(jax.ShapeDtypeStruct((m, nn), F32), _ospec(tm, tn))],
        tm=tm, tn=tn, n_total=nn, name="sgu_in")[0]


def _sgu_mix_body(u_ref, v_ref, gain_ref, w_ref, bt_ref, o_ref):
    v = v_ref[...]
    ms = jnp.mean(v * v, axis=-1, keepdims=True)
    vn = (v * lax.rsqrt(ms + RMS_EPS) * gain_ref[...]).astype(BF16)
    t = w_ref.shape[1]
    causal = (lax.broadcasted_iota(jnp.int32, (t, t), 0) >= lax.broadcasted_iota(jnp.int32, (t, t), 1))
    bt = bt_ref[...]
    gd = vn.shape[1] // SG_GROUPS
    for g in range(SG_GROUPS):
        gs = slice(g * gd, (g + 1) * gd)
        w = jnp.where(causal, w_ref[g], 0.0).astype(BF16)
        mixed = _dot(w, vn[:, gs]) + bt[:, g:g + 1]
        o_ref[:, gs] = (u_ref[:, gs] * mixed).astype(BF16)


def _sgu_mix(uv, v_gain, w_s, b_s):
    s = uv.shape[0]
    wd = uv.shape[1] // 2
    t = SG_CHUNK
    return pl.pallas_call(
        _sgu_mix_body,
        grid=(s // t,),
        in_specs=[pl.BlockSpec((t, wd), lambda c: (c, 0)),
                  pl.BlockSpec((t, wd), lambda c: (c, 1)),
                  pl.BlockSpec((1, wd), lambda c: (0, 0)),
                  pl.BlockSpec((SG_GROUPS, t, t), lambda c: (0, 0, 0)),
                  pl.BlockSpec((t, SG_GROUPS), lambda c: (0, 0))],
        out_specs=pl.BlockSpec((t, wd), lambda c: (c, 0)),
        out_shape=jax.ShapeDtypeStruct((s, wd), BF16),
        compiler_params=_cparams(("parallel",)),
        name="sgu_mix",
    )(uv, uv, v_gain.reshape(1, wd), w_s, b_s.T)


def _sgu_mixer(h, hn, w_in, v_gain, w_s, b_s, w_out):
    uv = _gelu_proj(hn, w_in.astype(BF16))
    y = _sgu_mix(uv, v_gain, w_s, b_s)
    return _matmul_resid(y, w_out.astype(BF16), h, name="sgu_out")


def _gla_body(q_ref, k_ref, v_ref, r_ref, g1_ref, wg_ref, bg_ref, og_ref, o_ref, state_sc, diag_sc):
    c = pl.program_id(1)

    @pl.when(c == 0)
    def _():
        state_sc[...] = jnp.zeros_like(state_sc)

    ch = q_ref.shape[0]
    dk = q_ref.shape[1]
    x = _dot(g1_ref[...].astype(BF16), wg_ref[...]) + bg_ref[...]
    log_a = (jnp.minimum(x, 0.0) - jnp.log(1.0 + jnp.exp(-jnp.abs(x)))) * (1.0 / GLA_GATE_TEMP)
    tri = (lax.broadcasted_iota(jnp.int32, (ch, ch), 0)
           >= lax.broadcasted_iota(jnp.int32, (ch, ch), 1))
    tri_b = jnp.where(tri, 1.0, 0.0).astype(BF16)
    a_hi = log_a.astype(BF16)
    a_lo = (log_a - a_hi.astype(F32)).astype(BF16)
    b = _dot(tri_b, a_hi) + _dot(tri_b, a_lo)
    q = q_ref[...].astype(F32) * (dk ** -0.5)
    k = k_ref[...].astype(F32)
    v = v_ref[...]
    state = state_sc[...]
    o = _dot((q * jnp.exp(b)).astype(BF16), state.astype(BF16))
    nsub = ch // GLA_SUB
    ends = [jnp.broadcast_to(b[(jb + 1) * GLA_SUB - 1:(jb + 1) * GLA_SUB], (GLA_SUB, dk))
            for jb in range(nsub)]
    b_end = jnp.concatenate(ends, axis=0)
    b_start = jnp.concatenate([jnp.zeros((GLA_SUB, dk), F32)] + ends[:-1], axis=0)
    rblk = lax.broadcasted_iota(jnp.int32, (ch, ch), 0) // GLA_SUB
    cblk = lax.broadcasted_iota(jnp.int32, (ch, ch), 1) // GLA_SUB
    spread = jnp.max(b_start - b_end)

    @pl.when(spread < GLA_SAFE_EXP)
    def _():
        q_diag = (q * jnp.exp(b - b_start)).astype(BF16)
        k_diag = (k * jnp.exp(b_start - b)).astype(BF16)
        diag_sc[...] = _dot_nt(q_diag, k_diag)

    @pl.when(spread >= GLA_SAFE_EXP)
    def _():
        lane = lax.broadcasted_iota(jnp.int32, (GLA_SUB, ch), 1)
        strips = []
        for ib in range(nsub):
            rs = slice(ib * GLA_SUB, (ib + 1) * GLA_SUB)
            qi, ki, bi = q[rs], k[rs], b[rs]
            strip = jnp.zeros((GLA_SUB, ch), F32)
            for j in range(GLA_SUB):
                decay = jnp.exp(jnp.minimum(bi - bi[j:j + 1], 0.0))
                col = jnp.sum(qi * ki[j:j + 1] * decay, axis=-1, keepdims=True)
                strip = jnp.where(lane == ib * GLA_SUB + j, col, strip)
            strips.append(strip)
        diag_sc[...] = jnp.concatenate(strips, axis=0)

    attn = jnp.where(jnp.logical_and(tri, rblk == cblk), diag_sc[...], 0.0)
    k_hat = (k * jnp.exp(b_end - b)).astype(BF16)
    for jb in range(nsub - 1):
        q_hat = (q * jnp.exp(jnp.minimum(b - ends[jb][0:1], 0.0))).astype(BF16)
        attn = attn + jnp.where(jnp.logical_and(cblk == jb, rblk > jb), _dot_nt(q_hat, k_hat), 0.0)
    o = o + _dot(attn.astype(BF16), v)
    b_last = b[ch - 1:ch]
    k_dec = k * jnp.exp(b_last - b)
    k_dec_t = jnp.transpose(k_dec).astype(BF16)
    decay_t = jnp.transpose(jnp.broadcast_to(jnp.exp(b_last), (LANES, dk)))
    state_sc[...] = decay_t[:, 0:1] * state + _dot(k_dec_t, v)
    ms = jnp.mean(o * o, axis=-1, keepdims=True)
    on = o * lax.rsqrt(ms + RMS_EPS) * og_ref[...]
    r = r_ref[...].astype(F32)
    o_ref[...] = (on * _silu(r)).astype(BF16)


def _gla_mixer(h, hn, w_in, w_gate2, b_gate, o_gain, w_out):
    s, d = h.shape
    nh = GLA_HEADS
    dk = d // 2
    dv = d
    dkh, dvh = dk // nh, dv // nh
    w_in = w_in.astype(BF16)
    nmain = 2 * dk + 2 * dv

    def epi_bf16(accs, e_refs, o_refs, n):
        o_refs[0][...] = accs[0].astype(BF16)

    def epi_f32(accs, e_refs, o_refs, n):
        o_refs[0][...] = accs[0]

    tm, tn = 512, 1024
    proj = _matmul(hn, [(w_in, _wspec(d, tn))], epi_bf16,
                   [(jax.ShapeDtypeStruct((s, nmain), BF16), _ospec(tm, tn))],
                   tm=tm, tn=tn, n_total=nmain, name="gla_in")[0]
    wg1 = jnp.pad(w_in[:, nmain:], ((0, 0), (0, LANES - GLA_GATE_RANK)))
    g1 = _matmul(hn, [(wg1, _wspec(d, LANES))], epi_f32,
                 [(jax.ShapeDtypeStruct((s, LANES), F32), _ospec(tm, LANES))],
                 tm=tm, tn=LANES, n_total=LANES, name="gla_gate_in")[0]
    wg2 = jnp.pad(w_gate2.astype(BF16), ((0, LANES - GLA_GATE_RANK), (0, 0)))
    ch = GLA_CHUNK
    o = pl.pallas_call(
        _gla_body,
        grid=(nh, s // ch),
        in_specs=[
            pl.BlockSpec((ch, dkh), lambda hh, c: (c, hh)),
            pl.BlockSpec((ch, dkh), lambda hh, c: (c, nh + hh)),
            pl.BlockSpec((ch, dvh), lambda hh, c: (c, 2 * dk // dvh + hh)),
            pl.BlockSpec((ch, dvh), lambda hh, c: (c, (2 * dk + dv) // dvh + hh)),
            pl.BlockSpec((ch, LANES), lambda hh, c: (c, 0)),
            pl.BlockSpec((LANES, dkh), lambda hh, c: (0, hh)),
            pl.BlockSpec((1, dkh), lambda hh, c: (0, hh)),
            pl.BlockSpec((1, dvh), lambda hh, c: (0, 0)),
        ],
        out_specs=pl.BlockSpec((ch, dvh), lambda hh, c: (c, hh)),
        out_shape=jax.ShapeDtypeStruct((s, dv), BF16),
        scratch_shapes=[pltpu.VMEM((dkh, dvh), F32), pltpu.VMEM((ch, ch), F32)],
        compiler_params=_cparams(("parallel", "arbitrary")),
        name="gla_scan",
    )(proj, proj, proj, proj, g1, wg2, b_gate.reshape(1, dk), o_gain.reshape(1, dvh))
    return _matmul_resid(o, w_out.astype(BF16), h, name="gla_out")


def _dense_ffn(h, hn, w_up, w_down):
    act = _swiglu_up(hn, w_up.astype(BF16))
    return _matmul_resid(act, w_down.astype(BF16), h, name="ffn_down")


def _moe_ffn(h, gain, router, w_up, w_down):
    s, _ = h.shape
    info, cnt = _moe_router(h, gain, router)
    dest1, dest2, src, tile_expert, nvalid = _moe_plan(info, cnt, s)
    xs = _moe_gather(h, gain, src, nvalid)
    act = _moe_up(xs, w_up.astype(BF16), tile_expert, nvalid)
    y = _moe_down(act, w_down.astype(BF16), tile_expert, nvalid)
    return _moe_combine(h, y, info, dest1, dest2)


def kernel(x, rel_bias, norm_gain, nsa_w_in, nsa_cmp_pe, nsa_cmp_w1, nsa_cmp_w2, nsa_q_gain, nsa_k_gain, nsa_w_out, sg_w_in, sg_v_gain, sg_w_s, sg_b_s, sg_w_out, gla_w_in, gla_w_gate2, gla_b_gate, gla_o_gain, gla_w_out, ffn_w_up, ffn_w_down, moe_router, moe_w_up, moe_w_down):
    bsz, s, d = x.shape
    tables = _nsa_tables(rel_bias)
    outs = []
    for bi in range(bsz):
        h = x[bi]
        for i in range(DEPTH):
            mixer = i % N_MIXERS
            j = i // N_MIXERS
            hn = _rmsnorm(h, norm_gain[i, 0])
            if mixer == 0:
                h = _nsa_mixer(h, hn, tables, nsa_w_in[j], nsa_cmp_pe[j], nsa_cmp_w1[j], nsa_cmp_w2[j],
                               nsa_q_gain[j], nsa_k_gain[j], nsa_w_out[j])
            elif mixer == 1:
                h = _sgu_mixer(h, hn, sg_w_in[j], sg_v_gain[j], sg_w_s[j], sg_b_s[j], sg_w_out[j])
            else:
                h = _gla_mixer(h, hn, gla_w_in[j], gla_w_gate2[j], gla_b_gate[j], gla_o_gain[j], gla_w_out[j])
            f = i // 2
            if i % 2 == 0:
                hn = _rmsnorm(h, norm_gain[i, 1])
                h = _dense_ffn(h, hn, ffn_w_up[f], ffn_w_down[f])
            else:
                h = _moe_ffn(h, norm_gain[i, 1], moe_router[f], moe_w_up[f], moe_w_down[f])
        outs.append(h)
    return jnp.stack(outs, axis=0)
```

```python
import math

import jax
import jax.numpy as jnp
from jax import lax
from jax.experimental import pallas as pl
from jax.experimental.pallas import tpu as pltpu

F32 = jnp.float32
BF16 = jnp.bfloat16

DEPTH = 4
N_MIXERS = 3
RMS_EPS = 1e-6
NEG = -1e30
HEAD_DIM = 128
NSA_KV_HEADS = 4
NSA_GROUP = 8
CMP_BLOCK = 32
CMP_STRIDE = 16
SEL_BLOCK = 64
SEL_TOPK = 16
WINDOW = 512
Q_BLOCK = 128
FORCED_SCORE = 1e4
T5_BUCKETS = 32
T5_MAX_DISTANCE = 2048
SG_CHUNK = 128
SG_GROUPS = 32
GLA_HEADS = 4
GLA_GATE_RANK = 16
GLA_GATE_TEMP = 16.0
GLA_CHUNK = 64
GLA_SUB = 16
GLA_SAFE_EXP = 80.0
MOE_EXPERTS = 8
MOE_TM = 512

LANES = 128
SEL_TILE = 512
SEL_NEAR_TILES = 5
SEL_FAR_WIDTH = 2
CMP_NEAR_SHIFT = 12
CW_HEADS = 4
VMEM_MB = 56
LOG2E = 1.4426950408889634
SAFE_LOGIT = 60.0
SEL_PAD_BLOCKS = 8


def _cparams(sem, vmem_mb=VMEM_MB):
    return pltpu.CompilerParams(dimension_semantics=sem, vmem_limit_bytes=vmem_mb * 2**20)


def _dot(a, b):
    return jnp.dot(a, b, preferred_element_type=F32)


def _dot_nt(a, b):
    return lax.dot_general(a, b, (((1,), (1,)), ((), ())), preferred_element_type=F32)


def _rmsnorm_body(x_ref, g_ref, o_ref):
    x = x_ref[...]
    ms = jnp.mean(x * x, axis=-1, keepdims=True)
    o_ref[...] = (x * lax.rsqrt(ms + RMS_EPS) * g_ref[...]).astype(o_ref.dtype)


def _rmsnorm(x, gain, tm=256):
    m, d = x.shape
    return pl.pallas_call(
        _rmsnorm_body,
        grid=(m // tm,),
        in_specs=[pl.BlockSpec((tm, d), lambda i: (i, 0)), pl.BlockSpec((1, d), lambda i: (0, 0))],
        out_specs=pl.BlockSpec((tm, d), lambda i: (i, 0)),
        out_shape=jax.ShapeDtypeStruct((m, d), BF16),
        compiler_params=_cparams(("parallel",)),
        name="rmsnorm",
    )(x, gain.reshape(1, d))


def _norm_rows(x, gain):
    ms = jnp.mean(x * x, axis=-1, keepdims=True)
    return (x * lax.rsqrt(ms + RMS_EPS) * gain).astype(BF16)


RT_E1, RT_E2, RT_W1, RT_W2, RT_R1, RT_R2 = range(6)


def _moe_router_body(x_ref, g_ref, r_ref, info_ref, cnt_ref, base_sc):
    @pl.when(pl.program_id(0) == 0)
    def _():
        base_sc[...] = jnp.zeros_like(base_sc)

    hn = _norm_rows(x_ref[...], g_ref[...])
    logits = _dot(hn, r_ref[...])
    lane = lax.broadcasted_iota(jnp.int32, logits.shape, 1).astype(F32)
    logits = jnp.where(lane < MOE_EXPERTS, logits, NEG)
    v1 = jnp.max(logits, axis=-1, keepdims=True)
    i1 = jnp.min(jnp.where(logits == v1, lane, 1e3), axis=-1, keepdims=True)
    rest = jnp.where(lane == i1, NEG, logits)
    v2 = jnp.max(rest, axis=-1, keepdims=True)
    i2 = jnp.min(jnp.where(rest == v2, lane, 1e3), axis=-1, keepdims=True)
    e2 = jnp.exp(v2 - v1)
    den = 1.0 + e2
    hot = jnp.where(jnp.logical_or(lane == i1, lane == i2), 1.0, 0.0)
    tm = hot.shape[0]
    earlier = (lax.broadcasted_iota(jnp.int32, (tm, tm), 0) > lax.broadcasted_iota(jnp.int32, (tm, tm), 1))
    before = _dot(jnp.where(earlier, 1.0, 0.0).astype(BF16), hot.astype(BF16)) + base_sc[...]
    r1 = jnp.sum(jnp.where(lane == i1, before, 0.0), axis=-1, keepdims=True)
    r2 = jnp.sum(jnp.where(lane == i2, before, 0.0), axis=-1, keepdims=True)
    rec = jnp.zeros_like(logits)
    for pos, val in ((RT_E1, i1), (RT_E2, i2), (RT_W1, 1.0 / den), (RT_W2, e2 / den), (RT_R1, r1), (RT_R2, r2)):
        rec = jnp.where(lane == float(pos), val, rec)
    info_ref[...] = rec
    total = base_sc[...] + jnp.sum(hot, axis=0, keepdims=True)
    base_sc[...] = total
    cnt_ref[...] = jnp.broadcast_to(total, cnt_ref.shape)


def _moe_router(x, gain, router, tm=256):
    m, d = x.shape
    rpad = jnp.zeros((d, LANES), BF16).at[:, :MOE_EXPERTS].set(router.astype(BF16))
    return pl.pallas_call(
        _moe_router_body,
        grid=(m // tm,),
        in_specs=[pl.BlockSpec((tm, d), lambda i: (i, 0)), pl.BlockSpec((1, d), lambda i: (0, 0)),
                  pl.BlockSpec((d, LANES), lambda i: (0, 0))],
        out_specs=[pl.BlockSpec((tm, LANES), lambda i: (i, 0)), pl.BlockSpec((8, LANES), lambda i: (0, 0))],
        out_shape=[jax.ShapeDtypeStruct((m, LANES), F32), jax.ShapeDtypeStruct((8, LANES), F32)],
        scratch_shapes=[pltpu.VMEM((1, LANES), F32)],
        compiler_params=_cparams(("arbitrary",)),
        name="moe_router",
    )(x, gain.reshape(1, d), rpad)


def _matmul(x, w_list, epilogue, out_list, *, tm, tn, n_total, tk=None, extras=(), name="matmul"):
    m, kdim = x.shape
    tk = kdim if tk is None else tk
    nk = kdim // tk
    nw, ne, no = len(w_list), len(extras), len(out_list)

    def body(*refs):
        x_ref = refs[0]
        w_refs = refs[1:1 + nw]
        e_refs = refs[1 + nw:1 + nw + ne]
        o_refs = refs[1 + nw + ne:1 + nw + ne + no]
        acc_refs = refs[1 + nw + ne + no:]
        n = pl.program_id(0)
        if nk == 1:
            xv = x_ref[...]
            epilogue([_dot(xv, w[...]) for w in w_refs], e_refs, o_refs, n)
        else:
            k = pl.program_id(2)

            @pl.when(k == 0)
            def _():
                for a in acc_refs:
                    a[...] = jnp.zeros_like(a)

            xv = x_ref[...]
            for a, w in zip(acc_refs, w_refs):
                a[...] += _dot(xv, w[...])

            @pl.when(k == nk - 1)
            def _():
                epilogue([a[...] for a in acc_refs], e_refs, o_refs, n)

    in_specs = [pl.BlockSpec((tm, tk), lambda n, mi, k: (mi, k))]
    in_specs += [s for _, s in w_list] + [s for _, s in extras]
    scratch = [] if nk == 1 else [pltpu.VMEM((tm, tn), F32) for _ in range(nw)]
    return pl.pallas_call(
        body,
        grid=(n_total // tn, m // tm, nk),
        in_specs=in_specs,
        out_specs=[s for _, s in out_list],
        out_shape=[s for s, _ in out_list],
        scratch_shapes=scratch,
        compiler_params=_cparams(("parallel", "parallel", "arbitrary")),
        name=name,
    )(x, *[a for a, _ in w_list], *[a for a, _ in extras])


def _wspec(tk, tn, off=0, layer=None):
    if layer is None:
        return pl.BlockSpec((tk, tn), lambda n, mi, k: (k, n + off))
    return pl.BlockSpec((None, tk, tn), lambda n, mi, k: (layer, k, n + off))


def _ospec(tm, tn):
    return pl.BlockSpec((tm, tn), lambda n, mi, k: (mi, n))


def _epi_resid(accs, e_refs, o_refs, n):
    o_refs[0][...] = e_refs[0][...] + accs[0]


def _matmul_resid(x, w, resid, *, layer=None, tm=512, tn=1024, tk=None, name="matmul_resid"):
    m, kdim = x.shape
    nn = w.shape[-1]
    tk = min(kdim, 4096) if tk is None else tk
    return _matmul(
        x, [(w, _wspec(tk, tn, 0, layer))], _epi_resid,
        [(jax.ShapeDtypeStruct((m, nn), F32), _ospec(tm, tn))],
        tm=tm, tn=tn, tk=tk, n_total=nn, extras=[(resid, _ospec(tm, tn))], name=name)[0]


def _silu(a):
    return a * (1.0 / (1.0 + jnp.exp(-a)))


def _epi_swiglu(accs, e_refs, o_refs, n):
    a, b = accs
    o_refs[0][...] = (_silu(a) * b).astype(BF16)


def _swiglu_up(x, w_up, *, layer=None, tm=512, tn=512, name="swiglu_up"):
    m, kdim = x.shape
    ff = w_up.shape[-1] // 2
    return _matmul(
        x, [(w_up, _wspec(kdim, tn, 0, layer)), (w_up, _wspec(kdim, tn, ff // tn, layer))], _epi_swiglu,
        [(jax.ShapeDtypeStruct((m, ff), BF16), _ospec(tm, tn))],
        tm=tm, tn=tn, n_total=ff, name=name)[0]


def _moe_plan(info, cnt, s):
    ne, tm = MOE_EXPERTS, MOE_TM
    e1 = info[:, RT_E1].astype(jnp.int32)
    e2 = info[:, RT_E2].astype(jnp.int32)
    counts = cnt[0, :ne].astype(jnp.int32)
    padded = (counts + tm - 1) // tm * tm
    ends = jnp.cumsum(padded)
    off = ends - padded
    dest1 = off[e1] + info[:, RT_R1].astype(jnp.int32)
    dest2 = off[e2] + info[:, RT_R2].astype(jnp.int32)
    ntile = (2 * s) // tm + ne
    tile_start = jnp.arange(ntile, dtype=jnp.int32) * tm
    tile_expert = jnp.minimum(jnp.sum(tile_start[:, None] >= ends[None, :], axis=1), ne - 1).astype(jnp.int32)
    nvalid = (ends[ne - 1:ne] // tm).astype(jnp.int32)
    tok = jnp.arange(s, dtype=jnp.int32)
    src = jnp.zeros((ntile * tm,), jnp.int32).at[dest1].set(tok).at[dest2].set(tok)
    return dest1, dest2, src, tile_expert, nvalid


def _row_copy(src_hbm, row, dst, r, sem):
    return pltpu.make_async_copy(src_hbm.at[pl.ds(row, 1)], dst.at[pl.ds(r, 1)], sem)


def _moe_gather_body(src_ref, nv_ref, h_hbm, g_ref, o_ref, buf, sem):
    t = pl.program_id(0)
    tm = buf.shape[0]

    @pl.when(t < nv_ref[0])
    def _():
        def issue(r, carry):
            _row_copy(h_hbm, src_ref[t * tm + r], buf, r, sem).start()
            return carry

        lax.fori_loop(0, tm, issue, 0)

        def drain(r, carry):
            _row_copy(h_hbm, src_ref[t * tm + r], buf, r, sem).wait()
            return carry

        lax.fori_loop(0, tm, drain, 0)
        o_ref[...] = _norm_rows(buf[...], g_ref[...])

    @pl.when(t >= nv_ref[0])
    def _():
        o_ref[...] = jnp.zeros_like(o_ref)


def _moe_gather(h, gain, src, nvalid):
    s, d = h.shape
    tm = MOE_TM
    ntile = src.shape[0] // tm
    return pl.pallas_call(
        _moe_gather_body,
        grid_spec=pltpu.PrefetchScalarGridSpec(
            num_scalar_prefetch=2,
            grid=(ntile,),
            in_specs=[pl.BlockSpec(memory_space=pl.ANY),
                      pl.BlockSpec((1, d), lambda t, src_, nv_: (0, 0))],
            out_specs=pl.BlockSpec((tm, d), lambda t, src_, nv_: (t, 0)),
            scratch_shapes=[pltpu.VMEM((tm, d), F32), pltpu.SemaphoreType.DMA(())]),
        out_shape=jax.ShapeDtypeStruct((ntile * tm, d), BF16),
        compiler_params=_cparams(("arbitrary",)),
        name="moe_gather",
    )(src, nvalid, h, gain.reshape(1, d))


def _moe_up_body(te_ref, nv_ref, x_ref, wa_ref, wb_ref, o_ref):
    t = pl.program_id(1)

    @pl.when(t < nv_ref[0])
    def _():
        x = x_ref[...]
        o_ref[...] = (_silu(_dot(x, wa_ref[...])) * _dot(x, wb_ref[...])).astype(BF16)

    @pl.when(t >= nv_ref[0])
    def _():
        o_ref[...] = jnp.zeros_like(o_ref)


def _moe_up(xs, w_up, layer, tile_expert, nvalid, *, tn=512):
    p, d = xs.shape
    tm = MOE_TM
    ff = w_up.shape[-1] // 2
    nn = ff // tn
    return pl.pallas_call(
        _moe_up_body,
        grid_spec=pltpu.PrefetchScalarGridSpec(
            num_scalar_prefetch=2,
            grid=(nn, p // tm),
            in_specs=[pl.BlockSpec((tm, d), lambda n, t, te, nv: (t, 0)),
                      pl.BlockSpec((None, None, d, tn), lambda n, t, te, nv: (layer, te[t], 0, n)),
                      pl.BlockSpec((None, None, d, tn), lambda n, t, te, nv: (layer, te[t], 0, n + nn))],
            out_specs=pl.BlockSpec((tm, tn), lambda n, t, te, nv: (t, n))),
        out_shape=jax.ShapeDtypeStruct((p, ff), BF16),
        compiler_params=_cparams(("parallel", "arbitrary")),
        name="moe_up",
    )(tile_expert, nvalid, xs, w_up, w_up)


def _moe_down_body(te_ref, nv_ref, x_ref, w_ref, o_ref):
    t = pl.program_id(1)

    @pl.when(t < nv_ref[0])
    def _():
        o_ref[...] = _dot(x_ref[...], w_ref[...])

    @pl.when(t >= nv_ref[0])
    def _():
        o_ref[...] = jnp.zeros_like(o_ref)


def _moe_down(act, w_down, layer, tile_expert, nvalid, *, tn=1024):
    p, ff = act.shape
    tm = MOE_TM
    d = w_down.shape[-1]
    return pl.pallas_call(
        _moe_down_body,
        grid_spec=pltpu.PrefetchScalarGridSpec(
            num_scalar_prefetch=2,
            grid=(d // tn, p // tm),
            in_specs=[pl.BlockSpec((tm, ff), lambda n, t, te, nv: (t, 0)),
                      pl.BlockSpec((None, None, ff, tn), lambda n, t, te, nv: (layer, te[t], 0, n))],
            out_specs=pl.BlockSpec((tm, tn), lambda n, t, te, nv: (t, n))),
        out_shape=jax.ShapeDtypeStruct((p, d), F32),
        compiler_params=_cparams(("parallel", "arbitrary")),
        name="moe_down",
    )(tile_expert, nvalid, act, w_down)


def _moe_combine_body(d1_ref, d2_ref, y_hbm, h_ref, info_ref, o_ref, buf, sems):
    t = pl.program_id(0)
    tm = h_ref.shape[0]

    def issue(r, carry):
        _row_copy(y_hbm, d1_ref[t * tm + r], buf.at[0], r, sems.at[0]).start()
        _row_copy(y_hbm, d2_ref[t * tm + r], buf.at[1], r, sems.at[1]).start()
        return carry

    lax.fori_loop(0, tm, issue, 0)

    def drain(r, carry):
        _row_copy(y_hbm, d1_ref[t * tm + r], buf.at[0], r, sems.at[0]).wait()
        _row_copy(y_hbm, d2_ref[t * tm + r], buf.at[1], r, sems.at[1]).wait()
        return carry

    lax.fori_loop(0, tm, drain, 0)
    info = info_ref[...]
    o_ref[...] = h_ref[...] + (info[:, RT_W1:RT_W1 + 1] * buf[0] + info[:, RT_W2:RT_W2 + 1] * buf[1])


def _moe_combine(h, y, info, dest1, dest2, tm=256):
    s, d = h.shape
    return pl.pallas_call(
        _moe_combine_body,
        grid_spec=pltpu.PrefetchScalarGridSpec(
            num_scalar_prefetch=2,
            grid=(s // tm,),
            in_specs=[pl.BlockSpec(memory_space=pl.ANY),
                      pl.BlockSpec((tm, d), lambda t, a, b: (t, 0)),
                      pl.BlockSpec((tm, LANES), lambda t, a, b: (t, 0))],
            out_specs=pl.BlockSpec((tm, d), lambda t, a, b: (t, 0)),
            scratch_shapes=[pltpu.VMEM((2, tm, d), F32), pltpu.SemaphoreType.DMA((2,))]),
        out_shape=jax.ShapeDtypeStruct((s, d), F32),
        compiler_params=_cparams(("arbitrary",)),
        name="moe_combine",
    )(dest1, dest2, y, h, info)


def _nsa_q_proj(hn, wq, layer, q_gain, nn, *, tm=512, tn=1024):
    m, kdim = hn.shape
    scale = HEAD_DIM ** -0.5 * LOG2E

    def epi(accs, e_refs, o_refs, n):
        acc = accs[0]
        gain = e_refs[0][...]
        for c in range(tn // HEAD_DIM):
            a = acc[:, c * HEAD_DIM:(c + 1) * HEAD_DIM]
            ms = jnp.mean(a * a, axis=-1, keepdims=True)
            o_refs[0][:, c * HEAD_DIM:(c + 1) * HEAD_DIM] = (
                a * lax.rsqrt(ms + RMS_EPS) * gain * scale).astype(BF16)

    return _matmul(
        hn, [(wq, _wspec(kdim, tn, 0, layer))], epi,
        [(jax.ShapeDtypeStruct((m, nn), BF16), _ospec(tm, tn))],
        tm=tm, tn=tn, n_total=nn,
        extras=[(q_gain.reshape(1, HEAD_DIM), pl.BlockSpec((1, HEAD_DIM), lambda n, mi, k: (0, 0)))],
        name="nsa_q_proj")[0]


def _nsa_kv_proj(hn, wkv, layer, k_gain, col0, *, tm=512):
    m, kdim = hn.shape
    g_ = NSA_KV_HEADS
    tn = g_ * HEAD_DIM
    gains = jnp.ones((6, 1, HEAD_DIM), F32).at[2, 0].set(k_gain[1]).at[4, 0].set(k_gain[2])

    def epi(accs, e_refs, o_refs, n):
        acc = accs[0]
        kg = e_refs[0][0]
        do_norm = jnp.logical_or(n == 2, n == 4)
        for g in range(g_):
            a = acc[:, g * HEAD_DIM:(g + 1) * HEAD_DIM]
            ms = jnp.mean(a * a, axis=-1, keepdims=True)
            an = a * lax.rsqrt(ms + RMS_EPS) * kg
            o_refs[0][g] = jnp.where(do_norm, an, a).astype(BF16)

    return _matmul(
        hn, [(wkv, _wspec(kdim, tn, col0 // tn, layer))], epi,
        [(jax.ShapeDtypeStruct((6 * g_, m, HEAD_DIM), BF16),
          pl.BlockSpec((g_, tm, HEAD_DIM), lambda n, mi, k: (n, mi, 0)))],
        tm=tm, tn=tn, n_total=6 * tn,
        extras=[(gains, pl.BlockSpec((1, 1, HEAD_DIM), lambda n, mi, k: (n, 0, 0)))],
        name="nsa_kv_proj")[0]


def _nsa_gate_proj(hn, wg, *, tm=512):
    m, kdim = hn.shape
    nn = wg.shape[1]

    def epi(accs, e_refs, o_refs, n):
        o_refs[0][...] = 1.0 / (1.0 + jnp.exp(-accs[0]))

    return _matmul(
        hn, [(wg, _wspec(kdim, nn))], epi,
        [(jax.ShapeDtypeStruct((m, nn), F32), _ospec(tm, nn))],
        tm=tm, tn=nn, n_total=nn, name="nsa_gate_proj")[0]


def _compress_body(r_ref, pe_ref, w1_ref, w2_ref, kg_ref, o_ref, ot_ref):
    j = pl.program_id(0)
    r = r_ref[0].astype(F32)
    pe = pe_ref[0]
    half = r.shape[1]
    top = (r + pe[0:1]).astype(BF16)
    bot = (r + pe[1:2]).astype(BF16)
    a = _dot(top, w1_ref[0, 0:half, :])
    b = _dot(bot, w1_ref[0, half:2 * half, :])
    nrow = r.shape[0]
    hid = a + pltpu.roll(b, nrow - 1, 0)
    y = _dot(jax.nn.gelu(hid).astype(BF16), w2_ref[0])
    ms = jnp.mean(y * y, axis=-1, keepdims=True)
    yn = y * lax.rsqrt(ms + RMS_EPS) * kg_ref[...]
    out = jnp.where(j == 0, yn, y)
    o_ref[0, 0] = out.astype(BF16)
    ot_ref[0, 0] = jnp.transpose(out).astype(BF16)


def _compress(kv, pe, w1, w2, k_gain0):
    g_ = NSA_KV_HEADS
    s = kv.shape[1]
    nrow = s // CMP_STRIDE
    wid = CMP_STRIDE * HEAD_DIM
    r = kv.reshape(6 * g_, nrow, wid)
    pe2 = pe.reshape(2, 2, wid)
    return pl.pallas_call(
        _compress_body,
        grid=(2, g_),
        in_specs=[pl.BlockSpec((1, nrow, wid), lambda j, g: (j * g_ + g, 0, 0)),
                  pl.BlockSpec((1, 2, wid), lambda j, g: (j, 0, 0)),
                  pl.BlockSpec((1, 2 * wid, HEAD_DIM), lambda j, g: (j, 0, 0)),
                  pl.BlockSpec((1, HEAD_DIM, HEAD_DIM), lambda j, g: (j, 0, 0)),
                  pl.BlockSpec((1, HEAD_DIM), lambda j, g: (0, 0))],
        out_specs=[pl.BlockSpec((1, 1, nrow, HEAD_DIM), lambda j, g: (j, g, 0, 0)),
                   pl.BlockSpec((1, 1, HEAD_DIM, nrow), lambda j, g: (j, g, 0, 0))],
        out_shape=[jax.ShapeDtypeStruct((2, g_, nrow, HEAD_DIM), BF16),
                   jax.ShapeDtypeStruct((2, g_, HEAD_DIM, nrow), BF16)],
        compiler_params=_cparams(("parallel", "parallel")),
        name="nsa_compress",
    )(r, pe2, w1.astype(BF16), w2.astype(BF16), k_gain0.reshape(1, HEAD_DIM))


def _t5_bucket(dist):
    dist = jnp.maximum(dist, 0)
    max_exact = T5_BUCKETS // 2
    d_f = jnp.maximum(dist, 1).astype(F32)
    log_b = max_exact + (jnp.log(d_f / max_exact) / math.log(T5_MAX_DISTANCE / max_exact)
                         * (T5_BUCKETS - max_exact)).astype(jnp.int32)
    log_b = jnp.minimum(log_b, T5_BUCKETS - 1)
    return jnp.where(dist < max_exact, dist, log_b)


def _bias_table(tbl, dist, valid, shift):
    onehot = jax.nn.one_hot(_t5_bucket(dist).reshape(-1), T5_BUCKETS, dtype=F32)
    t = tbl - tbl[T5_BUCKETS - 1:T5_BUCKETS] if shift else tbl
    vals = jnp.einsum("nb,bh->hn", onehot, t, precision=lax.Precision.HIGHEST)
    vals = vals.reshape((tbl.shape[1],) + dist.shape) * LOG2E
    return jnp.where(valid[None], vals, NEG)


def _nsa_tables(rel_bias):
    tbl = rel_bias.astype(F32)
    r = jnp.arange(Q_BLOCK)
    jw = jnp.arange(WINDOW + Q_BLOCK)
    dw = WINDOW + r[None, :] - jw[:, None]
    tab_w = _bias_table(tbl, dw, (dw >= 0) & (dw < WINDOW), False)
    a = jnp.arange(16)
    cc = jnp.arange(2 * LANES)
    dc = (CMP_NEAR_SHIFT * Q_BLOCK + Q_BLOCK * a[:, None, None] + r[None, None, :]
          - CMP_STRIDE * cc[None, :, None] - (CMP_BLOCK - 1))
    tab_c = _bias_table(tbl, dc, dc >= 0, True)
    jj = jnp.arange(SEL_NEAR_TILES)
    col = jnp.arange(SEL_TILE)
    ds_ = r[None, None, :] - Q_BLOCK + SEL_TILE * (jj[:, None, None] + 1) - col[None, :, None]
    tab_s = _bias_table(tbl, ds_, ds_ >= 0, True)
    return tab_w, tab_c, tab_s, tbl


def _sel_to_cmp(nsel, nc_pad):
    c_start = jnp.arange(nc_pad) * CMP_STRIDE
    s_start = jnp.arange(nsel) * SEL_BLOCK
    overlap = jnp.clip(jnp.minimum(c_start[None, :] + CMP_BLOCK, s_start[:, None] + SEL_BLOCK)
                       - jnp.maximum(c_start[None, :], s_start[:, None]), 0)
    return (overlap.astype(F32) / CMP_BLOCK).astype(BF16)


def _heads_t(q_ref, h0, nh):
    cols = [jnp.transpose(q_ref[:, (h0 + h) * HEAD_DIM:(h0 + h + 1) * HEAD_DIM].astype(F32)) for h in range(nh)]
    return jnp.concatenate(cols, axis=1).astype(BF16)


def _softmax_keys(s, bounded):
    if bounded:
        p = jnp.exp2(s)
        l = jnp.sum(p, axis=0, keepdims=True)
        return p, jnp.where(l > 0.0, 1.0 / l, 0.0)
    m = jnp.max(s, axis=0, keepdims=True)
    p = jnp.exp2(s - m)
    l = jnp.sum(p, axis=0, keepdims=True)
    return p, jnp.where(m > 0.5 * NEG, 1.0 / l, 0.0)


def _logit_bound(rel_bias, q_gain, k_gain):
    qk = HEAD_DIM * (HEAD_DIM ** -0.5 * LOG2E) * jnp.max(jnp.abs(q_gain)) * jnp.max(jnp.abs(k_gain))
    return (1.02 * qk + 2.0 * LOG2E * jnp.max(jnp.abs(rel_bias))).astype(F32).reshape(1)


def _nsa_cw_body(bnd_ref, q_ref, gt_ref, kc_ref, vct_ref, mselt_ref, tabc_ref, kw_ref, vwt_ref, tabw_ref,
                 ocwt_ref, negt_ref):
    bounded = bnd_ref[0] < SAFE_LOGIT
    i = pl.program_id(1)

    @pl.when(bounded)
    def _():
        _nsa_cw_branches(True, i, q_ref, gt_ref, kc_ref, vct_ref, mselt_ref, tabc_ref, kw_ref, vwt_ref, tabw_ref,
                         ocwt_ref, negt_ref)

    @pl.when(jnp.logical_not(bounded))
    def _():
        _nsa_cw_branches(False, i, q_ref, gt_ref, kc_ref, vct_ref, mselt_ref, tabc_ref, kw_ref, vwt_ref, tabw_ref,
                         ocwt_ref, negt_ref)


def _nsa_cw_branches(bounded, i, q_ref, gt_ref, kc_ref, vct_ref, mselt_ref, tabc_ref, kw_ref, vwt_ref, tabw_ref,
                     ocwt_ref, negt_ref):
    f = (i + 16 - CMP_NEAR_SHIFT) // 16 - 1
    kc = kc_ref[0, 0]
    vct = vct_ref[0, 0]
    nc = kc.shape[0]
    nsel = mselt_ref.shape[0]
    wlen = WINDOW + Q_BLOCK
    nslab = wlen // LANES
    start = pl.multiple_of(i * Q_BLOCK, Q_BLOCK)
    kw = kw_ref[0, pl.ds(start, wlen), :]
    vwt = jnp.concatenate([vwt_ref[0, i + c] for c in range(nslab)], axis=1)
    roww = lax.broadcasted_iota(jnp.int32, (wlen, LANES), 0)
    w_pad = roww < (WINDOW - Q_BLOCK * i)
    gt_t = jnp.transpose(gt_ref[...])
    psum = jnp.zeros((nc, Q_BLOCK), F32)
    hg = CW_HEADS
    for h0 in range(0, NSA_GROUP, hg):
        qt = _heads_t(q_ref, h0, hg)
        s = _dot(kc, qt)
        ta = jnp.concatenate([tabc_ref[h0 + h, 0, 0:LANES, :] for h in range(hg)], axis=1)
        tb = jnp.concatenate([tabc_ref[h0 + h, 0, LANES:2 * LANES, :] for h in range(hg)], axis=1)
        pieces = []
        for ch in range(nc // LANES):
            rest = jnp.where(ch > f + 1, NEG, 0.0)
            bias = jnp.where(ch == f, ta, jnp.where(ch == f + 1, tb, rest))
            pieces.append(s[ch * LANES:(ch + 1) * LANES] + bias)
        s = jnp.concatenate(pieces, axis=0)
        p, linv = _softmax_keys(s, bounded)
        pn = p * linv
        for h in range(hg):
            psum = psum + pn[:, h * Q_BLOCK:(h + 1) * Q_BLOCK]
        oc = _dot(vct, pn.astype(BF16))
        tw = jnp.concatenate([jnp.where(w_pad, NEG, tabw_ref[h0 + h]) for h in range(hg)], axis=1)
        sw = _dot(kw, qt) + tw
        pw, lwinv = _softmax_keys(sw, bounded)
        ow = _dot(vwt, pw.astype(BF16)) * lwinv
        for h in range(hg):
            hh = h0 + h
            cs = slice(h * Q_BLOCK, (h + 1) * Q_BLOCK)
            ocwt_ref[0, 0, hh * HEAD_DIM:(hh + 1) * HEAD_DIM, :] = (
                gt_t[hh:hh + 1] * oc[:, cs] + gt_t[16 + hh:17 + hh] * ow[:, cs])
    p_hi = psum.astype(BF16)
    p_lo = (psum - p_hi.astype(F32)).astype(BF16)
    mselt = mselt_ref[...]
    imp = _dot(mselt, p_hi) + _dot(mselt, p_lo)
    blk = lax.broadcasted_iota(jnp.int32, (nsel, Q_BLOCK), 0).astype(F32)
    qpos = lax.broadcasted_iota(jnp.int32, (nsel, Q_BLOCK), 1)
    cur = (i * (Q_BLOCK // SEL_BLOCK)).astype(F32) + jnp.where(qpos >= SEL_BLOCK, 1.0, 0.0)
    forced = jnp.logical_or(blk == cur, blk == 0.0)
    val = jnp.where(forced, FORCED_SCORE, jnp.where(blk <= cur, imp, -1.0))
    chosen = jnp.zeros((nsel, Q_BLOCK), F32)
    for _ in range(SEL_TOPK):
        mx = jnp.max(val, axis=0, keepdims=True)
        first = jnp.min(jnp.where(val == mx, blk, 1e4), axis=0, keepdims=True)
        hit = blk == first
        chosen = jnp.where(hit, 1.0, chosen)
        val = jnp.where(hit, -2.0, val)
    neg = jnp.where(chosen > 0.0, 0.0, NEG)
    for pr in range(SEL_PAD_BLOCKS // 2):
        negt_ref[0, 0, pr] = jnp.full((2, Q_BLOCK), NEG, F32)
    for pr in range(nsel // 2):
        negt_ref[0, 0, SEL_PAD_BLOCKS // 2 + pr] = neg[2 * pr:2 * pr + 2]


def _nsa_cw(bound, q, gates, kc, vct, mselt, tab_c, kw, vwt, tab_w):
    s, hd = q.shape
    g_ = NSA_KV_HEADS
    nq = s // Q_BLOCK
    gw = NSA_GROUP * HEAD_DIM
    nc = kc.shape[2]
    nsel = mselt.shape[0]
    spad = kw.shape[1]
    nslab = vwt.shape[1]
    wlen = WINDOW + Q_BLOCK

    def var(i):
        return (i + 16 - CMP_NEAR_SHIFT) % 16

    return pl.pallas_call(
        _nsa_cw_body,
        grid=(g_, nq),
        in_specs=[
            pl.BlockSpec(memory_space=pltpu.SMEM),
            pl.BlockSpec((Q_BLOCK, gw), lambda g, i: (i, g)),
            pl.BlockSpec((Q_BLOCK, LANES), lambda g, i: (i, g)),
            pl.BlockSpec((1, 1, nc, HEAD_DIM), lambda g, i: (0, g, 0, 0)),
            pl.BlockSpec((1, 1, HEAD_DIM, nc), lambda g, i: (1, g, 0, 0)),
            pl.BlockSpec((nsel, nc), lambda g, i: (0, 0)),
            pl.BlockSpec((NSA_GROUP, 1, 2 * LANES, Q_BLOCK), lambda g, i: (g, var(i), 0, 0)),
            pl.BlockSpec((1, spad, HEAD_DIM), lambda g, i: (g, 0, 0)),
            pl.BlockSpec((1, nslab, HEAD_DIM, LANES), lambda g, i: (g, 0, 0, 0)),
            pl.BlockSpec((NSA_GROUP, wlen, Q_BLOCK), lambda g, i: (g, 0, 0)),
        ],
        out_specs=[pl.BlockSpec((1, 1, gw, Q_BLOCK), lambda g, i: (g, i, 0, 0)),
                   pl.BlockSpec((1, 1, (SEL_PAD_BLOCKS + nsel) // 2, 2, Q_BLOCK), lambda g, i: (g, i, 0, 0, 0))],
        out_shape=[jax.ShapeDtypeStruct((g_, nq, gw, Q_BLOCK), F32),
                   jax.ShapeDtypeStruct((g_, nq, (SEL_PAD_BLOCKS + nsel) // 2, 2, Q_BLOCK), F32)],
        compiler_params=_cparams(("parallel", "arbitrary")),
        name="nsa_cmp_win",
    )(bound, q, gates, kc, vct, mselt, tab_c, kw, vwt, tab_w)


def _nsa_sel_body(bnd_ref, q_ref, gt_ref, ks_ref, vst_ref, negp_ref, tabs_ref, ocwt_ref, o_ref,
                  m_sc, l_sc, acc_sc):
    i = pl.program_id(1)
    ntile = i // (SEL_TILE // Q_BLOCK) + 1
    nh = NSA_GROUP
    spt = SEL_TILE // LANES
    ppt = SEL_TILE // (2 * SEL_BLOCK)
    qt = _heads_t(q_ref, 0, nh)

    def tile(jj, near, bounded, nsub=1):
        slab0 = (i + 1) - spt * jj
        row0 = pl.multiple_of(slab0 * LANES, LANES)
        kt = ks_ref[0, pl.ds(row0, nsub * SEL_TILE), :]
        vtt = jnp.concatenate([vst_ref[0, slab0 + c] for c in range(nsub * spt)], axis=1)
        rows = []
        for c in range(nsub * ppt):
            pair = negp_ref[0, 0, slab0 + c]
            rows += [jnp.broadcast_to(pair[r:r + 1], (SEL_BLOCK, Q_BLOCK)) for r in range(2)]
        mk = jnp.concatenate(rows, axis=0)
        if near:
            bias = jnp.concatenate([tabs_ref[h, jj] + mk for h in range(nh)], axis=1)
        else:
            bias = jnp.tile(mk, (1, nh))
        sc = _dot(kt, qt) + bias
        if bounded:
            p = jnp.exp2(sc)
            l_sc[...] = l_sc[...] + jnp.sum(p, axis=0, keepdims=True)
            acc_sc[...] = acc_sc[...] + _dot(vtt, p.astype(BF16))
        else:
            m_old = m_sc[...]
            m_new = jnp.maximum(m_old, jnp.max(sc, axis=0, keepdims=True))
            alpha = jnp.exp2(m_old - m_new)
            p = jnp.exp2(sc - m_new)
            l_sc[...] = alpha * l_sc[...] + jnp.sum(p, axis=0, keepdims=True)
            acc_sc[...] = alpha * acc_sc[...] + _dot(vtt, p.astype(BF16))
            m_sc[...] = m_new

    def run(bounded):
        m_sc[...] = jnp.full(m_sc.shape, NEG, F32)
        l_sc[...] = jnp.zeros(l_sc.shape, F32)
        acc_sc[...] = jnp.zeros(acc_sc.shape, F32)
        for jj in range(SEL_NEAR_TILES):
            @pl.when(jj < ntile)
            def _():
                tile(jj, True, bounded)

        nfar = jnp.maximum(ntile - SEL_NEAR_TILES, 0)

        def far(k, carry):
            tile(SEL_NEAR_TILES + SEL_FAR_WIDTH * k + SEL_FAR_WIDTH - 1, False, bounded, SEL_FAR_WIDTH)
            return carry

        lax.fori_loop(0, nfar // SEL_FAR_WIDTH, far, 0)

        def far_rest(jj, carry):
            tile(jj, False, bounded)
            return carry

        lax.fori_loop(SEL_NEAR_TILES + nfar // SEL_FAR_WIDTH * SEL_FAR_WIDTH, ntile, far_rest, 0)

    bounded = bnd_ref[0] < SAFE_LOGIT

    @pl.when(bounded)
    def _():
        run(True)

    @pl.when(jnp.logical_not(bounded))
    def _():
        run(False)

    o_t = acc_sc[...] * (1.0 / l_sc[...])
    gt_t = jnp.transpose(gt_ref[...])
    for h in range(nh):
        hs = slice(h * HEAD_DIM, (h + 1) * HEAD_DIM)
        oh = ocwt_ref[0, 0, hs, :] + gt_t[8 + h:9 + h] * o_t[:, h * Q_BLOCK:(h + 1) * Q_BLOCK]
        o_ref[:, hs] = jnp.transpose(oh).astype(BF16)


def _nsa_sel(bound, q, gates, ks, vst, negp, tab_s, ocwt):
    s, hd = q.shape
    g_ = NSA_KV_HEADS
    nq = s // Q_BLOCK
    gw = NSA_GROUP * HEAD_DIM
    spad = ks.shape[1]
    nslab = vst.shape[1]
    npair = negp.shape[2]
    return pl.pallas_call(
        _nsa_sel_body,
        grid=(g_, nq),
        in_specs=[
            pl.BlockSpec(memory_space=pltpu.SMEM),
            pl.BlockSpec((Q_BLOCK, gw), lambda g, i: (i, g)),
            pl.BlockSpec((Q_BLOCK, LANES), lambda g, i: (i, g)),
            pl.BlockSpec((1, spad, HEAD_DIM), lambda g, i: (g, 0, 0)),
            pl.BlockSpec((1, nslab, HEAD_DIM, LANES), lambda g, i: (g, 0, 0, 0)),
            pl.BlockSpec((1, 1, npair, 2, Q_BLOCK), lambda g, i: (g, i, 0, 0, 0)),
            pl.BlockSpec((NSA_GROUP, SEL_NEAR_TILES, SEL_TILE, Q_BLOCK), lambda g, i: (g, 0, 0, 0)),
            pl.BlockSpec((1, 1, gw, Q_BLOCK), lambda g, i: (g, i, 0, 0)),
        ],
        out_specs=pl.BlockSpec((Q_BLOCK, gw), lambda g, i: (i, g)),
        out_shape=jax.ShapeDtypeStruct((s, hd), BF16),
        scratch_shapes=[pltpu.VMEM((1, NSA_GROUP * Q_BLOCK), F32), pltpu.VMEM((1, NSA_GROUP * Q_BLOCK), F32),
                        pltpu.VMEM((HEAD_DIM, NSA_GROUP * Q_BLOCK), F32)],
        compiler_params=_cparams(("parallel", "arbitrary")),
        name="nsa_selected",
    )(bound, q, gates, ks, vst, negp, tab_s, ocwt)


def _pad_keys(x, pad):
    xp = jnp.pad(x, ((0, 0), (pad, 0), (0, 0)))
    g_, sp, dh = xp.shape
    return xp, xp.reshape(g_, sp // LANES, LANES, dh).swapaxes(-1, -2)


def _nsa_mixer(h, hn, tables, layer, w_in, cmp_pe, cmp_w1, cmp_w2, q_gain, k_gain, w_out):
    s, d = h.shape
    g_, hpg, dh = NSA_KV_HEADS, NSA_GROUP, HEAD_DIM
    hd = g_ * hpg * dh
    tab_w, tab_c, tab_s, tbl = tables
    q = _nsa_q_proj(hn, w_in, layer, q_gain, hd)
    kv = _nsa_kv_proj(hn, w_in, layer, k_gain, hd)
    wg = w_in[layer, :, hd + 6 * g_ * dh:].reshape(d, g_, hpg, 3).transpose(0, 1, 3, 2).reshape(d, g_, 3 * hpg)
    wg = jnp.pad(wg, ((0, 0), (0, 0), (0, LANES - 3 * hpg))).reshape(d, g_ * LANES)
    gates = _nsa_gate_proj(hn, wg)
    kc, kct = _compress(kv, cmp_pe, cmp_w1, cmp_w2, k_gain[0])
    mselt = _sel_to_cmp(s // SEL_BLOCK, s // CMP_STRIDE)
    ks, _ = _pad_keys(kv[2 * g_:3 * g_], SEL_TILE)
    _, vst = _pad_keys(kv[3 * g_:4 * g_], SEL_TILE)
    kw, _ = _pad_keys(kv[4 * g_:5 * g_], WINDOW)
    _, vwt = _pad_keys(kv[5 * g_:6 * g_], WINDOW)
    bound = _logit_bound(tbl, q_gain, k_gain)
    ocwt, negp = _nsa_cw(bound, q, gates, kc, kct, mselt, tab_c, kw, vwt, tab_w)
    o = _nsa_sel(bound, q, gates, ks, vst, negp, tab_s, ocwt)
    return _matmul_resid(o, w_out, h, layer=layer, name="nsa_out")


def _gelu_proj(hn, w, layer, *, tm=512, tn=1024):
    m, kdim = hn.shape
    nn = w.shape[-1]

    def epi(accs, e_refs, o_refs, n):
        o_refs[0][...] = jax.nn.gelu(accs[0])

    return _matmul(
        hn, [(w, _wspec(kdim, tn, 0, layer))], epi,
        [(jax.ShapeDtypeStruct((m, nn), F32), _ospec(tm, tn))],
        tm=tm, tn=tn, n_total=nn, name="sgu_in")[0]


def _sgu_mix_body(u_ref, v_ref, gain_ref, w_ref, bt_ref, o_ref):
    v = v_ref[...]
    ms = jnp.mean(v * v, axis=-1, keepdims=True)
    vn = (v * lax.rsqrt(ms + RMS_EPS) * gain_ref[...]).astype(BF16)
    t = w_ref.shape[1]
    causal = (lax.broadcasted_iota(jnp.int32, (t, t), 0) >= lax.broadcasted_iota(jnp.int32, (t, t), 1))
    bt = bt_ref[...]
    gd = vn.shape[1] // SG_GROUPS
    for g in range(SG_GROUPS):
        gs = slice(g * gd, (g + 1) * gd)
        w = jnp.where(causal, w_ref[g], 0.0).astype(BF16)
        mixed = _dot(w, vn[:, gs]) + bt[:, g:g + 1]
        o_ref[:, gs] = (u_ref[:, gs] * mixed).astype(BF16)


def _sgu_mix(uv, v_gain, w_s, b_s):
    s = uv.shape[0]
    wd = uv.shape[1] // 2
    t = SG_CHUNK
    return pl.pallas_call(
        _sgu_mix_body,
        grid=(s // t,),
        in_specs=[pl.BlockSpec((t, wd), lambda c: (c, 0)),
                  pl.BlockSpec((t, wd), lambda c: (c, 1)),
                  pl.BlockSpec((1, wd), lambda c: (0, 0)),
                  pl.BlockSpec((SG_GROUPS, t, t), lambda c: (0, 0, 0)),
                  pl.BlockSpec((t, SG_GROUPS), lambda c: (0, 0))],
        out_specs=pl.BlockSpec((t, wd), lambda c: (c, 0)),
        out_shape=jax.ShapeDtypeStruct((s, wd), BF16),
        compiler_params=_cparams(("parallel",)),
        name="sgu_mix",
    )(uv, uv, v_gain.reshape(1, wd), w_s, b_s.T)


def _sgu_mixer(h, hn, layer, w_in, v_gain, w_s, b_s, w_out):
    uv = _gelu_proj(hn, w_in, layer)
    y = _sgu_mix(uv, v_gain, w_s, b_s)
    return _matmul_resid(y, w_out, h, layer=layer, name="sgu_out")


def _gla_body(q_ref, k_ref, v_ref, r_ref, g1_ref, wg_ref, bg_ref, og_ref, o_ref, state_sc, diag_sc):
    @pl.when(pl.program_id(0) == 0)
    def _():
        state_sc[...] = jnp.zeros_like(state_sc)

    nh = state_sc.shape[0]
    dk = q_ref.shape[1] // nh
    dv = v_ref.shape[1] // nh
    g1 = g1_ref[...].astype(BF16)
    for hh in range(nh):
        ks = slice(hh * dk, (hh + 1) * dk)
        vs = slice(hh * dv, (hh + 1) * dv)
        o_ref[:, vs] = _gla_head(q_ref[:, ks], k_ref[:, ks], v_ref[:, vs], r_ref[:, vs], g1, wg_ref[:, ks],
                                 bg_ref[:, ks], og_ref[...], state_sc.at[hh], diag_sc.at[hh])


def _gla_head(q_in, k_in, v, r_in, g1, wg, bg, og, state_sc, diag_sc):
    ch, dk = q_in.shape
    x = _dot(g1, wg) + bg
    log_a = (jnp.minimum(x, 0.0) - jnp.log(1.0 + jnp.exp(-jnp.abs(x)))) * (1.0 / GLA_GATE_TEMP)
    tri = (lax.broadcasted_iota(jnp.int32, (ch, ch), 0)
           >= lax.broadcasted_iota(jnp.int32, (ch, ch), 1))
    tri_b = jnp.where(tri, 1.0, 0.0).astype(BF16)
    a_hi = log_a.astype(BF16)
    a_lo = (log_a - a_hi.astype(F32)).astype(BF16)
    b = _dot(tri_b, a_hi) + _dot(tri_b, a_lo)
    q = q_in.astype(F32) * (dk ** -0.5)
    k = k_in.astype(F32)
    state = state_sc[...]
    o = _dot((q * jnp.exp(b)).astype(BF16), state.astype(BF16))
    nsub = ch // GLA_SUB
    ends = [jnp.broadcast_to(b[(jb + 1) * GLA_SUB - 1:(jb + 1) * GLA_SUB], (GLA_SUB, dk))
            for jb in range(nsub)]
    b_end = jnp.concatenate(ends, axis=0)
    b_start = jnp.concatenate([jnp.zeros((GLA_SUB, dk), F32)] + ends[:-1], axis=0)
    rblk = lax.broadcasted_iota(jnp.int32, (ch, ch), 0) // GLA_SUB
    cblk = lax.broadcasted_iota(jnp.int32, (ch, ch), 1) // GLA_SUB
    zero_row = jnp.zeros((1, dk), F32)
    spread = jnp.max(jnp.concatenate(
        [(ends[jb - 1][0:1] if jb else zero_row) - ends[jb][0:1] for jb in range(nsub)], axis=0))

    @pl.when(spread < GLA_SAFE_EXP)
    def _():
        q_diag = (q * jnp.exp(b - b_start)).astype(BF16)
        k_diag = (k * jnp.exp(b_start - b)).astype(BF16)
        diag_sc[...] = _dot_nt(q_diag, k_diag)

    @pl.when(spread >= GLA_SAFE_EXP)
    def _():
        lane = lax.broadcasted_iota(jnp.int32, (GLA_SUB, ch), 1)
        strips = []
        for ib in range(nsub):
            rs = slice(ib * GLA_SUB, (ib + 1) * GLA_SUB)
            qi, ki, bi = q[rs], k[rs], b[rs]
            strip = jnp.zeros((GLA_SUB, ch), F32)
            for j in range(GLA_SUB):
                decay = jnp.exp(jnp.minimum(bi - bi[j:j + 1], 0.0))
                col = jnp.sum(qi * ki[j:j + 1] * decay, axis=-1, keepdims=True)
                strip = jnp.where(lane == ib * GLA_SUB + j, col, strip)
            strips.append(strip)
        diag_sc[...] = jnp.concatenate(strips, axis=0)

    attn = jnp.where(jnp.logical_and(tri, rblk == cblk), diag_sc[...], 0.0)
    k_hat = (k * jnp.exp(b_end - b)).astype(BF16)
    for jb in range(nsub - 1):
        q_hat = (q * jnp.exp(jnp.minimum(b - ends[jb][0:1], 0.0))).astype(BF16)
        attn = attn + jnp.where(jnp.logical_and(cblk == jb, rblk > jb), _dot_nt(q_hat, k_hat), 0.0)
    o = o + _dot(attn.astype(BF16), v)
    b_last = b[ch - 1:ch]
    k_dec = k * jnp.exp(b_last - b)
    k_dec_t = jnp.transpose(k_dec).astype(BF16)
    decay_t = jnp.transpose(jnp.broadcast_to(jnp.exp(b_last), (LANES, dk)))
    state_sc[...] = decay_t[:, 0:1] * state + _dot(k_dec_t, v)
    ms = jnp.mean(o * o, axis=-1, keepdims=True)
    on = o * lax.rsqrt(ms + RMS_EPS) * og
    return (on * _silu(r_in.astype(F32))).astype(BF16)


def _gla_mixer(h, hn, layer, w_in, w_gate2, b_gate, o_gain, w_out):
    s, d = h.shape
    nh = GLA_HEADS
    dk = d // 2
    dv = d
    dkh, dvh = dk // nh, dv // nh
    nmain = 2 * dk + 2 * dv

    def epi_bf16(accs, e_refs, o_refs, n):
        o_refs[0][...] = accs[0].astype(BF16)

    def epi_f32(accs, e_refs, o_refs, n):
        o_refs[0][...] = accs[0]

    tm, tn = 512, 1024
    proj = _matmul(hn, [(w_in, _wspec(d, tn, 0, layer))], epi_bf16,
                   [(jax.ShapeDtypeStruct((s, nmain), BF16), _ospec(tm, tn))],
                   tm=tm, tn=tn, n_total=nmain, name="gla_in")[0]
    wg1 = jnp.pad(w_in[layer, :, nmain:], ((0, 0), (0, LANES - GLA_GATE_RANK)))
    g1 = _matmul(hn, [(wg1, _wspec(d, LANES))], epi_f32,
                 [(jax.ShapeDtypeStruct((s, LANES), F32), _ospec(tm, LANES))],
                 tm=tm, tn=LANES, n_total=LANES, name="gla_gate_in")[0]
    wg2 = jnp.pad(w_gate2.astype(BF16), ((0, LANES - GLA_GATE_RANK), (0, 0)))
    ch = GLA_CHUNK
    o = pl.pallas_call(
        _gla_body,
        grid=(s // ch,),
        in_specs=[
            pl.BlockSpec((ch, dk), lambda c: (c, 0)),
            pl.BlockSpec((ch, dk), lambda c: (c, 1)),
            pl.BlockSpec((ch, dv), lambda c: (c, 2 * dk // dv)),
            pl.BlockSpec((ch, dv), lambda c: (c, 2 * dk // dv + 1)),
            pl.BlockSpec((ch, LANES), lambda c: (c, 0)),
            pl.BlockSpec((LANES, dk), lambda c: (0, 0)),
            pl.BlockSpec((1, dk), lambda c: (0, 0)),
            pl.BlockSpec((1, dvh), lambda c: (0, 0)),
        ],
        out_specs=pl.BlockSpec((ch, dv), lambda c: (c, 0)),
        out_shape=jax.ShapeDtypeStruct((s, dv), BF16),
        scratch_shapes=[pltpu.VMEM((nh, dkh, dvh), F32), pltpu.VMEM((nh, ch, ch), F32)],
        compiler_params=_cparams(("arbitrary",)),
        name="gla_scan",
    )(proj, proj, proj, proj, g1, wg2, b_gate.reshape(1, dk), o_gain.reshape(1, dvh))
    return _matmul_resid(o, w_out, h, layer=layer, name="gla_out")


def _dense_ffn(h, hn, layer, w_up, w_down):
    act = _swiglu_up(hn, w_up, layer=layer)
    return _matmul_resid(act, w_down, h, layer=layer, name="ffn_down")


def _moe_ffn(h, gain, router, layer, w_up, w_down):
    s, _ = h.shape
    info, cnt = _moe_router(h, gain, router)
    dest1, dest2, src, tile_expert, nvalid = _moe_plan(info, cnt, s)
    xs = _moe_gather(h, gain, src, nvalid)
    act = _moe_up(xs, w_up, layer, tile_expert, nvalid)
    y = _moe_down(act, w_down, layer, tile_expert, nvalid)
    return _moe_combine(h, y, info, dest1, dest2)


def kernel(x, rel_bias, norm_gain, nsa_w_in, nsa_cmp_pe, nsa_cmp_w1, nsa_cmp_w2, nsa_q_gain, nsa_k_gain, nsa_w_out, sg_w_in, sg_v_gain, sg_w_s, sg_b_s, sg_w_out, gla_w_in, gla_w_gate2, gla_b_gate, gla_o_gain, gla_w_out, ffn_w_up, ffn_w_down, moe_router, moe_w_up, moe_w_down):
    bsz, s, d = x.shape
    tables = _nsa_tables(rel_bias)
    nsa_in, nsa_out = nsa_w_in.astype(BF16), nsa_w_out.astype(BF16)
    sg_in, sg_out = sg_w_in.astype(BF16), sg_w_out.astype(BF16)
    gla_in, gla_out = gla_w_in.astype(BF16), gla_w_out.astype(BF16)
    ffn_up, ffn_down = ffn_w_up.astype(BF16), ffn_w_down.astype(BF16)
    moe_up, moe_down = moe_w_up.astype(BF16), moe_w_down.astype(BF16)
    outs = []
    for bi in range(bsz):
        h = x[bi]
        for i in range(DEPTH):
            mixer = i % N_MIXERS
            j = i // N_MIXERS
            hn = _rmsnorm(h, norm_gain[i, 0])
            if mixer == 0:
                h = _nsa_mixer(h, hn, tables, j, nsa_in, nsa_cmp_pe[j], nsa_cmp_w1[j], nsa_cmp_w2[j],
                               nsa_q_gain[j], nsa_k_gain[j], nsa_out)
            elif mixer == 1:
                h = _sgu_mixer(h, hn, j, sg_in, sg_v_gain[j], sg_w_s[j], sg_b_s[j], sg_out)
            else:
                h = _gla_mixer(h, hn, j, gla_in, gla_w_gate2[j], gla_b_gate[j], gla_o_gain[j], gla_out)
            f = i // 2
            if i % 2 == 0:
                hn = _rmsnorm(h, norm_gain[i, 1])
                h = _dense_ffn(h, hn, f, ffn_up, ffn_down)
            else:
                h = _moe_ffn(h, norm_gain[i, 1], moe_router[f], f, moe_up, moe_down)
        outs.append(h)
    return jnp.stack(outs, axis=0)
```

```python
import math

import jax
import jax.numpy as jnp
from jax import lax
from jax.experimental import pallas as pl
from jax.experimental.pallas import tpu as pltpu

F32 = jnp.float32
BF16 = jnp.bfloat16

DEPTH = 4
N_MIXERS = 3
RMS_EPS = 1e-6
NEG = -1e30
HEAD_DIM = 128
NSA_KV_HEADS = 4
NSA_GROUP = 8
CMP_BLOCK = 32
CMP_STRIDE = 16
SEL_BLOCK = 64
SEL_TOPK = 16
WINDOW = 512
Q_BLOCK = 128
FORCED_SCORE = 1e4
T5_BUCKETS = 32
T5_MAX_DISTANCE = 2048
SG_CHUNK = 128
SG_GROUPS = 32
GLA_HEADS = 4
GLA_GATE_RANK = 16
GLA_GATE_TEMP = 16.0
GLA_CHUNK = 64
GLA_SUB = 16
GLA_SAFE_EXP = 80.0
MOE_EXPERTS = 8
MOE_TM = 512

LANES = 128
SEL_TILE = 512
SEL_NEAR_TILES = 5
SEL_FAR_WIDTH = 2
CMP_NEAR_SHIFT = 12
CW_HEADS = 4
VMEM_MB = 56
LOG2E = 1.4426950408889634
SAFE_LOGIT = 60.0
SEL_PAD_BLOCKS = 8


def _cparams(sem, vmem_mb=VMEM_MB):
    return pltpu.CompilerParams(dimension_semantics=sem, vmem_limit_bytes=vmem_mb * 2**20)


def _dot(a, b):
    return jnp.dot(a, b, preferred_element_type=F32)


def _dot_nt(a, b):
    return lax.dot_general(a, b, (((1,), (1,)), ((), ())), preferred_element_type=F32)


def _rmsnorm_body(x_ref, g_ref, o_ref):
    x = x_ref[...]
    ms = jnp.mean(x * x, axis=-1, keepdims=True)
    o_ref[...] = (x * lax.rsqrt(ms + RMS_EPS) * g_ref[...]).astype(o_ref.dtype)


def _rmsnorm(x, gain, tm=256):
    m, d = x.shape
    return pl.pallas_call(
        _rmsnorm_body,
        grid=(m // tm,),
        in_specs=[pl.BlockSpec((tm, d), lambda i: (i, 0)), pl.BlockSpec((1, d), lambda i: (0, 0))],
        out_specs=pl.BlockSpec((tm, d), lambda i: (i, 0)),
        out_shape=jax.ShapeDtypeStruct((m, d), BF16),
        compiler_params=_cparams(("parallel",)),
        name="rmsnorm",
    )(x, gain.reshape(1, d))


def _norm_rows(x, gain):
    ms = jnp.mean(x * x, axis=-1, keepdims=True)
    return (x * lax.rsqrt(ms + RMS_EPS) * gain).astype(BF16)


RT_E1, RT_E2, RT_W1, RT_W2, RT_R1, RT_R2 = range(6)


def _moe_router_body(x_ref, g_ref, r_ref, info_ref, cnt_ref, base_sc):
    @pl.when(pl.program_id(0) == 0)
    def _():
        base_sc[...] = jnp.zeros_like(base_sc)

    hn = _norm_rows(x_ref[...], g_ref[...])
    logits = _dot(hn, r_ref[...])
    lane = lax.broadcasted_iota(jnp.int32, logits.shape, 1).astype(F32)
    logits = jnp.where(lane < MOE_EXPERTS, logits, NEG)
    v1 = jnp.max(logits, axis=-1, keepdims=True)
    i1 = jnp.min(jnp.where(logits == v1, lane, 1e3), axis=-1, keepdims=True)
    rest = jnp.where(lane == i1, NEG, logits)
    v2 = jnp.max(rest, axis=-1, keepdims=True)
    i2 = jnp.min(jnp.where(rest == v2, lane, 1e3), axis=-1, keepdims=True)
    e2 = jnp.exp(v2 - v1)
    den = 1.0 + e2
    hot = jnp.where(jnp.logical_or(lane == i1, lane == i2), 1.0, 0.0)
    tm = hot.shape[0]
    earlier = (lax.broadcasted_iota(jnp.int32, (tm, tm), 0) > lax.broadcasted_iota(jnp.int32, (tm, tm), 1))
    before = _dot(jnp.where(earlier, 1.0, 0.0).astype(BF16), hot.astype(BF16)) + base_sc[...]
    r1 = jnp.sum(jnp.where(lane == i1, before, 0.0), axis=-1, keepdims=True)
    r2 = jnp.sum(jnp.where(lane == i2, before, 0.0), axis=-1, keepdims=True)
    rec = jnp.zeros_like(logits)
    for pos, val in ((RT_E1, i1), (RT_E2, i2), (RT_W1, 1.0 / den), (RT_W2, e2 / den), (RT_R1, r1), (RT_R2, r2)):
        rec = jnp.where(lane == float(pos), val, rec)
    info_ref[...] = rec
    total = base_sc[...] + jnp.sum(hot, axis=0, keepdims=True)
    base_sc[...] = total
    cnt_ref[...] = jnp.broadcast_to(total, cnt_ref.shape)


def _moe_router(x, gain, router, tm=256):
    m, d = x.shape
    rpad = jnp.zeros((d, LANES), BF16).at[:, :MOE_EXPERTS].set(router.astype(BF16))
    return pl.pallas_call(
        _moe_router_body,
        grid=(m // tm,),
        in_specs=[pl.BlockSpec((tm, d), lambda i: (i, 0)), pl.BlockSpec((1, d), lambda i: (0, 0)),
                  pl.BlockSpec((d, LANES), lambda i: (0, 0))],
        out_specs=[pl.BlockSpec((tm, LANES), lambda i: (i, 0)), pl.BlockSpec((8, LANES), lambda i: (0, 0))],
        out_shape=[jax.ShapeDtypeStruct((m, LANES), F32), jax.ShapeDtypeStruct((8, LANES), F32)],
        scratch_shapes=[pltpu.VMEM((1, LANES), F32)],
        compiler_params=_cparams(("arbitrary",)),
        name="moe_router",
    )(x, gain.reshape(1, d), rpad)


def _matmul(x, w_list, epilogue, out_list, *, tm, tn, n_total, tk=None, extras=(), name="matmul"):
    m, kdim = x.shape
    tk = kdim if tk is None else tk
    nk = kdim // tk
    nw, ne, no = len(w_list), len(extras), len(out_list)

    def body(*refs):
        x_ref = refs[0]
        w_refs = refs[1:1 + nw]
        e_refs = refs[1 + nw:1 + nw + ne]
        o_refs = refs[1 + nw + ne:1 + nw + ne + no]
        acc_refs = refs[1 + nw + ne + no:]
        n = pl.program_id(0)
        if nk == 1:
            xv = x_ref[...]
            epilogue([_dot(xv, w[...]) for w in w_refs], e_refs, o_refs, n)
        else:
            k = pl.program_id(2)

            @pl.when(k == 0)
            def _():
                for a in acc_refs:
                    a[...] = jnp.zeros_like(a)

            xv = x_ref[...]
            for a, w in zip(acc_refs, w_refs):
                a[...] += _dot(xv, w[...])

            @pl.when(k == nk - 1)
            def _():
                epilogue([a[...] for a in acc_refs], e_refs, o_refs, n)

    in_specs = [pl.BlockSpec((tm, tk), lambda n, mi, k: (mi, k))]
    in_specs += [s for _, s in w_list] + [s for _, s in extras]
    scratch = [] if nk == 1 else [pltpu.VMEM((tm, tn), F32) for _ in range(nw)]
    return pl.pallas_call(
        body,
        grid=(n_total // tn, m // tm, nk),
        in_specs=in_specs,
        out_specs=[s for _, s in out_list],
        out_shape=[s for s, _ in out_list],
        scratch_shapes=scratch,
        compiler_params=_cparams(("parallel", "parallel", "arbitrary")),
        name=name,
    )(x, *[a for a, _ in w_list], *[a for a, _ in extras])


def _wspec(tk, tn, off=0, layer=None):
    if layer is None:
        return pl.BlockSpec((tk, tn), lambda n, mi, k: (k, n + off))
    return pl.BlockSpec((None, tk, tn), lambda n, mi, k: (layer, k, n + off))


def _ospec(tm, tn):
    return pl.BlockSpec((tm, tn), lambda n, mi, k: (mi, n))


def _epi_resid(accs, e_refs, o_refs, n):
    o_refs[0][...] = e_refs[0][...] + accs[0]


def _matmul_resid(x, w, resid, *, layer=None, tm=512, tn=1024, tk=None, name="matmul_resid"):
    m, kdim = x.shape
    nn = w.shape[-1]
    tk = min(kdim, 4096) if tk is None else tk
    return _matmul(
        x, [(w, _wspec(tk, tn, 0, layer))], _epi_resid,
        [(jax.ShapeDtypeStruct((m, nn), F32), _ospec(tm, tn))],
        tm=tm, tn=tn, tk=tk, n_total=nn, extras=[(resid, _ospec(tm, tn))], name=name)[0]


def _silu(a):
    return a * (1.0 / (1.0 + jnp.exp(-a)))


def _epi_swiglu(accs, e_refs, o_refs, n):
    a, b = accs
    o_refs[0][...] = (_silu(a) * b).astype(BF16)


def _swiglu_up(x, w_up, *, layer=None, tm=512, tn=512, name="swiglu_up"):
    m, kdim = x.shape
    ff = w_up.shape[-1] // 2
    return _matmul(
        x, [(w_up, _wspec(kdim, tn, 0, layer)), (w_up, _wspec(kdim, tn, ff // tn, layer))], _epi_swiglu,
        [(jax.ShapeDtypeStruct((m, ff), BF16), _ospec(tm, tn))],
        tm=tm, tn=tn, n_total=ff, name=name)[0]


def _moe_plan(info, cnt, s):
    ne, tm = MOE_EXPERTS, MOE_TM
    e1 = info[:, RT_E1].astype(jnp.int32)
    e2 = info[:, RT_E2].astype(jnp.int32)
    counts = cnt[0, :ne].astype(jnp.int32)
    padded = (counts + tm - 1) // tm * tm
    ends = jnp.cumsum(padded)
    off = ends - padded
    dest1 = off[e1] + info[:, RT_R1].astype(jnp.int32)
    dest2 = off[e2] + info[:, RT_R2].astype(jnp.int32)
    ntile = (2 * s) // tm + ne
    tile_start = jnp.arange(ntile, dtype=jnp.int32) * tm
    tile_expert = jnp.minimum(jnp.sum(tile_start[:, None] >= ends[None, :], axis=1), ne - 1).astype(jnp.int32)
    nvalid = (ends[ne - 1:ne] // tm).astype(jnp.int32)
    tok = jnp.arange(s, dtype=jnp.int32)
    src = jnp.zeros((ntile * tm,), jnp.int32).at[dest1].set(tok).at[dest2].set(tok)
    return dest1, dest2, src, tile_expert, nvalid


def _row_copy(src_hbm, row, dst, r, sem):
    return pltpu.make_async_copy(src_hbm.at[pl.ds(row, 1)], dst.at[pl.ds(r, 1)], sem)


def _moe_gather_body(src_ref, nv_ref, h_hbm, g_ref, o_ref, buf, sems):
    t = pl.program_id(0)
    nv = nv_ref[0]
    tm = buf.shape[1]

    def rows(tile, slot, start):
        def body(r, carry):
            cp = _row_copy(h_hbm, src_ref[tile * tm + r], buf.at[slot], r, sems.at[slot])
            if start:
                cp.start()
            else:
                cp.wait()
            return carry

        lax.fori_loop(0, tm, body, 0)

    @pl.when(jnp.logical_and(t == 0, nv > 0))
    def _():
        rows(0, 0, True)

    @pl.when(t + 1 < nv)
    def _():
        rows(t + 1, (t + 1) % 2, True)

    @pl.when(t < nv)
    def _():
        rows(t, t % 2, False)
        o_ref[...] = _norm_rows(buf[t % 2], g_ref[...])

    @pl.when(t >= nv)
    def _():
        o_ref[...] = jnp.zeros_like(o_ref)


def _moe_gather(h, gain, src, nvalid):
    s, d = h.shape
    tm = MOE_TM
    ntile = src.shape[0] // tm
    return pl.pallas_call(
        _moe_gather_body,
        grid_spec=pltpu.PrefetchScalarGridSpec(
            num_scalar_prefetch=2,
            grid=(ntile,),
            in_specs=[pl.BlockSpec(memory_space=pl.ANY),
                      pl.BlockSpec((1, d), lambda t, src_, nv_: (0, 0))],
            out_specs=pl.BlockSpec((tm, d), lambda t, src_, nv_: (t, 0)),
            scratch_shapes=[pltpu.VMEM((2, tm, d), F32), pltpu.SemaphoreType.DMA((2,))]),
        out_shape=jax.ShapeDtypeStruct((ntile * tm, d), BF16),
        compiler_params=_cparams(("arbitrary",)),
        name="moe_gather",
    )(src, nvalid, h, gain.reshape(1, d))


def _moe_up_body(te_ref, nv_ref, x_ref, wa_ref, wb_ref, o_ref):
    t = pl.program_id(1)

    @pl.when(t < nv_ref[0])
    def _():
        x = x_ref[...]
        o_ref[...] = (_silu(_dot(x, wa_ref[...])) * _dot(x, wb_ref[...])).astype(BF16)

    @pl.when(t >= nv_ref[0])
    def _():
        o_ref[...] = jnp.zeros_like(o_ref)


def _moe_up(xs, w_up, layer, tile_expert, nvalid, *, tn=512):
    p, d = xs.shape
    tm = MOE_TM
    ff = w_up.shape[-1] // 2
    nn = ff // tn
    return pl.pallas_call(
        _moe_up_body,
        grid_spec=pltpu.PrefetchScalarGridSpec(
            num_scalar_prefetch=2,
            grid=(nn, p // tm),
            in_specs=[pl.BlockSpec((tm, d), lambda n, t, te, nv: (t, 0)),
                      pl.BlockSpec((None, None, d, tn), lambda n, t, te, nv: (layer, te[t], 0, n)),
                      pl.BlockSpec((None, None, d, tn), lambda n, t, te, nv: (layer, te[t], 0, n + nn))],
            out_specs=pl.BlockSpec((tm, tn), lambda n, t, te, nv: (t, n))),
        out_shape=jax.ShapeDtypeStruct((p, ff), BF16),
        compiler_params=_cparams(("parallel", "arbitrary")),
        name="moe_up",
    )(tile_expert, nvalid, xs, w_up, w_up)


def _moe_down_body(te_ref, nv_ref, x_ref, w_ref, o_ref):
    t = pl.program_id(1)

    @pl.when(t < nv_ref[0])
    def _():
        o_ref[...] = _dot(x_ref[...], w_ref[...])

    @pl.when(t >= nv_ref[0])
    def _():
        o_ref[...] = jnp.zeros_like(o_ref)


def _moe_down(act, w_down, layer, tile_expert, nvalid, *, tn=1024):
    p, ff = act.shape
    tm = MOE_TM
    d = w_down.shape[-1]
    return pl.pallas_call(
        _moe_down_body,
        grid_spec=pltpu.PrefetchScalarGridSpec(
            num_scalar_prefetch=2,
            grid=(d // tn, p // tm),
            in_specs=[pl.BlockSpec((tm, ff), lambda n, t, te, nv: (t, 0)),
                      pl.BlockSpec((None, None, ff, tn), lambda n, t, te, nv: (layer, te[t], 0, n))],
            out_specs=pl.BlockSpec((tm, tn), lambda n, t, te, nv: (t, n))),
        out_shape=jax.ShapeDtypeStruct((p, d), F32),
        compiler_params=_cparams(("parallel", "arbitrary")),
        name="moe_down",
    )(tile_expert, nvalid, act, w_down)


def _moe_combine_body(d1_ref, d2_ref, y_hbm, h_ref, info_ref, o_ref, buf, sems):
    t = pl.program_id(0)
    nt = pl.num_programs(0)
    tm = h_ref.shape[0]

    def rows(tile, slot, start):
        def body(r, carry):
            for k, dest in enumerate((d1_ref, d2_ref)):
                cp = _row_copy(y_hbm, dest[tile * tm + r], buf.at[2 * slot + k], r, sems.at[2 * slot + k])
                if start:
                    cp.start()
                else:
                    cp.wait()
            return carry

        lax.fori_loop(0, tm, body, 0)

    @pl.when(t == 0)
    def _():
        rows(0, 0, True)

    @pl.when(t + 1 < nt)
    def _():
        rows(t + 1, (t + 1) % 2, True)

    slot = t % 2
    rows(t, slot, False)
    info = info_ref[...]
    o_ref[...] = h_ref[...] + (info[:, RT_W1:RT_W1 + 1] * buf[2 * slot]
                               + info[:, RT_W2:RT_W2 + 1] * buf[2 * slot + 1])


def _moe_combine(h, y, info, dest1, dest2, tm=256):
    s, d = h.shape
    return pl.pallas_call(
        _moe_combine_body,
        grid_spec=pltpu.PrefetchScalarGridSpec(
            num_scalar_prefetch=2,
            grid=(s // tm,),
            in_specs=[pl.BlockSpec(memory_space=pl.ANY),
                      pl.BlockSpec((tm, d), lambda t, a, b: (t, 0)),
                      pl.BlockSpec((tm, LANES), lambda t, a, b: (t, 0))],
            out_specs=pl.BlockSpec((tm, d), lambda t, a, b: (t, 0)),
            scratch_shapes=[pltpu.VMEM((4, tm, d), F32), pltpu.SemaphoreType.DMA((4,))]),
        out_shape=jax.ShapeDtypeStruct((s, d), F32),
        compiler_params=_cparams(("arbitrary",)),
        name="moe_combine",
    )(dest1, dest2, y, h, info)


def _nsa_q_proj(hn, wq, layer, q_gain, nn, *, tm=512, tn=1024):
    m, kdim = hn.shape
    scale = HEAD_DIM ** -0.5 * LOG2E

    def epi(accs, e_refs, o_refs, n):
        acc = accs[0]
        gain = e_refs[0][...]
        for c in range(tn // HEAD_DIM):
            a = acc[:, c * HEAD_DIM:(c + 1) * HEAD_DIM]
            ms = jnp.mean(a * a, axis=-1, keepdims=True)
            o_refs[0][:, c * HEAD_DIM:(c + 1) * HEAD_DIM] = (
                a * lax.rsqrt(ms + RMS_EPS) * gain * scale).astype(BF16)

    return _matmul(
        hn, [(wq, _wspec(kdim, tn, 0, layer))], epi,
        [(jax.ShapeDtypeStruct((m, nn), BF16), _ospec(tm, tn))],
        tm=tm, tn=tn, n_total=nn,
        extras=[(q_gain.reshape(1, HEAD_DIM), pl.BlockSpec((1, HEAD_DIM), lambda n, mi, k: (0, 0)))],
        name="nsa_q_proj")[0]


def _nsa_kv_proj(hn, wkv, layer, k_gain, col0, *, tm=512):
    m, kdim = hn.shape
    g_ = NSA_KV_HEADS
    tn = g_ * HEAD_DIM
    gains = jnp.ones((6, 1, HEAD_DIM), F32).at[2, 0].set(k_gain[1]).at[4, 0].set(k_gain[2])

    def epi(accs, e_refs, o_refs, n):
        acc = accs[0]
        kg = e_refs[0][0]
        do_norm = jnp.logical_or(n == 2, n == 4)
        for g in range(g_):
            a = acc[:, g * HEAD_DIM:(g + 1) * HEAD_DIM]
            ms = jnp.mean(a * a, axis=-1, keepdims=True)
            an = a * lax.rsqrt(ms + RMS_EPS) * kg
            o_refs[0][g] = jnp.where(do_norm, an, a).astype(BF16)

    return _matmul(
        hn, [(wkv, _wspec(kdim, tn, col0 // tn, layer))], epi,
        [(jax.ShapeDtypeStruct((6 * g_, m, HEAD_DIM), BF16),
          pl.BlockSpec((g_, tm, HEAD_DIM), lambda n, mi, k: (n, mi, 0)))],
        tm=tm, tn=tn, n_total=6 * tn,
        extras=[(gains, pl.BlockSpec((1, 1, HEAD_DIM), lambda n, mi, k: (n, 0, 0)))],
        name="nsa_kv_proj")[0]


def _nsa_gate_proj(hn, wg, *, tm=512):
    m, kdim = hn.shape
    nn = wg.shape[1]

    def epi(accs, e_refs, o_refs, n):
        o_refs[0][...] = 1.0 / (1.0 + jnp.exp(-accs[0]))

    return _matmul(
        hn, [(wg, _wspec(kdim, nn))], epi,
        [(jax.ShapeDtypeStruct((m, nn), F32), _ospec(tm, nn))],
        tm=tm, tn=nn, n_total=nn, name="nsa_gate_proj")[0]


def _compress_body(r_ref, pe_ref, w1_ref, w2_ref, kg_ref, o_ref, ot_ref):
    j = pl.program_id(0)
    r = r_ref[0].astype(F32)
    pe = pe_ref[0]
    half = r.shape[1]
    top = (r + pe[0:1]).astype(BF16)
    bot = (r + pe[1:2]).astype(BF16)
    a = _dot(top, w1_ref[0, 0:half, :])
    b = _dot(bot, w1_ref[0, half:2 * half, :])
    nrow = r.shape[0]
    hid = a + pltpu.roll(b, nrow - 1, 0)
    y = _dot(jax.nn.gelu(hid).astype(BF16), w2_ref[0])
    ms = jnp.mean(y * y, axis=-1, keepdims=True)
    yn = y * lax.rsqrt(ms + RMS_EPS) * kg_ref[...]
    out = jnp.where(j == 0, yn, y)
    o_ref[0, 0] = out.astype(BF16)
    ot_ref[0, 0] = jnp.transpose(out).astype(BF16)


def _compress(kv, pe, w1, w2, k_gain0):
    g_ = NSA_KV_HEADS
    s = kv.shape[1]
    nrow = s // CMP_STRIDE
    wid = CMP_STRIDE * HEAD_DIM
    r = kv[:2 * g_].reshape(2 * g_, nrow, wid)
    pe2 = pe.reshape(2, 2, wid)
    return pl.pallas_call(
        _compress_body,
        grid=(2, g_),
        in_specs=[pl.BlockSpec((1, nrow, wid), lambda j, g: (j * g_ + g, 0, 0)),
                  pl.BlockSpec((1, 2, wid), lambda j, g: (j, 0, 0)),
                  pl.BlockSpec((1, 2 * wid, HEAD_DIM), lambda j, g: (j, 0, 0)),
                  pl.BlockSpec((1, HEAD_DIM, HEAD_DIM), lambda j, g: (j, 0, 0)),
                  pl.BlockSpec((1, HEAD_DIM), lambda j, g: (0, 0))],
        out_specs=[pl.BlockSpec((1, 1, nrow, HEAD_DIM), lambda j, g: (j, g, 0, 0)),
                   pl.BlockSpec((1, 1, HEAD_DIM, nrow), lambda j, g: (j, g, 0, 0))],
        out_shape=[jax.ShapeDtypeStruct((2, g_, nrow, HEAD_DIM), BF16),
                   jax.ShapeDtypeStruct((2, g_, HEAD_DIM, nrow), BF16)],
        compiler_params=_cparams(("parallel", "parallel")),
        name="nsa_compress",
    )(r, pe2, w1.astype(BF16), w2.astype(BF16), k_gain0.reshape(1, HEAD_DIM))


def _t5_bucket(dist):
    dist = jnp.maximum(dist, 0)
    max_exact = T5_BUCKETS // 2
    d_f = jnp.maximum(dist, 1).astype(F32)
    log_b = max_exact + (jnp.log(d_f / max_exact) / math.log(T5_MAX_DISTANCE / max_exact)
                         * (T5_BUCKETS - max_exact)).astype(jnp.int32)
    log_b = jnp.minimum(log_b, T5_BUCKETS - 1)
    return jnp.where(dist < max_exact, dist, log_b)


def _bias_table(tbl, dist, valid, shift):
    onehot = jax.nn.one_hot(_t5_bucket(dist).reshape(-1), T5_BUCKETS, dtype=F32)
    t = tbl - tbl[T5_BUCKETS - 1:T5_BUCKETS] if shift else tbl
    vals = jnp.einsum("nb,bh->hn", onehot, t, precision=lax.Precision.HIGHEST)
    vals = vals.reshape((tbl.shape[1],) + dist.shape) * LOG2E
    return jnp.where(valid[None], vals, NEG)


def _nsa_tables(rel_bias):
    tbl = rel_bias.astype(F32)
    r = jnp.arange(Q_BLOCK)
    jw = jnp.arange(WINDOW + Q_BLOCK)
    dw = WINDOW + r[None, :] - jw[:, None]
    tab_w = _bias_table(tbl, dw, (dw >= 0) & (dw < WINDOW), False)
    a = jnp.arange(16)
    cc = jnp.arange(2 * LANES)
    dc = (CMP_NEAR_SHIFT * Q_BLOCK + Q_BLOCK * a[:, None, None] + r[None, None, :]
          - CMP_STRIDE * cc[None, :, None] - (CMP_BLOCK - 1))
    tab_c = _bias_table(tbl, dc, dc >= 0, True)
    jj = jnp.arange(SEL_NEAR_TILES)
    col = jnp.arange(SEL_TILE)
    ds_ = r[None, None, :] - Q_BLOCK + SEL_TILE * (jj[:, None, None] + 1) - col[None, :, None]
    tab_s = _bias_table(tbl, ds_, ds_ >= 0, True)
    return tab_w, tab_c, tab_s, tbl


def _sel_to_cmp(nsel, nc_pad):
    c_start = jnp.arange(nc_pad) * CMP_STRIDE
    s_start = jnp.arange(nsel) * SEL_BLOCK
    overlap = jnp.clip(jnp.minimum(c_start[None, :] + CMP_BLOCK, s_start[:, None] + SEL_BLOCK)
                       - jnp.maximum(c_start[None, :], s_start[:, None]), 0)
    return (overlap.astype(F32) / CMP_BLOCK).astype(BF16)


def _heads_t(q_ref, h0, nh):
    cols = [jnp.transpose(q_ref[:, (h0 + h) * HEAD_DIM:(h0 + h + 1) * HEAD_DIM].astype(F32)) for h in range(nh)]
    return jnp.concatenate(cols, axis=1).astype(BF16)


def _softmax_keys(s, bounded):
    if bounded:
        p = jnp.exp2(s)
        l = jnp.sum(p, axis=0, keepdims=True)
        return p, jnp.where(l > 0.0, 1.0 / l, 0.0)
    m = jnp.max(s, axis=0, keepdims=True)
    p = jnp.exp2(s - m)
    l = jnp.sum(p, axis=0, keepdims=True)
    return p, jnp.where(m > 0.5 * NEG, 1.0 / l, 0.0)


def _logit_bound(rel_bias, q_gain, k_gain):
    qk = HEAD_DIM * (HEAD_DIM ** -0.5 * LOG2E) * jnp.max(jnp.abs(q_gain)) * jnp.max(jnp.abs(k_gain))
    return (1.02 * qk + 2.0 * LOG2E * jnp.max(jnp.abs(rel_bias))).astype(F32).reshape(1)


def _nsa_cw_body(bnd_ref, q_ref, gt_ref, kc_ref, vct_ref, mselt_ref, tabc_ref, kw_ref, vwt_ref, tabw_ref,
                 ocwt_ref, negt_ref):
    bounded = bnd_ref[0] < SAFE_LOGIT
    i = pl.program_id(1)

    @pl.when(bounded)
    def _():
        _nsa_cw_branches(True, i, q_ref, gt_ref, kc_ref, vct_ref, mselt_ref, tabc_ref, kw_ref, vwt_ref, tabw_ref,
                         ocwt_ref, negt_ref)

    @pl.when(jnp.logical_not(bounded))
    def _():
        _nsa_cw_branches(False, i, q_ref, gt_ref, kc_ref, vct_ref, mselt_ref, tabc_ref, kw_ref, vwt_ref, tabw_ref,
                         ocwt_ref, negt_ref)


def _nsa_cw_branches(bounded, i, q_ref, gt_ref, kc_ref, vct_ref, mselt_ref, tabc_ref, kw_ref, vwt_ref, tabw_ref,
                     ocwt_ref, negt_ref):
    f = (i + 16 - CMP_NEAR_SHIFT) // 16 - 1
    kc = kc_ref[0, 0]
    vct = vct_ref[0, 0]
    nc = kc.shape[0]
    nsel = mselt_ref.shape[0]
    wlen = WINDOW + Q_BLOCK
    nslab = wlen // LANES
    start = pl.multiple_of(i * Q_BLOCK, Q_BLOCK)
    kw = kw_ref[0, pl.ds(start, wlen), :]
    vwt = jnp.concatenate([vwt_ref[0, i + c] for c in range(nslab)], axis=1)
    roww = lax.broadcasted_iota(jnp.int32, (wlen, LANES), 0)
    w_pad = roww < (WINDOW - Q_BLOCK * i)
    gt_t = jnp.transpose(gt_ref[...])
    psum = jnp.zeros((nc, Q_BLOCK), F32)
    hg = CW_HEADS
    for h0 in range(0, NSA_GROUP, hg):
        qt = _heads_t(q_ref, h0, hg)
        s = _dot(kc, qt)
        ta = jnp.concatenate([tabc_ref[h0 + h, 0, 0:LANES, :] for h in range(hg)], axis=1)
        tb = jnp.concatenate([tabc_ref[h0 + h, 0, LANES:2 * LANES, :] for h in range(hg)], axis=1)
        pieces = []
        for ch in range(nc // LANES):
            rest = jnp.where(ch > f + 1, NEG, 0.0)
            bias = jnp.where(ch == f, ta, jnp.where(ch == f + 1, tb, rest))
            pieces.append(s[ch * LANES:(ch + 1) * LANES] + bias)
        s = jnp.concatenate(pieces, axis=0)
        p, linv = _softmax_keys(s, bounded)
        pn = p * linv
        for h in range(hg):
            psum = psum + pn[:, h * Q_BLOCK:(h + 1) * Q_BLOCK]
        oc = _dot(vct, pn.astype(BF16))
        tw = jnp.concatenate([jnp.where(w_pad, NEG, tabw_ref[h0 + h]) for h in range(hg)], axis=1)
        sw = _dot(kw, qt) + tw
        pw, lwinv = _softmax_keys(sw, bounded)
        ow = _dot(vwt, pw.astype(BF16)) * lwinv
        for h in range(hg):
            hh = h0 + h
            cs = slice(h * Q_BLOCK, (h + 1) * Q_BLOCK)
            ocwt_ref[0, 0, hh * HEAD_DIM:(hh + 1) * HEAD_DIM, :] = (
                gt_t[hh:hh + 1] * oc[:, cs] + gt_t[16 + hh:17 + hh] * ow[:, cs])
    p_hi = psum.astype(BF16)
    p_lo = (psum - p_hi.astype(F32)).astype(BF16)
    mselt = mselt_ref[...]
    imp = _dot(mselt, p_hi) + _dot(mselt, p_lo)
    blk = lax.broadcasted_iota(jnp.int32, (nsel, Q_BLOCK), 0).astype(F32)
    qpos = lax.broadcasted_iota(jnp.int32, (nsel, Q_BLOCK), 1)
    cur = (i * (Q_BLOCK // SEL_BLOCK)).astype(F32) + jnp.where(qpos >= SEL_BLOCK, 1.0, 0.0)
    forced = jnp.logical_or(blk == cur, blk == 0.0)
    val = jnp.where(forced, FORCED_SCORE, jnp.where(blk <= cur, imp, -1.0))
    chosen = jnp.zeros((nsel, Q_BLOCK), F32)
    for _ in range(SEL_TOPK):
        mx = jnp.max(val, axis=0, keepdims=True)
        first = jnp.min(jnp.where(val == mx, blk, 1e4), axis=0, keepdims=True)
        hit = blk == first
        chosen = jnp.where(hit, 1.0, chosen)
        val = jnp.where(hit, -2.0, val)
    neg = jnp.where(chosen > 0.0, 0.0, NEG)
    for pr in range(SEL_PAD_BLOCKS // 2):
        negt_ref[0, 0, pr] = jnp.full((2, Q_BLOCK), NEG, F32)
    for pr in range(nsel // 2):
        negt_ref[0, 0, SEL_PAD_BLOCKS // 2 + pr] = neg[2 * pr:2 * pr + 2]


def _nsa_cw(bound, q, gates, kc, vct, mselt, tab_c, kw, vwt, tab_w):
    s, hd = q.shape
    g_ = NSA_KV_HEADS
    nq = s // Q_BLOCK
    gw = NSA_GROUP * HEAD_DIM
    nc = kc.shape[2]
    nsel = mselt.shape[0]
    spad = kw.shape[1]
    nslab = vwt.shape[1]
    wlen = WINDOW + Q_BLOCK

    def var(i):
        return (i + 16 - CMP_NEAR_SHIFT) % 16

    return pl.pallas_call(
        _nsa_cw_body,
        grid=(g_, nq),
        in_specs=[
            pl.BlockSpec(memory_space=pltpu.SMEM),
            pl.BlockSpec((Q_BLOCK, gw), lambda g, i: (i, g)),
            pl.BlockSpec((Q_BLOCK, LANES), lambda g, i: (i, g)),
            pl.BlockSpec((1, 1, nc, HEAD_DIM), lambda g, i: (0, g, 0, 0)),
            pl.BlockSpec((1, 1, HEAD_DIM, nc), lambda g, i: (1, g, 0, 0)),
            pl.BlockSpec((nsel, nc), lambda g, i: (0, 0)),
            pl.BlockSpec((NSA_GROUP, 1, 2 * LANES, Q_BLOCK), lambda g, i: (g, var(i), 0, 0)),
            pl.BlockSpec((1, spad, HEAD_DIM), lambda g, i: (g, 0, 0)),
            pl.BlockSpec((1, nslab, HEAD_DIM, LANES), lambda g, i: (g, 0, 0, 0)),
            pl.BlockSpec((NSA_GROUP, wlen, Q_BLOCK), lambda g, i: (g, 0, 0)),
        ],
        out_specs=[pl.BlockSpec((1, 1, gw, Q_BLOCK), lambda g, i: (g, i, 0, 0)),
                   pl.BlockSpec((1, 1, (SEL_PAD_BLOCKS + nsel) // 2, 2, Q_BLOCK), lambda g, i: (g, i, 0, 0, 0))],
        out_shape=[jax.ShapeDtypeStruct((g_, nq, gw, Q_BLOCK), F32),
                   jax.ShapeDtypeStruct((g_, nq, (SEL_PAD_BLOCKS + nsel) // 2, 2, Q_BLOCK), F32)],
        compiler_params=_cparams(("parallel", "arbitrary")),
        name="nsa_cmp_win",
    )(bound, q, gates, kc, vct, mselt, tab_c, kw, vwt, tab_w)


def _nsa_sel_body(bnd_ref, q_ref, gt_ref, ks_ref, vst_ref, negp_ref, tabs_ref, ocwt_ref, o_ref,
                  m_sc, l_sc, acc_sc):
    i = pl.program_id(1)
    ntile = i // (SEL_TILE // Q_BLOCK) + 1
    nh = NSA_GROUP
    spt = SEL_TILE // LANES
    ppt = SEL_TILE // (2 * SEL_BLOCK)
    qt = _heads_t(q_ref, 0, nh)

    def tile(jj, near, bounded, nsub=1):
        slab0 = (i + 1) - spt * jj
        row0 = pl.multiple_of(slab0 * LANES, LANES)
        kt = ks_ref[0, pl.ds(row0, nsub * SEL_TILE), :]
        vtt = jnp.concatenate([vst_ref[0, slab0 + c] for c in range(nsub * spt)], axis=1)
        rows = []
        for c in range(nsub * ppt):
            pair = negp_ref[0, 0, slab0 + c]
            rows += [jnp.broadcast_to(pair[r:r + 1], (SEL_BLOCK, Q_BLOCK)) for r in range(2)]
        mk = jnp.concatenate(rows, axis=0)
        if near:
            bias = jnp.concatenate(
                [jnp.concatenate([tabs_ref[h, jj - c] for c in range(nsub)], axis=0) + mk for h in range(nh)],
                axis=1)
        else:
            bias = jnp.tile(mk, (1, nh))
        sc = _dot(kt, qt) + bias
        if bounded:
            p = jnp.exp2(sc)
            l_sc[...] = l_sc[...] + jnp.sum(p, axis=0, keepdims=True)
            acc_sc[...] = acc_sc[...] + _dot(vtt, p.astype(BF16))
        else:
            m_old = m_sc[...]
            m_new = jnp.maximum(m_old, jnp.max(sc, axis=0, keepdims=True))
            alpha = jnp.exp2(m_old - m_new)
            p = jnp.exp2(sc - m_new)
            l_sc[...] = alpha * l_sc[...] + jnp.sum(p, axis=0, keepdims=True)
            acc_sc[...] = alpha * acc_sc[...] + _dot(vtt, p.astype(BF16))
            m_sc[...] = m_new

    def run(bounded):
        m_sc[...] = jnp.full(m_sc.shape, NEG, F32)
        l_sc[...] = jnp.zeros(l_sc.shape, F32)
        acc_sc[...] = jnp.zeros(acc_sc.shape, F32)
        @pl.when(ntile >= SEL_NEAR_TILES)
        def _():
            for jj in range(SEL_FAR_WIDTH - 1, SEL_NEAR_TILES, SEL_FAR_WIDTH):
                tile(jj, True, bounded, SEL_FAR_WIDTH)
            for jj in range(SEL_NEAR_TILES - SEL_NEAR_TILES % SEL_FAR_WIDTH, SEL_NEAR_TILES):
                tile(jj, True, bounded)

        @pl.when(ntile < SEL_NEAR_TILES)
        def _():
            for jj in range(SEL_NEAR_TILES - 1):
                @pl.when(jj < ntile)
                def _():
                    tile(jj, True, bounded)

        nfar = jnp.maximum(ntile - SEL_NEAR_TILES, 0)

        def far(k, carry):
            tile(SEL_NEAR_TILES + SEL_FAR_WIDTH * k + SEL_FAR_WIDTH - 1, False, bounded, SEL_FAR_WIDTH)
            return carry

        lax.fori_loop(0, nfar // SEL_FAR_WIDTH, far, 0)

        def far_rest(jj, carry):
            tile(jj, False, bounded)
            return carry

        lax.fori_loop(SEL_NEAR_TILES + nfar // SEL_FAR_WIDTH * SEL_FAR_WIDTH, ntile, far_rest, 0)

    bounded = bnd_ref[0] < SAFE_LOGIT

    @pl.when(bounded)
    def _():
        run(True)

    @pl.when(jnp.logical_not(bounded))
    def _():
        run(False)

    o_t = acc_sc[...] * (1.0 / l_sc[...])
    gt_t = jnp.transpose(gt_ref[...])
    for h in range(nh):
        hs = slice(h * HEAD_DIM, (h + 1) * HEAD_DIM)
        oh = ocwt_ref[0, 0, hs, :] + gt_t[8 + h:9 + h] * o_t[:, h * Q_BLOCK:(h + 1) * Q_BLOCK]
        o_ref[:, hs] = jnp.transpose(oh).astype(BF16)


def _nsa_sel(bound, q, gates, ks, vst, negp, tab_s, ocwt):
    s, hd = q.shape
    g_ = NSA_KV_HEADS
    nq = s // Q_BLOCK
    gw = NSA_GROUP * HEAD_DIM
    spad = ks.shape[1]
    nslab = vst.shape[1]
    npair = negp.shape[2]
    return pl.pallas_call(
        _nsa_sel_body,
        grid=(g_, nq),
        in_specs=[
            pl.BlockSpec(memory_space=pltpu.SMEM),
            pl.BlockSpec((Q_BLOCK, gw), lambda g, i: (i, g)),
            pl.BlockSpec((Q_BLOCK, LANES), lambda g, i: (i, g)),
            pl.BlockSpec((1, spad, HEAD_DIM), lambda g, i: (g, 0, 0)),
            pl.BlockSpec((1, nslab, HEAD_DIM, LANES), lambda g, i: (g, 0, 0, 0)),
            pl.BlockSpec((1, 1, npair, 2, Q_BLOCK), lambda g, i: (g, i, 0, 0, 0)),
            pl.BlockSpec((NSA_GROUP, SEL_NEAR_TILES, SEL_TILE, Q_BLOCK), lambda g, i: (g, 0, 0, 0)),
            pl.BlockSpec((1, 1, gw, Q_BLOCK), lambda g, i: (g, i, 0, 0)),
        ],
        out_specs=pl.BlockSpec((Q_BLOCK, gw), lambda g, i: (i, g)),
        out_shape=jax.ShapeDtypeStruct((s, hd), BF16),
        scratch_shapes=[pltpu.VMEM((1, NSA_GROUP * Q_BLOCK), F32), pltpu.VMEM((1, NSA_GROUP * Q_BLOCK), F32),
                        pltpu.VMEM((HEAD_DIM, NSA_GROUP * Q_BLOCK), F32)],
        compiler_params=_cparams(("parallel", "arbitrary")),
        name="nsa_selected",
    )(bound, q, gates, ks, vst, negp, tab_s, ocwt)


def _pad_keys(x, pad):
    xp = jnp.pad(x, ((0, 0), (pad, 0), (0, 0)))
    g_, sp, dh = xp.shape
    return xp, xp.reshape(g_, sp // LANES, LANES, dh).swapaxes(-1, -2)


def _nsa_mixer(h, hn, tables, layer, w_in, cmp_pe, cmp_w1, cmp_w2, q_gain, k_gain, w_out):
    s, d = h.shape
    g_, hpg, dh = NSA_KV_HEADS, NSA_GROUP, HEAD_DIM
    hd = g_ * hpg * dh
    tab_w, tab_c, tab_s, tbl = tables
    q = _nsa_q_proj(hn, w_in, layer, q_gain, hd)
    kv = _nsa_kv_proj(hn, w_in, layer, k_gain, hd)
    wg = w_in[layer, :, hd + 6 * g_ * dh:].reshape(d, g_, hpg, 3).transpose(0, 1, 3, 2).reshape(d, g_, 3 * hpg)
    wg = jnp.pad(wg, ((0, 0), (0, 0), (0, LANES - 3 * hpg))).reshape(d, g_ * LANES)
    gates = _nsa_gate_proj(hn, wg)
    kc, kct = _compress(kv, cmp_pe, cmp_w1, cmp_w2, k_gain[0])
    mselt = _sel_to_cmp(s // SEL_BLOCK, s // CMP_STRIDE)
    ks, _ = _pad_keys(kv[2 * g_:3 * g_], SEL_TILE)
    _, vst = _pad_keys(kv[3 * g_:4 * g_], SEL_TILE)
    kw, _ = _pad_keys(kv[4 * g_:5 * g_], WINDOW)
    _, vwt = _pad_keys(kv[5 * g_:6 * g_], WINDOW)
    bound = _logit_bound(tbl, q_gain, k_gain)
    ocwt, negp = _nsa_cw(bound, q, gates, kc, kct, mselt, tab_c, kw, vwt, tab_w)
    o = _nsa_sel(bound, q, gates, ks, vst, negp, tab_s, ocwt)
    return _matmul_resid(o, w_out, h, layer=layer, name="nsa_out")


def _gelu_proj(hn, w, layer, *, tm=512, tn=1024):
    m, kdim = hn.shape
    nn = w.shape[-1]

    def epi(accs, e_refs, o_refs, n):
        o_refs[0][...] = jax.nn.gelu(accs[0]).astype(BF16)

    return _matmul(
        hn, [(w, _wspec(kdim, tn, 0, layer))], epi,
        [(jax.ShapeDtypeStruct((m, nn), BF16), _ospec(tm, tn))],
        tm=tm, tn=tn, n_total=nn, name="sgu_in")[0]


def _sgu_mix_body(u_ref, v_ref, gain_ref, w_ref, bt_ref, o_ref):
    vn = _norm_rows(v_ref[...].astype(F32), gain_ref[...])
    t = w_ref.shape[1]
    causal = (lax.broadcasted_iota(jnp.int32, (t, t), 0) >= lax.broadcasted_iota(jnp.int32, (t, t), 1))
    bt = bt_ref[...]
    gd = vn.shape[1] // SG_GROUPS
    for g in range(SG_GROUPS):
        gs = slice(g * gd, (g + 1) * gd)
        w = jnp.where(causal, w_ref[g], 0.0).astype(BF16)
        mixed = _dot(w, vn[:, gs]) + bt[:, g:g + 1]
        o_ref[:, gs] = (u_ref[:, gs].astype(F32) * mixed).astype(BF16)


def _sgu_mix(uv, v_gain, w_s, b_s):
    s = uv.shape[0]
    wd = uv.shape[1] // 2
    t = SG_CHUNK
    return pl.pallas_call(
        _sgu_mix_body,
        grid=(s // t,),
        in_specs=[pl.BlockSpec((t, wd), lambda c: (c, 0)),
                  pl.BlockSpec((t, wd), lambda c: (c, 1)),
                  pl.BlockSpec((1, wd), lambda c: (0, 0)),
                  pl.BlockSpec((SG_GROUPS, t, t), lambda c: (0, 0, 0)),
                  pl.BlockSpec((t, SG_GROUPS), lambda c: (0, 0))],
        out_specs=pl.BlockSpec((t, wd), lambda c: (c, 0)),
        out_shape=jax.ShapeDtypeStruct((s, wd), BF16),
        compiler_params=_cparams(("parallel",)),
        name="sgu_mix",
    )(uv, uv, v_gain.reshape(1, wd), w_s, b_s.T)


def _sgu_mixer(h, hn, layer, w_in, v_gain, w_s, b_s, w_out):
    uv = _gelu_proj(hn, w_in, layer)
    y = _sgu_mix(uv, v_gain, w_s, b_s)
    return _matmul_resid(y, w_out, h, layer=layer, name="sgu_out")


def _gla_body(q_ref, k_ref, v_ref, r_ref, g1_ref, wg_ref, bg_ref, og_ref, o_ref, state_sc, diag_sc):
    @pl.when(pl.program_id(0) == 0)
    def _():
        state_sc[...] = jnp.zeros_like(state_sc)

    nh = state_sc.shape[0]
    dk = q_ref.shape[1] // nh
    dv = v_ref.shape[1] // nh
    g1 = g1_ref[...].astype(BF16)
    for hh in range(nh):
        ks = slice(hh * dk, (hh + 1) * dk)
        vs = slice(hh * dv, (hh + 1) * dv)
        o_ref[:, vs] = _gla_head(q_ref[:, ks], k_ref[:, ks], v_ref[:, vs], r_ref[:, vs], g1, wg_ref[:, ks],
                                 bg_ref[:, ks], og_ref[...], state_sc.at[hh], diag_sc.at[hh])


def _gla_head(q_in, k_in, v, r_in, g1, wg, bg, og, state_sc, diag_sc):
    ch, dk = q_in.shape
    x = _dot(g1, wg) + bg
    log_a = (jnp.minimum(x, 0.0) - jnp.log(1.0 + jnp.exp(-jnp.abs(x)))) * (1.0 / GLA_GATE_TEMP)
    tri = (lax.broadcasted_iota(jnp.int32, (ch, ch), 0)
           >= lax.broadcasted_iota(jnp.int32, (ch, ch), 1))
    tri_b = jnp.where(tri, 1.0, 0.0).astype(BF16)
    a_hi = log_a.astype(BF16)
    a_lo = (log_a - a_hi.astype(F32)).astype(BF16)
    b = _dot(tri_b, a_hi) + _dot(tri_b, a_lo)
    q = q_in.astype(F32) * (dk ** -0.5)
    k = k_in.astype(F32)
    state = state_sc[...]
    o = _dot((q * jnp.exp(b)).astype(BF16), state.astype(BF16))
    nsub = ch // GLA_SUB
    ends = [jnp.broadcast_to(b[(jb + 1) * GLA_SUB - 1:(jb + 1) * GLA_SUB], (GLA_SUB, dk))
            for jb in range(nsub)]
    b_end = jnp.concatenate(ends, axis=0)
    b_start = jnp.concatenate([jnp.zeros((GLA_SUB, dk), F32)] + ends[:-1], axis=0)
    rblk = lax.broadcasted_iota(jnp.int32, (ch, ch), 0) // GLA_SUB
    cblk = lax.broadcasted_iota(jnp.int32, (ch, ch), 1) // GLA_SUB
    zero_row = jnp.zeros((1, dk), F32)
    spread = jnp.max(jnp.concatenate(
        [(ends[jb - 1][0:1] if jb else zero_row) - ends[jb][0:1] for jb in range(nsub)], axis=0))

    @pl.when(spread < GLA_SAFE_EXP)
    def _():
        q_diag = (q * jnp.exp(b - b_start)).astype(BF16)
        k_diag = (k * jnp.exp(b_start - b)).astype(BF16)
        diag_sc[...] = _dot_nt(q_diag, k_diag)

    @pl.when(spread >= GLA_SAFE_EXP)
    def _():
        lane = lax.broadcasted_iota(jnp.int32, (GLA_SUB, ch), 1)
        strips = []
        for ib in range(nsub):
            rs = slice(ib * GLA_SUB, (ib + 1) * GLA_SUB)
            qi, ki, bi = q[rs], k[rs], b[rs]
            strip = jnp.zeros((GLA_SUB, ch), F32)
            for j in range(GLA_SUB):
                decay = jnp.exp(jnp.minimum(bi - bi[j:j + 1], 0.0))
                col = jnp.sum(qi * ki[j:j + 1] * decay, axis=-1, keepdims=True)
                strip = jnp.where(lane == ib * GLA_SUB + j, col, strip)
            strips.append(strip)
        diag_sc[...] = jnp.concatenate(strips, axis=0)

    attn = jnp.where(jnp.logical_and(tri, rblk == cblk), diag_sc[...], 0.0)
    k_hat = (k * jnp.exp(b_end - b)).astype(BF16)
    for jb in range(nsub - 1):
        q_hat = (q * jnp.exp(jnp.minimum(b - ends[jb][0:1], 0.0))).astype(BF16)
        attn = attn + jnp.where(jnp.logical_and(cblk == jb, rblk > jb), _dot_nt(q_hat, k_hat), 0.0)
    o = o + _dot(attn.astype(BF16), v)
    b_last = b[ch - 1:ch]
    k_dec = k * jnp.exp(b_last - b)
    k_dec_t = jnp.transpose(k_dec).astype(BF16)
    decay_t = jnp.transpose(jnp.broadcast_to(jnp.exp(b_last), (LANES, dk)))
    state_sc[...] = decay_t[:, 0:1] * state + _dot(k_dec_t, v)
    ms = jnp.mean(o * o, axis=-1, keepdims=True)
    on = o * lax.rsqrt(ms + RMS_EPS) * og
    return (on * _silu(r_in.astype(F32))).astype(BF16)


def _gla_mixer(h, hn, layer, w_in, w_gate2, b_gate, o_gain, w_out):
    s, d = h.shape
    nh = GLA_HEADS
    dk = d // 2
    dv = d
    dkh, dvh = dk // nh, dv // nh
    nmain = 2 * dk + 2 * dv

    def epi_bf16(accs, e_refs, o_refs, n):
        o_refs[0][...] = accs[0].astype(BF16)

    def epi_f32(accs, e_refs, o_refs, n):
        o_refs[0][...] = accs[0]

    tm, tn = 512, 1024
    proj = _matmul(hn, [(w_in, _wspec(d, tn, 0, layer))], epi_bf16,
                   [(jax.ShapeDtypeStruct((s, nmain), BF16), _ospec(tm, tn))],
                   tm=tm, tn=tn, n_total=nmain, name="gla_in")[0]
    wg1 = jnp.pad(w_in[layer, :, nmain:], ((0, 0), (0, LANES - GLA_GATE_RANK)))
    g1 = _matmul(hn, [(wg1, _wspec(d, LANES))], epi_f32,
                 [(jax.ShapeDtypeStruct((s, LANES), F32), _ospec(tm, LANES))],
                 tm=tm, tn=LANES, n_total=LANES, name="gla_gate_in")[0]
    wg2 = jnp.pad(w_gate2.astype(BF16), ((0, LANES - GLA_GATE_RANK), (0, 0)))
    ch = GLA_CHUNK
    o = pl.pallas_call(
        _gla_body,
        grid=(s // ch,),
        in_specs=[
            pl.BlockSpec((ch, dk), lambda c: (c, 0)),
            pl.BlockSpec((ch, dk), lambda c: (c, 1)),
            pl.BlockSpec((ch, dv), lambda c: (c, 2 * dk // dv)),
            pl.BlockSpec((ch, dv), lambda c: (c, 2 * dk // dv + 1)),
            pl.BlockSpec((ch, LANES), lambda c: (c, 0)),
            pl.BlockSpec((LANES, dk), lambda c: (0, 0)),
            pl.BlockSpec((1, dk), lambda c: (0, 0)),
            pl.BlockSpec((1, dvh), lambda c: (0, 0)),
        ],
        out_specs=pl.BlockSpec((ch, dv), lambda c: (c, 0)),
        out_shape=jax.ShapeDtypeStruct((s, dv), BF16),
        scratch_shapes=[pltpu.VMEM((nh, dkh, dvh), F32), pltpu.VMEM((nh, ch, ch), F32)],
        compiler_params=_cparams(("arbitrary",)),
        name="gla_scan",
    )(proj, proj, proj, proj, g1, wg2, b_gate.reshape(1, dk), o_gain.reshape(1, dvh))
    return _matmul_resid(o, w_out, h, layer=layer, name="gla_out")


def _dense_ffn(h, hn, layer, w_up, w_down):
    act = _swiglu_up(hn, w_up, layer=layer)
    return _matmul_resid(act, w_down, h, layer=layer, tn=512, tk=w_down.shape[1], name="ffn_down")


def _moe_ffn(h, gain, router, layer, w_up, w_down):
    s, _ = h.shape
    info, cnt = _moe_router(h, gain, router)
    dest1, dest2, src, tile_expert, nvalid = _moe_plan(info, cnt, s)
    xs = _moe_gather(h, gain, src, nvalid)
    act = _moe_up(xs, w_up, layer, tile_expert, nvalid)
    y = _moe_down(act, w_down, layer, tile_expert, nvalid)
    return _moe_combine(h, y, info, dest1, dest2)


def kernel(x, rel_bias, norm_gain, nsa_w_in, nsa_cmp_pe, nsa_cmp_w1, nsa_cmp_w2, nsa_q_gain, nsa_k_gain, nsa_w_out, sg_w_in, sg_v_gain, sg_w_s, sg_b_s, sg_w_out, gla_w_in, gla_w_gate2, gla_b_gate, gla_o_gain, gla_w_out, ffn_w_up, ffn_w_down, moe_router, moe_w_up, moe_w_down):
    bsz, s, d = x.shape
    tables = _nsa_tables(rel_bias)
    nsa_in, nsa_out = nsa_w_in.astype(BF16), nsa_w_out.astype(BF16)
    sg_in, sg_out = sg_w_in.astype(BF16), sg_w_out.astype(BF16)
    gla_in, gla_out = gla_w_in.astype(BF16), gla_w_out.astype(BF16)
    ffn_up, ffn_down = ffn_w_up.astype(BF16), ffn_w_down.astype(BF16)
    moe_up, moe_down = moe_w_up.astype(BF16), moe_w_down.astype(BF16)
    outs = []
    for bi in range(bsz):
        h = x[bi]
        for i in range(DEPTH):
            mixer = i % N_MIXERS
            j = i // N_MIXERS
            hn = _rmsnorm(h, norm_gain[i, 0])
            if mixer == 0:
                h = _nsa_mixer(h, hn, tables, j, nsa_in, nsa_cmp_pe[j], nsa_cmp_w1[j], nsa_cmp_w2[j],
                               nsa_q_gain[j], nsa_k_gain[j], nsa_out)
            elif mixer == 1:
                h = _sgu_mixer(h, hn, j, sg_in, sg_v_gain[j], sg_w_s[j], sg_b_s[j], sg_out)
            else:
                h = _gla_mixer(h, hn, j, gla_in, gla_w_gate2[j], gla_b_gate[j], gla_o_gain[j], gla_out)
            f = i // 2
            if i % 2 == 0:
                hn = _rmsnorm(h, norm_gain[i, 1])
                h = _dense_ffn(h, hn, f, ffn_up, ffn_down)
            else:
                h = _moe_ffn(h, norm_gain[i, 1], moe_router[f], f, moe_up, moe_down)
        outs.append(h)
    return jnp.stack(outs, axis=0)
```

```python
import math

import jax
import jax.numpy as jnp
from jax import lax
from jax.experimental import pallas as pl
from jax.experimental.pallas import tpu as pltpu

F32 = jnp.float32
BF16 = jnp.bfloat16

DEPTH = 4
N_MIXERS = 3
RMS_EPS = 1e-6
NEG = -1e30
HEAD_DIM = 128
NSA_KV_HEADS = 4
NSA_GROUP = 8
CMP_BLOCK = 32
CMP_STRIDE = 16
SEL_BLOCK = 64
SEL_TOPK = 16
WINDOW = 512
Q_BLOCK = 128
FORCED_SCORE = 1e4
TAKEN = -2.0
T5_BUCKETS = 32
T5_MAX_DISTANCE = 2048
SG_CHUNK = 128
SG_GROUPS = 32
GLA_HEADS = 4
GLA_GATE_RANK = 16
GLA_GATE_TEMP = 16.0
GLA_CHUNK = 64
GLA_SUB = 16
GLA_SAFE_EXP = 80.0
MOE_EXPERTS = 8
MOE_TM = 512

LANES = 128
SEL_TILE = 512
SEL_NEAR_TILES = 5
SEL_FAR_WIDTH = 2
CMP_NEAR_SHIFT = 12
CW_HEADS = 8
VMEM_MB = 56
LOG2E = 1.4426950408889634
SAFE_LOGIT = 60.0
SEL_PAD_BLOCKS = 8


def _cparams(sem, vmem_mb=VMEM_MB):
    return pltpu.CompilerParams(dimension_semantics=sem, vmem_limit_bytes=vmem_mb * 2**20)


def _dot(a, b):
    return jnp.dot(a, b, preferred_element_type=F32)


def _dot_nt(a, b):
    return lax.dot_general(a, b, (((1,), (1,)), ((), ())), preferred_element_type=F32)


def _rmsnorm_body(x_ref, g_ref, o_ref):
    x = x_ref[...]
    ms = jnp.mean(x * x, axis=-1, keepdims=True)
    o_ref[...] = (x * lax.rsqrt(ms + RMS_EPS) * g_ref[...]).astype(o_ref.dtype)


def _rmsnorm(x, gain, tm=256):
    m, d = x.shape
    return pl.pallas_call(
        _rmsnorm_body,
        grid=(m // tm,),
        in_specs=[pl.BlockSpec((tm, d), lambda i: (i, 0)), pl.BlockSpec((1, d), lambda i: (0, 0))],
        out_specs=pl.BlockSpec((tm, d), lambda i: (i, 0)),
        out_shape=jax.ShapeDtypeStruct((m, d), BF16),
        compiler_params=_cparams(("parallel",)),
        name="rmsnorm",
    )(x, gain.reshape(1, d))


def _norm_rows(x, gain):
    ms = jnp.mean(x * x, axis=-1, keepdims=True)
    return (x * lax.rsqrt(ms + RMS_EPS) * gain).astype(BF16)


RT_E1, RT_E2, RT_W1, RT_W2, RT_R1, RT_R2 = range(6)


def _moe_router_body(x_ref, g_ref, r_ref, info_ref, cnt_ref, base_sc):
    @pl.when(pl.program_id(0) == 0)
    def _():
        base_sc[...] = jnp.zeros_like(base_sc)

    hn = _norm_rows(x_ref[...], g_ref[...])
    logits = _dot(hn, r_ref[...])
    lane = lax.broadcasted_iota(jnp.int32, logits.shape, 1).astype(F32)
    logits = jnp.where(lane < MOE_EXPERTS, logits, NEG)
    v1 = jnp.max(logits, axis=-1, keepdims=True)
    i1 = jnp.min(jnp.where(logits == v1, lane, 1e3), axis=-1, keepdims=True)
    rest = jnp.where(lane == i1, NEG, logits)
    v2 = jnp.max(rest, axis=-1, keepdims=True)
    i2 = jnp.min(jnp.where(rest == v2, lane, 1e3), axis=-1, keepdims=True)
    e2 = jnp.exp(v2 - v1)
    den = 1.0 + e2
    hot = jnp.where(jnp.logical_or(lane == i1, lane == i2), 1.0, 0.0)
    tm = hot.shape[0]
    earlier = (lax.broadcasted_iota(jnp.int32, (tm, tm), 0) > lax.broadcasted_iota(jnp.int32, (tm, tm), 1))
    before = _dot(jnp.where(earlier, 1.0, 0.0).astype(BF16), hot.astype(BF16)) + base_sc[...]
    r1 = jnp.sum(jnp.where(lane == i1, before, 0.0), axis=-1, keepdims=True)
    r2 = jnp.sum(jnp.where(lane == i2, before, 0.0), axis=-1, keepdims=True)
    rec = jnp.zeros_like(logits)
    for pos, val in ((RT_E1, i1), (RT_E2, i2), (RT_W1, 1.0 / den), (RT_W2, e2 / den), (RT_R1, r1), (RT_R2, r2)):
        rec = jnp.where(lane == float(pos), val, rec)
    info_ref[...] = rec
    total = base_sc[...] + jnp.sum(hot, axis=0, keepdims=True)
    base_sc[...] = total
    cnt_ref[...] = jnp.broadcast_to(total, cnt_ref.shape)


def _moe_router(x, gain, router, tm=256):
    m, d = x.shape
    rpad = jnp.zeros((d, LANES), BF16).at[:, :MOE_EXPERTS].set(router.astype(BF16))
    return pl.pallas_call(
        _moe_router_body,
        grid=(m // tm,),
        in_specs=[pl.BlockSpec((tm, d), lambda i: (i, 0)), pl.BlockSpec((1, d), lambda i: (0, 0)),
                  pl.BlockSpec((d, LANES), lambda i: (0, 0))],
        out_specs=[pl.BlockSpec((tm, LANES), lambda i: (i, 0)), pl.BlockSpec((8, LANES), lambda i: (0, 0))],
        out_shape=[jax.ShapeDtypeStruct((m, LANES), F32), jax.ShapeDtypeStruct((8, LANES), F32)],
        scratch_shapes=[pltpu.VMEM((1, LANES), F32)],
        compiler_params=_cparams(("arbitrary",)),
        name="moe_router",
    )(x, gain.reshape(1, d), rpad)


def _matmul(x, w_list, epilogue, out_list, *, tm, tn, n_total, tk=None, extras=(), name="matmul"):
    m, kdim = x.shape
    tk = kdim if tk is None else tk
    nk = kdim // tk
    nw, ne, no = len(w_list), len(extras), len(out_list)

    def body(*refs):
        x_ref = refs[0]
        w_refs = refs[1:1 + nw]
        e_refs = refs[1 + nw:1 + nw + ne]
        o_refs = refs[1 + nw + ne:1 + nw + ne + no]
        acc_refs = refs[1 + nw + ne + no:]
        n = pl.program_id(0)
        if nk == 1:
            xv = x_ref[...]
            epilogue([_dot(xv, w[...]) for w in w_refs], e_refs, o_refs, n)
        else:
            k = pl.program_id(2)

            @pl.when(k == 0)
            def _():
                for a in acc_refs:
                    a[...] = jnp.zeros_like(a)

            xv = x_ref[...]
            for a, w in zip(acc_refs, w_refs):
                a[...] += _dot(xv, w[...])

            @pl.when(k == nk - 1)
            def _():
                epilogue([a[...] for a in acc_refs], e_refs, o_refs, n)

    in_specs = [pl.BlockSpec((tm, tk), lambda n, mi, k: (mi, k))]
    in_specs += [s for _, s in w_list] + [s for _, s in extras]
    scratch = [] if nk == 1 else [pltpu.VMEM((tm, tn), F32) for _ in range(nw)]
    return pl.pallas_call(
        body,
        grid=(n_total // tn, m // tm, nk),
        in_specs=in_specs,
        out_specs=[s for _, s in out_list],
        out_shape=[s for s, _ in out_list],
        scratch_shapes=scratch,
        compiler_params=_cparams(("parallel", "parallel", "arbitrary")),
        name=name,
    )(x, *[a for a, _ in w_list], *[a for a, _ in extras])


def _wspec(tk, tn, off=0, layer=None):
    if layer is None:
        return pl.BlockSpec((tk, tn), lambda n, mi, k: (k, n + off))
    return pl.BlockSpec((None, tk, tn), lambda n, mi, k: (layer, k, n + off))


def _ospec(tm, tn):
    return pl.BlockSpec((tm, tn), lambda n, mi, k: (mi, n))


def _epi_resid(accs, e_refs, o_refs, n):
    o_refs[0][...] = e_refs[0][...] + accs[0]


def _matmul_resid(x, w, resid, *, layer=None, tm=512, tn=1024, tk=None, name="matmul_resid"):
    m, kdim = x.shape
    nn = w.shape[-1]
    tk = min(kdim, 4096) if tk is None else tk
    return _matmul(
        x, [(w, _wspec(tk, tn, 0, layer))], _epi_resid,
        [(jax.ShapeDtypeStruct((m, nn), F32), _ospec(tm, tn))],
        tm=tm, tn=tn, tk=tk, n_total=nn, extras=[(resid, _ospec(tm, tn))], name=name)[0]


def _silu(a):
    return a * (1.0 / (1.0 + jnp.exp(-a)))


def _epi_swiglu(accs, e_refs, o_refs, n):
    a, b = accs
    o_refs[0][...] = (_silu(a) * b).astype(BF16)


def _swiglu_up(x, w_up, *, layer=None, tm=512, tn=512, name="swiglu_up"):
    m, kdim = x.shape
    ff = w_up.shape[-1] // 2
    return _matmul(
        x, [(w_up, _wspec(kdim, tn, 0, layer)), (w_up, _wspec(kdim, tn, ff // tn, layer))], _epi_swiglu,
        [(jax.ShapeDtypeStruct((m, ff), BF16), _ospec(tm, tn))],
        tm=tm, tn=tn, n_total=ff, name=name)[0]


def _moe_plan(info, cnt, s):
    ne, tm = MOE_EXPERTS, MOE_TM
    e1 = info[:, RT_E1].astype(jnp.int32)
    e2 = info[:, RT_E2].astype(jnp.int32)
    counts = cnt[0, :ne].astype(jnp.int32)
    padded = (counts + tm - 1) // tm * tm
    ends = jnp.cumsum(padded)
    off = ends - padded
    dest1 = off[e1] + info[:, RT_R1].astype(jnp.int32)
    dest2 = off[e2] + info[:, RT_R2].astype(jnp.int32)
    ntile = (2 * s) // tm + ne
    tile_start = jnp.arange(ntile, dtype=jnp.int32) * tm
    tile_expert = jnp.minimum(jnp.sum(tile_start[:, None] >= ends[None, :], axis=1), ne - 1).astype(jnp.int32)
    nvalid = (ends[ne - 1:ne] // tm).astype(jnp.int32)
    tok = jnp.arange(s, dtype=jnp.int32)
    src = jnp.zeros((ntile * tm,), jnp.int32).at[dest1].set(tok).at[dest2].set(tok)
    return dest1, dest2, src, tile_expert, nvalid


def _row_copy(src_hbm, row, dst, r, sem):
    return pltpu.make_async_copy(src_hbm.at[pl.ds(row, 1)], dst.at[pl.ds(r, 1)], sem)


def _moe_gather_body(src_ref, nv_ref, h_hbm, g_ref, o_ref, buf, sems):
    t = pl.program_id(0)
    nv = nv_ref[0]
    tm = buf.shape[1]

    def rows(tile, slot, start):
        def body(r, carry):
            cp = _row_copy(h_hbm, src_ref[tile * tm + r], buf.at[slot], r, sems.at[slot])
            if start:
                cp.start()
            else:
                cp.wait()
            return carry

        lax.fori_loop(0, tm, body, 0)

    @pl.when(jnp.logical_and(t == 0, nv > 0))
    def _():
        rows(0, 0, True)

    @pl.when(t + 1 < nv)
    def _():
        rows(t + 1, (t + 1) % 2, True)

    @pl.when(t < nv)
    def _():
        rows(t, t % 2, False)
        o_ref[...] = _norm_rows(buf[t % 2], g_ref[...])

    @pl.when(t >= nv)
    def _():
        o_ref[...] = jnp.zeros_like(o_ref)


def _moe_gather(h, gain, src, nvalid):
    s, d = h.shape
    tm = MOE_TM
    ntile = src.shape[0] // tm
    return pl.pallas_call(
        _moe_gather_body,
        grid_spec=pltpu.PrefetchScalarGridSpec(
            num_scalar_prefetch=2,
            grid=(ntile,),
            in_specs=[pl.BlockSpec(memory_space=pl.ANY),
                      pl.BlockSpec((1, d), lambda t, src_, nv_: (0, 0))],
            out_specs=pl.BlockSpec((tm, d), lambda t, src_, nv_: (t, 0)),
            scratch_shapes=[pltpu.VMEM((2, tm, d), F32), pltpu.SemaphoreType.DMA((2,))]),
        out_shape=jax.ShapeDtypeStruct((ntile * tm, d), BF16),
        compiler_params=_cparams(("arbitrary",)),
        name="moe_gather",
    )(src, nvalid, h, gain.reshape(1, d))


def _new_expert(te_ref, t):
    return jnp.logical_or(t == 0, te_ref[t] != te_ref[jnp.maximum(t - 1, 0)])


def _moe_up_body(te_ref, nv_ref, x_ref, wa_ref, wb_ref, o_ref, wa_sc, wb_sc):
    t = pl.program_id(1)
    live = t < nv_ref[0]

    @pl.when(jnp.logical_and(live, _new_expert(te_ref, t)))
    def _():
        wa_sc[...] = wa_ref[...].astype(BF16)
        wb_sc[...] = wb_ref[...].astype(BF16)

    @pl.when(live)
    def _():
        x = x_ref[...]
        o_ref[...] = (_silu(_dot(x, wa_sc[...])) * _dot(x, wb_sc[...])).astype(BF16)

    @pl.when(jnp.logical_not(live))
    def _():
        o_ref[...] = jnp.zeros_like(o_ref)


def _moe_up(xs, w_up, layer, tile_expert, nvalid, *, tn=256):
    p, d = xs.shape
    tm = MOE_TM
    ff = w_up.shape[-1] // 2
    nn = ff // tn
    return pl.pallas_call(
        _moe_up_body,
        grid_spec=pltpu.PrefetchScalarGridSpec(
            num_scalar_prefetch=2,
            grid=(nn, p // tm),
            in_specs=[pl.BlockSpec((tm, d), lambda n, t, te, nv: (t, 0)),
                      pl.BlockSpec((None, None, d, tn), lambda n, t, te, nv: (layer, te[t], 0, n)),
                      pl.BlockSpec((None, None, d, tn), lambda n, t, te, nv: (layer, te[t], 0, n + nn))],
            out_specs=pl.BlockSpec((tm, tn), lambda n, t, te, nv: (t, n)),
            scratch_shapes=[pltpu.VMEM((d, tn), BF16), pltpu.VMEM((d, tn), BF16)]),
        out_shape=jax.ShapeDtypeStruct((p, ff), BF16),
        compiler_params=_cparams(("arbitrary", "arbitrary")),
        name="moe_up",
    )(tile_expert, nvalid, xs, w_up, w_up)


def _moe_down_body(te_ref, nv_ref, x_ref, w_ref, o_ref, w_sc):
    t = pl.program_id(1)
    live = t < nv_ref[0]

    @pl.when(jnp.logical_and(live, _new_expert(te_ref, t)))
    def _():
        w_sc[...] = w_ref[...].astype(BF16)

    @pl.when(live)
    def _():
        o_ref[...] = _dot(x_ref[...], w_sc[...])

    @pl.when(jnp.logical_not(live))
    def _():
        o_ref[...] = jnp.zeros_like(o_ref)


def _moe_down(act, w_down, layer, tile_expert, nvalid, *, tn=1024):
    p, ff = act.shape
    tm = MOE_TM
    d = w_down.shape[-1]
    return pl.pallas_call(
        _moe_down_body,
        grid_spec=pltpu.PrefetchScalarGridSpec(
            num_scalar_prefetch=2,
            grid=(d // tn, p // tm),
            in_specs=[pl.BlockSpec((tm, ff), lambda n, t, te, nv: (t, 0)),
                      pl.BlockSpec((None, None, ff, tn), lambda n, t, te, nv: (layer, te[t], 0, n))],
            out_specs=pl.BlockSpec((tm, tn), lambda n, t, te, nv: (t, n)),
            scratch_shapes=[pltpu.VMEM((ff, tn), BF16)]),
        out_shape=jax.ShapeDtypeStruct((p, d), F32),
        compiler_params=_cparams(("arbitrary", "arbitrary")),
        name="moe_down",
    )(tile_expert, nvalid, act, w_down)


def _moe_combine_body(d1_ref, d2_ref, y_hbm, h_ref, info_ref, o_ref, buf, sems):
    t = pl.program_id(0)
    nt = pl.num_programs(0)
    tm = h_ref.shape[0]

    def rows(tile, slot, start):
        def body(r, carry):
            for k, dest in enumerate((d1_ref, d2_ref)):
                cp = _row_copy(y_hbm, dest[tile * tm + r], buf.at[2 * slot + k], r, sems.at[2 * slot + k])
                if start:
                    cp.start()
                else:
                    cp.wait()
            return carry

        lax.fori_loop(0, tm, body, 0)

    @pl.when(t == 0)
    def _():
        rows(0, 0, True)

    @pl.when(t + 1 < nt)
    def _():
        rows(t + 1, (t + 1) % 2, True)

    slot = t % 2
    rows(t, slot, False)
    info = info_ref[...]
    o_ref[...] = h_ref[...] + (info[:, RT_W1:RT_W1 + 1] * buf[2 * slot]
                               + info[:, RT_W2:RT_W2 + 1] * buf[2 * slot + 1])


def _moe_combine(h, y, info, dest1, dest2, tm=256):
    s, d = h.shape
    return pl.pallas_call(
        _moe_combine_body,
        grid_spec=pltpu.PrefetchScalarGridSpec(
            num_scalar_prefetch=2,
            grid=(s // tm,),
            in_specs=[pl.BlockSpec(memory_space=pl.ANY),
                      pl.BlockSpec((tm, d), lambda t, a, b: (t, 0)),
                      pl.BlockSpec((tm, LANES), lambda t, a, b: (t, 0))],
            out_specs=pl.BlockSpec((tm, d), lambda t, a, b: (t, 0)),
            scratch_shapes=[pltpu.VMEM((4, tm, d), F32), pltpu.SemaphoreType.DMA((4,))]),
        out_shape=jax.ShapeDtypeStruct((s, d), F32),
        compiler_params=_cparams(("arbitrary",)),
        name="moe_combine",
    )(dest1, dest2, y, h, info)


def _nsa_q_proj(hn, wq, layer, q_gain, nn, *, tm=512, tn=1024):
    m, kdim = hn.shape
    scale = HEAD_DIM ** -0.5 * LOG2E

    def epi(accs, e_refs, o_refs, n):
        acc = accs[0]
        gain = e_refs[0][...]
        for c in range(tn // HEAD_DIM):
            a = acc[:, c * HEAD_DIM:(c + 1) * HEAD_DIM]
            ms = jnp.mean(a * a, axis=-1, keepdims=True)
            o_refs[0][:, c * HEAD_DIM:(c + 1) * HEAD_DIM] = (
                a * lax.rsqrt(ms + RMS_EPS) * gain * scale).astype(BF16)

    return _matmul(
        hn, [(wq, _wspec(kdim, tn, 0, layer))], epi,
        [(jax.ShapeDtypeStruct((m, nn), BF16), _ospec(tm, tn))],
        tm=tm, tn=tn, n_total=nn,
        extras=[(q_gain.reshape(1, HEAD_DIM), pl.BlockSpec((1, HEAD_DIM), lambda n, mi, k: (0, 0)))],
        name="nsa_q_proj")[0]


def _nsa_kv_proj(hn, wkv, layer, k_gain, col0, *, tm=512):
    m, kdim = hn.shape
    g_ = NSA_KV_HEADS
    tn = g_ * HEAD_DIM
    gains = jnp.ones((6, 1, HEAD_DIM), F32).at[2, 0].set(k_gain[1]).at[4, 0].set(k_gain[2])

    def epi(accs, e_refs, o_refs, n):
        acc = accs[0]
        kg = e_refs[0][0]
        do_norm = jnp.logical_or(n == 2, n == 4)
        for g in range(g_):
            a = acc[:, g * HEAD_DIM:(g + 1) * HEAD_DIM]
            ms = jnp.mean(a * a, axis=-1, keepdims=True)
            an = a * lax.rsqrt(ms + RMS_EPS) * kg
            o_refs[0][g] = jnp.where(do_norm, an, a).astype(BF16)

    return _matmul(
        hn, [(wkv, _wspec(kdim, tn, col0 // tn, layer))], epi,
        [(jax.ShapeDtypeStruct((6 * g_, m, HEAD_DIM), BF16),
          pl.BlockSpec((g_, tm, HEAD_DIM), lambda n, mi, k: (n, mi, 0)))],
        tm=tm, tn=tn, n_total=6 * tn,
        extras=[(gains, pl.BlockSpec((1, 1, HEAD_DIM), lambda n, mi, k: (n, 0, 0)))],
        name="nsa_kv_proj")[0]


def _nsa_gate_proj(hn, wg, *, tm=512):
    m, kdim = hn.shape
    nn = wg.shape[1]

    def epi(accs, e_refs, o_refs, n):
        o_refs[0][...] = 1.0 / (1.0 + jnp.exp(-accs[0]))

    return _matmul(
        hn, [(wg, _wspec(kdim, nn))], epi,
        [(jax.ShapeDtypeStruct((m, nn), F32), _ospec(tm, nn))],
        tm=tm, tn=nn, n_total=nn, name="nsa_gate_proj")[0]


def _compress_body(r_ref, pe_ref, w1_ref, w2_ref, kg_ref, o_ref, ot_ref):
    j = pl.program_id(0)
    r = r_ref[0].astype(F32)
    pe = pe_ref[0]
    half = r.shape[1]
    top = (r + pe[0:1]).astype(BF16)
    bot = (r + pe[1:2]).astype(BF16)
    a = _dot(top, w1_ref[0, 0:half, :])
    b = _dot(bot, w1_ref[0, half:2 * half, :])
    nrow = r.shape[0]
    hid = a + pltpu.roll(b, nrow - 1, 0)
    y = _dot(jax.nn.gelu(hid).astype(BF16), w2_ref[0])
    ms = jnp.mean(y * y, axis=-1, keepdims=True)
    yn = y * lax.rsqrt(ms + RMS_EPS) * kg_ref[...]
    out = jnp.where(j == 0, yn, y)
    o_ref[0, 0] = out.astype(BF16)
    ot_ref[0, 0] = jnp.transpose(out).astype(BF16)


def _compress(kv, pe, w1, w2, k_gain0):
    g_ = NSA_KV_HEADS
    s = kv.shape[1]
    nrow = s // CMP_STRIDE
    wid = CMP_STRIDE * HEAD_DIM
    r = kv[:2 * g_].reshape(2 * g_, nrow, wid)
    pe2 = pe.reshape(2, 2, wid)
    return pl.pallas_call(
        _compress_body,
        grid=(2, g_),
        in_specs=[pl.BlockSpec((1, nrow, wid), lambda j, g: (j * g_ + g, 0, 0)),
                  pl.BlockSpec((1, 2, wid), lambda j, g: (j, 0, 0)),
                  pl.BlockSpec((1, 2 * wid, HEAD_DIM), lambda j, g: (j, 0, 0)),
                  pl.BlockSpec((1, HEAD_DIM, HEAD_DIM), lambda j, g: (j, 0, 0)),
                  pl.BlockSpec((1, HEAD_DIM), lambda j, g: (0, 0))],
        out_specs=[pl.BlockSpec((1, 1, nrow, HEAD_DIM), lambda j, g: (j, g, 0, 0)),
                   pl.BlockSpec((1, 1, HEAD_DIM, nrow), lambda j, g: (j, g, 0, 0))],
        out_shape=[jax.ShapeDtypeStruct((2, g_, nrow, HEAD_DIM), BF16),
                   jax.ShapeDtypeStruct((2, g_, HEAD_DIM, nrow), BF16)],
        compiler_params=_cparams(("parallel", "parallel")),
        name="nsa_compress",
    )(r, pe2, w1.astype(BF16), w2.astype(BF16), k_gain0.reshape(1, HEAD_DIM))


def _t5_bucket(dist):
    dist = jnp.maximum(dist, 0)
    max_exact = T5_BUCKETS // 2
    d_f = jnp.maximum(dist, 1).astype(F32)
    log_b = max_exact + (jnp.log(d_f / max_exact) / math.log(T5_MAX_DISTANCE / max_exact)
                         * (T5_BUCKETS - max_exact)).astype(jnp.int32)
    log_b = jnp.minimum(log_b, T5_BUCKETS - 1)
    return jnp.where(dist < max_exact, dist, log_b)


def _bias_table(tbl, dist, valid, shift):
    onehot = jax.nn.one_hot(_t5_bucket(dist).reshape(-1), T5_BUCKETS, dtype=F32)
    t = tbl - tbl[T5_BUCKETS - 1:T5_BUCKETS] if shift else tbl
    vals = jnp.einsum("nb,bh->hn", onehot, t, precision=lax.Precision.HIGHEST)
    vals = vals.reshape((tbl.shape[1],) + dist.shape) * LOG2E
    return jnp.where(valid[None], vals, NEG)


def _nsa_tables(rel_bias):
    tbl = rel_bias.astype(F32)
    r = jnp.arange(Q_BLOCK)
    jw = jnp.arange(WINDOW + Q_BLOCK)
    dw = WINDOW + r[None, :] - jw[:, None]
    tab_w = _bias_table(tbl, dw, (dw >= 0) & (dw < WINDOW), False)
    a = jnp.arange(16)
    cc = jnp.arange(2 * LANES)
    dc = (CMP_NEAR_SHIFT * Q_BLOCK + Q_BLOCK * a[:, None, None] + r[None, None, :]
          - CMP_STRIDE * cc[None, :, None] - (CMP_BLOCK - 1))
    tab_c = _bias_table(tbl, dc, dc >= 0, True)
    jj = jnp.arange(SEL_NEAR_TILES)
    col = jnp.arange(SEL_TILE)
    ds_ = r[None, None, :] - Q_BLOCK + SEL_TILE * (jj[:, None, None] + 1) - col[None, :, None]
    tab_s = _bias_table(tbl, ds_, ds_ >= 0, True)
    return tab_w, tab_c, tab_s, tbl


def _sel_to_cmp(nsel, nc_pad):
    c_start = jnp.arange(nc_pad) * CMP_STRIDE
    s_start = jnp.arange(nsel) * SEL_BLOCK
    overlap = jnp.clip(jnp.minimum(c_start[None, :] + CMP_BLOCK, s_start[:, None] + SEL_BLOCK)
                       - jnp.maximum(c_start[None, :], s_start[:, None]), 0)
    return (overlap.astype(F32) / CMP_BLOCK).astype(BF16)


def _heads_t(q_ref, h0, nh):
    cols = [jnp.transpose(q_ref[:, (h0 + h) * HEAD_DIM:(h0 + h + 1) * HEAD_DIM].astype(F32)) for h in range(nh)]
    return jnp.concatenate(cols, axis=1).astype(BF16)


def _softmax_keys(s, bounded):
    if bounded:
        p = jnp.exp2(s)
        l = jnp.sum(p, axis=0, keepdims=True)
        return p, jnp.where(l > 0.0, 1.0 / l, 0.0)
    m = jnp.max(s, axis=0, keepdims=True)
    p = jnp.exp2(s - m)
    l = jnp.sum(p, axis=0, keepdims=True)
    return p, jnp.where(m > 0.5 * NEG, 1.0 / l, 0.0)


def _logit_bound(rel_bias, q_gain, k_gain):
    qk = HEAD_DIM * (HEAD_DIM ** -0.5 * LOG2E) * jnp.max(jnp.abs(q_gain)) * jnp.max(jnp.abs(k_gain))
    return (1.02 * qk + 2.0 * LOG2E * jnp.max(jnp.abs(rel_bias))).astype(F32).reshape(1)


def _nsa_cw_body(bnd_ref, q_ref, gt_ref, kc_ref, vct_ref, mselt_ref, tabc_ref, kw_ref, vwt_ref, tabw_ref,
                 ocwt_ref, negt_ref):
    bounded = bnd_ref[0] < SAFE_LOGIT
    i = pl.program_id(1)

    @pl.when(bounded)
    def _():
        _nsa_cw_branches(True, i, q_ref, gt_ref, kc_ref, vct_ref, mselt_ref, tabc_ref, kw_ref, vwt_ref, tabw_ref,
                         ocwt_ref, negt_ref)

    @pl.when(jnp.logical_not(bounded))
    def _():
        _nsa_cw_branches(False, i, q_ref, gt_ref, kc_ref, vct_ref, mselt_ref, tabc_ref, kw_ref, vwt_ref, tabw_ref,
                         ocwt_ref, negt_ref)


def _nsa_cw_branches(bounded, i, q_ref, gt_ref, kc_ref, vct_ref, mselt_ref, tabc_ref, kw_ref, vwt_ref, tabw_ref,
                     ocwt_ref, negt_ref):
    f = (i + 16 - CMP_NEAR_SHIFT) // 16 - 1
    kc = kc_ref[0, 0]
    vct = vct_ref[0, 0]
    nc = kc.shape[0]
    nsel = mselt_ref.shape[0]
    wlen = WINDOW + Q_BLOCK
    nslab = wlen // LANES
    start = pl.multiple_of(i * Q_BLOCK, Q_BLOCK)
    kw = kw_ref[0, pl.ds(start, wlen), :]
    vwt = jnp.concatenate([vwt_ref[0, i + c] for c in range(nslab)], axis=1)
    roww = lax.broadcasted_iota(jnp.int32, (wlen, LANES), 0)
    w_pad = roww < (WINDOW - Q_BLOCK * i)
    gt_t = jnp.transpose(gt_ref[...])
    psum = jnp.zeros((nc, Q_BLOCK), F32)
    hg = CW_HEADS
    for h0 in range(0, NSA_GROUP, hg):
        qt = _heads_t(q_ref, h0, hg)
        s = _dot(kc, qt)
        ta = jnp.concatenate([tabc_ref[h0 + h, 0, 0:LANES, :] for h in range(hg)], axis=1)
        tb = jnp.concatenate([tabc_ref[h0 + h, 0, LANES:2 * LANES, :] for h in range(hg)], axis=1)
        pieces = []
        for ch in range(nc // LANES):
            rest = jnp.where(ch > f + 1, NEG, 0.0)
            bias = jnp.where(ch == f, ta, jnp.where(ch == f + 1, tb, rest))
            pieces.append(s[ch * LANES:(ch + 1) * LANES] + bias)
        s = jnp.concatenate(pieces, axis=0)
        p, linv = _softmax_keys(s, bounded)
        pn = p * linv
        for h in range(hg):
            psum = psum + pn[:, h * Q_BLOCK:(h + 1) * Q_BLOCK]
        oc = _dot(vct, pn.astype(BF16))
        tw = jnp.concatenate([jnp.where(w_pad, NEG, tabw_ref[h0 + h]) for h in range(hg)], axis=1)
        sw = _dot(kw, qt) + tw
        pw, lwinv = _softmax_keys(sw, bounded)
        ow = _dot(vwt, pw.astype(BF16)) * lwinv
        for h in range(hg):
            hh = h0 + h
            cs = slice(h * Q_BLOCK, (h + 1) * Q_BLOCK)
            ocwt_ref[0, 0, hh * HEAD_DIM:(hh + 1) * HEAD_DIM, :] = (
                gt_t[hh:hh + 1] * oc[:, cs] + gt_t[16 + hh:17 + hh] * ow[:, cs])
    p_hi = psum.astype(BF16)
    p_lo = (psum - p_hi.astype(F32)).astype(BF16)
    mselt = mselt_ref[...]
    imp = _dot(mselt, p_hi) + _dot(mselt, p_lo)
    blk = lax.broadcasted_iota(jnp.int32, (nsel, Q_BLOCK), 0).astype(F32)
    qpos = lax.broadcasted_iota(jnp.int32, (nsel, Q_BLOCK), 1)
    cur = (i * (Q_BLOCK // SEL_BLOCK)).astype(F32) + jnp.where(qpos >= SEL_BLOCK, 1.0, 0.0)
    forced = jnp.logical_or(blk == cur, blk == 0.0)
    val = jnp.where(forced, FORCED_SCORE, jnp.where(blk <= cur, imp, -1.0))
    for _ in range(SEL_TOPK):
        mx = jnp.max(val, axis=0, keepdims=True)
        first = jnp.min(jnp.where(val == mx, blk, 1e4), axis=0, keepdims=True)
        val = jnp.where(blk == first, TAKEN, val)
    neg = jnp.where(val == TAKEN, 0.0, NEG)
    for pr in range(SEL_PAD_BLOCKS // 2):
        negt_ref[0, 0, pr] = jnp.full((2, Q_BLOCK), NEG, F32)
    for pr in range(nsel // 2):
        negt_ref[0, 0, SEL_PAD_BLOCKS // 2 + pr] = neg[2 * pr:2 * pr + 2]


def _nsa_cw(bound, q, gates, kc, vct, mselt, tab_c, kw, vwt, tab_w):
    s, hd = q.shape
    g_ = NSA_KV_HEADS
    nq = s // Q_BLOCK
    gw = NSA_GROUP * HEAD_DIM
    nc = kc.shape[2]
    nsel = mselt.shape[0]
    spad = kw.shape[1]
    nslab = vwt.shape[1]
    wlen = WINDOW + Q_BLOCK

    def var(i):
        return (i + 16 - CMP_NEAR_SHIFT) % 16

    return pl.pallas_call(
        _nsa_cw_body,
        grid=(g_, nq),
        in_specs=[
            pl.BlockSpec(memory_space=pltpu.SMEM),
            pl.BlockSpec((Q_BLOCK, gw), lambda g, i: (i, g)),
            pl.BlockSpec((Q_BLOCK, LANES), lambda g, i: (i, g)),
            pl.BlockSpec((1, 1, nc, HEAD_DIM), lambda g, i: (0, g, 0, 0)),
            pl.BlockSpec((1, 1, HEAD_DIM, nc), lambda g, i: (1, g, 0, 0)),
            pl.BlockSpec((nsel, nc), lambda g, i: (0, 0)),
            pl.BlockSpec((NSA_GROUP, 1, 2 * LANES, Q_BLOCK), lambda g, i: (g, var(i), 0, 0)),
            pl.BlockSpec((1, spad, HEAD_DIM), lambda g, i: (g, 0, 0)),
            pl.BlockSpec((1, nslab, HEAD_DIM, LANES), lambda g, i: (g, 0, 0, 0)),
            pl.BlockSpec((NSA_GROUP, wlen, Q_BLOCK), lambda g, i: (g, 0, 0)),
        ],
        out_specs=[pl.BlockSpec((1, 1, gw, Q_BLOCK), lambda g, i: (g, i, 0, 0)),
                   pl.BlockSpec((1, 1, (SEL_PAD_BLOCKS + nsel) // 2, 2, Q_BLOCK), lambda g, i: (g, i, 0, 0, 0))],
        out_shape=[jax.ShapeDtypeStruct((g_, nq, gw, Q_BLOCK), F32),
                   jax.ShapeDtypeStruct((g_, nq, (SEL_PAD_BLOCKS + nsel) // 2, 2, Q_BLOCK), F32)],
        compiler_params=_cparams(("parallel", "arbitrary")),
        name="nsa_cmp_win",
    )(bound, q, gates, kc, vct, mselt, tab_c, kw, vwt, tab_w)


def _nsa_sel_body(bnd_ref, q_ref, gt_ref, ks_ref, vst_ref, negp_ref, tabs_ref, ocwt_ref, o_ref,
                  m_sc, l_sc, acc_sc):
    i = pl.program_id(1)
    ntile = i // (SEL_TILE // Q_BLOCK) + 1
    nh = NSA_GROUP
    spt = SEL_TILE // LANES
    ppt = SEL_TILE // (2 * SEL_BLOCK)
    qt = _heads_t(q_ref, 0, nh)

    def tile(jj, near, bounded, nsub=1):
        slab0 = (i + 1) - spt * jj
        row0 = pl.multiple_of(slab0 * LANES, LANES)
        kt = ks_ref[0, pl.ds(row0, nsub * SEL_TILE), :]
        vtt = jnp.concatenate([vst_ref[0, slab0 + c] for c in range(nsub * spt)], axis=1)
        rows = []
        for c in range(nsub * ppt):
            pair = negp_ref[0, 0, slab0 + c]
            rows += [jnp.broadcast_to(pair[r:r + 1], (SEL_BLOCK, Q_BLOCK)) for r in range(2)]
        mk = jnp.concatenate(rows, axis=0)
        if near:
            bias = jnp.concatenate(
                [jnp.concatenate([tabs_ref[h, jj - c] for c in range(nsub)], axis=0) + mk for h in range(nh)],
                axis=1)
        else:
            bias = jnp.tile(mk, (1, nh))
        sc = _dot(kt, qt) + bias
        if bounded:
            p = jnp.exp2(sc)
            l_sc[...] = l_sc[...] + jnp.sum(p, axis=0, keepdims=True)
            acc_sc[...] = acc_sc[...] + _dot(vtt, p.astype(BF16))
        else:
            m_old = m_sc[...]
            m_new = jnp.maximum(m_old, jnp.max(sc, axis=0, keepdims=True))
            alpha = jnp.exp2(m_old - m_new)
            p = jnp.exp2(sc - m_new)
            l_sc[...] = alpha * l_sc[...] + jnp.sum(p, axis=0, keepdims=True)
            acc_sc[...] = alpha * acc_sc[...] + _dot(vtt, p.astype(BF16))
            m_sc[...] = m_new

    def run(bounded):
        m_sc[...] = jnp.full(m_sc.shape, NEG, F32)
        l_sc[...] = jnp.zeros(l_sc.shape, F32)
        acc_sc[...] = jnp.zeros(acc_sc.shape, F32)
        @pl.when(ntile >= SEL_NEAR_TILES)
        def _():
            for jj in range(SEL_FAR_WIDTH - 1, SEL_NEAR_TILES, SEL_FAR_WIDTH):
                tile(jj, True, bounded, SEL_FAR_WIDTH)
            for jj in range(SEL_NEAR_TILES - SEL_NEAR_TILES % SEL_FAR_WIDTH, SEL_NEAR_TILES):
                tile(jj, True, bounded)

        @pl.when(ntile < SEL_NEAR_TILES)
        def _():
            for jj in range(SEL_NEAR_TILES - 1):
                @pl.when(jj < ntile)
                def _():
                    tile(jj, True, bounded)

        nfar = jnp.maximum(ntile - SEL_NEAR_TILES, 0)

        def far(k, carry):
            tile(SEL_NEAR_TILES + SEL_FAR_WIDTH * k + SEL_FAR_WIDTH - 1, False, bounded, SEL_FAR_WIDTH)
            return carry

        lax.fori_loop(0, nfar // SEL_FAR_WIDTH, far, 0)

        def far_rest(jj, carry):
            tile(jj, False, bounded)
            return carry

        lax.fori_loop(SEL_NEAR_TILES + nfar // SEL_FAR_WIDTH * SEL_FAR_WIDTH, ntile, far_rest, 0)

    bounded = bnd_ref[0] < SAFE_LOGIT

    @pl.when(bounded)
    def _():
        run(True)

    @pl.when(jnp.logical_not(bounded))
    def _():
        run(False)

    o_t = acc_sc[...] * (1.0 / l_sc[...])
    gt_t = jnp.transpose(gt_ref[...])
    for h in range(nh):
        hs = slice(h * HEAD_DIM, (h + 1) * HEAD_DIM)
        oh = ocwt_ref[0, 0, hs, :] + gt_t[8 + h:9 + h] * o_t[:, h * Q_BLOCK:(h + 1) * Q_BLOCK]
        o_ref[:, hs] = jnp.transpose(oh).astype(BF16)


def _nsa_sel(bound, q, gates, ks, vst, negp, tab_s, ocwt):
    s, hd = q.shape
    g_ = NSA_KV_HEADS
    nq = s // Q_BLOCK
    gw = NSA_GROUP * HEAD_DIM
    spad = ks.shape[1]
    nslab = vst.shape[1]
    npair = negp.shape[2]
    return pl.pallas_call(
        _nsa_sel_body,
        grid=(g_, nq),
        in_specs=[
            pl.BlockSpec(memory_space=pltpu.SMEM),
            pl.BlockSpec((Q_BLOCK, gw), lambda g, i: (i, g)),
            pl.BlockSpec((Q_BLOCK, LANES), lambda g, i: (i, g)),
            pl.BlockSpec((1, spad, HEAD_DIM), lambda g, i: (g, 0, 0)),
            pl.BlockSpec((1, nslab, HEAD_DIM, LANES), lambda g, i: (g, 0, 0, 0)),
            pl.BlockSpec((1, 1, npair, 2, Q_BLOCK), lambda g, i: (g, i, 0, 0, 0)),
            pl.BlockSpec((NSA_GROUP, SEL_NEAR_TILES, SEL_TILE, Q_BLOCK), lambda g, i: (g, 0, 0, 0)),
            pl.BlockSpec((1, 1, gw, Q_BLOCK), lambda g, i: (g, i, 0, 0)),
        ],
        out_specs=pl.BlockSpec((Q_BLOCK, gw), lambda g, i: (i, g)),
        out_shape=jax.ShapeDtypeStruct((s, hd), BF16),
        scratch_shapes=[pltpu.VMEM((1, NSA_GROUP * Q_BLOCK), F32), pltpu.VMEM((1, NSA_GROUP * Q_BLOCK), F32),
                        pltpu.VMEM((HEAD_DIM, NSA_GROUP * Q_BLOCK), F32)],
        compiler_params=_cparams(("parallel", "arbitrary")),
        name="nsa_selected",
    )(bound, q, gates, ks, vst, negp, tab_s, ocwt)


def _pad_keys(x, pad):
    xp = jnp.pad(x, ((0, 0), (pad, 0), (0, 0)))
    g_, sp, dh = xp.shape
    return xp, xp.reshape(g_, sp // LANES, LANES, dh).swapaxes(-1, -2)


def _nsa_mixer(h, hn, tables, layer, w_in, cmp_pe, cmp_w1, cmp_w2, q_gain, k_gain, w_out):
    s, d = h.shape
    g_, hpg, dh = NSA_KV_HEADS, NSA_GROUP, HEAD_DIM
    hd = g_ * hpg * dh
    tab_w, tab_c, tab_s, tbl = tables
    q = _nsa_q_proj(hn, w_in, layer, q_gain, hd)
    kv = _nsa_kv_proj(hn, w_in, layer, k_gain, hd)
    wg = w_in[layer, :, hd + 6 * g_ * dh:].reshape(d, g_, hpg, 3).transpose(0, 1, 3, 2).reshape(d, g_, 3 * hpg)
    wg = jnp.pad(wg, ((0, 0), (0, 0), (0, LANES - 3 * hpg))).reshape(d, g_ * LANES)
    gates = _nsa_gate_proj(hn, wg)
    kc, kct = _compress(kv, cmp_pe, cmp_w1, cmp_w2, k_gain[0])
    mselt = _sel_to_cmp(s // SEL_BLOCK, s // CMP_STRIDE)
    ks, _ = _pad_keys(kv[2 * g_:3 * g_], SEL_TILE)
    _, vst = _pad_keys(kv[3 * g_:4 * g_], SEL_TILE)
    kw, _ = _pad_keys(kv[4 * g_:5 * g_], WINDOW)
    _, vwt = _pad_keys(kv[5 * g_:6 * g_], WINDOW)
    bound = _logit_bound(tbl, q_gain, k_gain)
    ocwt, negp = _nsa_cw(bound, q, gates, kc, kct, mselt, tab_c, kw, vwt, tab_w)
    o = _nsa_sel(bound, q, gates, ks, vst, negp, tab_s, ocwt)
    return _matmul_resid(o, w_out, h, layer=layer, name="nsa_out")


def _gelu_proj(hn, w, layer, *, tm=512, tn=1024):
    m, kdim = hn.shape
    nn = w.shape[-1]

    def epi(accs, e_refs, o_refs, n):
        o_refs[0][...] = jax.nn.gelu(accs[0]).astype(BF16)

    return _matmul(
        hn, [(w, _wspec(kdim, tn, 0, layer))], epi,
        [(jax.ShapeDtypeStruct((m, nn), BF16), _ospec(tm, tn))],
        tm=tm, tn=tn, n_total=nn, name="sgu_in")[0]


def _sgu_mix_body(u_ref, v_ref, gain_ref, w_ref, bt_ref, o_ref):
    vn = _norm_rows(v_ref[...].astype(F32), gain_ref[...])
    t = w_ref.shape[1]
    causal = (lax.broadcasted_iota(jnp.int32, (t, t), 0) >= lax.broadcasted_iota(jnp.int32, (t, t), 1))
    bt = bt_ref[...]
    gd = vn.shape[1] // SG_GROUPS
    for g in range(SG_GROUPS):
        gs = slice(g * gd, (g + 1) * gd)
        w = jnp.where(causal, w_ref[g], 0.0).astype(BF16)
        mixed = _dot(w, vn[:, gs]) + bt[:, g:g + 1]
        o_ref[:, gs] = (u_ref[:, gs].astype(F32) * mixed).astype(BF16)


def _sgu_mix(uv, v_gain, w_s, b_s):
    s = uv.shape[0]
    wd = uv.shape[1] // 2
    t = SG_CHUNK
    return pl.pallas_call(
        _sgu_mix_body,
        grid=(s // t,),
        in_specs=[pl.BlockSpec((t, wd), lambda c: (c, 0)),
                  pl.BlockSpec((t, wd), lambda c: (c, 1)),
                  pl.BlockSpec((1, wd), lambda c: (0, 0)),
                  pl.BlockSpec((SG_GROUPS, t, t), lambda c: (0, 0, 0)),
                  pl.BlockSpec((t, SG_GROUPS), lambda c: (0, 0))],
        out_specs=pl.BlockSpec((t, wd), lambda c: (c, 0)),
        out_shape=jax.ShapeDtypeStruct((s, wd), BF16),
        compiler_params=_cparams(("parallel",)),
        name="sgu_mix",
    )(uv, uv, v_gain.reshape(1, wd), w_s, b_s.T)


def _sgu_mixer(h, hn, layer, w_in, v_gain, w_s, b_s, w_out):
    uv = _gelu_proj(hn, w_in, layer)
    y = _sgu_mix(uv, v_gain, w_s, b_s)
    return _matmul_resid(y, w_out, h, layer=layer, name="sgu_out")


def _gla_body(q_ref, k_ref, v_ref, r_ref, g1_ref, wg_ref, bg_ref, og_ref, o_ref, state_sc, diag_sc):
    @pl.when(pl.program_id(0) == 0)
    def _():
        state_sc[...] = jnp.zeros_like(state_sc)

    nh = state_sc.shape[0]
    dk = q_ref.shape[1] // nh
    dv = v_ref.shape[1] // nh
    g1 = g1_ref[...].astype(BF16)
    for hh in range(nh):
        ks = slice(hh * dk, (hh + 1) * dk)
        vs = slice(hh * dv, (hh + 1) * dv)
        o_ref[:, vs] = _gla_head(q_ref[:, ks], k_ref[:, ks], v_ref[:, vs], r_ref[:, vs], g1, wg_ref[:, ks],
                                 bg_ref[:, ks], og_ref[...], state_sc.at[hh], diag_sc.at[hh])


def _gla_head(q_in, k_in, v, r_in, g1, wg, bg, og, state_sc, diag_sc):
    ch, dk = q_in.shape
    x = _dot(g1, wg) + bg
    log_a = (jnp.minimum(x, 0.0) - jnp.log(1.0 + jnp.exp(-jnp.abs(x)))) * (1.0 / GLA_GATE_TEMP)
    tri = (lax.broadcasted_iota(jnp.int32, (ch, ch), 0)
           >= lax.broadcasted_iota(jnp.int32, (ch, ch), 1))
    tri_b = jnp.where(tri, 1.0, 0.0).astype(BF16)
    a_hi = log_a.astype(BF16)
    a_lo = (log_a - a_hi.astype(F32)).astype(BF16)
    b = _dot(tri_b, a_hi) + _dot(tri_b, a_lo)
    q = q_in.astype(F32) * (dk ** -0.5)
    k = k_in.astype(F32)
    state = state_sc[...]
    o = _dot((q * jnp.exp(b)).astype(BF16), state.astype(BF16))
    nsub = ch // GLA_SUB
    ends = [jnp.broadcast_to(b[(jb + 1) * GLA_SUB - 1:(jb + 1) * GLA_SUB], (GLA_SUB, dk))
            for jb in range(nsub)]
    b_end = jnp.concatenate(ends, axis=0)
    b_start = jnp.concatenate([jnp.zeros((GLA_SUB, dk), F32)] + ends[:-1], axis=0)
    rblk = lax.broadcasted_iota(jnp.int32, (ch, ch), 0) // GLA_SUB
    cblk = lax.broadcasted_iota(jnp.int32, (ch, ch), 1) // GLA_SUB
    zero_row = jnp.zeros((1, dk), F32)
    spread = jnp.max(jnp.concatenate(
        [(ends[jb - 1][0:1] if jb else zero_row) - ends[jb][0:1] for jb in range(nsub)], axis=0))

    @pl.when(spread < GLA_SAFE_EXP)
    def _():
        q_diag = (q * jnp.exp(b - b_start)).astype(BF16)
        k_diag = (k * jnp.exp(b_start - b)).astype(BF16)
        diag_sc[...] = _dot_nt(q_diag, k_diag)

    @pl.when(spread >= GLA_SAFE_EXP)
    def _():
        lane = lax.broadcasted_iota(jnp.int32, (GLA_SUB, ch), 1)
        strips = []
        for ib in range(nsub):
            rs = slice(ib * GLA_SUB, (ib + 1) * GLA_SUB)
            qi, ki, bi = q[rs], k[rs], b[rs]
            strip = jnp.zeros((GLA_SUB, ch), F32)
            for j in range(GLA_SUB):
                decay = jnp.exp(jnp.minimum(bi - bi[j:j + 1], 0.0))
                col = jnp.sum(qi * ki[j:j + 1] * decay, axis=-1, keepdims=True)
                strip = jnp.where(lane == ib * GLA_SUB + j, col, strip)
            strips.append(strip)
        diag_sc[...] = jnp.concatenate(strips, axis=0)

    attn = jnp.where(jnp.logical_and(tri, rblk == cblk), diag_sc[...], 0.0)
    k_hat = (k * jnp.exp(b_end - b)).astype(BF16)
    for jb in range(nsub - 1):
        q_hat = (q * jnp.exp(jnp.minimum(b - ends[jb][0:1], 0.0))).astype(BF16)
        attn = attn + jnp.where(jnp.logical_and(cblk == jb, rblk > jb), _dot_nt(q_hat, k_hat), 0.0)
    o = o + _dot(attn.astype(BF16), v)
    b_last = b[ch - 1:ch]
    k_dec = k * jnp.exp(b_last - b)
    k_dec_t = jnp.transpose(k_dec).astype(BF16)
    decay_t = jnp.transpose(jnp.broadcast_to(jnp.exp(b_last), (LANES, dk)))
    state_sc[...] = decay_t[:, 0:1] * state + _dot(k_dec_t, v)
    ms = jnp.mean(o * o, axis=-1, keepdims=True)
    on = o * lax.rsqrt(ms + RMS_EPS) * og
    return (on * _silu(r_in.astype(F32))).astype(BF16)


def _gla_mixer(h, hn, layer, w_in, w_gate2, b_gate, o_gain, w_out):
    s, d = h.shape
    nh = GLA_HEADS
    dk = d // 2
    dv = d
    dkh, dvh = dk // nh, dv // nh
    nmain = 2 * dk + 2 * dv

    def epi_bf16(accs, e_refs, o_refs, n):
        o_refs[0][...] = accs[0].astype(BF16)

    def epi_f32(accs, e_refs, o_refs, n):
        o_refs[0][...] = accs[0]

    tm, tn = 512, 1024
    proj = _matmul(hn, [(w_in, _wspec(d, tn, 0, layer))], epi_bf16,
                   [(jax.ShapeDtypeStruct((s, nmain), BF16), _ospec(tm, tn))],
                   tm=tm, tn=tn, n_total=nmain, name="gla_in")[0]
    wg1 = jnp.pad(w_in[layer, :, nmain:], ((0, 0), (0, LANES - GLA_GATE_RANK)))
    g1 = _matmul(hn, [(wg1, _wspec(d, LANES))], epi_f32,
                 [(jax.ShapeDtypeStruct((s, LANES), F32), _ospec(tm, LANES))],
                 tm=tm, tn=LANES, n_total=LANES, name="gla_gate_in")[0]
    wg2 = jnp.pad(w_gate2.astype(BF16), ((0, LANES - GLA_GATE_RANK), (0, 0)))
    ch = GLA_CHUNK
    o = pl.pallas_call(
        _gla_body,
        grid=(s // ch,),
        in_specs=[
            pl.BlockSpec((ch, dk), lambda c: (c, 0)),
            pl.BlockSpec((ch, dk), lambda c: (c, 1)),
            pl.BlockSpec((ch, dv), lambda c: (c, 2 * dk // dv)),
            pl.BlockSpec((ch, dv), lambda c: (c, 2 * dk // dv + 1)),
            pl.BlockSpec((ch, LANES), lambda c: (c, 0)),
            pl.BlockSpec((LANES, dk), lambda c: (0, 0)),
            pl.BlockSpec((1, dk), lambda c: (0, 0)),
            pl.BlockSpec((1, dvh), lambda c: (0, 0)),
        ],
        out_specs=pl.BlockSpec((ch, dv), lambda c: (c, 0)),
        out_shape=jax.ShapeDtypeStruct((s, dv), BF16),
        scratch_shapes=[pltpu.VMEM((nh, dkh, dvh), F32), pltpu.VMEM((nh, ch, ch), F32)],
        compiler_params=_cparams(("arbitrary",)),
        name="gla_scan",
    )(proj, proj, proj, proj, g1, wg2, b_gate.reshape(1, dk), o_gain.reshape(1, dvh))
    return _matmul_resid(o, w_out, h, layer=layer, name="gla_out")


def _dense_ffn(h, hn, layer, w_up, w_down):
    act = _swiglu_up(hn, w_up, layer=layer)
    return _matmul_resid(act, w_down, h, layer=layer, tn=512, tk=w_down.shape[1], name="ffn_down")


def _moe_ffn(h, gain, router, layer, w_up, w_down):
    s, _ = h.shape
    info, cnt = _moe_router(h, gain, router)
    dest1, dest2, src, tile_expert, nvalid = _moe_plan(info, cnt, s)
    xs = _moe_gather(h, gain, src, nvalid)
    act = _moe_up(xs, w_up, layer, tile_expert, nvalid)
    y = _moe_down(act, w_down, layer, tile_expert, nvalid)
    return _moe_combine(h, y, info, dest1, dest2)


def kernel(x, rel_bias, norm_gain, nsa_w_in, nsa_cmp_pe, nsa_cmp_w1, nsa_cmp_w2, nsa_q_gain, nsa_k_gain, nsa_w_out, sg_w_in, sg_v_gain, sg_w_s, sg_b_s, sg_w_out, gla_w_in, gla_w_gate2, gla_b_gate, gla_o_gain, gla_w_out, ffn_w_up, ffn_w_down, moe_router, moe_w_up, moe_w_down):
    bsz, s, d = x.shape
    tables = _nsa_tables(rel_bias)
    nsa_in, nsa_out = nsa_w_in.astype(BF16), nsa_w_out.astype(BF16)
    sg_in, sg_out = sg_w_in.astype(BF16), sg_w_out.astype(BF16)
    gla_in, gla_out = gla_w_in.astype(BF16), gla_w_out.astype(BF16)
    ffn_up, ffn_down = ffn_w_up.astype(BF16), ffn_w_down.astype(BF16)
    outs = []
    for bi in range(bsz):
        h = x.reshape(s, d) if bsz == 1 else x[bi]
        for i in range(DEPTH):
            mixer = i % N_MIXERS
            j = i // N_MIXERS
            hn = _rmsnorm(h, norm_gain[i, 0])
            if mixer == 0:
                h = _nsa_mixer(h, hn, tables, j, nsa_in, nsa_cmp_pe[j], nsa_cmp_w1[j], nsa_cmp_w2[j],
                               nsa_q_gain[j], nsa_k_gain[j], nsa_out)
            elif mixer == 1:
                h = _sgu_mixer(h, hn, j, sg_in, sg_v_gain[j], sg_w_s[j], sg_b_s[j], sg_out)
            else:
                h = _gla_mixer(h, hn, j, gla_in, gla_w_gate2[j], gla_b_gate[j], gla_o_gain[j], gla_out)
            f = i // 2
            if i % 2 == 0:
                hn = _rmsnorm(h, norm_gain[i, 1])
                h = _dense_ffn(h, hn, f, ffn_up, ffn_down)
            else:
                h = _moe_ffn(h, norm_gain[i, 1], moe_router[f], f, moe_w_up, moe_w_down)
        outs.append(h)
    return outs[0].reshape(1, s, d) if bsz == 1 else jnp.stack(outs, axis=0)
```

```python
import math

import jax
import jax.numpy as jnp
from jax import lax
from jax.experimental import pallas as pl
from jax.experimental.pallas import tpu as pltpu

F32 = jnp.float32
BF16 = jnp.bfloat16

DEPTH = 4
N_MIXERS = 3
RMS_EPS = 1e-6
NEG = -1e30
HEAD_DIM = 128
NSA_KV_HEADS = 4
NSA_GROUP = 8
CMP_BLOCK = 32
CMP_STRIDE = 16
SEL_BLOCK = 64
SEL_TOPK = 16
WINDOW = 512
Q_BLOCK = 128
FORCED_SCORE = 1e4
TAKEN = -2.0
T5_BUCKETS = 32
T5_MAX_DISTANCE = 2048
SG_CHUNK = 128
SG_GROUPS = 32
GLA_HEADS = 4
GLA_GATE_RANK = 16
GLA_GATE_TEMP = 16.0
GLA_CHUNK = 64
GLA_SUB = 16
GLA_SAFE_EXP = 80.0
MOE_EXPERTS = 8
MOE_TM = 512

LANES = 128
SEL_TILE = 512
SEL_NEAR_TILES = 5
SEL_FAR_WIDTH = 2
CMP_NEAR_SHIFT = 12
CW_HEADS = 8
VMEM_MB = 56
LOG2E = 1.4426950408889634
SAFE_LOGIT = 60.0
SEL_PAD_BLOCKS = 8


def _cparams(sem, vmem_mb=VMEM_MB):
    return pltpu.CompilerParams(dimension_semantics=sem, vmem_limit_bytes=vmem_mb * 2**20)


def _dot(a, b):
    return jnp.dot(a, b, preferred_element_type=F32)


def _dot_nt(a, b):
    return lax.dot_general(a, b, (((1,), (1,)), ((), ())), preferred_element_type=F32)


def _rmsnorm_body(x_ref, g_ref, o_ref):
    x = x_ref[...]
    ms = jnp.mean(x * x, axis=-1, keepdims=True)
    o_ref[...] = (x * lax.rsqrt(ms + RMS_EPS) * g_ref[...]).astype(o_ref.dtype)


def _rmsnorm(x, gain, tm=256):
    m, d = x.shape
    return pl.pallas_call(
        _rmsnorm_body,
        grid=(m // tm,),
        in_specs=[pl.BlockSpec((tm, d), lambda i: (i, 0)), pl.BlockSpec((1, d), lambda i: (0, 0))],
        out_specs=pl.BlockSpec((tm, d), lambda i: (i, 0)),
        out_shape=jax.ShapeDtypeStruct((m, d), BF16),
        compiler_params=_cparams(("parallel",)),
        name="rmsnorm",
    )(x, gain.reshape(1, d))


def _norm_rows(x, gain):
    ms = jnp.mean(x * x, axis=-1, keepdims=True)
    return (x * lax.rsqrt(ms + RMS_EPS) * gain).astype(BF16)


RT_E1, RT_E2, RT_W1, RT_W2, RT_R1, RT_R2 = range(6)


def _moe_router_body(x_ref, g_ref, r_ref, info_ref, cnt_ref, base_sc):
    @pl.when(pl.program_id(0) == 0)
    def _():
        base_sc[...] = jnp.zeros_like(base_sc)

    hn = _norm_rows(x_ref[...], g_ref[...])
    logits = _dot(hn, r_ref[...])
    lane = lax.broadcasted_iota(jnp.int32, logits.shape, 1).astype(F32)
    logits = jnp.where(lane < MOE_EXPERTS, logits, NEG)
    v1 = jnp.max(logits, axis=-1, keepdims=True)
    i1 = jnp.min(jnp.where(logits == v1, lane, 1e3), axis=-1, keepdims=True)
    rest = jnp.where(lane == i1, NEG, logits)
    v2 = jnp.max(rest, axis=-1, keepdims=True)
    i2 = jnp.min(jnp.where(rest == v2, lane, 1e3), axis=-1, keepdims=True)
    e2 = jnp.exp(v2 - v1)
    den = 1.0 + e2
    hot = jnp.where(jnp.logical_or(lane == i1, lane == i2), 1.0, 0.0)
    tm = hot.shape[0]
    earlier = (lax.broadcasted_iota(jnp.int32, (tm, tm), 0) > lax.broadcasted_iota(jnp.int32, (tm, tm), 1))
    before = _dot(jnp.where(earlier, 1.0, 0.0).astype(BF16), hot.astype(BF16)) + base_sc[...]
    r1 = jnp.sum(jnp.where(lane == i1, before, 0.0), axis=-1, keepdims=True)
    r2 = jnp.sum(jnp.where(lane == i2, before, 0.0), axis=-1, keepdims=True)
    rec = jnp.zeros_like(logits)
    for pos, val in ((RT_E1, i1), (RT_E2, i2), (RT_W1, 1.0 / den), (RT_W2, e2 / den), (RT_R1, r1), (RT_R2, r2)):
        rec = jnp.where(lane == float(pos), val, rec)
    info_ref[...] = rec
    total = base_sc[...] + jnp.sum(hot, axis=0, keepdims=True)
    base_sc[...] = total
    cnt_ref[...] = jnp.broadcast_to(total, cnt_ref.shape)


def _moe_router(x, gain, router, tm=256):
    m, d = x.shape
    rpad = jnp.zeros((d, LANES), BF16).at[:, :MOE_EXPERTS].set(router.astype(BF16))
    return pl.pallas_call(
        _moe_router_body,
        grid=(m // tm,),
        in_specs=[pl.BlockSpec((tm, d), lambda i: (i, 0)), pl.BlockSpec((1, d), lambda i: (0, 0)),
                  pl.BlockSpec((d, LANES), lambda i: (0, 0))],
        out_specs=[pl.BlockSpec((tm, LANES), lambda i: (i, 0)), pl.BlockSpec((8, LANES), lambda i: (0, 0))],
        out_shape=[jax.ShapeDtypeStruct((m, LANES), F32), jax.ShapeDtypeStruct((8, LANES), F32)],
        scratch_shapes=[pltpu.VMEM((1, LANES), F32)],
        compiler_params=_cparams(("arbitrary",)),
        name="moe_router",
    )(x, gain.reshape(1, d), rpad)


def _matmul(x, w_list, epilogue, out_list, *, tm, tn, n_total, tk=None, extras=(), name="matmul",
            dimsem=("parallel", "parallel", "arbitrary")):
    m, kdim = x.shape
    tk = kdim if tk is None else tk
    nk = kdim // tk
    nw, ne, no = len(w_list), len(extras), len(out_list)

    def body(*refs):
        x_ref = refs[0]
        w_refs = refs[1:1 + nw]
        e_refs = refs[1 + nw:1 + nw + ne]
        o_refs = refs[1 + nw + ne:1 + nw + ne + no]
        acc_refs = refs[1 + nw + ne + no:]
        n = pl.program_id(0)
        if nk == 1:
            xv = x_ref[...]
            epilogue([_dot(xv, w[...]) for w in w_refs], e_refs, o_refs, n)
        else:
            k = pl.program_id(2)

            @pl.when(k == 0)
            def _():
                for a in acc_refs:
                    a[...] = jnp.zeros_like(a)

            xv = x_ref[...]
            for a, w in zip(acc_refs, w_refs):
                a[...] += _dot(xv, w[...])

            @pl.when(k == nk - 1)
            def _():
                epilogue([a[...] for a in acc_refs], e_refs, o_refs, n)

    in_specs = [pl.BlockSpec((tm, tk), lambda n, mi, k: (mi, k))]
    in_specs += [s for _, s in w_list] + [s for _, s in extras]
    scratch = [] if nk == 1 else [pltpu.VMEM((tm, tn), F32) for _ in range(nw)]
    return pl.pallas_call(
        body,
        grid=(n_total // tn, m // tm, nk),
        in_specs=in_specs,
        out_specs=[s for _, s in out_list],
        out_shape=[s for s, _ in out_list],
        scratch_shapes=scratch,
        compiler_params=_cparams(dimsem),
        name=name,
    )(x, *[a for a, _ in w_list], *[a for a, _ in extras])


def _wspec(tk, tn, off=0, layer=None):
    if layer is None:
        return pl.BlockSpec((tk, tn), lambda n, mi, k: (k, n + off))
    return pl.BlockSpec((None, tk, tn), lambda n, mi, k: (layer, k, n + off))


def _ospec(tm, tn):
    return pl.BlockSpec((tm, tn), lambda n, mi, k: (mi, n))


def _epi_resid(accs, e_refs, o_refs, n):
    o_refs[0][...] = e_refs[0][...] + accs[0]


def _matmul_resid(x, w, resid, *, layer=None, tm=512, tn=1024, tk=None, name="matmul_resid"):
    m, kdim = x.shape
    nn = w.shape[-1]
    tk = min(kdim, 4096) if tk is None else tk
    return _matmul(
        x, [(w, _wspec(tk, tn, 0, layer))], _epi_resid,
        [(jax.ShapeDtypeStruct((m, nn), F32), _ospec(tm, tn))],
        tm=tm, tn=tn, tk=tk, n_total=nn, extras=[(resid, _ospec(tm, tn))], name=name)[0]


def _silu(a):
    return a * (1.0 / (1.0 + jnp.exp(-a)))


def _epi_swiglu(accs, e_refs, o_refs, n):
    a, b = accs
    o_refs[0][...] = (_silu(a) * b).astype(BF16)


def _swiglu_up(x, w_up, *, layer=None, tm=512, tn=512, name="swiglu_up"):
    m, kdim = x.shape
    ff = w_up.shape[-1] // 2
    return _matmul(
        x, [(w_up, _wspec(kdim, tn, 0, layer)), (w_up, _wspec(kdim, tn, ff // tn, layer))], _epi_swiglu,
        [(jax.ShapeDtypeStruct((m, ff), BF16), _ospec(tm, tn))],
        tm=tm, tn=tn, n_total=ff, name=name)[0]


def _moe_plan(info, cnt, s):
    ne, tm = MOE_EXPERTS, MOE_TM
    e1 = info[:, RT_E1].astype(jnp.int32)
    e2 = info[:, RT_E2].astype(jnp.int32)
    counts = cnt[0, :ne].astype(jnp.int32)
    padded = (counts + tm - 1) // tm * tm
    ends = jnp.cumsum(padded)
    off = ends - padded
    dest1 = off[e1] + info[:, RT_R1].astype(jnp.int32)
    dest2 = off[e2] + info[:, RT_R2].astype(jnp.int32)
    ntile = (2 * s) // tm + ne
    tile_start = jnp.arange(ntile, dtype=jnp.int32) * tm
    tile_expert = jnp.minimum(jnp.sum(tile_start[:, None] >= ends[None, :], axis=1), ne - 1).astype(jnp.int32)
    nvalid = (ends[ne - 1:ne] // tm).astype(jnp.int32)
    tok = jnp.arange(s, dtype=jnp.int32)
    src = jnp.zeros((ntile * tm,), jnp.int32).at[dest1].set(tok).at[dest2].set(tok)
    return dest1, dest2, src, tile_expert, nvalid


def _row_copy(src_hbm, row, dst, r, sem):
    return pltpu.make_async_copy(src_hbm.at[pl.ds(row, 1)], dst.at[pl.ds(r, 1)], sem)


def _moe_gather_body(src_ref, nv_ref, h_hbm, g_ref, o_ref, buf, sems):
    t = pl.program_id(0)
    nv = nv_ref[0]
    tm = buf.shape[1]

    def rows(tile, slot, start):
        def body(r, carry):
            cp = _row_copy(h_hbm, src_ref[tile * tm + r], buf.at[slot], r, sems.at[slot])
            if start:
                cp.start()
            else:
                cp.wait()
            return carry

        lax.fori_loop(0, tm, body, 0)

    @pl.when(jnp.logical_and(t == 0, nv > 0))
    def _():
        rows(0, 0, True)

    @pl.when(t + 1 < nv)
    def _():
        rows(t + 1, (t + 1) % 2, True)

    @pl.when(t < nv)
    def _():
        rows(t, t % 2, False)
        o_ref[...] = _norm_rows(buf[t % 2], g_ref[...])

    @pl.when(t >= nv)
    def _():
        o_ref[...] = jnp.zeros_like(o_ref)


def _moe_gather(h, gain, src, nvalid):
    s, d = h.shape
    tm = MOE_TM
    ntile = src.shape[0] // tm
    return pl.pallas_call(
        _moe_gather_body,
        grid_spec=pltpu.PrefetchScalarGridSpec(
            num_scalar_prefetch=2,
            grid=(ntile,),
            in_specs=[pl.BlockSpec(memory_space=pl.ANY),
                      pl.BlockSpec((1, d), lambda t, src_, nv_: (0, 0))],
            out_specs=pl.BlockSpec((tm, d), lambda t, src_, nv_: (t, 0)),
            scratch_shapes=[pltpu.VMEM((2, tm, d), F32), pltpu.SemaphoreType.DMA((2,))]),
        out_shape=jax.ShapeDtypeStruct((ntile * tm, d), BF16),
        compiler_params=_cparams(("arbitrary",)),
        name="moe_gather",
    )(src, nvalid, h, gain.reshape(1, d))


def _new_expert(te_ref, t):
    return jnp.logical_or(t == 0, te_ref[t] != te_ref[jnp.maximum(t - 1, 0)])


def _moe_up_body(te_ref, nv_ref, x_ref, wa_ref, wb_ref, o_ref, wa_sc, wb_sc):
    t = pl.program_id(1)
    live = t < nv_ref[0]

    @pl.when(jnp.logical_and(live, _new_expert(te_ref, t)))
    def _():
        wa_sc[...] = wa_ref[...].astype(BF16)
        wb_sc[...] = wb_ref[...].astype(BF16)

    @pl.when(live)
    def _():
        x = x_ref[...]
        o_ref[...] = (_silu(_dot(x, wa_sc[...])) * _dot(x, wb_sc[...])).astype(BF16)

    @pl.when(jnp.logical_not(live))
    def _():
        o_ref[...] = jnp.zeros_like(o_ref)


def _moe_up(xs, w_up, layer, tile_expert, nvalid, *, tn=256):
    p, d = xs.shape
    tm = MOE_TM
    ff = w_up.shape[-1] // 2
    nn = ff // tn
    return pl.pallas_call(
        _moe_up_body,
        grid_spec=pltpu.PrefetchScalarGridSpec(
            num_scalar_prefetch=2,
            grid=(nn, p // tm),
            in_specs=[pl.BlockSpec((tm, d), lambda n, t, te, nv: (t, 0)),
                      pl.BlockSpec((None, None, d, tn), lambda n, t, te, nv: (layer, te[t], 0, n)),
                      pl.BlockSpec((None, None, d, tn), lambda n, t, te, nv: (layer, te[t], 0, n + nn))],
            out_specs=pl.BlockSpec((tm, tn), lambda n, t, te, nv: (t, n)),
            scratch_shapes=[pltpu.VMEM((d, tn), BF16), pltpu.VMEM((d, tn), BF16)]),
        out_shape=jax.ShapeDtypeStruct((p, ff), BF16),
        compiler_params=_cparams(("arbitrary", "arbitrary")),
        name="moe_up",
    )(tile_expert, nvalid, xs, w_up, w_up)


def _moe_down_body(te_ref, nv_ref, x_ref, w_ref, o_ref, w_sc):
    t = pl.program_id(1)
    live = t < nv_ref[0]

    @pl.when(jnp.logical_and(live, _new_expert(te_ref, t)))
    def _():
        w_sc[...] = w_ref[...].astype(BF16)

    @pl.when(live)
    def _():
        o_ref[...] = _dot(x_ref[...], w_sc[...])

    @pl.when(jnp.logical_not(live))
    def _():
        o_ref[...] = jnp.zeros_like(o_ref)


def _moe_down(act, w_down, layer, tile_expert, nvalid, *, tn=1024):
    p, ff = act.shape
    tm = MOE_TM
    d = w_down.shape[-1]
    return pl.pallas_call(
        _moe_down_body,
        grid_spec=pltpu.PrefetchScalarGridSpec(
            num_scalar_prefetch=2,
            grid=(d // tn, p // tm),
            in_specs=[pl.BlockSpec((tm, ff), lambda n, t, te, nv: (t, 0)),
                      pl.BlockSpec((None, None, ff, tn), lambda n, t, te, nv: (layer, te[t], 0, n))],
            out_specs=pl.BlockSpec((tm, tn), lambda n, t, te, nv: (t, n)),
            scratch_shapes=[pltpu.VMEM((ff, tn), BF16)]),
        out_shape=jax.ShapeDtypeStruct((p, d), F32),
        compiler_params=_cparams(("arbitrary", "arbitrary")),
        name="moe_down",
    )(tile_expert, nvalid, act, w_down)


def _moe_combine_body(d1_ref, d2_ref, y_hbm, h_ref, info_ref, o_ref, buf, sems):
    t = pl.program_id(0)
    nt = pl.num_programs(0)
    tm = h_ref.shape[0]

    def rows(tile, slot, start):
        def body(r, carry):
            for k, dest in enumerate((d1_ref, d2_ref)):
                cp = _row_copy(y_hbm, dest[tile * tm + r], buf.at[2 * slot + k], r, sems.at[2 * slot + k])
                if start:
                    cp.start()
                else:
                    cp.wait()
            return carry

        lax.fori_loop(0, tm, body, 0)

    @pl.when(t == 0)
    def _():
        rows(0, 0, True)

    @pl.when(t + 1 < nt)
    def _():
        rows(t + 1, (t + 1) % 2, True)

    slot = t % 2
    rows(t, slot, False)
    info = info_ref[...]
    o_ref[...] = h_ref[...] + (info[:, RT_W1:RT_W1 + 1] * buf[2 * slot]
                               + info[:, RT_W2:RT_W2 + 1] * buf[2 * slot + 1])


def _moe_combine(h, y, info, dest1, dest2, tm=256):
    s, d = h.shape
    return pl.pallas_call(
        _moe_combine_body,
        grid_spec=pltpu.PrefetchScalarGridSpec(
            num_scalar_prefetch=2,
            grid=(s // tm,),
            in_specs=[pl.BlockSpec(memory_space=pl.ANY),
                      pl.BlockSpec((tm, d), lambda t, a, b: (t, 0)),
                      pl.BlockSpec((tm, LANES), lambda t, a, b: (t, 0))],
            out_specs=pl.BlockSpec((tm, d), lambda t, a, b: (t, 0)),
            scratch_shapes=[pltpu.VMEM((4, tm, d), F32), pltpu.SemaphoreType.DMA((4,))]),
        out_shape=jax.ShapeDtypeStruct((s, d), F32),
        compiler_params=_cparams(("arbitrary",)),
        name="moe_combine",
    )(dest1, dest2, y, h, info)


def _nsa_q_proj(hn, wq, layer, q_gain, nn, *, tm=512, tn=1024):
    m, kdim = hn.shape
    scale = HEAD_DIM ** -0.5 * LOG2E

    def epi(accs, e_refs, o_refs, n):
        acc = accs[0]
        gain = e_refs[0][...]
        for c in range(tn // HEAD_DIM):
            a = acc[:, c * HEAD_DIM:(c + 1) * HEAD_DIM]
            ms = jnp.mean(a * a, axis=-1, keepdims=True)
            o_refs[0][:, c * HEAD_DIM:(c + 1) * HEAD_DIM] = (
                a * lax.rsqrt(ms + RMS_EPS) * gain * scale).astype(BF16)

    return _matmul(
        hn, [(wq, _wspec(kdim, tn, 0, layer))], epi,
        [(jax.ShapeDtypeStruct((m, nn), BF16), _ospec(tm, tn))],
        tm=tm, tn=tn, n_total=nn,
        extras=[(q_gain.reshape(1, HEAD_DIM), pl.BlockSpec((1, HEAD_DIM), lambda n, mi, k: (0, 0)))],
        name="nsa_q_proj")[0]


def _nsa_kv_proj(hn, wkv, layer, k_gain, col0, *, tm=512):
    m, kdim = hn.shape
    g_ = NSA_KV_HEADS
    tn = g_ * HEAD_DIM
    spt = tm // LANES
    gains = jnp.ones((6, 1, HEAD_DIM), F32).at[2, 0].set(k_gain[1]).at[4, 0].set(k_gain[2])

    def epi(accs, e_refs, o_refs, n):
        acc = accs[0]
        kg = e_refs[0][0]
        do_norm = jnp.logical_or(n == 2, n == 4)
        for g in range(g_):
            a = acc[:, g * HEAD_DIM:(g + 1) * HEAD_DIM]
            ms = jnp.mean(a * a, axis=-1, keepdims=True)
            an = a * lax.rsqrt(ms + RMS_EPS) * kg
            o_refs[0][g] = jnp.where(do_norm, an, a).astype(BF16)
        is_v = jnp.logical_or(n == 3, n == 5)

        @pl.when(is_v)
        def _():
            for g in range(g_):
                for c in range(spt):
                    blk = acc[c * LANES:(c + 1) * LANES, g * HEAD_DIM:(g + 1) * HEAD_DIM]
                    o_refs[1][g, c] = jnp.transpose(blk).astype(BF16)

        @pl.when(jnp.logical_not(is_v))
        def _():
            o_refs[1][...] = jnp.zeros_like(o_refs[1])

    return _matmul(
        hn, [(wkv, _wspec(kdim, tn, col0 // tn, layer))], epi,
        [(jax.ShapeDtypeStruct((6 * g_, m, HEAD_DIM), BF16),
          pl.BlockSpec((g_, tm, HEAD_DIM), lambda n, mi, k: (n, mi, 0))),
         (jax.ShapeDtypeStruct((6, g_, m // LANES, HEAD_DIM, LANES), BF16),
          pl.BlockSpec((None, g_, spt, HEAD_DIM, LANES), lambda n, mi, k: (n, 0, mi, 0, 0)))],
        tm=tm, tn=tn, n_total=6 * tn,
        extras=[(gains, pl.BlockSpec((1, 1, HEAD_DIM), lambda n, mi, k: (n, 0, 0)))],
        name="nsa_kv_proj")


def _nsa_gate_proj(hn, wg, *, tm=512):
    m, kdim = hn.shape
    nn = wg.shape[1]

    def epi(accs, e_refs, o_refs, n):
        o_refs[0][...] = 1.0 / (1.0 + jnp.exp(-accs[0]))

    return _matmul(
        hn, [(wg, _wspec(kdim, nn))], epi,
        [(jax.ShapeDtypeStruct((m, nn), F32), _ospec(tm, nn))],
        tm=tm, tn=nn, n_total=nn, name="nsa_gate_proj")[0]


def _compress_body(r_ref, pe_ref, w1_ref, w2_ref, kg_ref, o_ref, ot_ref):
    j = pl.program_id(0)
    r = r_ref[0].astype(F32)
    pe = pe_ref[0]
    half = r.shape[1]
    top = (r + pe[0:1]).astype(BF16)
    bot = (r + pe[1:2]).astype(BF16)
    a = _dot(top, w1_ref[0, 0:half, :])
    b = _dot(bot, w1_ref[0, half:2 * half, :])
    nrow = r.shape[0]
    hid = a + pltpu.roll(b, nrow - 1, 0)
    y = _dot(jax.nn.gelu(hid).astype(BF16), w2_ref[0])
    ms = jnp.mean(y * y, axis=-1, keepdims=True)
    yn = y * lax.rsqrt(ms + RMS_EPS) * kg_ref[...]
    out = jnp.where(j == 0, yn, y)
    o_ref[0, 0] = out.astype(BF16)
    ot_ref[0, 0] = jnp.transpose(out).astype(BF16)


def _compress(kv, pe, w1, w2, k_gain0):
    g_ = NSA_KV_HEADS
    s = kv.shape[1]
    nrow = s // CMP_STRIDE
    wid = CMP_STRIDE * HEAD_DIM
    r = kv[:2 * g_].reshape(2 * g_, nrow, wid)
    pe2 = pe.reshape(2, 2, wid)
    return pl.pallas_call(
        _compress_body,
        grid=(2, g_),
        in_specs=[pl.BlockSpec((1, nrow, wid), lambda j, g: (j * g_ + g, 0, 0)),
                  pl.BlockSpec((1, 2, wid), lambda j, g: (j, 0, 0)),
                  pl.BlockSpec((1, 2 * wid, HEAD_DIM), lambda j, g: (j, 0, 0)),
                  pl.BlockSpec((1, HEAD_DIM, HEAD_DIM), lambda j, g: (j, 0, 0)),
                  pl.BlockSpec((1, HEAD_DIM), lambda j, g: (0, 0))],
        out_specs=[pl.BlockSpec((1, 1, nrow, HEAD_DIM), lambda j, g: (j, g, 0, 0)),
                   pl.BlockSpec((1, 1, HEAD_DIM, nrow), lambda j, g: (j, g, 0, 0))],
        out_shape=[jax.ShapeDtypeStruct((2, g_, nrow, HEAD_DIM), BF16),
                   jax.ShapeDtypeStruct((2, g_, HEAD_DIM, nrow), BF16)],
        compiler_params=_cparams(("parallel", "parallel")),
        name="nsa_compress",
    )(r, pe2, w1.astype(BF16), w2.astype(BF16), k_gain0.reshape(1, HEAD_DIM))


def _t5_bucket(dist):
    dist = jnp.maximum(dist, 0)
    max_exact = T5_BUCKETS // 2
    d_f = jnp.maximum(dist, 1).astype(F32)
    log_b = max_exact + (jnp.log(d_f / max_exact) / math.log(T5_MAX_DISTANCE / max_exact)
                         * (T5_BUCKETS - max_exact)).astype(jnp.int32)
    log_b = jnp.minimum(log_b, T5_BUCKETS - 1)
    return jnp.where(dist < max_exact, dist, log_b)


def _bias_table(tbl, dist, valid, shift):
    onehot = jax.nn.one_hot(_t5_bucket(dist).reshape(-1), T5_BUCKETS, dtype=F32)
    t = tbl - tbl[T5_BUCKETS - 1:T5_BUCKETS] if shift else tbl
    vals = jnp.einsum("nb,bh->hn", onehot, t, precision=lax.Precision.HIGHEST)
    vals = vals.reshape((tbl.shape[1],) + dist.shape) * LOG2E
    return jnp.where(valid[None], vals, NEG)


def _nsa_tables(rel_bias):
    tbl = rel_bias.astype(F32)
    r = jnp.arange(Q_BLOCK)
    jw = jnp.arange(WINDOW + Q_BLOCK)
    dw = WINDOW + r[None, :] - jw[:, None]
    tab_w = _bias_table(tbl, dw, (dw >= 0) & (dw < WINDOW), False)
    a = jnp.arange(16)
    cc = jnp.arange(2 * LANES)
    dc = (CMP_NEAR_SHIFT * Q_BLOCK + Q_BLOCK * a[:, None, None] + r[None, None, :]
          - CMP_STRIDE * cc[None, :, None] - (CMP_BLOCK - 1))
    tab_c = _bias_table(tbl, dc, dc >= 0, True)
    jj = jnp.arange(SEL_NEAR_TILES)
    col = jnp.arange(SEL_TILE)
    ds_ = r[None, None, :] - Q_BLOCK + SEL_TILE * (jj[:, None, None] + 1) - col[None, :, None]
    tab_s = _bias_table(tbl, ds_, ds_ >= 0, True)
    return tab_w, tab_c, tab_s, tbl


def _sel_to_cmp(nsel, nc_pad):
    c_start = jnp.arange(nc_pad) * CMP_STRIDE
    s_start = jnp.arange(nsel) * SEL_BLOCK
    overlap = jnp.clip(jnp.minimum(c_start[None, :] + CMP_BLOCK, s_start[:, None] + SEL_BLOCK)
                       - jnp.maximum(c_start[None, :], s_start[:, None]), 0)
    return (overlap.astype(F32) / CMP_BLOCK).astype(BF16)


def _heads_t(q_ref, h0, nh):
    cols = [jnp.transpose(q_ref[:, (h0 + h) * HEAD_DIM:(h0 + h + 1) * HEAD_DIM].astype(F32)) for h in range(nh)]
    return jnp.concatenate(cols, axis=1).astype(BF16)


def _softmax_keys(s, bounded):
    if bounded:
        p = jnp.exp2(s)
        l = jnp.sum(p, axis=0, keepdims=True)
        return p, jnp.where(l > 0.0, 1.0 / l, 0.0)
    m = jnp.max(s, axis=0, keepdims=True)
    p = jnp.exp2(s - m)
    l = jnp.sum(p, axis=0, keepdims=True)
    return p, jnp.where(m > 0.5 * NEG, 1.0 / l, 0.0)


def _logit_bound(rel_bias, q_gain, k_gain):
    qk = HEAD_DIM * (HEAD_DIM ** -0.5 * LOG2E) * jnp.max(jnp.abs(q_gain)) * jnp.max(jnp.abs(k_gain))
    return (1.02 * qk + 2.0 * LOG2E * jnp.max(jnp.abs(rel_bias))).astype(F32).reshape(1)


def _nsa_cw_body(bnd_ref, q_ref, gt_ref, kc_ref, vct_ref, mselt_ref, tabc_ref, kw_ref, vwt_ref, tabw_ref,
                 ocwt_ref, negt_ref):
    bounded = bnd_ref[0] < SAFE_LOGIT
    i = pl.program_id(1)

    @pl.when(bounded)
    def _():
        _nsa_cw_branches(True, i, q_ref, gt_ref, kc_ref, vct_ref, mselt_ref, tabc_ref, kw_ref, vwt_ref, tabw_ref,
                         ocwt_ref, negt_ref)

    @pl.when(jnp.logical_not(bounded))
    def _():
        _nsa_cw_branches(False, i, q_ref, gt_ref, kc_ref, vct_ref, mselt_ref, tabc_ref, kw_ref, vwt_ref, tabw_ref,
                         ocwt_ref, negt_ref)


def _nsa_cw_branches(bounded, i, q_ref, gt_ref, kc_ref, vct_ref, mselt_ref, tabc_ref, kw_ref, vwt_ref, tabw_ref,
                     ocwt_ref, negt_ref):
    f = (i + 16 - CMP_NEAR_SHIFT) // 16 - 1
    kc = kc_ref[0, 0]
    vct = vct_ref[0, 0]
    nc = kc.shape[0]
    nsel = mselt_ref.shape[0]
    wlen = WINDOW + Q_BLOCK
    nslab = wlen // LANES
    start = pl.multiple_of(i * Q_BLOCK, Q_BLOCK)
    kw = kw_ref[0, pl.ds(start, wlen), :]
    vwt = jnp.concatenate([vwt_ref[0, jnp.maximum(i + c - WINDOW // LANES, 0)] for c in range(nslab)],
                          axis=1)
    roww = lax.broadcasted_iota(jnp.int32, (wlen, LANES), 0)
    w_pad = roww < (WINDOW - Q_BLOCK * i)
    gt_t = jnp.transpose(gt_ref[...])
    psum = jnp.zeros((nc, Q_BLOCK), F32)
    hg = CW_HEADS
    for h0 in range(0, NSA_GROUP, hg):
        qt = _heads_t(q_ref, h0, hg)
        s = _dot(kc, qt)
        ta = jnp.concatenate([tabc_ref[h0 + h, 0, 0:LANES, :] for h in range(hg)], axis=1)
        tb = jnp.concatenate([tabc_ref[h0 + h, 0, LANES:2 * LANES, :] for h in range(hg)], axis=1)
        pieces = []
        for ch in range(nc // LANES):
            rest = jnp.where(ch > f + 1, NEG, 0.0)
            bias = jnp.where(ch == f, ta, jnp.where(ch == f + 1, tb, rest))
            pieces.append(s[ch * LANES:(ch + 1) * LANES] + bias)
        s = jnp.concatenate(pieces, axis=0)
        p, linv = _softmax_keys(s, bounded)
        pn = p * linv
        for h in range(hg):
            psum = psum + pn[:, h * Q_BLOCK:(h + 1) * Q_BLOCK]
        oc = _dot(vct, pn.astype(BF16))
        tw = jnp.concatenate([jnp.where(w_pad, NEG, tabw_ref[h0 + h]) for h in range(hg)], axis=1)
        sw = _dot(kw, qt) + tw
        pw, lwinv = _softmax_keys(sw, bounded)
        ow = _dot(vwt, pw.astype(BF16)) * lwinv
        for h in range(hg):
            hh = h0 + h
            cs = slice(h * Q_BLOCK, (h + 1) * Q_BLOCK)
            ocwt_ref[0, 0, hh * HEAD_DIM:(hh + 1) * HEAD_DIM, :] = (
                gt_t[hh:hh + 1] * oc[:, cs] + gt_t[16 + hh:17 + hh] * ow[:, cs])
    p_hi = psum.astype(BF16)
    p_lo = (psum - p_hi.astype(F32)).astype(BF16)
    mselt = mselt_ref[...]
    imp = _dot(mselt, p_hi) + _dot(mselt, p_lo)
    blk = lax.broadcasted_iota(jnp.int32, (nsel, Q_BLOCK), 0).astype(F32)
    qpos = lax.broadcasted_iota(jnp.int32, (nsel, Q_BLOCK), 1)
    cur = (i * (Q_BLOCK // SEL_BLOCK)).astype(F32) + jnp.where(qpos >= SEL_BLOCK, 1.0, 0.0)
    forced = jnp.logical_or(blk == cur, blk == 0.0)
    val = jnp.where(forced, FORCED_SCORE, jnp.where(blk <= cur, imp, -1.0))
    for _ in range(SEL_TOPK):
        mx = jnp.max(val, axis=0, keepdims=True)
        first = jnp.min(jnp.where(val == mx, blk, 1e4), axis=0, keepdims=True)
        val = jnp.where(blk == first, TAKEN, val)
    neg = jnp.where(val == TAKEN, 0.0, NEG)
    for pr in range(SEL_PAD_BLOCKS // 2):
        negt_ref[0, 0, pr] = jnp.full((2, Q_BLOCK), NEG, F32)
    for pr in range(nsel // 2):
        negt_ref[0, 0, SEL_PAD_BLOCKS // 2 + pr] = neg[2 * pr:2 * pr + 2]


def _nsa_cw(bound, q, gates, kc, vct, mselt, tab_c, kw, vt, tab_w):
    s, hd = q.shape
    g_ = NSA_KV_HEADS
    nq = s // Q_BLOCK
    gw = NSA_GROUP * HEAD_DIM
    nc = kc.shape[2]
    nsel = mselt.shape[0]
    spad = kw.shape[1]
    nslab = vt.shape[2]
    wlen = WINDOW + Q_BLOCK

    def var(i):
        return (i + 16 - CMP_NEAR_SHIFT) % 16

    return pl.pallas_call(
        _nsa_cw_body,
        grid=(g_, nq),
        in_specs=[
            pl.BlockSpec(memory_space=pltpu.SMEM),
            pl.BlockSpec((Q_BLOCK, gw), lambda g, i: (i, g)),
            pl.BlockSpec((Q_BLOCK, LANES), lambda g, i: (i, g)),
            pl.BlockSpec((1, 1, nc, HEAD_DIM), lambda g, i: (0, g, 0, 0)),
            pl.BlockSpec((1, 1, HEAD_DIM, nc), lambda g, i: (1, g, 0, 0)),
            pl.BlockSpec((nsel, nc), lambda g, i: (0, 0)),
            pl.BlockSpec((NSA_GROUP, 1, 2 * LANES, Q_BLOCK), lambda g, i: (g, var(i), 0, 0)),
            pl.BlockSpec((1, spad, HEAD_DIM), lambda g, i: (g, 0, 0)),
            pl.BlockSpec((None, 1, nslab, HEAD_DIM, LANES), lambda g, i: (5, g, 0, 0, 0)),
            pl.BlockSpec((NSA_GROUP, wlen, Q_BLOCK), lambda g, i: (g, 0, 0)),
        ],
        out_specs=[pl.BlockSpec((1, 1, gw, Q_BLOCK), lambda g, i: (g, i, 0, 0)),
                   pl.BlockSpec((1, 1, (SEL_PAD_BLOCKS + nsel) // 2, 2, Q_BLOCK), lambda g, i: (g, i, 0, 0, 0))],
        out_shape=[jax.ShapeDtypeStruct((g_, nq, gw, Q_BLOCK), F32),
                   jax.ShapeDtypeStruct((g_, nq, (SEL_PAD_BLOCKS + nsel) // 2, 2, Q_BLOCK), F32)],
        compiler_params=_cparams(("parallel", "arbitrary")),
        name="nsa_cmp_win",
    )(bound, q, gates, kc, vct, mselt, tab_c, kw, vt, tab_w)


def _nsa_sel_body(bnd_ref, q_ref, gt_ref, ks_ref, vst_ref, negp_ref, tabs_ref, ocwt_ref, o_ref,
                  m_sc, l_sc, acc_sc):
    i = pl.program_id(1)
    ntile = i // (SEL_TILE // Q_BLOCK) + 1
    nh = NSA_GROUP
    spt = SEL_TILE // LANES
    ppt = SEL_TILE // (2 * SEL_BLOCK)
    qt = _heads_t(q_ref, 0, nh)

    def tile(jj, near, bounded, nsub=1):
        slab0 = (i + 1) - spt * jj
        row0 = pl.multiple_of(slab0 * LANES, LANES)
        kt = ks_ref[0, pl.ds(row0, nsub * SEL_TILE), :]
        vtt = jnp.concatenate([vst_ref[0, jnp.maximum(slab0 - spt + c, 0)] for c in range(nsub * spt)],
                              axis=1)
        rows = []
        for c in range(nsub * ppt):
            pair = negp_ref[0, 0, slab0 + c]
            rows += [jnp.broadcast_to(pair[r:r + 1], (SEL_BLOCK, Q_BLOCK)) for r in range(2)]
        mk = jnp.concatenate(rows, axis=0)
        if near:
            bias = jnp.concatenate(
                [jnp.concatenate([tabs_ref[h, jj - c] for c in range(nsub)], axis=0) + mk for h in range(nh)],
                axis=1)
        else:
            bias = jnp.tile(mk, (1, nh))
        sc = _dot(kt, qt) + bias
        if bounded:
            p = jnp.exp2(sc)
            l_sc[...] = l_sc[...] + jnp.sum(p, axis=0, keepdims=True)
            acc_sc[...] = acc_sc[...] + _dot(vtt, p.astype(BF16))
        else:
            m_old = m_sc[...]
            m_new = jnp.maximum(m_old, jnp.max(sc, axis=0, keepdims=True))
            alpha = jnp.exp2(m_old - m_new)
            p = jnp.exp2(sc - m_new)
            l_sc[...] = alpha * l_sc[...] + jnp.sum(p, axis=0, keepdims=True)
            acc_sc[...] = alpha * acc_sc[...] + _dot(vtt, p.astype(BF16))
            m_sc[...] = m_new

    def run(bounded):
        m_sc[...] = jnp.full(m_sc.shape, NEG, F32)
        l_sc[...] = jnp.zeros(l_sc.shape, F32)
        acc_sc[...] = jnp.zeros(acc_sc.shape, F32)
        @pl.when(ntile >= SEL_NEAR_TILES)
        def _():
            for jj in range(SEL_FAR_WIDTH - 1, SEL_NEAR_TILES, SEL_FAR_WIDTH):
                tile(jj, True, bounded, SEL_FAR_WIDTH)
            for jj in range(SEL_NEAR_TILES - SEL_NEAR_TILES % SEL_FAR_WIDTH, SEL_NEAR_TILES):
                tile(jj, True, bounded)

        @pl.when(ntile < SEL_NEAR_TILES)
        def _():
            for jj in range(SEL_NEAR_TILES - 1):
                @pl.when(jj < ntile)
                def _():
                    tile(jj, True, bounded)

        nfar = jnp.maximum(ntile - SEL_NEAR_TILES, 0)

        def far(k, carry):
            tile(SEL_NEAR_TILES + SEL_FAR_WIDTH * k + SEL_FAR_WIDTH - 1, False, bounded, SEL_FAR_WIDTH)
            return carry

        lax.fori_loop(0, nfar // SEL_FAR_WIDTH, far, 0)

        def far_rest(jj, carry):
            tile(jj, False, bounded)
            return carry

        lax.fori_loop(SEL_NEAR_TILES + nfar // SEL_FAR_WIDTH * SEL_FAR_WIDTH, ntile, far_rest, 0)

    bounded = bnd_ref[0] < SAFE_LOGIT

    @pl.when(bounded)
    def _():
        run(True)

    @pl.when(jnp.logical_not(bounded))
    def _():
        run(False)

    o_t = acc_sc[...] * (1.0 / l_sc[...])
    gt_t = jnp.transpose(gt_ref[...])
    for h in range(nh):
        hs = slice(h * HEAD_DIM, (h + 1) * HEAD_DIM)
        oh = ocwt_ref[0, 0, hs, :] + gt_t[8 + h:9 + h] * o_t[:, h * Q_BLOCK:(h + 1) * Q_BLOCK]
        o_ref[:, hs] = jnp.transpose(oh).astype(BF16)


def _nsa_sel(bound, q, gates, ks, vt, negp, tab_s, ocwt):
    s, hd = q.shape
    g_ = NSA_KV_HEADS
    nq = s // Q_BLOCK
    gw = NSA_GROUP * HEAD_DIM
    spad = ks.shape[1]
    nslab = vt.shape[2]
    npair = negp.shape[2]
    return pl.pallas_call(
        _nsa_sel_body,
        grid=(g_, nq),
        in_specs=[
            pl.BlockSpec(memory_space=pltpu.SMEM),
            pl.BlockSpec((Q_BLOCK, gw), lambda g, i: (i, g)),
            pl.BlockSpec((Q_BLOCK, LANES), lambda g, i: (i, g)),
            pl.BlockSpec((1, spad, HEAD_DIM), lambda g, i: (g, 0, 0)),
            pl.BlockSpec((None, 1, nslab, HEAD_DIM, LANES), lambda g, i: (3, g, 0, 0, 0)),
            pl.BlockSpec((1, 1, npair, 2, Q_BLOCK), lambda g, i: (g, i, 0, 0, 0)),
            pl.BlockSpec((NSA_GROUP, SEL_NEAR_TILES, SEL_TILE, Q_BLOCK), lambda g, i: (g, 0, 0, 0)),
            pl.BlockSpec((1, 1, gw, Q_BLOCK), lambda g, i: (g, i, 0, 0)),
        ],
        out_specs=pl.BlockSpec((Q_BLOCK, gw), lambda g, i: (i, g)),
        out_shape=jax.ShapeDtypeStruct((s, hd), BF16),
        scratch_shapes=[pltpu.VMEM((1, NSA_GROUP * Q_BLOCK), F32), pltpu.VMEM((1, NSA_GROUP * Q_BLOCK), F32),
                        pltpu.VMEM((HEAD_DIM, NSA_GROUP * Q_BLOCK), F32)],
        compiler_params=_cparams(("parallel", "arbitrary")),
        name="nsa_selected",
    )(bound, q, gates, ks, vt, negp, tab_s, ocwt)


def _pad_keys(x, pad):
    return jnp.pad(x, ((0, 0), (pad, 0), (0, 0)))


def _nsa_mixer(h, hn, tables, layer, w_in, cmp_pe, cmp_w1, cmp_w2, q_gain, k_gain, w_out):
    s, d = h.shape
    g_, hpg, dh = NSA_KV_HEADS, NSA_GROUP, HEAD_DIM
    hd = g_ * hpg * dh
    tab_w, tab_c, tab_s, tbl = tables
    q = _nsa_q_proj(hn, w_in, layer, q_gain, hd)
    kv, vt = _nsa_kv_proj(hn, w_in, layer, k_gain, hd)
    wg = w_in[layer, :, hd + 6 * g_ * dh:].reshape(d, g_, hpg, 3).transpose(0, 1, 3, 2).reshape(d, g_, 3 * hpg)
    wg = jnp.pad(wg, ((0, 0), (0, 0), (0, LANES - 3 * hpg))).reshape(d, g_ * LANES)
    gates = _nsa_gate_proj(hn, wg)
    kc, kct = _compress(kv, cmp_pe, cmp_w1, cmp_w2, k_gain[0])
    mselt = _sel_to_cmp(s // SEL_BLOCK, s // CMP_STRIDE)
    ks = _pad_keys(kv[2 * g_:3 * g_], SEL_TILE)
    kw = _pad_keys(kv[4 * g_:5 * g_], WINDOW)
    bound = _logit_bound(tbl, q_gain, k_gain)
    ocwt, negp = _nsa_cw(bound, q, gates, kc, kct, mselt, tab_c, kw, vt, tab_w)
    o = _nsa_sel(bound, q, gates, ks, vt, negp, tab_s, ocwt)
    return _matmul_resid(o, w_out, h, layer=layer, name="nsa_out")


def _gelu_proj(hn, w, layer, *, tm=512, tn=1024):
    m, kdim = hn.shape
    nn = w.shape[-1]

    def epi(accs, e_refs, o_refs, n):
        o_refs[0][...] = jax.nn.gelu(accs[0]).astype(BF16)

    return _matmul(
        hn, [(w, _wspec(kdim, tn, 0, layer))], epi,
        [(jax.ShapeDtypeStruct((m, nn), BF16), _ospec(tm, tn))],
        tm=tm, tn=tn, n_total=nn, name="sgu_in")[0]


def _sgu_mix_body(u_ref, v_ref, gain_ref, w_ref, bt_ref, o_ref):
    vn = _norm_rows(v_ref[...].astype(F32), gain_ref[...])
    t = w_ref.shape[1]
    causal = (lax.broadcasted_iota(jnp.int32, (t, t), 0) >= lax.broadcasted_iota(jnp.int32, (t, t), 1))
    bt = bt_ref[...]
    gd = vn.shape[1] // SG_GROUPS
    for g in range(SG_GROUPS):
        gs = slice(g * gd, (g + 1) * gd)
        w = jnp.where(causal, w_ref[g], 0.0).astype(BF16)
        mixed = _dot(w, vn[:, gs]) + bt[:, g:g + 1]
        o_ref[:, gs] = (u_ref[:, gs].astype(F32) * mixed).astype(BF16)


def _sgu_mix(uv, v_gain, w_s, b_s):
    s = uv.shape[0]
    wd = uv.shape[1] // 2
    t = SG_CHUNK
    return pl.pallas_call(
        _sgu_mix_body,
        grid=(s // t,),
        in_specs=[pl.BlockSpec((t, wd), lambda c: (c, 0)),
                  pl.BlockSpec((t, wd), lambda c: (c, 1)),
                  pl.BlockSpec((1, wd), lambda c: (0, 0)),
                  pl.BlockSpec((SG_GROUPS, t, t), lambda c: (0, 0, 0)),
                  pl.BlockSpec((t, SG_GROUPS), lambda c: (0, 0))],
        out_specs=pl.BlockSpec((t, wd), lambda c: (c, 0)),
        out_shape=jax.ShapeDtypeStruct((s, wd), BF16),
        compiler_params=_cparams(("parallel",)),
        name="sgu_mix",
    )(uv, uv, v_gain.reshape(1, wd), w_s, b_s.T)


def _sgu_mixer(h, hn, layer, w_in, v_gain, w_s, b_s, w_out):
    uv = _gelu_proj(hn, w_in, layer)
    y = _sgu_mix(uv, v_gain, w_s, b_s)
    return _matmul_resid(y, w_out, h, layer=layer, name="sgu_out")


def _gla_body(q_ref, k_ref, v_ref, r_ref, g1_ref, wg_ref, bg_ref, og_ref, o_ref, state_sc, diag_sc):
    @pl.when(pl.program_id(0) == 0)
    def _():
        state_sc[...] = jnp.zeros_like(state_sc)

    nh = state_sc.shape[0]
    dk = q_ref.shape[1] // nh
    dv = v_ref.shape[1] // nh
    g1 = g1_ref[...].astype(BF16)
    for hh in range(nh):
        ks = slice(hh * dk, (hh + 1) * dk)
        vs = slice(hh * dv, (hh + 1) * dv)
        o_ref[:, vs] = _gla_head(q_ref[:, ks], k_ref[:, ks], v_ref[:, vs], r_ref[:, vs], g1, wg_ref[:, ks],
                                 bg_ref[:, ks], og_ref[...], state_sc.at[hh], diag_sc.at[hh])


def _gla_head(q_in, k_in, v, r_in, g1, wg, bg, og, state_sc, diag_sc):
    ch, dk = q_in.shape
    x = _dot(g1, wg) + bg
    log_a = (jnp.minimum(x, 0.0) - jnp.log(1.0 + jnp.exp(-jnp.abs(x)))) * (1.0 / GLA_GATE_TEMP)
    tri = (lax.broadcasted_iota(jnp.int32, (ch, ch), 0)
           >= lax.broadcasted_iota(jnp.int32, (ch, ch), 1))
    tri_b = jnp.where(tri, 1.0, 0.0).astype(BF16)
    a_hi = log_a.astype(BF16)
    a_lo = (log_a - a_hi.astype(F32)).astype(BF16)
    b = _dot(tri_b, a_hi) + _dot(tri_b, a_lo)
    q = q_in.astype(F32) * (dk ** -0.5)
    k = k_in.astype(F32)
    state = state_sc[...]
    o = _dot((q * jnp.exp(b)).astype(BF16), state.astype(BF16))
    nsub = ch // GLA_SUB
    ends = [jnp.broadcast_to(b[(jb + 1) * GLA_SUB - 1:(jb + 1) * GLA_SUB], (GLA_SUB, dk))
            for jb in range(nsub)]
    b_end = jnp.concatenate(ends, axis=0)
    b_start = jnp.concatenate([jnp.zeros((GLA_SUB, dk), F32)] + ends[:-1], axis=0)
    rblk = lax.broadcasted_iota(jnp.int32, (ch, ch), 0) // GLA_SUB
    cblk = lax.broadcasted_iota(jnp.int32, (ch, ch), 1) // GLA_SUB
    zero_row = jnp.zeros((1, dk), F32)
    spread = jnp.max(jnp.concatenate(
        [(ends[jb - 1][0:1] if jb else zero_row) - ends[jb][0:1] for jb in range(nsub)], axis=0))

    @pl.when(spread < GLA_SAFE_EXP)
    def _():
        q_diag = (q * jnp.exp(b - b_start)).astype(BF16)
        k_diag = (k * jnp.exp(b_start - b)).astype(BF16)
        diag_sc[...] = _dot_nt(q_diag, k_diag)

    @pl.when(spread >= GLA_SAFE_EXP)
    def _():
        lane = lax.broadcasted_iota(jnp.int32, (GLA_SUB, ch), 1)
        strips = []
        for ib in range(nsub):
            rs = slice(ib * GLA_SUB, (ib + 1) * GLA_SUB)
            qi, ki, bi = q[rs], k[rs], b[rs]
            strip = jnp.zeros((GLA_SUB, ch), F32)
            for j in range(GLA_SUB):
                decay = jnp.exp(jnp.minimum(bi - bi[j:j + 1], 0.0))
                col = jnp.sum(qi * ki[j:j + 1] * decay, axis=-1, keepdims=True)
                strip = jnp.where(lane == ib * GLA_SUB + j, col, strip)
            strips.append(strip)
        diag_sc[...] = jnp.concatenate(strips, axis=0)

    attn = jnp.where(jnp.logical_and(tri, rblk == cblk), diag_sc[...], 0.0)
    k_hat = (k * jnp.exp(b_end - b)).astype(BF16)
    for jb in range(nsub - 1):
        q_hat = (q * jnp.exp(jnp.minimum(b - ends[jb][0:1], 0.0))).astype(BF16)
        attn = attn + jnp.where(jnp.logical_and(cblk == jb, rblk > jb), _dot_nt(q_hat, k_hat), 0.0)
    o = o + _dot(attn.astype(BF16), v)
    b_last = b[ch - 1:ch]
    k_dec = k * jnp.exp(b_last - b)
    k_dec_t = jnp.transpose(k_dec).astype(BF16)
    decay_t = jnp.transpose(jnp.broadcast_to(jnp.exp(b_last), (LANES, dk)))
    state_sc[...] = decay_t[:, 0:1] * state + _dot(k_dec_t, v)
    ms = jnp.mean(o * o, axis=-1, keepdims=True)
    on = o * lax.rsqrt(ms + RMS_EPS) * og
    return (on * _silu(r_in.astype(F32))).astype(BF16)


def _gla_mixer(h, hn, layer, w_in, w_gate2, b_gate, o_gain, w_out):
    s, d = h.shape
    nh = GLA_HEADS
    dk = d // 2
    dv = d
    dkh, dvh = dk // nh, dv // nh
    nmain = 2 * dk + 2 * dv

    def epi_bf16(accs, e_refs, o_refs, n):
        o_refs[0][...] = accs[0].astype(BF16)

    def epi_f32(accs, e_refs, o_refs, n):
        o_refs[0][...] = accs[0]

    tm, tn = 512, 1024
    proj = _matmul(hn, [(w_in, _wspec(d, tn, 0, layer))], epi_bf16,
                   [(jax.ShapeDtypeStruct((s, nmain), BF16), _ospec(tm, tn))],
                   tm=tm, tn=tn, n_total=nmain, name="gla_in")[0]
    wg1 = jnp.pad(w_in[layer, :, nmain:], ((0, 0), (0, LANES - GLA_GATE_RANK)))
    g1 = _matmul(hn, [(wg1, _wspec(d, LANES))], epi_f32,
                 [(jax.ShapeDtypeStruct((s, LANES), F32), _ospec(tm, LANES))],
                 tm=tm, tn=LANES, n_total=LANES, name="gla_gate_in")[0]
    wg2 = jnp.pad(w_gate2.astype(BF16), ((0, LANES - GLA_GATE_RANK), (0, 0)))
    ch = GLA_CHUNK
    o = pl.pallas_call(
        _gla_body,
        grid=(s // ch,),
        in_specs=[
            pl.BlockSpec((ch, dk), lambda c: (c, 0)),
            pl.BlockSpec((ch, dk), lambda c: (c, 1)),
            pl.BlockSpec((ch, dv), lambda c: (c, 2 * dk // dv)),
            pl.BlockSpec((ch, dv), lambda c: (c, 2 * dk // dv + 1)),
            pl.BlockSpec((ch, LANES), lambda c: (c, 0)),
            pl.BlockSpec((LANES, dk), lambda c: (0, 0)),
            pl.BlockSpec((1, dk), lambda c: (0, 0)),
            pl.BlockSpec((1, dvh), lambda c: (0, 0)),
        ],
        out_specs=pl.BlockSpec((ch, dv), lambda c: (c, 0)),
        out_shape=jax.ShapeDtypeStruct((s, dv), BF16),
        scratch_shapes=[pltpu.VMEM((nh, dkh, dvh), F32), pltpu.VMEM((nh, ch, ch), F32)],
        compiler_params=_cparams(("arbitrary",)),
        name="gla_scan",
    )(proj, proj, proj, proj, g1, wg2, b_gate.reshape(1, dk), o_gain.reshape(1, dvh))
    return _matmul_resid(o, w_out, h, layer=layer, name="gla_out")


def _dense_ffn(h, hn, layer, w_up, w_down):
    act = _swiglu_up(hn, w_up, layer=layer)
    return _matmul_resid(act, w_down, h, layer=layer, tn=512, tk=w_down.shape[1], name="ffn_down")


def _moe_ffn(h, gain, router, layer, w_up, w_down):
    s, _ = h.shape
    info, cnt = _moe_router(h, gain, router)
    dest1, dest2, src, tile_expert, nvalid = _moe_plan(info, cnt, s)
    xs = _moe_gather(h, gain, src, nvalid)
    act = _moe_up(xs, w_up, layer, tile_expert, nvalid)
    y = _moe_down(act, w_down, layer, tile_expert, nvalid)
    return _moe_combine(h, y, info, dest1, dest2)


def kernel(x, rel_bias, norm_gain, nsa_w_in, nsa_cmp_pe, nsa_cmp_w1, nsa_cmp_w2, nsa_q_gain, nsa_k_gain, nsa_w_out, sg_w_in, sg_v_gain, sg_w_s, sg_b_s, sg_w_out, gla_w_in, gla_w_gate2, gla_b_gate, gla_o_gain, gla_w_out, ffn_w_up, ffn_w_down, moe_router, moe_w_up, moe_w_down):
    bsz, s, d = x.shape
    tables = _nsa_tables(rel_bias)
    nsa_in, nsa_out = nsa_w_in.astype(BF16), nsa_w_out.astype(BF16)
    sg_in, sg_out = sg_w_in.astype(BF16), sg_w_out.astype(BF16)
    gla_in, gla_out = gla_w_in.astype(BF16), gla_w_out.astype(BF16)
    ffn_up, ffn_down = ffn_w_up.astype(BF16), ffn_w_down.astype(BF16)
    outs = []
    for bi in range(bsz):
        h = x.reshape(s, d) if bsz == 1 else x[bi]
        for i in range(DEPTH):
            mixer = i % N_MIXERS
            j = i // N_MIXERS
            hn = _rmsnorm(h, norm_gain[i, 0])
            if mixer == 0:
                h = _nsa_mixer(h, hn, tables, j, nsa_in, nsa_cmp_pe[j], nsa_cmp_w1[j], nsa_cmp_w2[j],
                               nsa_q_gain[j], nsa_k_gain[j], nsa_out)
            elif mixer == 1:
                h = _sgu_mixer(h, hn, j, sg_in, sg_v_gain[j], sg_w_s[j], sg_b_s[j], sg_out)
            else:
                h = _gla_mixer(h, hn, j, gla_in, gla_w_gate2[j], gla_b_gate[j], gla_o_gain[j], gla_out)
            f = i // 2
            if i % 2 == 0:
                hn = _rmsnorm(h, norm_gain[i, 1])
                h = _dense_ffn(h, hn, f, ffn_up, ffn_down)
            else:
                h = _moe_ffn(h, norm_gain[i, 1], moe_router[f], f, moe_w_up, moe_w_down)
        outs.append(h)
    return outs[0].reshape(1, s, d) if bsz == 1 else jnp.stack(outs, axis=0)
```

```python
import math

import jax
import jax.numpy as jnp
from jax import lax
from jax.experimental import pallas as pl
from jax.experimental.pallas import tpu as pltpu

F32 = jnp.float32
BF16 = jnp.bfloat16

DEPTH = 4
N_MIXERS = 3
RMS_EPS = 1e-6
NEG = -1e30
HEAD_DIM = 128
NSA_KV_HEADS = 4
NSA_GROUP = 8
CMP_BLOCK = 32
CMP_STRIDE = 16
SEL_BLOCK = 64
SEL_TOPK = 16
WINDOW = 512
Q_BLOCK = 128
FORCED_SCORE = 1e4
TAKEN = -2.0
T5_BUCKETS = 32
T5_MAX_DISTANCE = 2048
SG_CHUNK = 128
SG_GROUPS = 32
GLA_HEADS = 4
GLA_GATE_RANK = 16
GLA_GATE_TEMP = 16.0
GLA_CHUNK = 64
GLA_SUB = 16
GLA_SAFE_EXP = 80.0
MOE_EXPERTS = 8
MOE_TM = 512

LANES = 128
SEL_TILE = 512
SEL_NEAR_TILES = 5
SEL_FAR_WIDTH = 2
CMP_NEAR_SHIFT = 12
CW_HEADS = 8
VMEM_MB = 56
LOG2E = 1.4426950408889634
SAFE_LOGIT = 60.0
SEL_PAD_BLOCKS = 8


def _cparams(sem, vmem_mb=VMEM_MB):
    return pltpu.CompilerParams(dimension_semantics=sem, vmem_limit_bytes=vmem_mb * 2**20)


def _dot(a, b):
    return jnp.dot(a, b, preferred_element_type=F32)


def _dot_nt(a, b):
    return lax.dot_general(a, b, (((1,), (1,)), ((), ())), preferred_element_type=F32)


def _rmsnorm_body(x_ref, g_ref, o_ref):
    x = x_ref[...]
    ms = jnp.mean(x * x, axis=-1, keepdims=True)
    o_ref[...] = (x * lax.rsqrt(ms + RMS_EPS) * g_ref[...]).astype(o_ref.dtype)


def _rmsnorm(x, gain, tm=256):
    m, d = x.shape
    return pl.pallas_call(
        _rmsnorm_body,
        grid=(m // tm,),
        in_specs=[pl.BlockSpec((tm, d), lambda i: (i, 0)), pl.BlockSpec((1, d), lambda i: (0, 0))],
        out_specs=pl.BlockSpec((tm, d), lambda i: (i, 0)),
        out_shape=jax.ShapeDtypeStruct((m, d), BF16),
        compiler_params=_cparams(("parallel",)),
        name="rmsnorm",
    )(x, gain.reshape(1, d))


def _norm_rows(x, gain):
    ms = jnp.mean(x * x, axis=-1, keepdims=True)
    return (x * lax.rsqrt(ms + RMS_EPS) * gain).astype(BF16)


RT_E1, RT_E2, RT_W1, RT_W2, RT_R1, RT_R2 = range(6)


def _moe_router_body(x_ref, g_ref, r_ref, info_ref, cnt_ref, base_sc):
    @pl.when(pl.program_id(0) == 0)
    def _():
        base_sc[...] = jnp.zeros_like(base_sc)

    hn = _norm_rows(x_ref[...], g_ref[...])
    logits = _dot(hn, r_ref[...])
    lane = lax.broadcasted_iota(jnp.int32, logits.shape, 1).astype(F32)
    logits = jnp.where(lane < MOE_EXPERTS, logits, NEG)
    v1 = jnp.max(logits, axis=-1, keepdims=True)
    i1 = jnp.min(jnp.where(logits == v1, lane, 1e3), axis=-1, keepdims=True)
    rest = jnp.where(lane == i1, NEG, logits)
    v2 = jnp.max(rest, axis=-1, keepdims=True)
    i2 = jnp.min(jnp.where(rest == v2, lane, 1e3), axis=-1, keepdims=True)
    e2 = jnp.exp(v2 - v1)
    den = 1.0 + e2
    hot = jnp.where(jnp.logical_or(lane == i1, lane == i2), 1.0, 0.0)
    tm = hot.shape[0]
    earlier = (lax.broadcasted_iota(jnp.int32, (tm, tm), 0) > lax.broadcasted_iota(jnp.int32, (tm, tm), 1))
    before = _dot(jnp.where(earlier, 1.0, 0.0).astype(BF16), hot.astype(BF16)) + base_sc[...]
    r1 = jnp.sum(jnp.where(lane == i1, before, 0.0), axis=-1, keepdims=True)
    r2 = jnp.sum(jnp.where(lane == i2, before, 0.0), axis=-1, keepdims=True)
    rec = jnp.zeros_like(logits)
    for pos, val in ((RT_E1, i1), (RT_E2, i2), (RT_W1, 1.0 / den), (RT_W2, e2 / den), (RT_R1, r1), (RT_R2, r2)):
        rec = jnp.where(lane == float(pos), val, rec)
    info_ref[...] = rec
    total = base_sc[...] + jnp.sum(hot, axis=0, keepdims=True)
    base_sc[...] = total
    cnt_ref[...] = jnp.broadcast_to(total, cnt_ref.shape)


def _moe_router(x, gain, router, tm=256):
    m, d = x.shape
    rpad = jnp.zeros((d, LANES), BF16).at[:, :MOE_EXPERTS].set(router.astype(BF16))
    return pl.pallas_call(
        _moe_router_body,
        grid=(m // tm,),
        in_specs=[pl.BlockSpec((tm, d), lambda i: (i, 0)), pl.BlockSpec((1, d), lambda i: (0, 0)),
                  pl.BlockSpec((d, LANES), lambda i: (0, 0))],
        out_specs=[pl.BlockSpec((tm, LANES), lambda i: (i, 0)), pl.BlockSpec((8, LANES), lambda i: (0, 0))],
        out_shape=[jax.ShapeDtypeStruct((m, LANES), F32), jax.ShapeDtypeStruct((8, LANES), F32)],
        scratch_shapes=[pltpu.VMEM((1, LANES), F32)],
        compiler_params=_cparams(("arbitrary",)),
        name="moe_router",
    )(x, gain.reshape(1, d), rpad)


def _matmul(x, w_list, epilogue, out_list, *, tm, tn, n_total, tk=None, extras=(), name="matmul",
            dimsem=("parallel", "parallel", "arbitrary")):
    m, kdim = x.shape
    tk = kdim if tk is None else tk
    nk = kdim // tk
    nw, ne, no = len(w_list), len(extras), len(out_list)

    def body(*refs):
        x_ref = refs[0]
        w_refs = refs[1:1 + nw]
        e_refs = refs[1 + nw:1 + nw + ne]
        o_refs = refs[1 + nw + ne:1 + nw + ne + no]
        acc_refs = refs[1 + nw + ne + no:]
        n = pl.program_id(0)
        if nk == 1:
            xv = x_ref[...]
            epilogue([_dot(xv, w[...].astype(xv.dtype)) for w in w_refs], e_refs, o_refs, n)
        else:
            k = pl.program_id(2)

            @pl.when(k == 0)
            def _():
                for a in acc_refs:
                    a[...] = jnp.zeros_like(a)

            xv = x_ref[...]
            for a, w in zip(acc_refs, w_refs):
                a[...] += _dot(xv, w[...])

            @pl.when(k == nk - 1)
            def _():
                epilogue([a[...] for a in acc_refs], e_refs, o_refs, n)

    in_specs = [pl.BlockSpec((tm, tk), lambda n, mi, k: (mi, k))]
    in_specs += [s for _, s in w_list] + [s for _, s in extras]
    scratch = [] if nk == 1 else [pltpu.VMEM((tm, tn), F32) for _ in range(nw)]
    return pl.pallas_call(
        body,
        grid=(n_total // tn, m // tm, nk),
        in_specs=in_specs,
        out_specs=[s for _, s in out_list],
        out_shape=[s for s, _ in out_list],
        scratch_shapes=scratch,
        compiler_params=_cparams(dimsem),
        name=name,
    )(x, *[a for a, _ in w_list], *[a for a, _ in extras])


def _wspec(tk, tn, off=0, layer=None):
    if layer is None:
        return pl.BlockSpec((tk, tn), lambda n, mi, k: (k, n + off))
    return pl.BlockSpec((None, tk, tn), lambda n, mi, k: (layer, k, n + off))


def _ospec(tm, tn):
    return pl.BlockSpec((tm, tn), lambda n, mi, k: (mi, n))


def _epi_resid(accs, e_refs, o_refs, n):
    o_refs[0][...] = e_refs[0][...] + accs[0]


def _matmul_resid(x, w, resid, *, layer=None, tm=512, tn=1024, tk=None, name="matmul_resid"):
    m, kdim = x.shape
    nn = w.shape[-1]
    tk = min(kdim, 4096) if tk is None else tk
    return _matmul(
        x, [(w, _wspec(tk, tn, 0, layer))], _epi_resid,
        [(jax.ShapeDtypeStruct((m, nn), F32), _ospec(tm, tn))],
        tm=tm, tn=tn, tk=tk, n_total=nn, extras=[(resid, _ospec(tm, tn))], name=name)[0]


def _silu(a):
    return a * (1.0 / (1.0 + jnp.exp(-a)))


def _epi_swiglu(accs, e_refs, o_refs, n):
    a, b = accs
    o_refs[0][...] = (_silu(a) * b).astype(BF16)


def _swiglu_up(x, w_up, *, layer=None, tm=512, tn=512, name="swiglu_up"):
    m, kdim = x.shape
    ff = w_up.shape[-1] // 2
    return _matmul(
        x, [(w_up, _wspec(kdim, tn, 0, layer)), (w_up, _wspec(kdim, tn, ff // tn, layer))], _epi_swiglu,
        [(jax.ShapeDtypeStruct((m, ff), BF16), _ospec(tm, tn))],
        tm=tm, tn=tn, n_total=ff, name=name)[0]


def _moe_plan(info, cnt, s):
    ne, tm = MOE_EXPERTS, MOE_TM
    e1 = info[:, RT_E1].astype(jnp.int32)
    e2 = info[:, RT_E2].astype(jnp.int32)
    counts = cnt[0, :ne].astype(jnp.int32)
    padded = (counts + tm - 1) // tm * tm
    ends = jnp.cumsum(padded)
    off = ends - padded
    dest1 = off[e1] + info[:, RT_R1].astype(jnp.int32)
    dest2 = off[e2] + info[:, RT_R2].astype(jnp.int32)
    ntile = (2 * s) // tm + ne
    tile_start = jnp.arange(ntile, dtype=jnp.int32) * tm
    tile_expert = jnp.minimum(jnp.sum(tile_start[:, None] >= ends[None, :], axis=1), ne - 1).astype(jnp.int32)
    nvalid = (ends[ne - 1:ne] // tm).astype(jnp.int32)
    tok = jnp.arange(s, dtype=jnp.int32)
    src = jnp.zeros((ntile * tm,), jnp.int32).at[dest1].set(tok).at[dest2].set(tok)
    return dest1, dest2, src, tile_expert, nvalid


def _row_copy(src_hbm, row, dst, r, sem):
    return pltpu.make_async_copy(src_hbm.at[pl.ds(row, 1)], dst.at[pl.ds(r, 1)], sem)


def _moe_gather_body(src_ref, nv_ref, h_hbm, g_ref, o_ref, buf, sems):
    t = pl.program_id(0)
    nv = nv_ref[0]
    tm = buf.shape[1]

    def rows(tile, slot, start):
        def body(r, carry):
            cp = _row_copy(h_hbm, src_ref[tile * tm + r], buf.at[slot], r, sems.at[slot])
            if start:
                cp.start()
            else:
                cp.wait()
            return carry

        lax.fori_loop(0, tm, body, 0)

    @pl.when(jnp.logical_and(t == 0, nv > 0))
    def _():
        rows(0, 0, True)

    @pl.when(t + 1 < nv)
    def _():
        rows(t + 1, (t + 1) % 2, True)

    @pl.when(t < nv)
    def _():
        rows(t, t % 2, False)
        o_ref[...] = _norm_rows(buf[t % 2], g_ref[...])

    @pl.when(t >= nv)
    def _():
        o_ref[...] = jnp.zeros_like(o_ref)


def _moe_gather(h, gain, src, nvalid):
    s, d = h.shape
    tm = MOE_TM
    ntile = src.shape[0] // tm
    return pl.pallas_call(
        _moe_gather_body,
        grid_spec=pltpu.PrefetchScalarGridSpec(
            num_scalar_prefetch=2,
            grid=(ntile,),
            in_specs=[pl.BlockSpec(memory_space=pl.ANY),
                      pl.BlockSpec((1, d), lambda t, src_, nv_: (0, 0))],
            out_specs=pl.BlockSpec((tm, d), lambda t, src_, nv_: (t, 0)),
            scratch_shapes=[pltpu.VMEM((2, tm, d), F32), pltpu.SemaphoreType.DMA((2,))]),
        out_shape=jax.ShapeDtypeStruct((ntile * tm, d), BF16),
        compiler_params=_cparams(("arbitrary",)),
        name="moe_gather",
    )(src, nvalid, h, gain.reshape(1, d))


def _new_expert(te_ref, t):
    return jnp.logical_or(t == 0, te_ref[t] != te_ref[jnp.maximum(t - 1, 0)])


def _moe_up_body(te_ref, nv_ref, x_ref, wa_ref, wb_ref, o_ref, wa_sc, wb_sc):
    t = pl.program_id(1)
    live = t < nv_ref[0]

    @pl.when(jnp.logical_and(live, _new_expert(te_ref, t)))
    def _():
        wa_sc[...] = wa_ref[...].astype(BF16)
        wb_sc[...] = wb_ref[...].astype(BF16)

    @pl.when(live)
    def _():
        x = x_ref[...]
        o_ref[...] = (_silu(_dot(x, wa_sc[...])) * _dot(x, wb_sc[...])).astype(BF16)

    @pl.when(jnp.logical_not(live))
    def _():
        o_ref[...] = jnp.zeros_like(o_ref)


def _moe_up(xs, w_up, layer, tile_expert, nvalid, *, tn=256):
    p, d = xs.shape
    tm = MOE_TM
    ff = w_up.shape[-1] // 2
    nn = ff // tn
    return pl.pallas_call(
        _moe_up_body,
        grid_spec=pltpu.PrefetchScalarGridSpec(
            num_scalar_prefetch=2,
            grid=(nn, p // tm),
            in_specs=[pl.BlockSpec((tm, d), lambda n, t, te, nv: (t, 0)),
                      pl.BlockSpec((None, None, d, tn), lambda n, t, te, nv: (layer, te[t], 0, n)),
                      pl.BlockSpec((None, None, d, tn), lambda n, t, te, nv: (layer, te[t], 0, n + nn))],
            out_specs=pl.BlockSpec((tm, tn), lambda n, t, te, nv: (t, n)),
            scratch_shapes=[pltpu.VMEM((d, tn), BF16), pltpu.VMEM((d, tn), BF16)]),
        out_shape=jax.ShapeDtypeStruct((p, ff), BF16),
        compiler_params=_cparams(("arbitrary", "arbitrary")),
        name="moe_up",
    )(tile_expert, nvalid, xs, w_up, w_up)


def _moe_down_body(te_ref, nv_ref, x_ref, w_ref, o_ref, w_sc):
    t = pl.program_id(1)
    live = t < nv_ref[0]

    @pl.when(jnp.logical_and(live, _new_expert(te_ref, t)))
    def _():
        w_sc[...] = w_ref[...].astype(BF16)

    @pl.when(live)
    def _():
        o_ref[...] = _dot(x_ref[...], w_sc[...])

    @pl.when(jnp.logical_not(live))
    def _():
        o_ref[...] = jnp.zeros_like(o_ref)


def _moe_down(act, w_down, layer, tile_expert, nvalid, *, tn=1024):
    p, ff = act.shape
    tm = MOE_TM
    d = w_down.shape[-1]
    return pl.pallas_call(
        _moe_down_body,
        grid_spec=pltpu.PrefetchScalarGridSpec(
            num_scalar_prefetch=2,
            grid=(d // tn, p // tm),
            in_specs=[pl.BlockSpec((tm, ff), lambda n, t, te, nv: (t, 0)),
                      pl.BlockSpec((None, None, ff, tn), lambda n, t, te, nv: (layer, te[t], 0, n))],
            out_specs=pl.BlockSpec((tm, tn), lambda n, t, te, nv: (t, n)),
            scratch_shapes=[pltpu.VMEM((ff, tn), BF16)]),
        out_shape=jax.ShapeDtypeStruct((p, d), F32),
        compiler_params=_cparams(("arbitrary", "arbitrary")),
        name="moe_down",
    )(tile_expert, nvalid, act, w_down)


def _moe_combine_body(d1_ref, d2_ref, y_hbm, h_ref, info_ref, o_ref, buf, sems):
    t = pl.program_id(0)
    nt = pl.num_programs(0)
    tm = h_ref.shape[0]

    def rows(tile, slot, start):
        def body(r, carry):
            for k, dest in enumerate((d1_ref, d2_ref)):
                cp = _row_copy(y_hbm, dest[tile * tm + r], buf.at[2 * slot + k], r, sems.at[2 * slot + k])
                if start:
                    cp.start()
                else:
                    cp.wait()
            return carry

        lax.fori_loop(0, tm, body, 0)

    @pl.when(t == 0)
    def _():
        rows(0, 0, True)

    @pl.when(t + 1 < nt)
    def _():
        rows(t + 1, (t + 1) % 2, True)

    slot = t % 2
    rows(t, slot, False)
    info = info_ref[...]
    o_ref[...] = h_ref[...] + (info[:, RT_W1:RT_W1 + 1] * buf[2 * slot]
                               + info[:, RT_W2:RT_W2 + 1] * buf[2 * slot + 1])


def _moe_combine(h, y, info, dest1, dest2, tm=256):
    s, d = h.shape
    return pl.pallas_call(
        _moe_combine_body,
        grid_spec=pltpu.PrefetchScalarGridSpec(
            num_scalar_prefetch=2,
            grid=(s // tm,),
            in_specs=[pl.BlockSpec(memory_space=pl.ANY),
                      pl.BlockSpec((tm, d), lambda t, a, b: (t, 0)),
                      pl.BlockSpec((tm, LANES), lambda t, a, b: (t, 0))],
            out_specs=pl.BlockSpec((tm, d), lambda t, a, b: (t, 0)),
            scratch_shapes=[pltpu.VMEM((4, tm, d), F32), pltpu.SemaphoreType.DMA((4,))]),
        out_shape=jax.ShapeDtypeStruct((s, d), F32),
        compiler_params=_cparams(("arbitrary",)),
        name="moe_combine",
    )(dest1, dest2, y, h, info)


def _nsa_q_proj(hn, wq, layer, q_gain, nn, *, tm=512, tn=1024):
    m, kdim = hn.shape
    scale = HEAD_DIM ** -0.5 * LOG2E

    def epi(accs, e_refs, o_refs, n):
        acc = accs[0]
        gain = e_refs[0][...]
        for c in range(tn // HEAD_DIM):
            a = acc[:, c * HEAD_DIM:(c + 1) * HEAD_DIM]
            ms = jnp.mean(a * a, axis=-1, keepdims=True)
            o_refs[0][:, c * HEAD_DIM:(c + 1) * HEAD_DIM] = (
                a * lax.rsqrt(ms + RMS_EPS) * gain * scale).astype(BF16)

    return _matmul(
        hn, [(wq, _wspec(kdim, tn, 0, layer))], epi,
        [(jax.ShapeDtypeStruct((m, nn), BF16), _ospec(tm, tn))],
        tm=tm, tn=tn, n_total=nn,
        extras=[(q_gain.reshape(1, HEAD_DIM), pl.BlockSpec((1, HEAD_DIM), lambda n, mi, k: (0, 0)))],
        name="nsa_q_proj")[0]


def _nsa_kv_proj(hn, wkv, layer, k_gain, col0, *, tm=512):
    m, kdim = hn.shape
    g_ = NSA_KV_HEADS
    tn = g_ * HEAD_DIM
    spt = tm // LANES
    gains = jnp.ones((6, 1, HEAD_DIM), F32).at[2, 0].set(k_gain[1]).at[4, 0].set(k_gain[2])

    def epi(accs, e_refs, o_refs, n):
        acc = accs[0]
        kg = e_refs[0][0]
        do_norm = jnp.logical_or(n == 2, n == 4)
        for g in range(g_):
            a = acc[:, g * HEAD_DIM:(g + 1) * HEAD_DIM]
            ms = jnp.mean(a * a, axis=-1, keepdims=True)
            an = a * lax.rsqrt(ms + RMS_EPS) * kg
            o_refs[0][g] = jnp.where(do_norm, an, a).astype(BF16)
        is_v = jnp.logical_or(n == 3, n == 5)

        @pl.when(is_v)
        def _():
            for g in range(g_):
                for c in range(spt):
                    blk = acc[c * LANES:(c + 1) * LANES, g * HEAD_DIM:(g + 1) * HEAD_DIM]
                    o_refs[1][g, c] = jnp.transpose(blk).astype(BF16)

        @pl.when(jnp.logical_not(is_v))
        def _():
            o_refs[1][...] = jnp.zeros_like(o_refs[1])

    return _matmul(
        hn, [(wkv, _wspec(kdim, tn, col0 // tn, layer))], epi,
        [(jax.ShapeDtypeStruct((6 * g_, m, HEAD_DIM), BF16),
          pl.BlockSpec((g_, tm, HEAD_DIM), lambda n, mi, k: (n, mi, 0))),
         (jax.ShapeDtypeStruct((6, g_, m // LANES, HEAD_DIM, LANES), BF16),
          pl.BlockSpec((None, g_, spt, HEAD_DIM, LANES), lambda n, mi, k: (n, 0, mi, 0, 0)))],
        tm=tm, tn=tn, n_total=6 * tn,
        extras=[(gains, pl.BlockSpec((1, 1, HEAD_DIM), lambda n, mi, k: (n, 0, 0)))],
        name="nsa_kv_proj")


def _nsa_gate_proj(hn, wg, *, tm=512):
    m, kdim = hn.shape
    nn = wg.shape[1]

    def epi(accs, e_refs, o_refs, n):
        o_refs[0][...] = 1.0 / (1.0 + jnp.exp(-accs[0]))

    return _matmul(
        hn, [(wg, _wspec(kdim, nn))], epi,
        [(jax.ShapeDtypeStruct((m, nn), F32), _ospec(tm, nn))],
        tm=tm, tn=nn, n_total=nn, name="nsa_gate_proj")[0]


def _compress_body(r_ref, pe_ref, w1_ref, w2_ref, kg_ref, o_ref, ot_ref):
    j = pl.program_id(0)
    r = r_ref[0].astype(F32)
    pe = pe_ref[0]
    half = r.shape[1]
    top = (r + pe[0:1]).astype(BF16)
    bot = (r + pe[1:2]).astype(BF16)
    a = _dot(top, w1_ref[0, 0:half, :])
    b = _dot(bot, w1_ref[0, half:2 * half, :])
    nrow = r.shape[0]
    hid = a + pltpu.roll(b, nrow - 1, 0)
    y = _dot(jax.nn.gelu(hid).astype(BF16), w2_ref[0])
    ms = jnp.mean(y * y, axis=-1, keepdims=True)
    yn = y * lax.rsqrt(ms + RMS_EPS) * kg_ref[...]
    out = jnp.where(j == 0, yn, y)
    o_ref[0, 0] = out.astype(BF16)
    ot_ref[0, 0] = jnp.transpose(out).astype(BF16)


def _compress(kv, pe, w1, w2, k_gain0):
    g_ = NSA_KV_HEADS
    s = kv.shape[1]
    nrow = s // CMP_STRIDE
    wid = CMP_STRIDE * HEAD_DIM
    r = kv[:2 * g_].reshape(2 * g_, nrow, wid)
    pe2 = pe.reshape(2, 2, wid)
    return pl.pallas_call(
        _compress_body,
        grid=(2, g_),
        in_specs=[pl.BlockSpec((1, nrow, wid), lambda j, g: (j * g_ + g, 0, 0)),
                  pl.BlockSpec((1, 2, wid), lambda j, g: (j, 0, 0)),
                  pl.BlockSpec((1, 2 * wid, HEAD_DIM), lambda j, g: (j, 0, 0)),
                  pl.BlockSpec((1, HEAD_DIM, HEAD_DIM), lambda j, g: (j, 0, 0)),
                  pl.BlockSpec((1, HEAD_DIM), lambda j, g: (0, 0))],
        out_specs=[pl.BlockSpec((1, 1, nrow, HEAD_DIM), lambda j, g: (j, g, 0, 0)),
                   pl.BlockSpec((1, 1, HEAD_DIM, nrow), lambda j, g: (j, g, 0, 0))],
        out_shape=[jax.ShapeDtypeStruct((2, g_, nrow, HEAD_DIM), BF16),
                   jax.ShapeDtypeStruct((2, g_, HEAD_DIM, nrow), BF16)],
        compiler_params=_cparams(("parallel", "parallel")),
        name="nsa_compress",
    )(r, pe2, w1.astype(BF16), w2.astype(BF16), k_gain0.reshape(1, HEAD_DIM))


def _t5_bucket(dist):
    dist = jnp.maximum(dist, 0)
    max_exact = T5_BUCKETS // 2
    d_f = jnp.maximum(dist, 1).astype(F32)
    log_b = max_exact + (jnp.log(d_f / max_exact) / math.log(T5_MAX_DISTANCE / max_exact)
                         * (T5_BUCKETS - max_exact)).astype(jnp.int32)
    log_b = jnp.minimum(log_b, T5_BUCKETS - 1)
    return jnp.where(dist < max_exact, dist, log_b)


def _bias_table(tbl, dist, valid, shift):
    onehot = jax.nn.one_hot(_t5_bucket(dist).reshape(-1), T5_BUCKETS, dtype=F32)
    t = tbl - tbl[T5_BUCKETS - 1:T5_BUCKETS] if shift else tbl
    vals = jnp.einsum("nb,bh->hn", onehot, t, precision=lax.Precision.HIGHEST)
    vals = vals.reshape((tbl.shape[1],) + dist.shape) * LOG2E
    return jnp.where(valid[None], vals, NEG)


def _nsa_tables(rel_bias):
    tbl = rel_bias.astype(F32)
    r = jnp.arange(Q_BLOCK)
    jw = jnp.arange(WINDOW + Q_BLOCK)
    dw = WINDOW + r[None, :] - jw[:, None]
    tab_w = _bias_table(tbl, dw, (dw >= 0) & (dw < WINDOW), False)
    a = jnp.arange(16)
    cc = jnp.arange(2 * LANES)
    dc = (CMP_NEAR_SHIFT * Q_BLOCK + Q_BLOCK * a[:, None, None] + r[None, None, :]
          - CMP_STRIDE * cc[None, :, None] - (CMP_BLOCK - 1))
    tab_c = _bias_table(tbl, dc, dc >= 0, True)
    jj = jnp.arange(SEL_NEAR_TILES)
    col = jnp.arange(SEL_TILE)
    ds_ = r[None, None, :] - Q_BLOCK + SEL_TILE * (jj[:, None, None] + 1) - col[None, :, None]
    tab_s = _bias_table(tbl, ds_, ds_ >= 0, True)
    return tab_w, tab_c, tab_s, tbl


def _sel_to_cmp(nsel, nc_pad):
    c_start = jnp.arange(nc_pad) * CMP_STRIDE
    s_start = jnp.arange(nsel) * SEL_BLOCK
    overlap = jnp.clip(jnp.minimum(c_start[None, :] + CMP_BLOCK, s_start[:, None] + SEL_BLOCK)
                       - jnp.maximum(c_start[None, :], s_start[:, None]), 0)
    return (overlap.astype(F32) / CMP_BLOCK).astype(BF16)


def _heads_t(q_ref, h0, nh):
    cols = [jnp.transpose(q_ref[:, (h0 + h) * HEAD_DIM:(h0 + h + 1) * HEAD_DIM].astype(F32)) for h in range(nh)]
    return jnp.concatenate(cols, axis=1).astype(BF16)


def _softmax_keys(s, bounded):
    if bounded:
        p = jnp.exp2(s)
        l = jnp.sum(p, axis=0, keepdims=True)
        return p, jnp.where(l > 0.0, 1.0 / l, 0.0)
    m = jnp.max(s, axis=0, keepdims=True)
    p = jnp.exp2(s - m)
    l = jnp.sum(p, axis=0, keepdims=True)
    return p, jnp.where(m > 0.5 * NEG, 1.0 / l, 0.0)


def _logit_bound(rel_bias, q_gain, k_gain):
    qk = HEAD_DIM * (HEAD_DIM ** -0.5 * LOG2E) * jnp.max(jnp.abs(q_gain)) * jnp.max(jnp.abs(k_gain))
    return (1.02 * qk + 2.0 * LOG2E * jnp.max(jnp.abs(rel_bias))).astype(F32).reshape(1)


def _nsa_cw_body(bnd_ref, q_ref, gt_ref, kc_ref, vct_ref, mselt_ref, tabc_ref, kw_ref, vwt_ref, tabw_ref,
                 ocwt_ref, negt_ref):
    bounded = bnd_ref[0] < SAFE_LOGIT
    i = pl.program_id(1)

    @pl.when(bounded)
    def _():
        _nsa_cw_branches(True, i, q_ref, gt_ref, kc_ref, vct_ref, mselt_ref, tabc_ref, kw_ref, vwt_ref, tabw_ref,
                         ocwt_ref, negt_ref)

    @pl.when(jnp.logical_not(bounded))
    def _():
        _nsa_cw_branches(False, i, q_ref, gt_ref, kc_ref, vct_ref, mselt_ref, tabc_ref, kw_ref, vwt_ref, tabw_ref,
                         ocwt_ref, negt_ref)


def _nsa_cw_branches(bounded, i, q_ref, gt_ref, kc_ref, vct_ref, mselt_ref, tabc_ref, kw_ref, vwt_ref, tabw_ref,
                     ocwt_ref, negt_ref):
    f = (i + 16 - CMP_NEAR_SHIFT) // 16 - 1
    kc = kc_ref[0, 0]
    vct = vct_ref[0, 0]
    nc = kc.shape[0]
    nsel = mselt_ref.shape[0]
    wlen = WINDOW + Q_BLOCK
    nslab = wlen // LANES
    start = pl.multiple_of(i * Q_BLOCK, Q_BLOCK)
    kw = kw_ref[0, pl.ds(start, wlen), :]
    vwt = jnp.concatenate([vwt_ref[0, jnp.maximum(i + c - WINDOW // LANES, 0)] for c in range(nslab)],
                          axis=1)
    roww = lax.broadcasted_iota(jnp.int32, (wlen, LANES), 0)
    w_pad = roww < (WINDOW - Q_BLOCK * i)
    gt_t = jnp.transpose(gt_ref[...])
    psum = jnp.zeros((nc, Q_BLOCK), F32)
    hg = CW_HEADS
    for h0 in range(0, NSA_GROUP, hg):
        qt = _heads_t(q_ref, h0, hg)
        s = _dot(kc, qt)
        ta = jnp.concatenate([tabc_ref[h0 + h, 0, 0:LANES, :] for h in range(hg)], axis=1)
        tb = jnp.concatenate([tabc_ref[h0 + h, 0, LANES:2 * LANES, :] for h in range(hg)], axis=1)
        pieces = []
        for ch in range(nc // LANES):
            rest = jnp.where(ch > f + 1, NEG, 0.0)
            bias = jnp.where(ch == f, ta, jnp.where(ch == f + 1, tb, rest))
            pieces.append(s[ch * LANES:(ch + 1) * LANES] + bias)
        s = jnp.concatenate(pieces, axis=0)
        p, linv = _softmax_keys(s, bounded)
        pn = p * linv
        for h in range(hg):
            psum = psum + pn[:, h * Q_BLOCK:(h + 1) * Q_BLOCK]
        oc = _dot(vct, pn.astype(BF16))
        tw = jnp.concatenate([jnp.where(w_pad, NEG, tabw_ref[h0 + h]) for h in range(hg)], axis=1)
        sw = _dot(kw, qt) + tw
        pw, lwinv = _softmax_keys(sw, bounded)
        ow = _dot(vwt, pw.astype(BF16)) * lwinv
        for h in range(hg):
            hh = h0 + h
            cs = slice(h * Q_BLOCK, (h + 1) * Q_BLOCK)
            ocwt_ref[0, 0, hh * HEAD_DIM:(hh + 1) * HEAD_DIM, :] = (
                gt_t[hh:hh + 1] * oc[:, cs] + gt_t[16 + hh:17 + hh] * ow[:, cs])
    p_hi = psum.astype(BF16)
    p_lo = (psum - p_hi.astype(F32)).astype(BF16)
    mselt = mselt_ref[...]
    imp = _dot(mselt, p_hi) + _dot(mselt, p_lo)
    blk = lax.broadcasted_iota(jnp.int32, (nsel, Q_BLOCK), 0).astype(F32)
    qpos = lax.broadcasted_iota(jnp.int32, (nsel, Q_BLOCK), 1)
    cur = (i * (Q_BLOCK // SEL_BLOCK)).astype(F32) + jnp.where(qpos >= SEL_BLOCK, 1.0, 0.0)
    forced = jnp.logical_or(blk == cur, blk == 0.0)
    val = jnp.where(forced, FORCED_SCORE, jnp.where(blk <= cur, imp, -1.0))
    for _ in range(SEL_TOPK):
        mx = jnp.max(val, axis=0, keepdims=True)
        first = jnp.min(jnp.where(val == mx, blk, 1e4), axis=0, keepdims=True)
        val = jnp.where(blk == first, TAKEN, val)
    neg = jnp.where(val == TAKEN, 0.0, NEG)
    for pr in range(SEL_PAD_BLOCKS // 2):
        negt_ref[0, 0, pr] = jnp.full((2, Q_BLOCK), NEG, F32)
    for pr in range(nsel // 2):
        negt_ref[0, 0, SEL_PAD_BLOCKS // 2 + pr] = neg[2 * pr:2 * pr + 2]


def _nsa_cw(bound, q, gates, kc, vct, mselt, tab_c, kw, vt, tab_w):
    s, hd = q.shape
    g_ = NSA_KV_HEADS
    nq = s // Q_BLOCK
    gw = NSA_GROUP * HEAD_DIM
    nc = kc.shape[2]
    nsel = mselt.shape[0]
    spad = kw.shape[1]
    nslab = vt.shape[2]
    wlen = WINDOW + Q_BLOCK

    def var(i):
        return (i + 16 - CMP_NEAR_SHIFT) % 16

    return pl.pallas_call(
        _nsa_cw_body,
        grid=(g_, nq),
        in_specs=[
            pl.BlockSpec(memory_space=pltpu.SMEM),
            pl.BlockSpec((Q_BLOCK, gw), lambda g, i: (i, g)),
            pl.BlockSpec((Q_BLOCK, LANES), lambda g, i: (i, g)),
            pl.BlockSpec((1, 1, nc, HEAD_DIM), lambda g, i: (0, g, 0, 0)),
            pl.BlockSpec((1, 1, HEAD_DIM, nc), lambda g, i: (1, g, 0, 0)),
            pl.BlockSpec((nsel, nc), lambda g, i: (0, 0)),
            pl.BlockSpec((NSA_GROUP, 1, 2 * LANES, Q_BLOCK), lambda g, i: (g, var(i), 0, 0)),
            pl.BlockSpec((1, spad, HEAD_DIM), lambda g, i: (g, 0, 0)),
            pl.BlockSpec((None, 1, nslab, HEAD_DIM, LANES), lambda g, i: (5, g, 0, 0, 0)),
            pl.BlockSpec((NSA_GROUP, wlen, Q_BLOCK), lambda g, i: (g, 0, 0)),
        ],
        out_specs=[pl.BlockSpec((1, 1, gw, Q_BLOCK), lambda g, i: (g, i, 0, 0)),
                   pl.BlockSpec((1, 1, (SEL_PAD_BLOCKS + nsel) // 2, 2, Q_BLOCK), lambda g, i: (g, i, 0, 0, 0))],
        out_shape=[jax.ShapeDtypeStruct((g_, nq, gw, Q_BLOCK), F32),
                   jax.ShapeDtypeStruct((g_, nq, (SEL_PAD_BLOCKS + nsel) // 2, 2, Q_BLOCK), F32)],
        compiler_params=_cparams(("parallel", "arbitrary")),
        name="nsa_cmp_win",
    )(bound, q, gates, kc, vct, mselt, tab_c, kw, vt, tab_w)


def _nsa_sel_body(bnd_ref, q_ref, gt_ref, ks_ref, vst_ref, negp_ref, tabs_ref, ocwt_ref, o_ref,
                  m_sc, l_sc, acc_sc):
    i = pl.program_id(1)
    ntile = i // (SEL_TILE // Q_BLOCK) + 1
    nh = NSA_GROUP
    spt = SEL_TILE // LANES
    ppt = SEL_TILE // (2 * SEL_BLOCK)
    qt = _heads_t(q_ref, 0, nh)

    def tile(jj, near, bounded, nsub=1):
        slab0 = (i + 1) - spt * jj
        row0 = pl.multiple_of(slab0 * LANES, LANES)
        kt = ks_ref[0, pl.ds(row0, nsub * SEL_TILE), :]
        vtt = jnp.concatenate([vst_ref[0, jnp.maximum(slab0 - spt + c, 0)] for c in range(nsub * spt)],
                              axis=1)
        rows = []
        for c in range(nsub * ppt):
            pair = negp_ref[0, 0, slab0 + c]
            rows += [jnp.broadcast_to(pair[r:r + 1], (SEL_BLOCK, Q_BLOCK)) for r in range(2)]
        mk = jnp.concatenate(rows, axis=0)
        if near:
            bias = jnp.concatenate(
                [jnp.concatenate([tabs_ref[h, jj - c] for c in range(nsub)], axis=0) + mk for h in range(nh)],
                axis=1)
        else:
            bias = jnp.tile(mk, (1, nh))
        sc = _dot(kt, qt) + bias
        if bounded:
            p = jnp.exp2(sc)
            l_sc[...] = l_sc[...] + jnp.sum(p, axis=0, keepdims=True)
            acc_sc[...] = acc_sc[...] + _dot(vtt, p.astype(BF16))
        else:
            m_old = m_sc[...]
            m_new = jnp.maximum(m_old, jnp.max(sc, axis=0, keepdims=True))
            alpha = jnp.exp2(m_old - m_new)
            p = jnp.exp2(sc - m_new)
            l_sc[...] = alpha * l_sc[...] + jnp.sum(p, axis=0, keepdims=True)
            acc_sc[...] = alpha * acc_sc[...] + _dot(vtt, p.astype(BF16))
            m_sc[...] = m_new

    def run(bounded):
        m_sc[...] = jnp.full(m_sc.shape, NEG, F32)
        l_sc[...] = jnp.zeros(l_sc.shape, F32)
        acc_sc[...] = jnp.zeros(acc_sc.shape, F32)
        @pl.when(ntile >= SEL_NEAR_TILES)
        def _():
            for jj in range(SEL_FAR_WIDTH - 1, SEL_NEAR_TILES, SEL_FAR_WIDTH):
                tile(jj, True, bounded, SEL_FAR_WIDTH)
            for jj in range(SEL_NEAR_TILES - SEL_NEAR_TILES % SEL_FAR_WIDTH, SEL_NEAR_TILES):
                tile(jj, True, bounded)

        @pl.when(ntile < SEL_NEAR_TILES)
        def _():
            for jj in range(SEL_NEAR_TILES - 1):
                @pl.when(jj < ntile)
                def _():
                    tile(jj, True, bounded)

        nfar = jnp.maximum(ntile - SEL_NEAR_TILES, 0)

        def far(k, carry):
            tile(SEL_NEAR_TILES + SEL_FAR_WIDTH * k + SEL_FAR_WIDTH - 1, False, bounded, SEL_FAR_WIDTH)
            return carry

        lax.fori_loop(0, nfar // SEL_FAR_WIDTH, far, 0)

        def far_rest(jj, carry):
            tile(jj, False, bounded)
            return carry

        lax.fori_loop(SEL_NEAR_TILES + nfar // SEL_FAR_WIDTH * SEL_FAR_WIDTH, ntile, far_rest, 0)

    bounded = bnd_ref[0] < SAFE_LOGIT

    @pl.when(bounded)
    def _():
        run(True)

    @pl.when(jnp.logical_not(bounded))
    def _():
        run(False)

    o_t = acc_sc[...] * (1.0 / l_sc[...])
    gt_t = jnp.transpose(gt_ref[...])
    for h in range(nh):
        hs = slice(h * HEAD_DIM, (h + 1) * HEAD_DIM)
        oh = ocwt_ref[0, 0, hs, :] + gt_t[8 + h:9 + h] * o_t[:, h * Q_BLOCK:(h + 1) * Q_BLOCK]
        o_ref[:, hs] = jnp.transpose(oh).astype(BF16)


def _nsa_sel(bound, q, gates, ks, vt, negp, tab_s, ocwt):
    s, hd = q.shape
    g_ = NSA_KV_HEADS
    nq = s // Q_BLOCK
    gw = NSA_GROUP * HEAD_DIM
    spad = ks.shape[1]
    nslab = vt.shape[2]
    npair = negp.shape[2]
    return pl.pallas_call(
        _nsa_sel_body,
        grid=(g_, nq),
        in_specs=[
            pl.BlockSpec(memory_space=pltpu.SMEM),
            pl.BlockSpec((Q_BLOCK, gw), lambda g, i: (i, g)),
            pl.BlockSpec((Q_BLOCK, LANES), lambda g, i: (i, g)),
            pl.BlockSpec((1, spad, HEAD_DIM), lambda g, i: (g, 0, 0)),
            pl.BlockSpec((None, 1, nslab, HEAD_DIM, LANES), lambda g, i: (3, g, 0, 0, 0)),
            pl.BlockSpec((1, 1, npair, 2, Q_BLOCK), lambda g, i: (g, i, 0, 0, 0)),
            pl.BlockSpec((NSA_GROUP, SEL_NEAR_TILES, SEL_TILE, Q_BLOCK), lambda g, i: (g, 0, 0, 0)),
            pl.BlockSpec((1, 1, gw, Q_BLOCK), lambda g, i: (g, i, 0, 0)),
        ],
        out_specs=pl.BlockSpec((Q_BLOCK, gw), lambda g, i: (i, g)),
        out_shape=jax.ShapeDtypeStruct((s, hd), BF16),
        scratch_shapes=[pltpu.VMEM((1, NSA_GROUP * Q_BLOCK), F32), pltpu.VMEM((1, NSA_GROUP * Q_BLOCK), F32),
                        pltpu.VMEM((HEAD_DIM, NSA_GROUP * Q_BLOCK), F32)],
        compiler_params=_cparams(("parallel", "arbitrary")),
        name="nsa_selected",
    )(bound, q, gates, ks, vt, negp, tab_s, ocwt)


def _pad_keys(x, pad):
    return jnp.pad(x, ((0, 0), (pad, 0), (0, 0)))


def _nsa_mixer(h, hn, tables, layer, w_in, w_gate, cmp_pe, cmp_w1, cmp_w2, q_gain, k_gain, w_out):
    s, d = h.shape
    g_, hpg, dh = NSA_KV_HEADS, NSA_GROUP, HEAD_DIM
    hd = g_ * hpg * dh
    tab_w, tab_c, tab_s, tbl = tables
    q = _nsa_q_proj(hn, w_in, layer, q_gain, hd)
    kv, vt = _nsa_kv_proj(hn, w_in, layer, k_gain, hd)
    wg = w_gate.reshape(d, g_, hpg, 3).transpose(0, 1, 3, 2).reshape(d, g_, 3 * hpg)
    wg = jnp.pad(wg, ((0, 0), (0, 0), (0, LANES - 3 * hpg))).reshape(d, g_ * LANES)
    gates = _nsa_gate_proj(hn, wg)
    kc, kct = _compress(kv, cmp_pe, cmp_w1, cmp_w2, k_gain[0])
    mselt = _sel_to_cmp(s // SEL_BLOCK, s // CMP_STRIDE)
    ks = _pad_keys(kv[2 * g_:3 * g_], SEL_TILE)
    kw = _pad_keys(kv[4 * g_:5 * g_], WINDOW)
    bound = _logit_bound(tbl, q_gain, k_gain)
    ocwt, negp = _nsa_cw(bound, q, gates, kc, kct, mselt, tab_c, kw, vt, tab_w)
    o = _nsa_sel(bound, q, gates, ks, vt, negp, tab_s, ocwt)
    return _matmul_resid(o, w_out, h, layer=layer, name="nsa_out")


def _gelu_proj(hn, w, layer, *, tm=512, tn=1024):
    m, kdim = hn.shape
    nn = w.shape[-1]

    def epi(accs, e_refs, o_refs, n):
        o_refs[0][...] = jax.nn.gelu(accs[0]).astype(BF16)

    return _matmul(
        hn, [(w, _wspec(kdim, tn, 0, layer))], epi,
        [(jax.ShapeDtypeStruct((m, nn), BF16), _ospec(tm, tn))],
        tm=tm, tn=tn, n_total=nn, name="sgu_in")[0]


def _sgu_mix_body(u_ref, v_ref, gain_ref, w_ref, bt_ref, o_ref):
    vn = _norm_rows(v_ref[...].astype(F32), gain_ref[...])
    t = w_ref.shape[1]
    causal = (lax.broadcasted_iota(jnp.int32, (t, t), 0) >= lax.broadcasted_iota(jnp.int32, (t, t), 1))
    bt = bt_ref[...]
    gd = vn.shape[1] // SG_GROUPS
    for g in range(SG_GROUPS):
        gs = slice(g * gd, (g + 1) * gd)
        w = jnp.where(causal, w_ref[g], 0.0).astype(BF16)
        mixed = _dot(w, vn[:, gs]) + bt[:, g:g + 1]
        o_ref[:, gs] = (u_ref[:, gs].astype(F32) * mixed).astype(BF16)


def _sgu_mix(uv, v_gain, w_s, b_s):
    s = uv.shape[0]
    wd = uv.shape[1] // 2
    t = SG_CHUNK
    return pl.pallas_call(
        _sgu_mix_body,
        grid=(s // t,),
        in_specs=[pl.BlockSpec((t, wd), lambda c: (c, 0)),
                  pl.BlockSpec((t, wd), lambda c: (c, 1)),
                  pl.BlockSpec((1, wd), lambda c: (0, 0)),
                  pl.BlockSpec((SG_GROUPS, t, t), lambda c: (0, 0, 0)),
                  pl.BlockSpec((t, SG_GROUPS), lambda c: (0, 0))],
        out_specs=pl.BlockSpec((t, wd), lambda c: (c, 0)),
        out_shape=jax.ShapeDtypeStruct((s, wd), BF16),
        compiler_params=_cparams(("parallel",)),
        name="sgu_mix",
    )(uv, uv, v_gain.reshape(1, wd), w_s, b_s.T)


def _sgu_mixer(h, hn, layer, w_in, v_gain, w_s, b_s, w_out):
    uv = _gelu_proj(hn, w_in, layer)
    y = _sgu_mix(uv, v_gain, w_s, b_s)
    return _matmul_resid(y, w_out, h, layer=layer, name="sgu_out")


def _gla_body(q_ref, k_ref, v_ref, r_ref, g1_ref, wg_ref, bg_ref, og_ref, o_ref, state_sc, diag_sc):
    @pl.when(pl.program_id(0) == 0)
    def _():
        state_sc[...] = jnp.zeros_like(state_sc)

    nh = state_sc.shape[0]
    dk = q_ref.shape[1] // nh
    dv = v_ref.shape[1] // nh
    g1 = g1_ref[...].astype(BF16)
    for hh in range(nh):
        ks = slice(hh * dk, (hh + 1) * dk)
        vs = slice(hh * dv, (hh + 1) * dv)
        o_ref[:, vs] = _gla_head(q_ref[:, ks], k_ref[:, ks], v_ref[:, vs], r_ref[:, vs], g1, wg_ref[:, ks],
                                 bg_ref[:, ks], og_ref[...], state_sc.at[hh], diag_sc.at[hh])


def _gla_head(q_in, k_in, v, r_in, g1, wg, bg, og, state_sc, diag_sc):
    ch, dk = q_in.shape
    x = _dot(g1, wg) + bg
    log_a = (jnp.minimum(x, 0.0) - jnp.log(1.0 + jnp.exp(-jnp.abs(x)))) * (1.0 / GLA_GATE_TEMP)
    tri = (lax.broadcasted_iota(jnp.int32, (ch, ch), 0)
           >= lax.broadcasted_iota(jnp.int32, (ch, ch), 1))
    tri_b = jnp.where(tri, 1.0, 0.0).astype(BF16)
    a_hi = log_a.astype(BF16)
    a_lo = (log_a - a_hi.astype(F32)).astype(BF16)
    b = _dot(tri_b, a_hi) + _dot(tri_b, a_lo)
    q = q_in.astype(F32) * (dk ** -0.5)
    k = k_in.astype(F32)
    state = state_sc[...]
    o = _dot((q * jnp.exp(b)).astype(BF16), state.astype(BF16))
    nsub = ch // GLA_SUB
    ends = [jnp.broadcast_to(b[(jb + 1) * GLA_SUB - 1:(jb + 1) * GLA_SUB], (GLA_SUB, dk))
            for jb in range(nsub)]
    b_end = jnp.concatenate(ends, axis=0)
    b_start = jnp.concatenate([jnp.zeros((GLA_SUB, dk), F32)] + ends[:-1], axis=0)
    rblk = lax.broadcasted_iota(jnp.int32, (ch, ch), 0) // GLA_SUB
    cblk = lax.broadcasted_iota(jnp.int32, (ch, ch), 1) // GLA_SUB
    zero_row = jnp.zeros((1, dk), F32)
    spread = jnp.max(jnp.concatenate(
        [(ends[jb - 1][0:1] if jb else zero_row) - ends[jb][0:1] for jb in range(nsub)], axis=0))

    @pl.when(spread < GLA_SAFE_EXP)
    def _():
        q_diag = (q * jnp.exp(b - b_start)).astype(BF16)
        k_diag = (k * jnp.exp(b_start - b)).astype(BF16)
        diag_sc[...] = _dot_nt(q_diag, k_diag)

    @pl.when(spread >= GLA_SAFE_EXP)
    def _():
        lane = lax.broadcasted_iota(jnp.int32, (GLA_SUB, ch), 1)
        strips = []
        for ib in range(nsub):
            rs = slice(ib * GLA_SUB, (ib + 1) * GLA_SUB)
            qi, ki, bi = q[rs], k[rs], b[rs]
            strip = jnp.zeros((GLA_SUB, ch), F32)
            for j in range(GLA_SUB):
                decay = jnp.exp(jnp.minimum(bi - bi[j:j + 1], 0.0))
                col = jnp.sum(qi * ki[j:j + 1] * decay, axis=-1, keepdims=True)
                strip = jnp.where(lane == ib * GLA_SUB + j, col, strip)
            strips.append(strip)
        diag_sc[...] = jnp.concatenate(strips, axis=0)

    attn = jnp.where(jnp.logical_and(tri, rblk == cblk), diag_sc[...], 0.0)
    k_hat = (k * jnp.exp(b_end - b)).astype(BF16)
    for jb in range(nsub - 1):
        q_hat = (q * jnp.exp(jnp.minimum(b - ends[jb][0:1], 0.0))).astype(BF16)
        attn = attn + jnp.where(jnp.logical_and(cblk == jb, rblk > jb), _dot_nt(q_hat, k_hat), 0.0)
    o = o + _dot(attn.astype(BF16), v)
    b_last = b[ch - 1:ch]
    k_dec = k * jnp.exp(b_last - b)
    k_dec_t = jnp.transpose(k_dec).astype(BF16)
    decay_t = jnp.transpose(jnp.broadcast_to(jnp.exp(b_last), (LANES, dk)))
    state_sc[...] = decay_t[:, 0:1] * state + _dot(k_dec_t, v)
    ms = jnp.mean(o * o, axis=-1, keepdims=True)
    on = o * lax.rsqrt(ms + RMS_EPS) * og
    return (on * _silu(r_in.astype(F32))).astype(BF16)


def _gla_mixer(h, hn, layer, w_in, w_gate1, w_gate2, b_gate, o_gain, w_out):
    s, d = h.shape
    nh = GLA_HEADS
    dk = d // 2
    dv = d
    dkh, dvh = dk // nh, dv // nh
    nmain = 2 * dk + 2 * dv

    def epi_bf16(accs, e_refs, o_refs, n):
        o_refs[0][...] = accs[0].astype(BF16)

    def epi_f32(accs, e_refs, o_refs, n):
        o_refs[0][...] = accs[0]

    tm, tn = 512, 1024
    proj = _matmul(hn, [(w_in, _wspec(d, tn, 0, layer))], epi_bf16,
                   [(jax.ShapeDtypeStruct((s, nmain), BF16), _ospec(tm, tn))],
                   tm=tm, tn=tn, n_total=nmain, name="gla_in")[0]
    wg1 = jnp.pad(w_gate1, ((0, 0), (0, LANES - GLA_GATE_RANK)))
    g1 = _matmul(hn, [(wg1, _wspec(d, LANES))], epi_f32,
                 [(jax.ShapeDtypeStruct((s, LANES), F32), _ospec(tm, LANES))],
                 tm=tm, tn=LANES, n_total=LANES, name="gla_gate_in")[0]
    wg2 = jnp.pad(w_gate2.astype(BF16), ((0, LANES - GLA_GATE_RANK), (0, 0)))
    ch = GLA_CHUNK
    o = pl.pallas_call(
        _gla_body,
        grid=(s // ch,),
        in_specs=[
            pl.BlockSpec((ch, dk), lambda c: (c, 0)),
            pl.BlockSpec((ch, dk), lambda c: (c, 1)),
            pl.BlockSpec((ch, dv), lambda c: (c, 2 * dk // dv)),
            pl.BlockSpec((ch, dv), lambda c: (c, 2 * dk // dv + 1)),
            pl.BlockSpec((ch, LANES), lambda c: (c, 0)),
            pl.BlockSpec((LANES, dk), lambda c: (0, 0)),
            pl.BlockSpec((1, dk), lambda c: (0, 0)),
            pl.BlockSpec((1, dvh), lambda c: (0, 0)),
        ],
        out_specs=pl.BlockSpec((ch, dv), lambda c: (c, 0)),
        out_shape=jax.ShapeDtypeStruct((s, dv), BF16),
        scratch_shapes=[pltpu.VMEM((nh, dkh, dvh), F32), pltpu.VMEM((nh, ch, ch), F32)],
        compiler_params=_cparams(("arbitrary",)),
        name="gla_scan",
    )(proj, proj, proj, proj, g1, wg2, b_gate.reshape(1, dk), o_gain.reshape(1, dvh))
    return _matmul_resid(o, w_out, h, layer=layer, name="gla_out")


def _dense_ffn(h, hn, layer, w_up, w_down):
    act = _swiglu_up(hn, w_up, layer=layer)
    return _matmul_resid(act, w_down, h, layer=layer, tn=512, tk=w_down.shape[1], name="ffn_down")


def _moe_ffn(h, gain, router, layer, w_up, w_down):
    s, _ = h.shape
    info, cnt = _moe_router(h, gain, router)
    dest1, dest2, src, tile_expert, nvalid = _moe_plan(info, cnt, s)
    xs = _moe_gather(h, gain, src, nvalid)
    act = _moe_up(xs, w_up, layer, tile_expert, nvalid)
    y = _moe_down(act, w_down, layer, tile_expert, nvalid)
    return _moe_combine(h, y, info, dest1, dest2)


def _lane_pad(w):
    extra = -w.shape[-1] % LANES
    return jnp.pad(w, [(0, 0)] * (w.ndim - 1) + [(0, extra)]) if extra else w


def kernel(x, rel_bias, norm_gain, nsa_w_in, nsa_cmp_pe, nsa_cmp_w1, nsa_cmp_w2, nsa_q_gain, nsa_k_gain, nsa_w_out, sg_w_in, sg_v_gain, sg_w_s, sg_b_s, sg_w_out, gla_w_in, gla_w_gate2, gla_b_gate, gla_o_gain, gla_w_out, ffn_w_up, ffn_w_down, moe_router, moe_w_up, moe_w_down):
    bsz, s, d = x.shape
    tables = _nsa_tables(rel_bias)
    nsa_main = (NSA_KV_HEADS * NSA_GROUP + 6 * NSA_KV_HEADS) * HEAD_DIM
    gla_main = 3 * d
    nsa_in, nsa_out = _lane_pad(nsa_w_in).astype(BF16), nsa_w_out.astype(BF16)
    sg_in, sg_out = sg_w_in.astype(BF16), sg_w_out.astype(BF16)
    gla_in, gla_out = _lane_pad(gla_w_in).astype(BF16), gla_w_out.astype(BF16)
    ffn_up, ffn_down = ffn_w_up.astype(BF16), ffn_w_down.astype(BF16)
    outs = []
    for bi in range(bsz):
        h = x.reshape(s, d) if bsz == 1 else x[bi]
        for i in range(DEPTH):
            mixer = i % N_MIXERS
            j = i // N_MIXERS
            hn = _rmsnorm(h, norm_gain[i, 0])
            if mixer == 0:
                h = _nsa_mixer(h, hn, tables, j, nsa_in, nsa_w_in[j, :, nsa_main:], nsa_cmp_pe[j], nsa_cmp_w1[j],
                               nsa_cmp_w2[j],
                               nsa_q_gain[j], nsa_k_gain[j], nsa_out)
            elif mixer == 1:
                h = _sgu_mixer(h, hn, j, sg_in, sg_v_gain[j], sg_w_s[j], sg_b_s[j], sg_out)
            else:
                h = _gla_mixer(h, hn, j, gla_in, gla_w_in[j, :, gla_main:], gla_w_gate2[j], gla_b_gate[j],
                               gla_o_gain[j], gla_out)
            f = i // 2
            if i % 2 == 0:
                hn = _rmsnorm(h, norm_gain[i, 1])
                h = _dense_ffn(h, hn, f, ffn_up, ffn_down)
            else:
                h = _moe_ffn(h, norm_gain[i, 1], moe_router[f], f, moe_w_up, moe_w_down)
        outs.append(h)
    return outs[0].reshape(1, s, d) if bsz == 1 else jnp.stack(outs, axis=0)
```

```python
import math

import jax
import jax.numpy as jnp
from jax import lax
from jax.experimental import pallas as pl
from jax.experimental.pallas import tpu as pltpu

F32 = jnp.float32
BF16 = jnp.bfloat16

DEPTH = 4
N_MIXERS = 3
RMS_EPS = 1e-6
NEG = -1e30
HEAD_DIM = 128
NSA_KV_HEADS = 4
NSA_GROUP = 8
CMP_BLOCK = 32
CMP_STRIDE = 16
SEL_BLOCK = 64
SEL_TOPK = 16
WINDOW = 512
Q_BLOCK = 128
FORCED_SCORE = 1e4
TAKEN = -2.0
T5_BUCKETS = 32
T5_MAX_DISTANCE = 2048
SG_CHUNK = 128
SG_GROUPS = 32
GLA_HEADS = 4
GLA_GATE_RANK = 16
GLA_GATE_TEMP = 16.0
GLA_CHUNK = 64
GLA_SUB = 16
GLA_SAFE_EXP = 80.0
MOE_EXPERTS = 8
MOE_TM = 512

LANES = 128
SEL_TILE = 512
SEL_NEAR_TILES = 5
SEL_FAR_WIDTH = 2
CMP_NEAR_SHIFT = 12
CW_HEADS = 8
VMEM_MB = 56
LOG2E = 1.4426950408889634
SAFE_LOGIT = 60.0
SEL_PAD_BLOCKS = 8


def _cparams(sem, vmem_mb=VMEM_MB):
    return pltpu.CompilerParams(dimension_semantics=sem, vmem_limit_bytes=vmem_mb * 2**20)


def _dot(a, b):
    return jnp.dot(a, b, preferred_element_type=F32)


def _dot_nt(a, b):
    return lax.dot_general(a, b, (((1,), (1,)), ((), ())), preferred_element_type=F32)


def _rmsnorm_body(x_ref, g_ref, o_ref):
    x = x_ref[...]
    ms = jnp.mean(x * x, axis=-1, keepdims=True)
    o_ref[...] = (x * lax.rsqrt(ms + RMS_EPS) * g_ref[...]).astype(o_ref.dtype)


def _rmsnorm(x, gain, tm=256):
    m, d = x.shape
    return pl.pallas_call(
        _rmsnorm_body,
        grid=(m // tm,),
        in_specs=[pl.BlockSpec((tm, d), lambda i: (i, 0)), pl.BlockSpec((1, d), lambda i: (0, 0))],
        out_specs=pl.BlockSpec((tm, d), lambda i: (i, 0)),
        out_shape=jax.ShapeDtypeStruct((m, d), BF16),
        compiler_params=_cparams(("parallel",)),
        name="rmsnorm",
    )(x, gain.reshape(1, d))


def _norm_rows(x, gain):
    ms = jnp.mean(x * x, axis=-1, keepdims=True)
    return (x * lax.rsqrt(ms + RMS_EPS) * gain).astype(BF16)


RT_E1, RT_E2, RT_W1, RT_W2, RT_R1, RT_R2 = range(6)


def _moe_router_body(x_ref, g_ref, r_ref, info_ref, cnt_ref, base_sc):
    @pl.when(pl.program_id(0) == 0)
    def _():
        base_sc[...] = jnp.zeros_like(base_sc)

    hn = _norm_rows(x_ref[...], g_ref[...])
    logits = _dot(hn, r_ref[...])
    lane = lax.broadcasted_iota(jnp.int32, logits.shape, 1).astype(F32)
    logits = jnp.where(lane < MOE_EXPERTS, logits, NEG)
    v1 = jnp.max(logits, axis=-1, keepdims=True)
    i1 = jnp.min(jnp.where(logits == v1, lane, 1e3), axis=-1, keepdims=True)
    rest = jnp.where(lane == i1, NEG, logits)
    v2 = jnp.max(rest, axis=-1, keepdims=True)
    i2 = jnp.min(jnp.where(rest == v2, lane, 1e3), axis=-1, keepdims=True)
    e2 = jnp.exp(v2 - v1)
    den = 1.0 + e2
    hot = jnp.where(jnp.logical_or(lane == i1, lane == i2), 1.0, 0.0)
    tm = hot.shape[0]
    earlier = (lax.broadcasted_iota(jnp.int32, (tm, tm), 0) > lax.broadcasted_iota(jnp.int32, (tm, tm), 1))
    before = _dot(jnp.where(earlier, 1.0, 0.0).astype(BF16), hot.astype(BF16)) + base_sc[...]
    r1 = jnp.sum(jnp.where(lane == i1, before, 0.0), axis=-1, keepdims=True)
    r2 = jnp.sum(jnp.where(lane == i2, before, 0.0), axis=-1, keepdims=True)
    rec = jnp.zeros_like(logits)
    for pos, val in ((RT_E1, i1), (RT_E2, i2), (RT_W1, 1.0 / den), (RT_W2, e2 / den), (RT_R1, r1), (RT_R2, r2)):
        rec = jnp.where(lane == float(pos), val, rec)
    info_ref[...] = rec
    total = base_sc[...] + jnp.sum(hot, axis=0, keepdims=True)
    base_sc[...] = total
    cnt_ref[...] = jnp.broadcast_to(total, cnt_ref.shape)


def _moe_router(x, gain, router, tm=256):
    m, d = x.shape
    rpad = jnp.zeros((d, LANES), BF16).at[:, :MOE_EXPERTS].set(router.astype(BF16))
    return pl.pallas_call(
        _moe_router_body,
        grid=(m // tm,),
        in_specs=[pl.BlockSpec((tm, d), lambda i: (i, 0)), pl.BlockSpec((1, d), lambda i: (0, 0)),
                  pl.BlockSpec((d, LANES), lambda i: (0, 0))],
        out_specs=[pl.BlockSpec((tm, LANES), lambda i: (i, 0)), pl.BlockSpec((8, LANES), lambda i: (0, 0))],
        out_shape=[jax.ShapeDtypeStruct((m, LANES), F32), jax.ShapeDtypeStruct((8, LANES), F32)],
        scratch_shapes=[pltpu.VMEM((1, LANES), F32)],
        compiler_params=_cparams(("arbitrary",)),
        name="moe_router",
    )(x, gain.reshape(1, d), rpad)


def _matmul(x, w_list, epilogue, out_list, *, tm, tn, n_total, tk=None, extras=(), name="matmul",
            dimsem=("parallel", "parallel", "arbitrary")):
    m, kdim = x.shape
    tk = kdim if tk is None else tk
    nk = kdim // tk
    nw, ne, no = len(w_list), len(extras), len(out_list)

    def body(*refs):
        x_ref = refs[0]
        w_refs = refs[1:1 + nw]
        e_refs = refs[1 + nw:1 + nw + ne]
        o_refs = refs[1 + nw + ne:1 + nw + ne + no]
        acc_refs = refs[1 + nw + ne + no:]
        n = pl.program_id(0)
        if nk == 1:
            xv = x_ref[...]
            epilogue([_dot(xv, w[...].astype(xv.dtype)) for w in w_refs], e_refs, o_refs, n)
        else:
            k = pl.program_id(2)

            @pl.when(k == 0)
            def _():
                for a in acc_refs:
                    a[...] = jnp.zeros_like(a)

            xv = x_ref[...]
            for a, w in zip(acc_refs, w_refs):
                a[...] += _dot(xv, w[...])

            @pl.when(k == nk - 1)
            def _():
                epilogue([a[...] for a in acc_refs], e_refs, o_refs, n)

    in_specs = [pl.BlockSpec((tm, tk), lambda n, mi, k: (mi, k))]
    in_specs += [s for _, s in w_list] + [s for _, s in extras]
    scratch = [] if nk == 1 else [pltpu.VMEM((tm, tn), F32) for _ in range(nw)]
    return pl.pallas_call(
        body,
        grid=(n_total // tn, m // tm, nk),
        in_specs=in_specs,
        out_specs=[s for _, s in out_list],
        out_shape=[s for s, _ in out_list],
        scratch_shapes=scratch,
        compiler_params=_cparams(dimsem),
        name=name,
    )(x, *[a for a, _ in w_list], *[a for a, _ in extras])


def _wspec(tk, tn, off=0, layer=None):
    if layer is None:
        return pl.BlockSpec((tk, tn), lambda n, mi, k: (k, n + off))
    return pl.BlockSpec((None, tk, tn), lambda n, mi, k: (layer, k, n + off))


def _ospec(tm, tn):
    return pl.BlockSpec((tm, tn), lambda n, mi, k: (mi, n))


def _epi_resid(accs, e_refs, o_refs, n):
    o_refs[0][...] = e_refs[0][...] + accs[0]


def _matmul_resid(x, w, resid, *, layer=None, tm=512, tn=1024, tk=None, name="matmul_resid"):
    m, kdim = x.shape
    nn = w.shape[-1]
    tk = min(kdim, 4096) if tk is None else tk
    return _matmul(
        x, [(w, _wspec(tk, tn, 0, layer))], _epi_resid,
        [(jax.ShapeDtypeStruct((m, nn), F32), _ospec(tm, tn))],
        tm=tm, tn=tn, tk=tk, n_total=nn, extras=[(resid, _ospec(tm, tn))], name=name)[0]


def _silu(a):
    return a * (1.0 / (1.0 + jnp.exp(-a)))


def _epi_swiglu(accs, e_refs, o_refs, n):
    a, b = accs
    o_refs[0][...] = (_silu(a) * b).astype(BF16)


def _swiglu_up(x, w_up, *, layer=None, tm=512, tn=512, name="swiglu_up"):
    m, kdim = x.shape
    ff = w_up.shape[-1] // 2
    return _matmul(
        x, [(w_up, _wspec(kdim, tn, 0, layer)), (w_up, _wspec(kdim, tn, ff // tn, layer))], _epi_swiglu,
        [(jax.ShapeDtypeStruct((m, ff), BF16), _ospec(tm, tn))],
        tm=tm, tn=tn, n_total=ff, name=name)[0]


def _moe_plan(info, cnt, s):
    ne, tm = MOE_EXPERTS, MOE_TM
    e1 = info[:, RT_E1].astype(jnp.int32)
    e2 = info[:, RT_E2].astype(jnp.int32)
    counts = cnt[0, :ne].astype(jnp.int32)
    padded = (counts + tm - 1) // tm * tm
    ends = jnp.cumsum(padded)
    off = ends - padded
    dest1 = off[e1] + info[:, RT_R1].astype(jnp.int32)
    dest2 = off[e2] + info[:, RT_R2].astype(jnp.int32)
    ntile = (2 * s) // tm + ne
    tile_start = jnp.arange(ntile, dtype=jnp.int32) * tm
    tile_expert = jnp.minimum(jnp.sum(tile_start[:, None] >= ends[None, :], axis=1), ne - 1).astype(jnp.int32)
    nvalid = (ends[ne - 1:ne] // tm).astype(jnp.int32)
    tok = jnp.arange(s, dtype=jnp.int32)
    src = jnp.zeros((ntile * tm,), jnp.int32).at[dest1].set(tok).at[dest2].set(tok)
    return dest1, dest2, src, tile_expert, nvalid


def _row_copy(src_hbm, row, dst, r, sem):
    return pltpu.make_async_copy(src_hbm.at[pl.ds(row, 1)], dst.at[pl.ds(r, 1)], sem)


def _moe_gather_body(src_ref, nv_ref, h_hbm, g_ref, o_ref, buf, sems):
    t = pl.program_id(0)
    nv = nv_ref[0]
    tm = buf.shape[1]

    def rows(tile, slot, start):
        def body(r, carry):
            cp = _row_copy(h_hbm, src_ref[tile * tm + r], buf.at[slot], r, sems.at[slot])
            if start:
                cp.start()
            else:
                cp.wait()
            return carry

        lax.fori_loop(0, tm, body, 0)

    @pl.when(jnp.logical_and(t == 0, nv > 0))
    def _():
        rows(0, 0, True)

    @pl.when(t + 1 < nv)
    def _():
        rows(t + 1, (t + 1) % 2, True)

    @pl.when(t < nv)
    def _():
        rows(t, t % 2, False)
        o_ref[...] = _norm_rows(buf[t % 2], g_ref[...])

    @pl.when(t >= nv)
    def _():
        o_ref[...] = jnp.zeros_like(o_ref)


def _moe_gather(h, gain, src, nvalid):
    s, d = h.shape
    tm = MOE_TM
    ntile = src.shape[0] // tm
    return pl.pallas_call(
        _moe_gather_body,
        grid_spec=pltpu.PrefetchScalarGridSpec(
            num_scalar_prefetch=2,
            grid=(ntile,),
            in_specs=[pl.BlockSpec(memory_space=pl.ANY),
                      pl.BlockSpec((1, d), lambda t, src_, nv_: (0, 0))],
            out_specs=pl.BlockSpec((tm, d), lambda t, src_, nv_: (t, 0)),
            scratch_shapes=[pltpu.VMEM((2, tm, d), F32), pltpu.SemaphoreType.DMA((2,))]),
        out_shape=jax.ShapeDtypeStruct((ntile * tm, d), BF16),
        compiler_params=_cparams(("arbitrary",)),
        name="moe_gather",
    )(src, nvalid, h, gain.reshape(1, d))


def _new_expert(te_ref, t):
    return jnp.logical_or(t == 0, te_ref[t] != te_ref[jnp.maximum(t - 1, 0)])


def _moe_up_body(te_ref, nv_ref, x_ref, wa_ref, wb_ref, o_ref, wa_sc, wb_sc):
    t = pl.program_id(1)
    live = t < nv_ref[0]

    @pl.when(jnp.logical_and(live, _new_expert(te_ref, t)))
    def _():
        wa_sc[...] = wa_ref[...].astype(BF16)
        wb_sc[...] = wb_ref[...].astype(BF16)

    @pl.when(live)
    def _():
        x = x_ref[...]
        o_ref[...] = (_silu(_dot(x, wa_sc[...])) * _dot(x, wb_sc[...])).astype(BF16)

    @pl.when(jnp.logical_not(live))
    def _():
        o_ref[...] = jnp.zeros_like(o_ref)


def _moe_up(xs, w_up, layer, tile_expert, nvalid, *, tn=512):
    p, d = xs.shape
    tm = MOE_TM
    ff = w_up.shape[-1] // 2
    nn = ff // tn
    return pl.pallas_call(
        _moe_up_body,
        grid_spec=pltpu.PrefetchScalarGridSpec(
            num_scalar_prefetch=2,
            grid=(nn, p // tm),
            in_specs=[pl.BlockSpec((tm, d), lambda n, t, te, nv: (t, 0)),
                      pl.BlockSpec((None, None, d, tn), lambda n, t, te, nv: (layer, te[t], 0, n)),
                      pl.BlockSpec((None, None, d, tn), lambda n, t, te, nv: (layer, te[t], 0, n + nn))],
            out_specs=pl.BlockSpec((tm, tn), lambda n, t, te, nv: (t, n)),
            scratch_shapes=[pltpu.VMEM((d, tn), BF16), pltpu.VMEM((d, tn), BF16)]),
        out_shape=jax.ShapeDtypeStruct((p, ff), BF16),
        compiler_params=_cparams(("arbitrary", "arbitrary")),
        name="moe_up",
    )(tile_expert, nvalid, xs, w_up, w_up)


def _moe_down_body(te_ref, nv_ref, x_ref, w_ref, o_ref, w_sc):
    t = pl.program_id(1)
    live = t < nv_ref[0]

    @pl.when(jnp.logical_and(live, _new_expert(te_ref, t)))
    def _():
        w_sc[...] = w_ref[...].astype(BF16)

    @pl.when(live)
    def _():
        o_ref[...] = _dot(x_ref[...], w_sc[...])

    @pl.when(jnp.logical_not(live))
    def _():
        o_ref[...] = jnp.zeros_like(o_ref)


def _moe_down(act, w_down, layer, tile_expert, nvalid, *, tn=1024):
    p, ff = act.shape
    tm = MOE_TM
    d = w_down.shape[-1]
    return pl.pallas_call(
        _moe_down_body,
        grid_spec=pltpu.PrefetchScalarGridSpec(
            num_scalar_prefetch=2,
            grid=(d // tn, p // tm),
            in_specs=[pl.BlockSpec((tm, ff), lambda n, t, te, nv: (t, 0)),
                      pl.BlockSpec((None, None, ff, tn), lambda n, t, te, nv: (layer, te[t], 0, n))],
            out_specs=pl.BlockSpec((tm, tn), lambda n, t, te, nv: (t, n)),
            scratch_shapes=[pltpu.VMEM((ff, tn), BF16)]),
        out_shape=jax.ShapeDtypeStruct((p, d), F32),
        compiler_params=_cparams(("arbitrary", "arbitrary")),
        name="moe_down",
    )(tile_expert, nvalid, act, w_down)


def _moe_combine_body(d1_ref, d2_ref, y_hbm, h_ref, info_ref, o_ref, buf, sems):
    t = pl.program_id(0)
    nt = pl.num_programs(0)
    tm = h_ref.shape[0]

    def rows(tile, slot, start):
        def body(r, carry):
            for k, dest in enumerate((d1_ref, d2_ref)):
                cp = _row_copy(y_hbm, dest[tile * tm + r], buf.at[2 * slot + k], r, sems.at[2 * slot + k])
                if start:
                    cp.start()
                else:
                    cp.wait()
            return carry

        lax.fori_loop(0, tm, body, 0)

    @pl.when(t == 0)
    def _():
        rows(0, 0, True)

    @pl.when(t + 1 < nt)
    def _():
        rows(t + 1, (t + 1) % 2, True)

    slot = t % 2
    rows(t, slot, False)
    info = info_ref[...]
    o_ref[...] = h_ref[...] + (info[:, RT_W1:RT_W1 + 1] * buf[2 * slot]
                               + info[:, RT_W2:RT_W2 + 1] * buf[2 * slot + 1])


def _moe_combine(h, y, info, dest1, dest2, tm=256):
    s, d = h.shape
    return pl.pallas_call(
        _moe_combine_body,
        grid_spec=pltpu.PrefetchScalarGridSpec(
            num_scalar_prefetch=2,
            grid=(s // tm,),
            in_specs=[pl.BlockSpec(memory_space=pl.ANY),
                      pl.BlockSpec((tm, d), lambda t, a, b: (t, 0)),
                      pl.BlockSpec((tm, LANES), lambda t, a, b: (t, 0))],
            out_specs=pl.BlockSpec((tm, d), lambda t, a, b: (t, 0)),
            scratch_shapes=[pltpu.VMEM((4, tm, d), F32), pltpu.SemaphoreType.DMA((4,))]),
        out_shape=jax.ShapeDtypeStruct((s, d), F32),
        compiler_params=_cparams(("arbitrary",)),
        name="moe_combine",
    )(dest1, dest2, y, h, info)


def _nsa_q_proj(hn, wq, layer, q_gain, nn, *, tm=512, tn=1024):
    m, kdim = hn.shape
    scale = HEAD_DIM ** -0.5 * LOG2E

    def epi(accs, e_refs, o_refs, n):
        acc = accs[0]
        gain = e_refs[0][...]
        for c in range(tn // HEAD_DIM):
            a = acc[:, c * HEAD_DIM:(c + 1) * HEAD_DIM]
            ms = jnp.mean(a * a, axis=-1, keepdims=True)
            o_refs[0][:, c * HEAD_DIM:(c + 1) * HEAD_DIM] = (
                a * lax.rsqrt(ms + RMS_EPS) * gain * scale).astype(BF16)

    return _matmul(
        hn, [(wq, _wspec(kdim, tn, 0, layer))], epi,
        [(jax.ShapeDtypeStruct((m, nn), BF16), _ospec(tm, tn))],
        tm=tm, tn=tn, n_total=nn,
        extras=[(q_gain.reshape(1, HEAD_DIM), pl.BlockSpec((1, HEAD_DIM), lambda n, mi, k: (0, 0)))],
        name="nsa_q_proj")[0]


def _nsa_kv_proj(hn, wkv, layer, k_gain, col0, *, tm=512):
    m, kdim = hn.shape
    g_ = NSA_KV_HEADS
    tn = g_ * HEAD_DIM
    spt = tm // LANES
    gains = jnp.ones((6, 1, HEAD_DIM), F32).at[2, 0].set(k_gain[1]).at[4, 0].set(k_gain[2])

    def epi(accs, e_refs, o_refs, n):
        acc = accs[0]
        kg = e_refs[0][0]
        do_norm = jnp.logical_or(n == 2, n == 4)
        for g in range(g_):
            a = acc[:, g * HEAD_DIM:(g + 1) * HEAD_DIM]
            ms = jnp.mean(a * a, axis=-1, keepdims=True)
            an = a * lax.rsqrt(ms + RMS_EPS) * kg
            o_refs[0][g] = jnp.where(do_norm, an, a).astype(BF16)
        is_v = jnp.logical_or(n == 3, n == 5)

        @pl.when(is_v)
        def _():
            for g in range(g_):
                for c in range(spt):
                    blk = acc[c * LANES:(c + 1) * LANES, g * HEAD_DIM:(g + 1) * HEAD_DIM]
                    o_refs[1][g, c] = jnp.transpose(blk).astype(BF16)

        @pl.when(jnp.logical_not(is_v))
        def _():
            o_refs[1][...] = jnp.zeros_like(o_refs[1])

    return _matmul(
        hn, [(wkv, _wspec(kdim, tn, col0 // tn, layer))], epi,
        [(jax.ShapeDtypeStruct((6 * g_, m, HEAD_DIM), BF16),
          pl.BlockSpec((g_, tm, HEAD_DIM), lambda n, mi, k: (n, mi, 0))),
         (jax.ShapeDtypeStruct((6, g_, m // LANES, HEAD_DIM, LANES), BF16),
          pl.BlockSpec((None, g_, spt, HEAD_DIM, LANES), lambda n, mi, k: (n, 0, mi, 0, 0)))],
        tm=tm, tn=tn, n_total=6 * tn,
        extras=[(gains, pl.BlockSpec((1, 1, HEAD_DIM), lambda n, mi, k: (n, 0, 0)))],
        name="nsa_kv_proj")


def _nsa_gate_proj(hn, wg, *, tm=512):
    m, kdim = hn.shape
    nn = wg.shape[1]

    def epi(accs, e_refs, o_refs, n):
        o_refs[0][...] = 1.0 / (1.0 + jnp.exp(-accs[0]))

    return _matmul(
        hn, [(wg, _wspec(kdim, nn))], epi,
        [(jax.ShapeDtypeStruct((m, nn), F32), _ospec(tm, nn))],
        tm=tm, tn=nn, n_total=nn, name="nsa_gate_proj")[0]


def _compress_body(r_ref, pe_ref, w1_ref, w2_ref, kg_ref, o_ref, ot_ref):
    j = pl.program_id(0)
    r = r_ref[0].astype(F32)
    pe = pe_ref[0]
    half = r.shape[1]
    top = (r + pe[0:1]).astype(BF16)
    bot = (r + pe[1:2]).astype(BF16)
    a = _dot(top, w1_ref[0, 0:half, :])
    b = _dot(bot, w1_ref[0, half:2 * half, :])
    nrow = r.shape[0]
    hid = a + pltpu.roll(b, nrow - 1, 0)
    y = _dot(jax.nn.gelu(hid).astype(BF16), w2_ref[0])
    ms = jnp.mean(y * y, axis=-1, keepdims=True)
    yn = y * lax.rsqrt(ms + RMS_EPS) * kg_ref[...]
    out = jnp.where(j == 0, yn, y)
    o_ref[0, 0] = out.astype(BF16)
    ot_ref[0, 0] = jnp.transpose(out).astype(BF16)


def _compress(kv, pe, w1, w2, k_gain0):
    g_ = NSA_KV_HEADS
    s = kv.shape[1]
    nrow = s // CMP_STRIDE
    wid = CMP_STRIDE * HEAD_DIM
    r = kv[:2 * g_].reshape(2 * g_, nrow, wid)
    pe2 = pe.reshape(2, 2, wid)
    return pl.pallas_call(
        _compress_body,
        grid=(2, g_),
        in_specs=[pl.BlockSpec((1, nrow, wid), lambda j, g: (j * g_ + g, 0, 0)),
                  pl.BlockSpec((1, 2, wid), lambda j, g: (j, 0, 0)),
                  pl.BlockSpec((1, 2 * wid, HEAD_DIM), lambda j, g: (j, 0, 0)),
                  pl.BlockSpec((1, HEAD_DIM, HEAD_DIM), lambda j, g: (j, 0, 0)),
                  pl.BlockSpec((1, HEAD_DIM), lambda j, g: (0, 0))],
        out_specs=[pl.BlockSpec((1, 1, nrow, HEAD_DIM), lambda j, g: (j, g, 0, 0)),
                   pl.BlockSpec((1, 1, HEAD_DIM, nrow), lambda j, g: (j, g, 0, 0))],
        out_shape=[jax.ShapeDtypeStruct((2, g_, nrow, HEAD_DIM), BF16),
                   jax.ShapeDtypeStruct((2, g_, HEAD_DIM, nrow), BF16)],
        compiler_params=_cparams(("parallel", "parallel")),
        name="nsa_compress",
    )(r, pe2, w1.astype(BF16), w2.astype(BF16), k_gain0.reshape(1, HEAD_DIM))


def _t5_bucket(dist):
    dist = jnp.maximum(dist, 0)
    max_exact = T5_BUCKETS // 2
    d_f = jnp.maximum(dist, 1).astype(F32)
    log_b = max_exact + (jnp.log(d_f / max_exact) / math.log(T5_MAX_DISTANCE / max_exact)
                         * (T5_BUCKETS - max_exact)).astype(jnp.int32)
    log_b = jnp.minimum(log_b, T5_BUCKETS - 1)
    return jnp.where(dist < max_exact, dist, log_b)


def _bias_table(tbl, dist, valid, shift):
    onehot = jax.nn.one_hot(_t5_bucket(dist).reshape(-1), T5_BUCKETS, dtype=F32)
    t = tbl - tbl[T5_BUCKETS - 1:T5_BUCKETS] if shift else tbl
    vals = jnp.einsum("nb,bh->hn", onehot, t, precision=lax.Precision.HIGHEST)
    vals = vals.reshape((tbl.shape[1],) + dist.shape) * LOG2E
    return jnp.where(valid[None], vals, NEG)


def _nsa_tables(rel_bias):
    tbl = rel_bias.astype(F32)
    r = jnp.arange(Q_BLOCK)
    jw = jnp.arange(WINDOW + Q_BLOCK)
    dw = WINDOW + r[None, :] - jw[:, None]
    tab_w = _bias_table(tbl, dw, (dw >= 0) & (dw < WINDOW), False)
    a = jnp.arange(16)
    cc = jnp.arange(2 * LANES)
    dc = (CMP_NEAR_SHIFT * Q_BLOCK + Q_BLOCK * a[:, None, None] + r[None, None, :]
          - CMP_STRIDE * cc[None, :, None] - (CMP_BLOCK - 1))
    tab_c = _bias_table(tbl, dc, dc >= 0, True)
    jj = jnp.arange(SEL_NEAR_TILES)
    col = jnp.arange(SEL_TILE)
    ds_ = r[None, None, :] - Q_BLOCK + SEL_TILE * (jj[:, None, None] + 1) - col[None, :, None]
    tab_s = _bias_table(tbl, ds_, ds_ >= 0, True)
    return tab_w, tab_c, tab_s, tbl


def _sel_to_cmp(nsel, nc_pad):
    c_start = jnp.arange(nc_pad) * CMP_STRIDE
    s_start = jnp.arange(nsel) * SEL_BLOCK
    overlap = jnp.clip(jnp.minimum(c_start[None, :] + CMP_BLOCK, s_start[:, None] + SEL_BLOCK)
                       - jnp.maximum(c_start[None, :], s_start[:, None]), 0)
    return (overlap.astype(F32) / CMP_BLOCK).astype(BF16)


def _heads_t(q_ref, h0, nh):
    cols = [jnp.transpose(q_ref[:, (h0 + h) * HEAD_DIM:(h0 + h + 1) * HEAD_DIM].astype(F32)) for h in range(nh)]
    return jnp.concatenate(cols, axis=1).astype(BF16)


def _softmax_keys(s, bounded):
    if bounded:
        p = jnp.exp2(s)
        l = jnp.sum(p, axis=0, keepdims=True)
        return p, jnp.where(l > 0.0, 1.0 / l, 0.0)
    m = jnp.max(s, axis=0, keepdims=True)
    p = jnp.exp2(s - m)
    l = jnp.sum(p, axis=0, keepdims=True)
    return p, jnp.where(m > 0.5 * NEG, 1.0 / l, 0.0)


def _logit_bound(rel_bias, q_gain, k_gain):
    qk = HEAD_DIM * (HEAD_DIM ** -0.5 * LOG2E) * jnp.max(jnp.abs(q_gain)) * jnp.max(jnp.abs(k_gain))
    return (1.02 * qk + 2.0 * LOG2E * jnp.max(jnp.abs(rel_bias))).astype(F32).reshape(1)


def _nsa_cw_body(bnd_ref, q_ref, gt_ref, kc_ref, vct_ref, mselt_ref, tabc_ref, kw_ref, vwt_ref, tabw_ref,
                 ocwt_ref, negt_ref):
    bounded = bnd_ref[0] < SAFE_LOGIT
    i = pl.program_id(1)

    @pl.when(bounded)
    def _():
        _nsa_cw_branches(True, i, q_ref, gt_ref, kc_ref, vct_ref, mselt_ref, tabc_ref, kw_ref, vwt_ref, tabw_ref,
                         ocwt_ref, negt_ref)

    @pl.when(jnp.logical_not(bounded))
    def _():
        _nsa_cw_branches(False, i, q_ref, gt_ref, kc_ref, vct_ref, mselt_ref, tabc_ref, kw_ref, vwt_ref, tabw_ref,
                         ocwt_ref, negt_ref)


def _nsa_cw_branches(bounded, i, q_ref, gt_ref, kc_ref, vct_ref, mselt_ref, tabc_ref, kw_ref, vwt_ref, tabw_ref,
                     ocwt_ref, negt_ref):
    f = (i + 16 - CMP_NEAR_SHIFT) // 16 - 1
    kc = kc_ref[0, 0]
    vct = vct_ref[0, 0]
    nc = kc.shape[0]
    nsel = mselt_ref.shape[0]
    wlen = WINDOW + Q_BLOCK
    nslab = wlen // LANES
    start = pl.multiple_of(i * Q_BLOCK, Q_BLOCK)
    kw = kw_ref[0, pl.ds(start, wlen), :]
    vwt = jnp.concatenate([vwt_ref[0, jnp.maximum(i + c - WINDOW // LANES, 0)] for c in range(nslab)],
                          axis=1)
    roww = lax.broadcasted_iota(jnp.int32, (wlen, LANES), 0)
    w_pad = roww < (WINDOW - Q_BLOCK * i)
    gt_t = jnp.transpose(gt_ref[...])
    psum = jnp.zeros((nc, Q_BLOCK), F32)
    hg = CW_HEADS
    for h0 in range(0, NSA_GROUP, hg):
        qt = _heads_t(q_ref, h0, hg)
        s = _dot(kc, qt)
        ta = jnp.concatenate([tabc_ref[h0 + h, 0, 0:LANES, :] for h in range(hg)], axis=1)
        tb = jnp.concatenate([tabc_ref[h0 + h, 0, LANES:2 * LANES, :] for h in range(hg)], axis=1)
        pieces = []
        for ch in range(nc // LANES):
            rest = jnp.where(ch > f + 1, NEG, 0.0)
            bias = jnp.where(ch == f, ta, jnp.where(ch == f + 1, tb, rest))
            pieces.append(s[ch * LANES:(ch + 1) * LANES] + bias)
        s = jnp.concatenate(pieces, axis=0)
        p, linv = _softmax_keys(s, bounded)
        pn = p * linv
        for h in range(hg):
            psum = psum + pn[:, h * Q_BLOCK:(h + 1) * Q_BLOCK]
        oc = _dot(vct, pn.astype(BF16))
        tw = jnp.concatenate([jnp.where(w_pad, NEG, tabw_ref[h0 + h]) for h in range(hg)], axis=1)
        sw = _dot(kw, qt) + tw
        pw, lwinv = _softmax_keys(sw, bounded)
        ow = _dot(vwt, pw.astype(BF16)) * lwinv
        for h in range(hg):
            hh = h0 + h
            cs = slice(h * Q_BLOCK, (h + 1) * Q_BLOCK)
            ocwt_ref[0, 0, hh * HEAD_DIM:(hh + 1) * HEAD_DIM, :] = (
                gt_t[hh:hh + 1] * oc[:, cs] + gt_t[16 + hh:17 + hh] * ow[:, cs])
    p_hi = psum.astype(BF16)
    p_lo = (psum - p_hi.astype(F32)).astype(BF16)
    mselt = mselt_ref[...]
    imp = _dot(mselt, p_hi) + _dot(mselt, p_lo)
    blk = lax.broadcasted_iota(jnp.int32, (nsel, Q_BLOCK), 0).astype(F32)
    qpos = lax.broadcasted_iota(jnp.int32, (nsel, Q_BLOCK), 1)
    cur = (i * (Q_BLOCK // SEL_BLOCK)).astype(F32) + jnp.where(qpos >= SEL_BLOCK, 1.0, 0.0)
    forced = jnp.logical_or(blk == cur, blk == 0.0)
    val = jnp.where(forced, FORCED_SCORE, jnp.where(blk <= cur, imp, -1.0))
    for _ in range(SEL_TOPK):
        mx = jnp.max(val, axis=0, keepdims=True)
        first = jnp.min(jnp.where(val == mx, blk, 1e4), axis=0, keepdims=True)
        val = jnp.where(blk == first, TAKEN, val)
    neg = jnp.where(val == TAKEN, 0.0, NEG)
    for pr in range(SEL_PAD_BLOCKS // 2):
        negt_ref[0, 0, pr] = jnp.full((2, Q_BLOCK), NEG, F32)
    for pr in range(nsel // 2):
        negt_ref[0, 0, SEL_PAD_BLOCKS // 2 + pr] = neg[2 * pr:2 * pr + 2]


def _nsa_cw(bound, q, gates, kc, vct, mselt, tab_c, kw, vt, tab_w):
    s, hd = q.shape
    g_ = NSA_KV_HEADS
    nq = s // Q_BLOCK
    gw = NSA_GROUP * HEAD_DIM
    nc = kc.shape[2]
    nsel = mselt.shape[0]
    spad = kw.shape[1]
    nslab = vt.shape[2]
    wlen = WINDOW + Q_BLOCK

    def var(i):
        return (i + 16 - CMP_NEAR_SHIFT) % 16

    return pl.pallas_call(
        _nsa_cw_body,
        grid=(g_, nq),
        in_specs=[
            pl.BlockSpec(memory_space=pltpu.SMEM),
            pl.BlockSpec((Q_BLOCK, gw), lambda g, i: (i, g)),
            pl.BlockSpec((Q_BLOCK, LANES), lambda g, i: (i, g)),
            pl.BlockSpec((1, 1, nc, HEAD_DIM), lambda g, i: (0, g, 0, 0)),
            pl.BlockSpec((1, 1, HEAD_DIM, nc), lambda g, i: (1, g, 0, 0)),
            pl.BlockSpec((nsel, nc), lambda g, i: (0, 0)),
            pl.BlockSpec((NSA_GROUP, 1, 2 * LANES, Q_BLOCK), lambda g, i: (g, var(i), 0, 0)),
            pl.BlockSpec((1, spad, HEAD_DIM), lambda g, i: (g, 0, 0)),
            pl.BlockSpec((None, 1, nslab, HEAD_DIM, LANES), lambda g, i: (5, g, 0, 0, 0)),
            pl.BlockSpec((NSA_GROUP, wlen, Q_BLOCK), lambda g, i: (g, 0, 0)),
        ],
        out_specs=[pl.BlockSpec((1, 1, gw, Q_BLOCK), lambda g, i: (g, i, 0, 0)),
                   pl.BlockSpec((1, 1, (SEL_PAD_BLOCKS + nsel) // 2, 2, Q_BLOCK), lambda g, i: (g, i, 0, 0, 0))],
        out_shape=[jax.ShapeDtypeStruct((g_, nq, gw, Q_BLOCK), F32),
                   jax.ShapeDtypeStruct((g_, nq, (SEL_PAD_BLOCKS + nsel) // 2, 2, Q_BLOCK), F32)],
        compiler_params=_cparams(("parallel", "arbitrary")),
        name="nsa_cmp_win",
    )(bound, q, gates, kc, vct, mselt, tab_c, kw, vt, tab_w)


def _nsa_sel_body(bnd_ref, q_ref, gt_ref, ks_ref, vst_ref, negp_ref, tabs_ref, ocwt_ref, o_ref,
                  m_sc, l_sc, acc_sc):
    i = pl.program_id(1)
    ntile = i // (SEL_TILE // Q_BLOCK) + 1
    nh = NSA_GROUP
    spt = SEL_TILE // LANES
    ppt = SEL_TILE // (2 * SEL_BLOCK)
    qt = _heads_t(q_ref, 0, nh)

    def tile(jj, near, bounded, nsub=1):
        slab0 = (i + 1) - spt * jj
        row0 = pl.multiple_of(slab0 * LANES, LANES)
        kt = ks_ref[0, pl.ds(row0, nsub * SEL_TILE), :]
        vtt = jnp.concatenate([vst_ref[0, jnp.maximum(slab0 - spt + c, 0)] for c in range(nsub * spt)],
                              axis=1)
        rows = []
        for c in range(nsub * ppt):
            pair = negp_ref[0, 0, slab0 + c]
            rows += [jnp.broadcast_to(pair[r:r + 1], (SEL_BLOCK, Q_BLOCK)) for r in range(2)]
        mk = jnp.concatenate(rows, axis=0)
        if near:
            bias = jnp.concatenate(
                [jnp.concatenate([tabs_ref[h, jj - c] for c in range(nsub)], axis=0) + mk for h in range(nh)],
                axis=1)
        else:
            bias = jnp.tile(mk, (1, nh))
        sc = _dot(kt, qt) + bias
        if bounded:
            p = jnp.exp2(sc)
            l_sc[...] = l_sc[...] + jnp.sum(p, axis=0, keepdims=True)
            acc_sc[...] = acc_sc[...] + _dot(vtt, p.astype(BF16))
        else:
            m_old = m_sc[...]
            m_new = jnp.maximum(m_old, jnp.max(sc, axis=0, keepdims=True))
            alpha = jnp.exp2(m_old - m_new)
            p = jnp.exp2(sc - m_new)
            l_sc[...] = alpha * l_sc[...] + jnp.sum(p, axis=0, keepdims=True)
            acc_sc[...] = alpha * acc_sc[...] + _dot(vtt, p.astype(BF16))
            m_sc[...] = m_new

    def run(bounded):
        m_sc[...] = jnp.full(m_sc.shape, NEG, F32)
        l_sc[...] = jnp.zeros(l_sc.shape, F32)
        acc_sc[...] = jnp.zeros(acc_sc.shape, F32)
        @pl.when(ntile >= SEL_NEAR_TILES)
        def _():
            for jj in range(SEL_FAR_WIDTH - 1, SEL_NEAR_TILES, SEL_FAR_WIDTH):
                tile(jj, True, bounded, SEL_FAR_WIDTH)
            for jj in range(SEL_NEAR_TILES - SEL_NEAR_TILES % SEL_FAR_WIDTH, SEL_NEAR_TILES):
                tile(jj, True, bounded)

        @pl.when(ntile < SEL_NEAR_TILES)
        def _():
            for jj in range(SEL_NEAR_TILES - 1):
                @pl.when(jj < ntile)
                def _():
                    tile(jj, True, bounded)

        nfar = jnp.maximum(ntile - SEL_NEAR_TILES, 0)

        def far(k, carry):
            tile(SEL_NEAR_TILES + SEL_FAR_WIDTH * k + SEL_FAR_WIDTH - 1, False, bounded, SEL_FAR_WIDTH)
            return carry

        lax.fori_loop(0, nfar // SEL_FAR_WIDTH, far, 0)

        def far_rest(jj, carry):
            tile(jj, False, bounded)
            return carry

        lax.fori_loop(SEL_NEAR_TILES + nfar // SEL_FAR_WIDTH * SEL_FAR_WIDTH, ntile, far_rest, 0)

    bounded = bnd_ref[0] < SAFE_LOGIT

    @pl.when(bounded)
    def _():
        run(True)

    @pl.when(jnp.logical_not(bounded))
    def _():
        run(False)

    o_t = acc_sc[...] * (1.0 / l_sc[...])
    gt_t = jnp.transpose(gt_ref[...])
    for h in range(nh):
        hs = slice(h * HEAD_DIM, (h + 1) * HEAD_DIM)
        oh = ocwt_ref[0, 0, hs, :] + gt_t[8 + h:9 + h] * o_t[:, h * Q_BLOCK:(h + 1) * Q_BLOCK]
        o_ref[:, hs] = jnp.transpose(oh).astype(BF16)


def _nsa_sel(bound, q, gates, ks, vt, negp, tab_s, ocwt):
    s, hd = q.shape
    g_ = NSA_KV_HEADS
    nq = s // Q_BLOCK
    gw = NSA_GROUP * HEAD_DIM
    spad = ks.shape[1]
    nslab = vt.shape[2]
    npair = negp.shape[2]
    return pl.pallas_call(
        _nsa_sel_body,
        grid=(g_, nq),
        in_specs=[
            pl.BlockSpec(memory_space=pltpu.SMEM),
            pl.BlockSpec((Q_BLOCK, gw), lambda g, i: (i, g)),
            pl.BlockSpec((Q_BLOCK, LANES), lambda g, i: (i, g)),
            pl.BlockSpec((1, spad, HEAD_DIM), lambda g, i: (g, 0, 0)),
            pl.BlockSpec((None, 1, nslab, HEAD_DIM, LANES), lambda g, i: (3, g, 0, 0, 0)),
            pl.BlockSpec((1, 1, npair, 2, Q_BLOCK), lambda g, i: (g, i, 0, 0, 0)),
            pl.BlockSpec((NSA_GROUP, SEL_NEAR_TILES, SEL_TILE, Q_BLOCK), lambda g, i: (g, 0, 0, 0)),
            pl.BlockSpec((1, 1, gw, Q_BLOCK), lambda g, i: (g, i, 0, 0)),
        ],
        out_specs=pl.BlockSpec((Q_BLOCK, gw), lambda g, i: (i, g)),
        out_shape=jax.ShapeDtypeStruct((s, hd), BF16),
        scratch_shapes=[pltpu.VMEM((1, NSA_GROUP * Q_BLOCK), F32), pltpu.VMEM((1, NSA_GROUP * Q_BLOCK), F32),
                        pltpu.VMEM((HEAD_DIM, NSA_GROUP * Q_BLOCK), F32)],
        compiler_params=_cparams(("parallel", "arbitrary")),
        name="nsa_selected",
    )(bound, q, gates, ks, vt, negp, tab_s, ocwt)


def _pad_keys(x, pad):
    return jnp.pad(x, ((0, 0), (pad, 0), (0, 0)))


def _nsa_mixer(h, hn, tables, layer, w_in, w_gate, cmp_pe, cmp_w1, cmp_w2, q_gain, k_gain, w_out):
    s, d = h.shape
    g_, hpg, dh = NSA_KV_HEADS, NSA_GROUP, HEAD_DIM
    hd = g_ * hpg * dh
    tab_w, tab_c, tab_s, tbl = tables
    q = _nsa_q_proj(hn, w_in, layer, q_gain, hd)
    kv, vt = _nsa_kv_proj(hn, w_in, layer, k_gain, hd)
    wg = w_gate.reshape(d, g_, hpg, 3).transpose(0, 1, 3, 2).reshape(d, g_, 3 * hpg)
    wg = jnp.pad(wg, ((0, 0), (0, 0), (0, LANES - 3 * hpg))).reshape(d, g_ * LANES)
    gates = _nsa_gate_proj(hn, wg)
    kc, kct = _compress(kv, cmp_pe, cmp_w1, cmp_w2, k_gain[0])
    mselt = _sel_to_cmp(s // SEL_BLOCK, s // CMP_STRIDE)
    ks = _pad_keys(kv[2 * g_:3 * g_], SEL_TILE)
    kw = _pad_keys(kv[4 * g_:5 * g_], WINDOW)
    bound = _logit_bound(tbl, q_gain, k_gain)
    ocwt, negp = _nsa_cw(bound, q, gates, kc, kct, mselt, tab_c, kw, vt, tab_w)
    o = _nsa_sel(bound, q, gates, ks, vt, negp, tab_s, ocwt)
    return _matmul_resid(o, w_out, h, layer=layer, name="nsa_out")


def _gelu_proj(hn, w, layer, *, tm=512, tn=1024):
    m, kdim = hn.shape
    nn = w.shape[-1]

    def epi(accs, e_refs, o_refs, n):
        o_refs[0][...] = jax.nn.gelu(accs[0]).astype(BF16)

    return _matmul(
        hn, [(w, _wspec(kdim, tn, 0, layer))], epi,
        [(jax.ShapeDtypeStruct((m, nn), BF16), _ospec(tm, tn))],
        tm=tm, tn=tn, n_total=nn, name="sgu_in")[0]


def _sgu_mix_body(u_ref, v_ref, gain_ref, w_ref, bt_ref, o_ref):
    vn = _norm_rows(v_ref[...].astype(F32), gain_ref[...])
    t = w_ref.shape[1]
    causal = (lax.broadcasted_iota(jnp.int32, (t, t), 0) >= lax.broadcasted_iota(jnp.int32, (t, t), 1))
    bt = bt_ref[...]
    gd = vn.shape[1] // SG_GROUPS
    for g in range(SG_GROUPS):
        gs = slice(g * gd, (g + 1) * gd)
        w = jnp.where(causal, w_ref[g], 0.0).astype(BF16)
        mixed = _dot(w, vn[:, gs]) + bt[:, g:g + 1]
        o_ref[:, gs] = (u_ref[:, gs].astype(F32) * mixed).astype(BF16)


def _sgu_mix(uv, v_gain, w_s, b_s):
    s = uv.shape[0]
    wd = uv.shape[1] // 2
    t = SG_CHUNK
    return pl.pallas_call(
        _sgu_mix_body,
        grid=(s // t,),
        in_specs=[pl.BlockSpec((t, wd), lambda c: (c, 0)),
                  pl.BlockSpec((t, wd), lambda c: (c, 1)),
                  pl.BlockSpec((1, wd), lambda c: (0, 0)),
                  pl.BlockSpec((SG_GROUPS, t, t), lambda c: (0, 0, 0)),
                  pl.BlockSpec((t, SG_GROUPS), lambda c: (0, 0))],
        out_specs=pl.BlockSpec((t, wd), lambda c: (c, 0)),
        out_shape=jax.ShapeDtypeStruct((s, wd), BF16),
        compiler_params=_cparams(("parallel",)),
        name="sgu_mix",
    )(uv, uv, v_gain.reshape(1, wd), w_s, b_s.T)


def _sgu_mixer(h, hn, layer, w_in, v_gain, w_s, b_s, w_out):
    uv = _gelu_proj(hn, w_in, layer)
    y = _sgu_mix(uv, v_gain, w_s, b_s)
    return _matmul_resid(y, w_out, h, layer=layer, name="sgu_out")


def _gla_body(q_ref, k_ref, v_ref, r_ref, g1_ref, wg_ref, bg_ref, og_ref, o_ref, state_sc, diag_sc):
    @pl.when(pl.program_id(0) == 0)
    def _():
        state_sc[...] = jnp.zeros_like(state_sc)

    nh = state_sc.shape[0]
    dk = q_ref.shape[1] // nh
    dv = v_ref.shape[1] // nh
    g1 = g1_ref[...].astype(BF16)
    for hh in range(nh):
        ks = slice(hh * dk, (hh + 1) * dk)
        vs = slice(hh * dv, (hh + 1) * dv)
        o_ref[:, vs] = _gla_head(q_ref[:, ks], k_ref[:, ks], v_ref[:, vs], r_ref[:, vs], g1, wg_ref[:, ks],
                                 bg_ref[:, ks], og_ref[...], state_sc.at[hh], diag_sc.at[hh])


def _gla_head(q_in, k_in, v, r_in, g1, wg, bg, og, state_sc, diag_sc):
    ch, dk = q_in.shape
    x = _dot(g1, wg) + bg
    log_a = (jnp.minimum(x, 0.0) - jnp.log(1.0 + jnp.exp(-jnp.abs(x)))) * (1.0 / GLA_GATE_TEMP)
    tri = (lax.broadcasted_iota(jnp.int32, (ch, ch), 0)
           >= lax.broadcasted_iota(jnp.int32, (ch, ch), 1))
    tri_b = jnp.where(tri, 1.0, 0.0).astype(BF16)
    a_hi = log_a.astype(BF16)
    a_lo = (log_a - a_hi.astype(F32)).astype(BF16)
    b = _dot(tri_b, a_hi) + _dot(tri_b, a_lo)
    q = q_in.astype(F32) * (dk ** -0.5)
    k = k_in.astype(F32)
    state = state_sc[...]
    o = _dot((q * jnp.exp(b)).astype(BF16), state.astype(BF16))
    nsub = ch // GLA_SUB
    ends = [jnp.broadcast_to(b[(jb + 1) * GLA_SUB - 1:(jb + 1) * GLA_SUB], (GLA_SUB, dk))
            for jb in range(nsub)]
    b_end = jnp.concatenate(ends, axis=0)
    b_start = jnp.concatenate([jnp.zeros((GLA_SUB, dk), F32)] + ends[:-1], axis=0)
    rblk = lax.broadcasted_iota(jnp.int32, (ch, ch), 0) // GLA_SUB
    cblk = lax.broadcasted_iota(jnp.int32, (ch, ch), 1) // GLA_SUB
    zero_row = jnp.zeros((1, dk), F32)
    spread = jnp.max(jnp.concatenate(
        [(ends[jb - 1][0:1] if jb else zero_row) - ends[jb][0:1] for jb in range(nsub)], axis=0))

    @pl.when(spread < GLA_SAFE_EXP)
    def _():
        q_diag = (q * jnp.exp(b - b_start)).astype(BF16)
        k_diag = (k * jnp.exp(b_start - b)).astype(BF16)
        diag_sc[...] = _dot_nt(q_diag, k_diag)

    @pl.when(spread >= GLA_SAFE_EXP)
    def _():
        lane = lax.broadcasted_iota(jnp.int32, (GLA_SUB, ch), 1)
        strips = []
        for ib in range(nsub):
            rs = slice(ib * GLA_SUB, (ib + 1) * GLA_SUB)
            qi, ki, bi = q[rs], k[rs], b[rs]
            strip = jnp.zeros((GLA_SUB, ch), F32)
            for j in range(GLA_SUB):
                decay = jnp.exp(jnp.minimum(bi - bi[j:j + 1], 0.0))
                col = jnp.sum(qi * ki[j:j + 1] * decay, axis=-1, keepdims=True)
                strip = jnp.where(lane == ib * GLA_SUB + j, col, strip)
            strips.append(strip)
        diag_sc[...] = jnp.concatenate(strips, axis=0)

    attn = jnp.where(jnp.logical_and(tri, rblk == cblk), diag_sc[...], 0.0)
    k_hat = (k * jnp.exp(b_end - b)).astype(BF16)
    for jb in range(nsub - 1):
        q_hat = (q * jnp.exp(jnp.minimum(b - ends[jb][0:1], 0.0))).astype(BF16)
        attn = attn + jnp.where(jnp.logical_and(cblk == jb, rblk > jb), _dot_nt(q_hat, k_hat), 0.0)
    o = o + _dot(attn.astype(BF16), v)
    b_last = b[ch - 1:ch]
    k_dec = k * jnp.exp(b_last - b)
    k_dec_t = jnp.transpose(k_dec).astype(BF16)
    decay_t = jnp.transpose(jnp.broadcast_to(jnp.exp(b_last), (LANES, dk)))
    state_sc[...] = decay_t[:, 0:1] * state + _dot(k_dec_t, v)
    ms = jnp.mean(o * o, axis=-1, keepdims=True)
    on = o * lax.rsqrt(ms + RMS_EPS) * og
    return (on * _silu(r_in.astype(F32))).astype(BF16)


def _gla_mixer(h, hn, layer, w_in, w_gate1, w_gate2, b_gate, o_gain, w_out):
    s, d = h.shape
    nh = GLA_HEADS
    dk = d // 2
    dv = d
    dkh, dvh = dk // nh, dv // nh
    nmain = 2 * dk + 2 * dv

    def epi_bf16(accs, e_refs, o_refs, n):
        o_refs[0][...] = accs[0].astype(BF16)

    def epi_f32(accs, e_refs, o_refs, n):
        o_refs[0][...] = accs[0]

    tm, tn = 512, 1024
    proj = _matmul(hn, [(w_in, _wspec(d, tn, 0, layer))], epi_bf16,
                   [(jax.ShapeDtypeStruct((s, nmain), BF16), _ospec(tm, tn))],
                   tm=tm, tn=tn, n_total=nmain, name="gla_in")[0]
    wg1 = jnp.pad(w_gate1, ((0, 0), (0, LANES - GLA_GATE_RANK)))
    g1 = _matmul(hn, [(wg1, _wspec(d, LANES))], epi_f32,
                 [(jax.ShapeDtypeStruct((s, LANES), F32), _ospec(tm, LANES))],
                 tm=tm, tn=LANES, n_total=LANES, name="gla_gate_in")[0]
    wg2 = jnp.pad(w_gate2.astype(BF16), ((0, LANES - GLA_GATE_RANK), (0, 0)))
    ch = GLA_CHUNK
    o = pl.pallas_call(
        _gla_body,
        grid=(s // ch,),
        in_specs=[
            pl.BlockSpec((ch, dk), lambda c: (c, 0)),
            pl.BlockSpec((ch, dk), lambda c: (c, 1)),
            pl.BlockSpec((ch, dv), lambda c: (c, 2 * dk // dv)),
            pl.BlockSpec((ch, dv), lambda c: (c, 2 * dk // dv + 1)),
            pl.BlockSpec((ch, LANES), lambda c: (c, 0)),
            pl.BlockSpec((LANES, dk), lambda c: (0, 0)),
            pl.BlockSpec((1, dk), lambda c: (0, 0)),
            pl.BlockSpec((1, dvh), lambda c: (0, 0)),
        ],
        out_specs=pl.BlockSpec((ch, dv), lambda c: (c, 0)),
        out_shape=jax.ShapeDtypeStruct((s, dv), BF16),
        scratch_shapes=[pltpu.VMEM((nh, dkh, dvh), F32), pltpu.VMEM((nh, ch, ch), F32)],
        compiler_params=_cparams(("arbitrary",)),
        name="gla_scan",
    )(proj, proj, proj, proj, g1, wg2, b_gate.reshape(1, dk), o_gain.reshape(1, dvh))
    return _matmul_resid(o, w_out, h, layer=layer, name="gla_out")


def _dense_ffn(h, hn, layer, w_up, w_down):
    act = _swiglu_up(hn, w_up, layer=layer)
    return _matmul_resid(act, w_down, h, layer=layer, tn=512, tk=w_down.shape[1], name="ffn_down")


def _moe_ffn(h, gain, router, layer, w_up, w_down):
    s, _ = h.shape
    info, cnt = _moe_router(h, gain, router)
    dest1, dest2, src, tile_expert, nvalid = _moe_plan(info, cnt, s)
    xs = _moe_gather(h, gain, src, nvalid)
    act = _moe_up(xs, w_up, layer, tile_expert, nvalid)
    y = _moe_down(act, w_down, layer, tile_expert, nvalid)
    return _moe_combine(h, y, info, dest1, dest2)


def _lane_pad(w):
    extra = -w.shape[-1] % LANES
    return jnp.pad(w, [(0, 0)] * (w.ndim - 1) + [(0, extra)]) if extra else w


def kernel(x, rel_bias, norm_gain, nsa_w_in, nsa_cmp_pe, nsa_cmp_w1, nsa_cmp_w2, nsa_q_gain, nsa_k_gain, nsa_w_out, sg_w_in, sg_v_gain, sg_w_s, sg_b_s, sg_w_out, gla_w_in, gla_w_gate2, gla_b_gate, gla_o_gain, gla_w_out, ffn_w_up, ffn_w_down, moe_router, moe_w_up, moe_w_down):
    bsz, s, d = x.shape
    tables = _nsa_tables(rel_bias)
    nsa_main = (NSA_KV_HEADS * NSA_GROUP + 6 * NSA_KV_HEADS) * HEAD_DIM
    gla_main = 3 * d
    nsa_in, nsa_out = _lane_pad(nsa_w_in).astype(BF16), nsa_w_out.astype(BF16)
    sg_in, sg_out = sg_w_in.astype(BF16), sg_w_out.astype(BF16)
    gla_in, gla_out = _lane_pad(gla_w_in).astype(BF16), gla_w_out.astype(BF16)
    ffn_up, ffn_down = ffn_w_up.astype(BF16), ffn_w_down.astype(BF16)
    outs = []
    for bi in range(bsz):
        h = x.reshape(s, d) if bsz == 1 else x[bi]
        for i in range(DEPTH):
            mixer = i % N_MIXERS
            j = i // N_MIXERS
            hn = _rmsnorm(h, norm_gain[i, 0])
            if mixer == 0:
                h = _nsa_mixer(h, hn, tables, j, nsa_in, nsa_w_in[j, :, nsa_main:], nsa_cmp_pe[j], nsa_cmp_w1[j],
                               nsa_cmp_w2[j],
                               nsa_q_gain[j], nsa_k_gain[j], nsa_out)
            elif mixer == 1:
                h = _sgu_mixer(h, hn, j, sg_in, sg_v_gain[j], sg_w_s[j], sg_b_s[j], sg_out)
            else:
                h = _gla_mixer(h, hn, j, gla_in, gla_w_in[j, :, gla_main:], gla_w_gate2[j], gla_b_gate[j],
                               gla_o_gain[j], gla_out)
            f = i // 2
            if i % 2 == 0:
                hn = _rmsnorm(h, norm_gain[i, 1])
                h = _dense_ffn(h, hn, f, ffn_up, ffn_down)
            else:
                h = _moe_ffn(h, norm_gain[i, 1], moe_router[f], f, moe_w_up, moe_w_down)
        outs.append(h)
    return outs[0].reshape(1, s, d) if bsz == 1 else jnp.stack(outs, axis=0)
```

```python
import math

import jax
import jax.numpy as jnp
from jax import lax
from jax.experimental import pallas as pl
from jax.experimental.pallas import tpu as pltpu

F32 = jnp.float32
BF16 = jnp.bfloat16

DEPTH = 4
N_MIXERS = 3
RMS_EPS = 1e-6
NEG = -1e30
HEAD_DIM = 128
NSA_KV_HEADS = 4
NSA_GROUP = 8
CMP_BLOCK = 32
CMP_STRIDE = 16
SEL_BLOCK = 64
SEL_TOPK = 16
WINDOW = 512
Q_BLOCK = 128
FORCED_SCORE = 1e4
TAKEN = -2.0
T5_BUCKETS = 32
T5_MAX_DISTANCE = 2048
SG_CHUNK = 128
SG_GROUPS = 32
GLA_HEADS = 4
GLA_GATE_RANK = 16
GLA_GATE_TEMP = 16.0
GLA_CHUNK = 64
GLA_SUB = 16
GLA_SAFE_EXP = 80.0
MOE_EXPERTS = 8
MOE_TM = 512

LANES = 128
SEL_TILE = 512
SEL_NEAR_TILES = 5
SEL_FAR_WIDTH = 2
CMP_NEAR_SHIFT = 12
CW_HEADS = 8
VMEM_MB = 56
LOG2E = 1.4426950408889634
SAFE_LOGIT = 60.0
SEL_PAD_BLOCKS = 8


def _cparams(sem, vmem_mb=VMEM_MB):
    return pltpu.CompilerParams(dimension_semantics=sem, vmem_limit_bytes=vmem_mb * 2**20)


def _dot(a, b):
    return jnp.dot(a, b, preferred_element_type=F32)


def _dot_nt(a, b):
    return lax.dot_general(a, b, (((1,), (1,)), ((), ())), preferred_element_type=F32)


def _rmsnorm_body(x_ref, g_ref, o_ref):
    x = x_ref[...]
    ms = jnp.mean(x * x, axis=-1, keepdims=True)
    o_ref[...] = (x * lax.rsqrt(ms + RMS_EPS) * g_ref[...]).astype(o_ref.dtype)


def _rmsnorm(x, gain, tm=256):
    m, d = x.shape
    return pl.pallas_call(
        _rmsnorm_body,
        grid=(m // tm,),
        in_specs=[pl.BlockSpec((tm, d), lambda i: (i, 0)), pl.BlockSpec((1, d), lambda i: (0, 0))],
        out_specs=pl.BlockSpec((tm, d), lambda i: (i, 0)),
        out_shape=jax.ShapeDtypeStruct((m, d), BF16),
        compiler_params=_cparams(("parallel",)),
        name="rmsnorm",
    )(x, gain.reshape(1, d))


def _norm_rows(x, gain):
    ms = jnp.mean(x * x, axis=-1, keepdims=True)
    return (x * lax.rsqrt(ms + RMS_EPS) * gain).astype(BF16)


RT_E1, RT_E2, RT_W1, RT_W2, RT_R1, RT_R2 = range(6)


def _moe_router_body(x_ref, g_ref, r_ref, info_ref, cnt_ref, base_sc):
    @pl.when(pl.program_id(0) == 0)
    def _():
        base_sc[...] = jnp.zeros_like(base_sc)

    hn = _norm_rows(x_ref[...], g_ref[...])
    logits = _dot(hn, r_ref[...])
    lane = lax.broadcasted_iota(jnp.int32, logits.shape, 1).astype(F32)
    logits = jnp.where(lane < MOE_EXPERTS, logits, NEG)
    v1 = jnp.max(logits, axis=-1, keepdims=True)
    i1 = jnp.min(jnp.where(logits == v1, lane, 1e3), axis=-1, keepdims=True)
    rest = jnp.where(lane == i1, NEG, logits)
    v2 = jnp.max(rest, axis=-1, keepdims=True)
    i2 = jnp.min(jnp.where(rest == v2, lane, 1e3), axis=-1, keepdims=True)
    e2 = jnp.exp(v2 - v1)
    den = 1.0 + e2
    hot = jnp.where(jnp.logical_or(lane == i1, lane == i2), 1.0, 0.0)
    tm = hot.shape[0]
    earlier = (lax.broadcasted_iota(jnp.int32, (tm, tm), 0) > lax.broadcasted_iota(jnp.int32, (tm, tm), 1))
    before = _dot(jnp.where(earlier, 1.0, 0.0).astype(BF16), hot.astype(BF16)) + base_sc[...]
    r1 = jnp.sum(jnp.where(lane == i1, before, 0.0), axis=-1, keepdims=True)
    r2 = jnp.sum(jnp.where(lane == i2, before, 0.0), axis=-1, keepdims=True)
    rec = jnp.zeros_like(logits)
    for pos, val in ((RT_E1, i1), (RT_E2, i2), (RT_W1, 1.0 / den), (RT_W2, e2 / den), (RT_R1, r1), (RT_R2, r2)):
        rec = jnp.where(lane == float(pos), val, rec)
    info_ref[...] = rec
    total = base_sc[...] + jnp.sum(hot, axis=0, keepdims=True)
    base_sc[...] = total
    cnt_ref[...] = jnp.broadcast_to(total, cnt_ref.shape)


def _moe_router(x, gain, router, tm=256):
    m, d = x.shape
    rpad = jnp.zeros((d, LANES), BF16).at[:, :MOE_EXPERTS].set(router.astype(BF16))
    return pl.pallas_call(
        _moe_router_body,
        grid=(m // tm,),
        in_specs=[pl.BlockSpec((tm, d), lambda i: (i, 0)), pl.BlockSpec((1, d), lambda i: (0, 0)),
                  pl.BlockSpec((d, LANES), lambda i: (0, 0))],
        out_specs=[pl.BlockSpec((tm, LANES), lambda i: (i, 0)), pl.BlockSpec((8, LANES), lambda i: (0, 0))],
        out_shape=[jax.ShapeDtypeStruct((m, LANES), F32), jax.ShapeDtypeStruct((8, LANES), F32)],
        scratch_shapes=[pltpu.VMEM((1, LANES), F32)],
        compiler_params=_cparams(("arbitrary",)),
        name="moe_router",
    )(x, gain.reshape(1, d), rpad)


def _matmul(x, w_list, epilogue, out_list, *, tm, tn, n_total, tk=None, extras=(), name="matmul",
            dimsem=("parallel", "parallel", "arbitrary")):
    m, kdim = x.shape
    tk = kdim if tk is None else tk
    nk = kdim // tk
    nw, ne, no = len(w_list), len(extras), len(out_list)

    def body(*refs):
        x_ref = refs[0]
        w_refs = refs[1:1 + nw]
        e_refs = refs[1 + nw:1 + nw + ne]
        o_refs = refs[1 + nw + ne:1 + nw + ne + no]
        acc_refs = refs[1 + nw + ne + no:]
        n = pl.program_id(0)
        if nk == 1:
            xv = x_ref[...]
            epilogue([_dot(xv, w[...].astype(xv.dtype)) for w in w_refs], e_refs, o_refs, n)
        else:
            k = pl.program_id(2)

            @pl.when(k == 0)
            def _():
                for a in acc_refs:
                    a[...] = jnp.zeros_like(a)

            xv = x_ref[...]
            for a, w in zip(acc_refs, w_refs):
                a[...] += _dot(xv, w[...])

            @pl.when(k == nk - 1)
            def _():
                epilogue([a[...] for a in acc_refs], e_refs, o_refs, n)

    in_specs = [pl.BlockSpec((tm, tk), lambda n, mi, k: (mi, k))]
    in_specs += [s for _, s in w_list] + [s for _, s in extras]
    scratch = [] if nk == 1 else [pltpu.VMEM((tm, tn), F32) for _ in range(nw)]
    return pl.pallas_call(
        body,
        grid=(n_total // tn, m // tm, nk),
        in_specs=in_specs,
        out_specs=[s for _, s in out_list],
        out_shape=[s for s, _ in out_list],
        scratch_shapes=scratch,
        compiler_params=_cparams(dimsem),
        name=name,
    )(x, *[a for a, _ in w_list], *[a for a, _ in extras])


def _wspec(tk, tn, off=0, layer=None):
    if layer is None:
        return pl.BlockSpec((tk, tn), lambda n, mi, k: (k, n + off))
    return pl.BlockSpec((None, tk, tn), lambda n, mi, k: (layer, k, n + off))


def _ospec(tm, tn):
    return pl.BlockSpec((tm, tn), lambda n, mi, k: (mi, n))


def _epi_resid(accs, e_refs, o_refs, n):
    o_refs[0][...] = e_refs[0][...] + accs[0]


def _matmul_resid(x, w, resid, *, layer=None, tm=512, tn=1024, tk=None, name="matmul_resid"):
    m, kdim = x.shape
    nn = w.shape[-1]
    tk = min(kdim, 4096) if tk is None else tk
    return _matmul(
        x, [(w, _wspec(tk, tn, 0, layer))], _epi_resid,
        [(jax.ShapeDtypeStruct((m, nn), F32), _ospec(tm, tn))],
        tm=tm, tn=tn, tk=tk, n_total=nn, extras=[(resid, _ospec(tm, tn))], name=name)[0]


def _silu(a):
    return a * (1.0 / (1.0 + jnp.exp(-a)))


def _epi_swiglu(accs, e_refs, o_refs, n):
    a, b = accs
    o_refs[0][...] = (_silu(a) * b).astype(BF16)


def _swiglu_up(x, w_up, *, layer=None, tm=512, tn=512, name="swiglu_up"):
    m, kdim = x.shape
    ff = w_up.shape[-1] // 2
    return _matmul(
        x, [(w_up, _wspec(kdim, tn, 0, layer)), (w_up, _wspec(kdim, tn, ff // tn, layer))], _epi_swiglu,
        [(jax.ShapeDtypeStruct((m, ff), BF16), _ospec(tm, tn))],
        tm=tm, tn=tn, n_total=ff, name=name)[0]


def _moe_plan(info, cnt, s):
    ne, tm = MOE_EXPERTS, MOE_TM
    e1 = info[:, RT_E1].astype(jnp.int32)
    e2 = info[:, RT_E2].astype(jnp.int32)
    counts = cnt[0, :ne].astype(jnp.int32)
    padded = (counts + tm - 1) // tm * tm
    ends = jnp.cumsum(padded)
    off = ends - padded
    dest1 = off[e1] + info[:, RT_R1].astype(jnp.int32)
    dest2 = off[e2] + info[:, RT_R2].astype(jnp.int32)
    ntile = (2 * s) // tm + ne
    tile_start = jnp.arange(ntile, dtype=jnp.int32) * tm
    tile_expert = jnp.minimum(jnp.sum(tile_start[:, None] >= ends[None, :], axis=1), ne - 1).astype(jnp.int32)
    nvalid = (ends[ne - 1:ne] // tm).astype(jnp.int32)
    tok = jnp.arange(s, dtype=jnp.int32)
    src = jnp.zeros((ntile * tm,), jnp.int32).at[dest1].set(tok).at[dest2].set(tok)
    return dest1, dest2, src, tile_expert, nvalid


def _row_copy(src_hbm, row, dst, r, sem):
    return pltpu.make_async_copy(src_hbm.at[pl.ds(row, 1)], dst.at[pl.ds(r, 1)], sem)


def _moe_gather_body(src_ref, nv_ref, h_hbm, g_ref, o_ref, buf, sems):
    t = pl.program_id(0)
    nv = nv_ref[0]
    tm = buf.shape[1]

    def rows(tile, slot, start):
        def body(r, carry):
            cp = _row_copy(h_hbm, src_ref[tile * tm + r], buf.at[slot], r, sems.at[slot])
            if start:
                cp.start()
            else:
                cp.wait()
            return carry

        lax.fori_loop(0, tm, body, 0)

    @pl.when(jnp.logical_and(t == 0, nv > 0))
    def _():
        rows(0, 0, True)

    @pl.when(t + 1 < nv)
    def _():
        rows(t + 1, (t + 1) % 2, True)

    @pl.when(t < nv)
    def _():
        rows(t, t % 2, False)
        o_ref[...] = _norm_rows(buf[t % 2], g_ref[...])

    @pl.when(t >= nv)
    def _():
        o_ref[...] = jnp.zeros_like(o_ref)


def _moe_gather(h, gain, src, nvalid):
    s, d = h.shape
    tm = MOE_TM
    ntile = src.shape[0] // tm
    return pl.pallas_call(
        _moe_gather_body,
        grid_spec=pltpu.PrefetchScalarGridSpec(
            num_scalar_prefetch=2,
            grid=(ntile,),
            in_specs=[pl.BlockSpec(memory_space=pl.ANY),
                      pl.BlockSpec((1, d), lambda t, src_, nv_: (0, 0))],
            out_specs=pl.BlockSpec((tm, d), lambda t, src_, nv_: (t, 0)),
            scratch_shapes=[pltpu.VMEM((2, tm, d), F32), pltpu.SemaphoreType.DMA((2,))]),
        out_shape=jax.ShapeDtypeStruct((ntile * tm, d), BF16),
        compiler_params=_cparams(("arbitrary",)),
        name="moe_gather",
    )(src, nvalid, h, gain.reshape(1, d))


def _new_expert(te_ref, t):
    return jnp.logical_or(t == 0, te_ref[t] != te_ref[jnp.maximum(t - 1, 0)])


def _moe_up_body(te_ref, nv_ref, x_ref, wa_ref, wb_ref, o_ref, wa_sc, wb_sc):
    t = pl.program_id(1)
    live = t < nv_ref[0]

    @pl.when(jnp.logical_and(live, _new_expert(te_ref, t)))
    def _():
        wa_sc[...] = wa_ref[...].astype(BF16)
        wb_sc[...] = wb_ref[...].astype(BF16)

    @pl.when(live)
    def _():
        x = x_ref[...]
        o_ref[...] = (_silu(_dot(x, wa_sc[...])) * _dot(x, wb_sc[...])).astype(BF16)

    @pl.when(jnp.logical_not(live))
    def _():
        o_ref[...] = jnp.zeros_like(o_ref)


def _moe_up(xs, w_up, layer, tile_expert, nvalid, *, tn=512):
    p, d = xs.shape
    tm = MOE_TM
    ff = w_up.shape[-1] // 2
    nn = ff // tn
    return pl.pallas_call(
        _moe_up_body,
        grid_spec=pltpu.PrefetchScalarGridSpec(
            num_scalar_prefetch=2,
            grid=(nn, p // tm),
            in_specs=[pl.BlockSpec((tm, d), lambda n, t, te, nv: (t, 0)),
                      pl.BlockSpec((None, None, d, tn), lambda n, t, te, nv: (layer, te[t], 0, n)),
                      pl.BlockSpec((None, None, d, tn), lambda n, t, te, nv: (layer, te[t], 0, n + nn))],
            out_specs=pl.BlockSpec((tm, tn), lambda n, t, te, nv: (t, n)),
            scratch_shapes=[pltpu.VMEM((d, tn), BF16), pltpu.VMEM((d, tn), BF16)]),
        out_shape=jax.ShapeDtypeStruct((p, ff), BF16),
        compiler_params=_cparams(("arbitrary", "arbitrary")),
        name="moe_up",
    )(tile_expert, nvalid, xs, w_up, w_up)


def _moe_down_body(te_ref, nv_ref, x_ref, w_ref, o_ref, w_sc):
    t = pl.program_id(1)
    live = t < nv_ref[0]

    @pl.when(jnp.logical_and(live, _new_expert(te_ref, t)))
    def _():
        w_sc[...] = w_ref[...].astype(BF16)

    @pl.when(live)
    def _():
        o_ref[...] = _dot(x_ref[...], w_sc[...])

    @pl.when(jnp.logical_not(live))
    def _():
        o_ref[...] = jnp.zeros_like(o_ref)


def _moe_down(act, w_down, layer, tile_expert, nvalid, *, tn=1024):
    p, ff = act.shape
    tm = MOE_TM
    d = w_down.shape[-1]
    return pl.pallas_call(
        _moe_down_body,
        grid_spec=pltpu.PrefetchScalarGridSpec(
            num_scalar_prefetch=2,
            grid=(d // tn, p // tm),
            in_specs=[pl.BlockSpec((tm, ff), lambda n, t, te, nv: (t, 0)),
                      pl.BlockSpec((None, None, ff, tn), lambda n, t, te, nv: (layer, te[t], 0, n))],
            out_specs=pl.BlockSpec((tm, tn), lambda n, t, te, nv: (t, n)),
            scratch_shapes=[pltpu.VMEM((ff, tn), BF16)]),
        out_shape=jax.ShapeDtypeStruct((p, d), F32),
        compiler_params=_cparams(("arbitrary", "arbitrary")),
        name="moe_down",
    )(tile_expert, nvalid, act, w_down)


def _moe_combine_body(d1_ref, d2_ref, y_hbm, h_ref, info_ref, o_ref, buf, sems):
    t = pl.program_id(0)
    nt = pl.num_programs(0)
    tm = h_ref.shape[0]

    def rows(tile, slot, start):
        def body(r, carry):
            for k, dest in enumerate((d1_ref, d2_ref)):
                cp = _row_copy(y_hbm, dest[tile * tm + r], buf.at[2 * slot + k], r, sems.at[2 * slot + k])
                if start:
                    cp.start()
                else:
                    cp.wait()
            return carry

        lax.fori_loop(0, tm, body, 0)

    @pl.when(t == 0)
    def _():
        rows(0, 0, True)

    @pl.when(t + 1 < nt)
    def _():
        rows(t + 1, (t + 1) % 2, True)

    slot = t % 2
    rows(t, slot, False)
    info = info_ref[...]
    o_ref[...] = h_ref[...] + (info[:, RT_W1:RT_W1 + 1] * buf[2 * slot]
                               + info[:, RT_W2:RT_W2 + 1] * buf[2 * slot + 1])


def _moe_combine(h, y, info, dest1, dest2, tm=256):
    s, d = h.shape
    return pl.pallas_call(
        _moe_combine_body,
        grid_spec=pltpu.PrefetchScalarGridSpec(
            num_scalar_prefetch=2,
            grid=(s // tm,),
            in_specs=[pl.BlockSpec(memory_space=pl.ANY),
                      pl.BlockSpec((tm, d), lambda t, a, b: (t, 0)),
                      pl.BlockSpec((tm, LANES), lambda t, a, b: (t, 0))],
            out_specs=pl.BlockSpec((tm, d), lambda t, a, b: (t, 0)),
            scratch_shapes=[pltpu.VMEM((4, tm, d), F32), pltpu.SemaphoreType.DMA((4,))]),
        out_shape=jax.ShapeDtypeStruct((s, d), F32),
        compiler_params=_cparams(("arbitrary",)),
        name="moe_combine",
    )(dest1, dest2, y, h, info)


def _nsa_q_proj(hn, wq, layer, q_gain, nn, *, tm=512, tn=1024):
    m, kdim = hn.shape
    scale = HEAD_DIM ** -0.5 * LOG2E

    def epi(accs, e_refs, o_refs, n):
        acc = accs[0]
        gain = e_refs[0][...]
        for c in range(tn // HEAD_DIM):
            a = acc[:, c * HEAD_DIM:(c + 1) * HEAD_DIM]
            ms = jnp.mean(a * a, axis=-1, keepdims=True)
            o_refs[0][:, c * HEAD_DIM:(c + 1) * HEAD_DIM] = (
                a * lax.rsqrt(ms + RMS_EPS) * gain * scale).astype(BF16)

    return _matmul(
        hn, [(wq, _wspec(kdim, tn, 0, layer))], epi,
        [(jax.ShapeDtypeStruct((m, nn), BF16), _ospec(tm, tn))],
        tm=tm, tn=tn, n_total=nn,
        extras=[(q_gain.reshape(1, HEAD_DIM), pl.BlockSpec((1, HEAD_DIM), lambda n, mi, k: (0, 0)))],
        name="nsa_q_proj")[0]


def _nsa_kv_proj(hn, wkv, layer, k_gain, col0, *, tm=512):
    m, kdim = hn.shape
    g_ = NSA_KV_HEADS
    tn = g_ * HEAD_DIM
    spt = tm // LANES
    gains = jnp.ones((6, 1, HEAD_DIM), F32).at[2, 0].set(k_gain[1]).at[4, 0].set(k_gain[2])

    def epi(accs, e_refs, o_refs, n):
        acc = accs[0]
        kg = e_refs[0][0]
        do_norm = jnp.logical_or(n == 2, n == 4)
        for g in range(g_):
            a = acc[:, g * HEAD_DIM:(g + 1) * HEAD_DIM]
            ms = jnp.mean(a * a, axis=-1, keepdims=True)
            an = a * lax.rsqrt(ms + RMS_EPS) * kg
            o_refs[0][g] = jnp.where(do_norm, an, a).astype(BF16)
        is_v = jnp.logical_or(n == 3, n == 5)

        @pl.when(is_v)
        def _():
            for g in range(g_):
                for c in range(spt):
                    blk = acc[c * LANES:(c + 1) * LANES, g * HEAD_DIM:(g + 1) * HEAD_DIM]
                    o_refs[1][g, c] = jnp.transpose(blk).astype(BF16)

        @pl.when(jnp.logical_not(is_v))
        def _():
            o_refs[1][...] = jnp.zeros_like(o_refs[1])

    return _matmul(
        hn, [(wkv, _wspec(kdim, tn, col0 // tn, layer))], epi,
        [(jax.ShapeDtypeStruct((6 * g_, m, HEAD_DIM), BF16),
          pl.BlockSpec((g_, tm, HEAD_DIM), lambda n, mi, k: (n, mi, 0))),
         (jax.ShapeDtypeStruct((6, g_, m // LANES, HEAD_DIM, LANES), BF16),
          pl.BlockSpec((None, g_, spt, HEAD_DIM, LANES), lambda n, mi, k: (n, 0, mi, 0, 0)))],
        tm=tm, tn=tn, n_total=6 * tn,
        extras=[(gains, pl.BlockSpec((1, 1, HEAD_DIM), lambda n, mi, k: (n, 0, 0)))],
        name="nsa_kv_proj")


def _nsa_gate_proj(hn, wg, *, tm=512):
    m, kdim = hn.shape
    nn = wg.shape[1]

    def epi(accs, e_refs, o_refs, n):
        o_refs[0][...] = 1.0 / (1.0 + jnp.exp(-accs[0]))

    return _matmul(
        hn, [(wg, _wspec(kdim, nn))], epi,
        [(jax.ShapeDtypeStruct((m, nn), F32), _ospec(tm, nn))],
        tm=tm, tn=nn, n_total=nn, name="nsa_gate_proj")[0]


def _compress_body(r_ref, pe_ref, w1_ref, w2_ref, kg_ref, o_ref, ot_ref):
    j = pl.program_id(0)
    r = r_ref[0].astype(F32)
    pe = pe_ref[0]
    half = r.shape[1]
    top = (r + pe[0:1]).astype(BF16)
    bot = (r + pe[1:2]).astype(BF16)
    a = _dot(top, w1_ref[0, 0:half, :])
    b = _dot(bot, w1_ref[0, half:2 * half, :])
    nrow = r.shape[0]
    hid = a + pltpu.roll(b, nrow - 1, 0)
    y = _dot(jax.nn.gelu(hid).astype(BF16), w2_ref[0])
    ms = jnp.mean(y * y, axis=-1, keepdims=True)
    yn = y * lax.rsqrt(ms + RMS_EPS) * kg_ref[...]
    out = jnp.where(j == 0, yn, y)
    o_ref[0, 0] = out.astype(BF16)
    ot_ref[0, 0] = jnp.transpose(out).astype(BF16)


def _compress(kv, pe, w1, w2, k_gain0):
    g_ = NSA_KV_HEADS
    s = kv.shape[1]
    nrow = s // CMP_STRIDE
    wid = CMP_STRIDE * HEAD_DIM
    r = kv[:2 * g_].reshape(2 * g_, nrow, wid)
    pe2 = pe.reshape(2, 2, wid)
    return pl.pallas_call(
        _compress_body,
        grid=(2, g_),
        in_specs=[pl.BlockSpec((1, nrow, wid), lambda j, g: (j * g_ + g, 0, 0)),
                  pl.BlockSpec((1, 2, wid), lambda j, g: (j, 0, 0)),
                  pl.BlockSpec((1, 2 * wid, HEAD_DIM), lambda j, g: (j, 0, 0)),
                  pl.BlockSpec((1, HEAD_DIM, HEAD_DIM), lambda j, g: (j, 0, 0)),
                  pl.BlockSpec((1, HEAD_DIM), lambda j, g: (0, 0))],
        out_specs=[pl.BlockSpec((1, 1, nrow, HEAD_DIM), lambda j, g: (j, g, 0, 0)),
                   pl.BlockSpec((1, 1, HEAD_DIM, nrow), lambda j, g: (j, g, 0, 0))],
        out_shape=[jax.ShapeDtypeStruct((2, g_, nrow, HEAD_DIM), BF16),
                   jax.ShapeDtypeStruct((2, g_, HEAD_DIM, nrow), BF16)],
        compiler_params=_cparams(("parallel", "parallel")),
        name="nsa_compress",
    )(r, pe2, w1.astype(BF16), w2.astype(BF16), k_gain0.reshape(1, HEAD_DIM))


def _t5_bucket(dist):
    dist = jnp.maximum(dist, 0)
    max_exact = T5_BUCKETS // 2
    d_f = jnp.maximum(dist, 1).astype(F32)
    log_b = max_exact + (jnp.log(d_f / max_exact) / math.log(T5_MAX_DISTANCE / max_exact)
                         * (T5_BUCKETS - max_exact)).astype(jnp.int32)
    log_b = jnp.minimum(log_b, T5_BUCKETS - 1)
    return jnp.where(dist < max_exact, dist, log_b)


def _bias_table(tbl, dist, valid, shift):
    onehot = jax.nn.one_hot(_t5_bucket(dist).reshape(-1), T5_BUCKETS, dtype=F32)
    t = tbl - tbl[T5_BUCKETS - 1:T5_BUCKETS] if shift else tbl
    vals = jnp.einsum("nb,bh->hn", onehot, t, precision=lax.Precision.HIGHEST)
    vals = vals.reshape((tbl.shape[1],) + dist.shape) * LOG2E
    return jnp.where(valid[None], vals, NEG)


def _nsa_tables(rel_bias):
    tbl = rel_bias.astype(F32)
    r = jnp.arange(Q_BLOCK)
    jw = jnp.arange(WINDOW + Q_BLOCK)
    dw = WINDOW + r[None, :] - jw[:, None]
    tab_w = _bias_table(tbl, dw, (dw >= 0) & (dw < WINDOW), False)
    a = jnp.arange(16)
    cc = jnp.arange(2 * LANES)
    dc = (CMP_NEAR_SHIFT * Q_BLOCK + Q_BLOCK * a[:, None, None] + r[None, None, :]
          - CMP_STRIDE * cc[None, :, None] - (CMP_BLOCK - 1))
    tab_c = _bias_table(tbl, dc, dc >= 0, True)
    jj = jnp.arange(SEL_NEAR_TILES)
    col = jnp.arange(SEL_TILE)
    ds_ = r[None, None, :] - Q_BLOCK + SEL_TILE * (jj[:, None, None] + 1) - col[None, :, None]
    tab_s = _bias_table(tbl, ds_, ds_ >= 0, True)
    return tab_w, tab_c, tab_s, tbl


def _sel_to_cmp(nsel, nc_pad):
    c_start = jnp.arange(nc_pad) * CMP_STRIDE
    s_start = jnp.arange(nsel) * SEL_BLOCK
    overlap = jnp.clip(jnp.minimum(c_start[None, :] + CMP_BLOCK, s_start[:, None] + SEL_BLOCK)
                       - jnp.maximum(c_start[None, :], s_start[:, None]), 0)
    return (overlap.astype(F32) / CMP_BLOCK).astype(BF16)


def _heads_t(q_ref, h0, nh):
    cols = [jnp.transpose(q_ref[:, (h0 + h) * HEAD_DIM:(h0 + h + 1) * HEAD_DIM].astype(F32)) for h in range(nh)]
    return jnp.concatenate(cols, axis=1).astype(BF16)


def _softmax_keys(s, bounded):
    if bounded:
        p = jnp.exp2(s)
        l = jnp.sum(p, axis=0, keepdims=True)
        return p, jnp.where(l > 0.0, 1.0 / l, 0.0)
    m = jnp.max(s, axis=0, keepdims=True)
    p = jnp.exp2(s - m)
    l = jnp.sum(p, axis=0, keepdims=True)
    return p, jnp.where(m > 0.5 * NEG, 1.0 / l, 0.0)


def _logit_bound(rel_bias, q_gain, k_gain):
    qk = HEAD_DIM * (HEAD_DIM ** -0.5 * LOG2E) * jnp.max(jnp.abs(q_gain)) * jnp.max(jnp.abs(k_gain))
    return (1.02 * qk + 2.0 * LOG2E * jnp.max(jnp.abs(rel_bias))).astype(F32).reshape(1)


def _nsa_cw_body(bnd_ref, q_ref, gt_ref, kc_ref, vct_ref, mselt_ref, tabc_ref, kw_ref, vwt_ref, tabw_ref,
                 ocwt_ref, negt_ref):
    bounded = bnd_ref[0] < SAFE_LOGIT
    i = pl.program_id(1)

    @pl.when(bounded)
    def _():
        _nsa_cw_branches(True, i, q_ref, gt_ref, kc_ref, vct_ref, mselt_ref, tabc_ref, kw_ref, vwt_ref, tabw_ref,
                         ocwt_ref, negt_ref)

    @pl.when(jnp.logical_not(bounded))
    def _():
        _nsa_cw_branches(False, i, q_ref, gt_ref, kc_ref, vct_ref, mselt_ref, tabc_ref, kw_ref, vwt_ref, tabw_ref,
                         ocwt_ref, negt_ref)


def _nsa_cw_branches(bounded, i, q_ref, gt_ref, kc_ref, vct_ref, mselt_ref, tabc_ref, kw_ref, vwt_ref, tabw_ref,
                     ocwt_ref, negt_ref):
    f = (i + 16 - CMP_NEAR_SHIFT) // 16 - 1
    kc = kc_ref[0, 0]
    vct = vct_ref[0, 0]
    nc = kc.shape[0]
    nsel = mselt_ref.shape[0]
    wlen = WINDOW + Q_BLOCK
    nslab = wlen // LANES
    start = pl.multiple_of(i * Q_BLOCK, Q_BLOCK)
    kw = kw_ref[0, pl.ds(start, wlen), :]
    vwt = jnp.concatenate([vwt_ref[0, jnp.maximum(i + c - WINDOW // LANES, 0)] for c in range(nslab)],
                          axis=1)
    roww = lax.broadcasted_iota(jnp.int32, (wlen, LANES), 0)
    w_pad = roww < (WINDOW - Q_BLOCK * i)
    gt_t = jnp.transpose(gt_ref[...])
    psum = jnp.zeros((nc, Q_BLOCK), F32)
    hg = CW_HEADS
    for h0 in range(0, NSA_GROUP, hg):
        qt = _heads_t(q_ref, h0, hg)
        s = _dot(kc, qt)
        ta = jnp.concatenate([tabc_ref[h0 + h, 0, 0:LANES, :] for h in range(hg)], axis=1)
        tb = jnp.concatenate([tabc_ref[h0 + h, 0, LANES:2 * LANES, :] for h in range(hg)], axis=1)
        pieces = []
        for ch in range(nc // LANES):
            rest = jnp.where(ch > f + 1, NEG, 0.0)
            bias = jnp.where(ch == f, ta, jnp.where(ch == f + 1, tb, rest))
            pieces.append(s[ch * LANES:(ch + 1) * LANES] + bias)
        s = jnp.concatenate(pieces, axis=0)
        p, linv = _softmax_keys(s, bounded)
        pn = p * linv
        for h in range(hg):
            psum = psum + pn[:, h * Q_BLOCK:(h + 1) * Q_BLOCK]
        oc = _dot(vct, pn.astype(BF16))
        tw = jnp.concatenate([jnp.where(w_pad, NEG, tabw_ref[h0 + h]) for h in range(hg)], axis=1)
        sw = _dot(kw, qt) + tw
        pw, lwinv = _softmax_keys(sw, bounded)
        ow = _dot(vwt, pw.astype(BF16)) * lwinv
        for h in range(hg):
            hh = h0 + h
            cs = slice(h * Q_BLOCK, (h + 1) * Q_BLOCK)
            ocwt_ref[0, 0, hh * HEAD_DIM:(hh + 1) * HEAD_DIM, :] = (
                gt_t[hh:hh + 1] * oc[:, cs] + gt_t[16 + hh:17 + hh] * ow[:, cs])
    p_hi = psum.astype(BF16)
    p_lo = (psum - p_hi.astype(F32)).astype(BF16)
    mselt = mselt_ref[...]
    imp = _dot(mselt, p_hi) + _dot(mselt, p_lo)
    blk = lax.broadcasted_iota(jnp.int32, (nsel, Q_BLOCK), 0).astype(F32)
    qpos = lax.broadcasted_iota(jnp.int32, (nsel, Q_BLOCK), 1)
    cur = (i * (Q_BLOCK // SEL_BLOCK)).astype(F32) + jnp.where(qpos >= SEL_BLOCK, 1.0, 0.0)
    forced = jnp.logical_or(blk == cur, blk == 0.0)
    val = jnp.where(forced, FORCED_SCORE, jnp.where(blk <= cur, imp, -1.0))
    for _ in range(SEL_TOPK):
        mx = jnp.max(val, axis=0, keepdims=True)
        first = jnp.min(jnp.where(val == mx, blk, 1e4), axis=0, keepdims=True)
        val = jnp.where(blk == first, TAKEN, val)
    neg = jnp.where(val == TAKEN, 0.0, NEG)
    for pr in range(SEL_PAD_BLOCKS // 2):
        negt_ref[0, 0, pr] = jnp.full((2, Q_BLOCK), NEG, F32)
    for pr in range(nsel // 2):
        negt_ref[0, 0, SEL_PAD_BLOCKS // 2 + pr] = neg[2 * pr:2 * pr + 2]


def _nsa_cw(bound, q, gates, kc, vct, mselt, tab_c, kw, vt, tab_w):
    s, hd = q.shape
    g_ = NSA_KV_HEADS
    nq = s // Q_BLOCK
    gw = NSA_GROUP * HEAD_DIM
    nc = kc.shape[2]
    nsel = mselt.shape[0]
    spad = kw.shape[1]
    nslab = vt.shape[2]
    wlen = WINDOW + Q_BLOCK

    def var(i):
        return (i + 16 - CMP_NEAR_SHIFT) % 16

    return pl.pallas_call(
        _nsa_cw_body,
        grid=(g_, nq),
        in_specs=[
            pl.BlockSpec(memory_space=pltpu.SMEM),
            pl.BlockSpec((Q_BLOCK, gw), lambda g, i: (i, g)),
            pl.BlockSpec((Q_BLOCK, LANES), lambda g, i: (i, g)),
            pl.BlockSpec((1, 1, nc, HEAD_DIM), lambda g, i: (0, g, 0, 0)),
            pl.BlockSpec((1, 1, HEAD_DIM, nc), lambda g, i: (1, g, 0, 0)),
            pl.BlockSpec((nsel, nc), lambda g, i: (0, 0)),
            pl.BlockSpec((NSA_GROUP, 1, 2 * LANES, Q_BLOCK), lambda g, i: (g, var(i), 0, 0)),
            pl.BlockSpec((1, spad, HEAD_DIM), lambda g, i: (g, 0, 0)),
            pl.BlockSpec((None, 1, nslab, HEAD_DIM, LANES), lambda g, i: (5, g, 0, 0, 0)),
            pl.BlockSpec((NSA_GROUP, wlen, Q_BLOCK), lambda g, i: (g, 0, 0)),
        ],
        out_specs=[pl.BlockSpec((1, 1, gw, Q_BLOCK), lambda g, i: (g, i, 0, 0)),
                   pl.BlockSpec((1, 1, (SEL_PAD_BLOCKS + nsel) // 2, 2, Q_BLOCK), lambda g, i: (g, i, 0, 0, 0))],
        out_shape=[jax.ShapeDtypeStruct((g_, nq, gw, Q_BLOCK), F32),
                   jax.ShapeDtypeStruct((g_, nq, (SEL_PAD_BLOCKS + nsel) // 2, 2, Q_BLOCK), F32)],
        compiler_params=_cparams(("parallel", "arbitrary")),
        name="nsa_cmp_win",
    )(bound, q, gates, kc, vct, mselt, tab_c, kw, vt, tab_w)


def _nsa_sel_body(bnd_ref, q_ref, gt_ref, ks_ref, vst_ref, negp_ref, tabs_ref, ocwt_ref, o_ref,
                  m_sc, l_sc, acc_sc):
    i = pl.program_id(1)
    ntile = i // (SEL_TILE // Q_BLOCK) + 1
    nh = NSA_GROUP
    spt = SEL_TILE // LANES
    ppt = SEL_TILE // (2 * SEL_BLOCK)
    qt = _heads_t(q_ref, 0, nh)

    def tile(jj, near, bounded, nsub=1):
        slab0 = (i + 1) - spt * jj
        row0 = pl.multiple_of(slab0 * LANES, LANES)
        kt = ks_ref[0, pl.ds(row0, nsub * SEL_TILE), :]
        vtt = jnp.concatenate([vst_ref[0, jnp.maximum(slab0 - spt + c, 0)] for c in range(nsub * spt)],
                              axis=1)
        rows = []
        for c in range(nsub * ppt):
            pair = negp_ref[0, 0, slab0 + c]
            rows += [jnp.broadcast_to(pair[r:r + 1], (SEL_BLOCK, Q_BLOCK)) for r in range(2)]
        mk = jnp.concatenate(rows, axis=0)
        if near:
            bias = jnp.concatenate(
                [jnp.concatenate([tabs_ref[h, jj - c] for c in range(nsub)], axis=0) + mk for h in range(nh)],
                axis=1)
        else:
            bias = jnp.tile(mk, (1, nh))
        sc = _dot(kt, qt) + bias
        if bounded:
            p = jnp.exp2(sc)
            l_sc[...] = l_sc[...] + jnp.sum(p, axis=0, keepdims=True)
            acc_sc[...] = acc_sc[...] + _dot(vtt, p.astype(BF16))
        else:
            m_old = m_sc[...]
            m_new = jnp.maximum(m_old, jnp.max(sc, axis=0, keepdims=True))
            alpha = jnp.exp2(m_old - m_new)
            p = jnp.exp2(sc - m_new)
            l_sc[...] = alpha * l_sc[...] + jnp.sum(p, axis=0, keepdims=True)
            acc_sc[...] = alpha * acc_sc[...] + _dot(vtt, p.astype(BF16))
            m_sc[...] = m_new

    def run(bounded):
        m_sc[...] = jnp.full(m_sc.shape, NEG, F32)
        l_sc[...] = jnp.zeros(l_sc.shape, F32)
        acc_sc[...] = jnp.zeros(acc_sc.shape, F32)
        @pl.when(ntile >= SEL_NEAR_TILES)
        def _():
            for jj in range(SEL_FAR_WIDTH - 1, SEL_NEAR_TILES, SEL_FAR_WIDTH):
                tile(jj, True, bounded, SEL_FAR_WIDTH)
            for jj in range(SEL_NEAR_TILES - SEL_NEAR_TILES % SEL_FAR_WIDTH, SEL_NEAR_TILES):
                tile(jj, True, bounded)

        @pl.when(ntile < SEL_NEAR_TILES)
        def _():
            for jj in range(SEL_NEAR_TILES - 1):
                @pl.when(jj < ntile)
                def _():
                    tile(jj, True, bounded)

        nfar = jnp.maximum(ntile - SEL_NEAR_TILES, 0)

        def far(k, carry):
            tile(SEL_NEAR_TILES + SEL_FAR_WIDTH * k + SEL_FAR_WIDTH - 1, False, bounded, SEL_FAR_WIDTH)
            return carry

        lax.fori_loop(0, nfar // SEL_FAR_WIDTH, far, 0)

        def far_rest(jj, carry):
            tile(jj, False, bounded)
            return carry

        lax.fori_loop(SEL_NEAR_TILES + nfar // SEL_FAR_WIDTH * SEL_FAR_WIDTH, ntile, far_rest, 0)

    bounded = bnd_ref[0] < SAFE_LOGIT

    @pl.when(bounded)
    def _():
        run(True)

    @pl.when(jnp.logical_not(bounded))
    def _():
        run(False)

    o_t = acc_sc[...] * (1.0 / l_sc[...])
    gt_t = jnp.transpose(gt_ref[...])
    for h in range(nh):
        hs = slice(h * HEAD_DIM, (h + 1) * HEAD_DIM)
        oh = ocwt_ref[0, 0, hs, :] + gt_t[8 + h:9 + h] * o_t[:, h * Q_BLOCK:(h + 1) * Q_BLOCK]
        o_ref[:, hs] = jnp.transpose(oh).astype(BF16)


def _nsa_sel(bound, q, gates, ks, vt, negp, tab_s, ocwt):
    s, hd = q.shape
    g_ = NSA_KV_HEADS
    nq = s // Q_BLOCK
    gw = NSA_GROUP * HEAD_DIM
    spad = ks.shape[1]
    nslab = vt.shape[2]
    npair = negp.shape[2]
    return pl.pallas_call(
        _nsa_sel_body,
        grid=(g_, nq),
        in_specs=[
            pl.BlockSpec(memory_space=pltpu.SMEM),
            pl.BlockSpec((Q_BLOCK, gw), lambda g, i: (i, g)),
            pl.BlockSpec((Q_BLOCK, LANES), lambda g, i: (i, g)),
            pl.BlockSpec((1, spad, HEAD_DIM), lambda g, i: (g, 0, 0)),
            pl.BlockSpec((None, 1, nslab, HEAD_DIM, LANES), lambda g, i: (3, g, 0, 0, 0)),
            pl.BlockSpec((1, 1, npair, 2, Q_BLOCK), lambda g, i: (g, i, 0, 0, 0)),
            pl.BlockSpec((NSA_GROUP, SEL_NEAR_TILES, SEL_TILE, Q_BLOCK), lambda g, i: (g, 0, 0, 0)),
            pl.BlockSpec((1, 1, gw, Q_BLOCK), lambda g, i: (g, i, 0, 0)),
        ],
        out_specs=pl.BlockSpec((Q_BLOCK, gw), lambda g, i: (i, g)),
        out_shape=jax.ShapeDtypeStruct((s, hd), BF16),
        scratch_shapes=[pltpu.VMEM((1, NSA_GROUP * Q_BLOCK), F32), pltpu.VMEM((1, NSA_GROUP * Q_BLOCK), F32),
                        pltpu.VMEM((HEAD_DIM, NSA_GROUP * Q_BLOCK), F32)],
        compiler_params=_cparams(("parallel", "arbitrary")),
        name="nsa_selected",
    )(bound, q, gates, ks, vt, negp, tab_s, ocwt)


def _pad_keys(x, pad):
    return jnp.pad(x, ((0, 0), (pad, 0), (0, 0)))


def _nsa_mixer(h, hn, tables, layer, w_in, w_gate, cmp_pe, cmp_w1, cmp_w2, q_gain, k_gain, w_out):
    s, d = h.shape
    g_, hpg, dh = NSA_KV_HEADS, NSA_GROUP, HEAD_DIM
    hd = g_ * hpg * dh
    tab_w, tab_c, tab_s, tbl = tables
    q = _nsa_q_proj(hn, w_in, layer, q_gain, hd)
    kv, vt = _nsa_kv_proj(hn, w_in, layer, k_gain, hd)
    wg = w_gate.reshape(d, g_, hpg, 3).transpose(0, 1, 3, 2).reshape(d, g_, 3 * hpg)
    wg = jnp.pad(wg, ((0, 0), (0, 0), (0, LANES - 3 * hpg))).reshape(d, g_ * LANES)
    gates = _nsa_gate_proj(hn, wg)
    kc, kct = _compress(kv, cmp_pe, cmp_w1, cmp_w2, k_gain[0])
    mselt = _sel_to_cmp(s // SEL_BLOCK, s // CMP_STRIDE)
    ks = _pad_keys(kv[2 * g_:3 * g_], SEL_TILE)
    kw = _pad_keys(kv[4 * g_:5 * g_], WINDOW)
    bound = _logit_bound(tbl, q_gain, k_gain)
    ocwt, negp = _nsa_cw(bound, q, gates, kc, kct, mselt, tab_c, kw, vt, tab_w)
    o = _nsa_sel(bound, q, gates, ks, vt, negp, tab_s, ocwt)
    return _matmul_resid(o, w_out, h, layer=layer, name="nsa_out")


def _gelu_proj(hn, w, layer, *, tm=512, tn=1024):
    m, kdim = hn.shape
    nn = w.shape[-1]

    def epi(accs, e_refs, o_refs, n):
        o_refs[0][...] = jax.nn.gelu(accs[0]).astype(BF16)

    return _matmul(
        hn, [(w, _wspec(kdim, tn, 0, layer))], epi,
        [(jax.ShapeDtypeStruct((m, nn), BF16), _ospec(tm, tn))],
        tm=tm, tn=tn, n_total=nn, name="sgu_in")[0]


def _sgu_mix_body(u_ref, v_ref, gain_ref, w_ref, bt_ref, o_ref):
    vn = _norm_rows(v_ref[...].astype(F32), gain_ref[...])
    t = w_ref.shape[1]
    causal = (lax.broadcasted_iota(jnp.int32, (t, t), 0) >= lax.broadcasted_iota(jnp.int32, (t, t), 1))
    bt = bt_ref[...]
    gd = vn.shape[1] // SG_GROUPS
    for g in range(SG_GROUPS):
        gs = slice(g * gd, (g + 1) * gd)
        w = jnp.where(causal, w_ref[g], 0.0).astype(BF16)
        mixed = _dot(w, vn[:, gs]) + bt[:, g:g + 1]
        o_ref[:, gs] = (u_ref[:, gs].astype(F32) * mixed).astype(BF16)


def _sgu_mix(uv, v_gain, w_s, b_s):
    s = uv.shape[0]
    wd = uv.shape[1] // 2
    t = SG_CHUNK
    return pl.pallas_call(
        _sgu_mix_body,
        grid=(s // t,),
        in_specs=[pl.BlockSpec((t, wd), lambda c: (c, 0)),
                  pl.BlockSpec((t, wd), lambda c: (c, 1)),
                  pl.BlockSpec((1, wd), lambda c: (0, 0)),
                  pl.BlockSpec((SG_GROUPS, t, t), lambda c: (0, 0, 0)),
                  pl.BlockSpec((t, SG_GROUPS), lambda c: (0, 0))],
        out_specs=pl.BlockSpec((t, wd), lambda c: (c, 0)),
        out_shape=jax.ShapeDtypeStruct((s, wd), BF16),
        compiler_params=_cparams(("parallel",)),
        name="sgu_mix",
    )(uv, uv, v_gain.reshape(1, wd), w_s, b_s.T)


def _sgu_mixer(h, hn, layer, w_in, v_gain, w_s, b_s, w_out):
    uv = _gelu_proj(hn, w_in, layer)
    y = _sgu_mix(uv, v_gain, w_s, b_s)
    return _matmul_resid(y, w_out, h, layer=layer, name="sgu_out")


def _gla_body(q_ref, k_ref, v_ref, r_ref, g1_ref, wg_ref, bg_ref, og_ref, o_ref, state_sc, diag_sc):
    @pl.when(pl.program_id(0) == 0)
    def _():
        state_sc[...] = jnp.zeros_like(state_sc)

    nh = state_sc.shape[0]
    dk = q_ref.shape[1] // nh
    dv = v_ref.shape[1] // nh
    g1 = g1_ref[...].astype(BF16)
    for hh in range(nh):
        ks = slice(hh * dk, (hh + 1) * dk)
        vs = slice(hh * dv, (hh + 1) * dv)
        o_ref[:, vs] = _gla_head(q_ref[:, ks], k_ref[:, ks], v_ref[:, vs], r_ref[:, vs], g1, wg_ref[:, ks],
                                 bg_ref[:, ks], og_ref[...], state_sc.at[hh], diag_sc.at[hh])


def _gla_head(q_in, k_in, v, r_in, g1, wg, bg, og, state_sc, diag_sc):
    ch, dk = q_in.shape
    x = _dot(g1, wg) + bg
    log_a = (jnp.minimum(x, 0.0) - jnp.log(1.0 + jnp.exp(-jnp.abs(x)))) * (1.0 / GLA_GATE_TEMP)
    tri = (lax.broadcasted_iota(jnp.int32, (ch, ch), 0)
           >= lax.broadcasted_iota(jnp.int32, (ch, ch), 1))
    tri_b = jnp.where(tri, 1.0, 0.0).astype(BF16)
    a_hi = log_a.astype(BF16)
    a_lo = (log_a - a_hi.astype(F32)).astype(BF16)
    b = _dot(tri_b, a_hi) + _dot(tri_b, a_lo)
    q = q_in.astype(F32) * (dk ** -0.5)
    k = k_in.astype(F32)
    state = state_sc[...]
    o = _dot((q * jnp.exp(b)).astype(BF16), state.astype(BF16))
    nsub = ch // GLA_SUB
    ends = [jnp.broadcast_to(b[(jb + 1) * GLA_SUB - 1:(jb + 1) * GLA_SUB], (GLA_SUB, dk))
            for jb in range(nsub)]
    b_end = jnp.concatenate(ends, axis=0)
    b_start = jnp.concatenate([jnp.zeros((GLA_SUB, dk), F32)] + ends[:-1], axis=0)
    rblk = lax.broadcasted_iota(jnp.int32, (ch, ch), 0) // GLA_SUB
    cblk = lax.broadcasted_iota(jnp.int32, (ch, ch), 1) // GLA_SUB
    zero_row = jnp.zeros((1, dk), F32)
    spread = jnp.max(jnp.concatenate(
        [(ends[jb - 1][0:1] if jb else zero_row) - ends[jb][0:1] for jb in range(nsub)], axis=0))

    @pl.when(spread < GLA_SAFE_EXP)
    def _():
        q_diag = (q * jnp.exp(b - b_start)).astype(BF16)
        k_diag = (k * jnp.exp(b_start - b)).astype(BF16)
        diag_sc[...] = _dot_nt(q_diag, k_diag)

    @pl.when(spread >= GLA_SAFE_EXP)
    def _():
        lane = lax.broadcasted_iota(jnp.int32, (GLA_SUB, ch), 1)
        strips = []
        for ib in range(nsub):
            rs = slice(ib * GLA_SUB, (ib + 1) * GLA_SUB)
            qi, ki, bi = q[rs], k[rs], b[rs]
            strip = jnp.zeros((GLA_SUB, ch), F32)
            for j in range(GLA_SUB):
                decay = jnp.exp(jnp.minimum(bi - bi[j:j + 1], 0.0))
                col = jnp.sum(qi * ki[j:j + 1] * decay, axis=-1, keepdims=True)
                strip = jnp.where(lane == ib * GLA_SUB + j, col, strip)
            strips.append(strip)
        diag_sc[...] = jnp.concatenate(strips, axis=0)

    attn = jnp.where(jnp.logical_and(tri, rblk == cblk), diag_sc[...], 0.0)
    k_hat = (k * jnp.exp(b_end - b)).astype(BF16)
    for jb in range(nsub - 1):
        q_hat = (q * jnp.exp(jnp.minimum(b - ends[jb][0:1], 0.0))).astype(BF16)
        attn = attn + jnp.where(jnp.logical_and(cblk == jb, rblk > jb), _dot_nt(q_hat, k_hat), 0.0)
    o = o + _dot(attn.astype(BF16), v)
    b_last = b[ch - 1:ch]
    k_dec = k * jnp.exp(b_last - b)
    k_dec_t = jnp.transpose(k_dec).astype(BF16)
    decay_t = jnp.transpose(jnp.broadcast_to(jnp.exp(b_last), (LANES, dk)))
    state_sc[...] = decay_t[:, 0:1] * state + _dot(k_dec_t, v)
    ms = jnp.mean(o * o, axis=-1, keepdims=True)
    on = o * lax.rsqrt(ms + RMS_EPS) * og
    return (on * _silu(r_in.astype(F32))).astype(BF16)


def _gla_mixer(h, hn, layer, w_in, w_gate1, w_gate2, b_gate, o_gain, w_out):
    s, d = h.shape
    nh = GLA_HEADS
    dk = d // 2
    dv = d
    dkh, dvh = dk // nh, dv // nh
    nmain = 2 * dk + 2 * dv

    def epi_bf16(accs, e_refs, o_refs, n):
        o_refs[0][...] = accs[0].astype(BF16)

    def epi_f32(accs, e_refs, o_refs, n):
        o_refs[0][...] = accs[0]

    tm, tn = 512, 1024
    proj = _matmul(hn, [(w_in, _wspec(d, tn, 0, layer))], epi_bf16,
                   [(jax.ShapeDtypeStruct((s, nmain), BF16), _ospec(tm, tn))],
                   tm=tm, tn=tn, n_total=nmain, name="gla_in")[0]
    wg1 = jnp.pad(w_gate1, ((0, 0), (0, LANES - GLA_GATE_RANK)))
    g1 = _matmul(hn, [(wg1, _wspec(d, LANES))], epi_f32,
                 [(jax.ShapeDtypeStruct((s, LANES), F32), _ospec(tm, LANES))],
                 tm=tm, tn=LANES, n_total=LANES, name="gla_gate_in")[0]
    wg2 = jnp.pad(w_gate2.astype(BF16), ((0, LANES - GLA_GATE_RANK), (0, 0)))
    ch = GLA_CHUNK
    o = pl.pallas_call(
        _gla_body,
        grid=(s // ch,),
        in_specs=[
            pl.BlockSpec((ch, dk), lambda c: (c, 0)),
            pl.BlockSpec((ch, dk), lambda c: (c, 1)),
            pl.BlockSpec((ch, dv), lambda c: (c, 2 * dk // dv)),
            pl.BlockSpec((ch, dv), lambda c: (c, 2 * dk // dv + 1)),
            pl.BlockSpec((ch, LANES), lambda c: (c, 0)),
            pl.BlockSpec((LANES, dk), lambda c: (0, 0)),
            pl.BlockSpec((1, dk), lambda c: (0, 0)),
            pl.BlockSpec((1, dvh), lambda c: (0, 0)),
        ],
        out_specs=pl.BlockSpec((ch, dv), lambda c: (c, 0)),
        out_shape=jax.ShapeDtypeStruct((s, dv), BF16),
        scratch_shapes=[pltpu.VMEM((nh, dkh, dvh), F32), pltpu.VMEM((nh, ch, ch), F32)],
        compiler_params=_cparams(("arbitrary",)),
        name="gla_scan",
    )(proj, proj, proj, proj, g1, wg2, b_gate.reshape(1, dk), o_gain.reshape(1, dvh))
    return _matmul_resid(o, w_out, h, layer=layer, name="gla_out")


def _dense_ffn(h, hn, layer, w_up, w_down):
    act = _swiglu_up(hn, w_up, layer=layer)
    return _matmul_resid(act, w_down, h, layer=layer, tn=512, tk=w_down.shape[1], name="ffn_down")


def _moe_ffn(h, gain, router, layer, w_up, w_down):
    s, _ = h.shape
    info, cnt = _moe_router(h, gain, router)
    dest1, dest2, src, tile_expert, nvalid = _moe_plan(info, cnt, s)
    xs = _moe_gather(h, gain, src, nvalid)
    act = _moe_up(xs, w_up, layer, tile_expert, nvalid)
    y = _moe_down(act, w_down, layer, tile_expert, nvalid)
    return _moe_combine(h, y, info, dest1, dest2)


def kernel(x, rel_bias, norm_gain, nsa_w_in, nsa_cmp_pe, nsa_cmp_w1, nsa_cmp_w2, nsa_q_gain, nsa_k_gain, nsa_w_out, sg_w_in, sg_v_gain, sg_w_s, sg_b_s, sg_w_out, gla_w_in, gla_w_gate2, gla_b_gate, gla_o_gain, gla_w_out, ffn_w_up, ffn_w_down, moe_router, moe_w_up, moe_w_down):
    bsz, s, d = x.shape
    tables = _nsa_tables(rel_bias)
    nsa_main = (NSA_KV_HEADS * NSA_GROUP + 6 * NSA_KV_HEADS) * HEAD_DIM
    gla_main = 3 * d
    nsa_in, nsa_out = nsa_w_in.astype(BF16), nsa_w_out.astype(BF16)
    sg_in, sg_out = sg_w_in.astype(BF16), sg_w_out.astype(BF16)
    gla_in, gla_out = gla_w_in.astype(BF16), gla_w_out.astype(BF16)
    ffn_up, ffn_down = ffn_w_up.astype(BF16), ffn_w_down.astype(BF16)
    outs = []
    for bi in range(bsz):
        h = x.reshape(s, d) if bsz == 1 else x[bi]
        for i in range(DEPTH):
            mixer = i % N_MIXERS
            j = i // N_MIXERS
            hn = _rmsnorm(h, norm_gain[i, 0])
            if mixer == 0:
                h = _nsa_mixer(h, hn, tables, j, nsa_in, nsa_w_in[j, :, nsa_main:], nsa_cmp_pe[j], nsa_cmp_w1[j],
                               nsa_cmp_w2[j],
                               nsa_q_gain[j], nsa_k_gain[j], nsa_out)
            elif mixer == 1:
                h = _sgu_mixer(h, hn, j, sg_in, sg_v_gain[j], sg_w_s[j], sg_b_s[j], sg_out)
            else:
                h = _gla_mixer(h, hn, j, gla_in, gla_w_in[j, :, gla_main:], gla_w_gate2[j], gla_b_gate[j],
                               gla_o_gain[j], gla_out)
            f = i // 2
            if i % 2 == 0:
                hn = _rmsnorm(h, norm_gain[i, 1])
                h = _dense_ffn(h, hn, f, ffn_up, ffn_down)
            else:
                h = _moe_ffn(h, norm_gain[i, 1], moe_router[f], f, moe_w_up, moe_w_down)
        outs.append(h)
    return outs[0].reshape(1, s, d) if bsz == 1 else jnp.stack(outs, axis=0)
```
